```python
import jax, jax.numpy as jnp
from jax import lax
import numpy as np

D_MODEL = 1024
BATCH = 8
SEQ = 2048
DEPTH = 2
DEC_BATCH = 128
DEC_SEQ = 4
PAST_LEN = 16384
PAGE_SIZE = 128

N_MIXERS = 2
N_GDN_LAYERS = (DEPTH + 1) // 2
N_SSM_LAYERS = DEPTH // 2
CONV_WIDTH = 4
EPS = 1e-6

GDN_QK_HEADS = 8
GDN_V_HEADS = 16
GDN_HEAD_DIM = 128
GDN_QK_DIM = GDN_QK_HEADS * GDN_HEAD_DIM
GDN_V_DIM = GDN_V_HEADS * GDN_HEAD_DIM
GDN_CONV_DIM = 2 * GDN_QK_DIM + GDN_V_DIM
GDN_PROJ = GDN_CONV_DIM + GDN_V_DIM + 2 * GDN_V_HEADS
GDN_CHUNK = 64

SSM_D_INNER = 2 * D_MODEL
SSM_HEAD_DIM = 64
SSM_HEADS = SSM_D_INNER // SSM_HEAD_DIM
SSM_GROUPS = 4
SSM_STATE = 128
SSM_CONV_DIM = SSM_D_INNER + 2 * SSM_GROUPS * SSM_STATE
SSM_PROJ = SSM_D_INNER + SSM_CONV_DIM + SSM_HEADS
SSM_CHUNK = 64

FFN_HIDDEN = ((-(-8 * D_MODEL // 3) + 255) // 256) * 256

kernel_name = "hybrid_gdn_ssd_adaln_decode_step"


def _rmsnorm(x, w):
    xf = x.astype(jnp.float32)
    y = xf * lax.rsqrt(jnp.mean(xf * xf, axis=-1, keepdims=True) + EPS)
    return (y * w.astype(jnp.float32)).astype(x.dtype)


def _l2norm(x):
    return x * lax.rsqrt(jnp.sum(x * x, axis=-1, keepdims=True) + EPS)


def _causal_conv(u, buf, w, bias=None):
    L = u.shape[1]
    ext = jnp.concatenate([buf.astype(u.dtype), u], axis=1)
    out = sum(ext[:, t:t + L] * w[t] for t in range(CONV_WIDTH))
    if bias is not None:
        out = out + bias
    return out, ext[:, L:].astype(buf.dtype)


def _chunk(t, c, n):
    pad = n * c - t.shape[1]
    t = jnp.pad(t, [(0, 0), (0, pad)] + [(0, 0)] * (t.ndim - 2))
    return t.reshape((t.shape[0], n, c) + t.shape[2:])


def _gated_delta_rule(q, k, v, g, beta, s0):
    bsz, L, H, _ = q.shape
    V = v.shape[-1]
    c = min(GDN_CHUNK, L)
    n = -(-L // c)
    blk = lambda t: jnp.swapaxes(_chunk(t, c, n), 2, 3)
    q, k, v, g, beta = blk(q), blk(k), blk(v), blk(g), blk(beta)
    cum = jnp.cumsum(g, axis=-1)
    causal = jnp.tril(jnp.ones((c, c), dtype=bool))
    strict = jnp.tril(jnp.ones((c, c), dtype=bool), k=-1)
    seg = cum[..., :, None] - cum[..., None, :]
    decay = jnp.exp(jnp.where(causal, seg, -jnp.inf))
    kb = k * beta[..., None]
    kk = jnp.einsum('bnhik,bnhjk->bnhij', kb, k) * decay
    tri = jnp.eye(c, dtype=jnp.float32) + jnp.where(strict, kk, 0.0)
    rhs = jnp.concatenate([v * beta[..., None], kb * jnp.exp(cum)[..., None]], axis=-1)
    sol = lax.linalg.triangular_solve(tri, rhs, left_side=True, lower=True, unit_diagonal=True)
    u, w = sol[..., :V], sol[..., V:]
    qk = jnp.einsum('bnhik,bnhjk->bnhij', q, k) * decay
    q_dec = q * jnp.exp(cum)[..., None]
    last = cum[..., -1]
    k_dec = k * jnp.exp(last[..., None] - cum)[..., None]

    def step(S, xs):
        u_, w_, qk_, qd_, kd_, l_ = xs
        delta = u_ - jnp.einsum('bhik,bhkv->bhiv', w_, S)
        o = jnp.einsum('bhik,bhkv->bhiv', qd_, S) + jnp.einsum('bhij,bhjv->bhiv', qk_, delta)
        S = jnp.exp(l_)[..., None, None] * S + jnp.einsum('bhjk,bhjv->bhkv', kd_, delta)
        return S, o

    xs = tuple(jnp.moveaxis(t, 1, 0) for t in (u, w, qk, q_dec, k_dec, last))
    S, o = lax.scan(step, s0.astype(jnp.float32), xs)
    o = jnp.swapaxes(jnp.moveaxis(o, 0, 1), 2, 3).reshape(bsz, n * c, H, V)[:, :L]
    return o, S


def _ssd_scan(x, dt, a, bm, cm, h0):
    bsz, L, H, P = x.shape
    G, N = bm.shape[2], bm.shape[3]
    R = H // G
    c = min(SSM_CHUNK, L)
    n = -(-L // c)
    xdt = _chunk(x * dt[..., None], c, n).reshape(bsz, n, c, G, R, P)
    la = _chunk(dt * a, c, n).reshape(bsz, n, c, G, R)
    bm, cm = _chunk(bm, c, n), _chunk(cm, c, n)
    cum = jnp.cumsum(la, axis=2)
    causal = jnp.tril(jnp.ones((c, c), dtype=bool))[:, :, None, None]
    seg = cum[:, :, :, None] - cum[:, :, None]
    decay = jnp.exp(jnp.where(causal, seg, -jnp.inf))
    cb = jnp.einsum('bnigs,bnjgs->bnijg', cm, bm)
    y_diag = jnp.einsum('bnijg,bnijgr,bnjgrp->bnigrp', cb, decay, xdt)
    last = cum[:, :, -1]
    to_end = jnp.exp(last[:, :, None] - cum)
    local = jnp.einsum('bnjgr,bnjgrp,bnjgs->bngrps', to_end, xdt, bm)

    def step(h, xs):
        l_, loc = xs
        return jnp.exp(l_)[..., None, None] * h + loc, h

    h_init = h0.astype(jnp.float32).reshape(bsz, G, R, P, N)
    h_fin, h_prev = lax.scan(step, h_init, (jnp.moveaxis(last, 1, 0), jnp.moveaxis(local, 1, 0)))
    h_prev = jnp.moveaxis(h_prev, 0, 1)
    y_off = jnp.einsum('bnigs,bngrps,bnigr->bnigrp', cm, h_prev, jnp.exp(cum))
    y = (y_diag + y_off).reshape(bsz, n * c, H, P)[:, :L]
    return y, h_fin.reshape(bsz, H, P, N)


def _gdn_mixer(h, s0, buf, w_in, conv_w, a_log, dt_bias, norm_w, w_out):
    bsz, L, _ = h.shape
    proj = h @ w_in
    qkv, z, beta_raw, a_raw = jnp.split(
        proj, [GDN_CONV_DIM, GDN_CONV_DIM + GDN_V_DIM, GDN_CONV_DIM + GDN_V_DIM + GDN_V_HEADS], axis=-1)
    qkv, new_buf = _causal_conv(qkv, buf, conv_w)
    qkv = jax.nn.silu(qkv.astype(jnp.float32))
    q, k, v = jnp.split(qkv, [GDN_QK_DIM, 2 * GDN_QK_DIM], axis=-1)
    rep = GDN_V_HEADS // GDN_QK_HEADS
    q = jnp.repeat(_l2norm(q.reshape(bsz, L, GDN_QK_HEADS, GDN_HEAD_DIM)), rep, axis=2) * GDN_HEAD_DIM ** -0.5
    k = jnp.repeat(_l2norm(k.reshape(bsz, L, GDN_QK_HEADS, GDN_HEAD_DIM)), rep, axis=2)
    v = v.reshape(bsz, L, GDN_V_HEADS, GDN_HEAD_DIM)
    beta = jax.nn.sigmoid(beta_raw.astype(jnp.float32))
    g = -jnp.exp(a_log.astype(jnp.float32)) * jax.nn.softplus(a_raw.astype(jnp.float32) + dt_bias.astype(jnp.float32))
    o, s_new = _gated_delta_rule(q, k, v, g, beta, s0)
    o = _rmsnorm(o, norm_w) * jax.nn.silu(z.astype(jnp.float32).reshape(bsz, L, GDN_V_HEADS, GDN_HEAD_DIM))
    out = o.reshape(bsz, L, GDN_V_DIM).astype(h.dtype) @ w_out
    return out, s_new.astype(s0.dtype), new_buf


def _ssm_mixer(h, s0, buf, w_in, conv_w, conv_b, a_log, dt_bias, d_skip, norm_w, w_out):
    bsz, L, _ = h.shape
    proj = h @ w_in
    z, xbc, dt_raw = jnp.split(proj, [SSM_D_INNER, SSM_D_INNER + SSM_CONV_DIM], axis=-1)
    xbc, new_buf = _causal_conv(xbc, buf, conv_w, conv_b)
    xbc = jax.nn.silu(xbc.astype(jnp.float32))
    xs, bm, cm = jnp.split(xbc, [SSM_D_INNER, SSM_D_INNER + SSM_GROUPS * SSM_STATE], axis=-1)
    xs = xs.reshape(bsz, L, SSM_HEADS, SSM_HEAD_DIM)
    bm = bm.reshape(bsz, L, SSM_GROUPS, SSM_STATE)
    cm = cm.reshape(bsz, L, SSM_GROUPS, SSM_STATE)
    dt = jax.nn.softplus(dt_raw.astype(jnp.float32) + dt_bias.astype(jnp.float32))
    a = -jnp.exp(a_log.astype(jnp.float32))
    y, s_new = _ssd_scan(xs, dt, a, bm, cm, s0)
    y = y + d_skip.astype(jnp.float32)[:, None] * xs
    y = y.reshape(bsz, L, SSM_D_INNER) * jax.nn.silu(z.astype(jnp.float32))
    gsz = SSM_D_INNER // SSM_GROUPS
    y = _rmsnorm(y.reshape(bsz, L, SSM_GROUPS, gsz), norm_w.reshape(SSM_GROUPS, gsz)).reshape(bsz, L, SSM_D_INNER)
    out = y.astype(h.dtype) @ w_out
    return out, s_new.astype(s0.dtype), new_buf


def _swiglu(h, w_gate_up, w_down):
    gate, up = jnp.split(h @ w_gate_up, 2, axis=-1)
    return (jax.nn.silu(gate) * up) @ w_down


def _trunk(x, c, st_gdn, cv_gdn, st_ssm, cv_ssm, p):
    new_gdn_s, new_gdn_c, new_ssm_s, new_ssm_c = [], [], [], []
    cs = jax.nn.silu(c)
    for i in range(DEPTH):
        mod = cs @ p['w_mod'][i] + p['b_mod'][i]
        sh1, sc1, gt1, sh2, sc2, gt2 = [m[:, None, :] for m in jnp.split(mod, 6, axis=-1)]
        h = _rmsnorm(x, p['norm_mix'][i]) * (1 + sc1) + sh1
        j = i // N_MIXERS
        if i % N_MIXERS == 0:
            out, s, cv = _gdn_mixer(h, st_gdn[j], cv_gdn[j], p['gdn_w_in'][j], p['gdn_conv_w'][j], p['gdn_a_log'][j],
                                    p['gdn_dt_bias'][j], p['gdn_norm'][j], p['gdn_w_out'][j])
            new_gdn_s.append(s)
            new_gdn_c.append(cv)
        else:
            out, s, cv = _ssm_mixer(h, st_ssm[j], cv_ssm[j], p['ssm_w_in'][j], p['ssm_conv_w'][j], p['ssm_conv_b'][j],
                                    p['ssm_a_log'][j], p['ssm_dt_bias'][j], p['ssm_d'][j], p['ssm_norm'][j],
                                    p['ssm_w_out'][j])
            new_ssm_s.append(s)
            new_ssm_c.append(cv)
        x = x + gt1 * out
        h = _rmsnorm(x, p['norm_ffn'][i]) * (1 + sc2) + sh2
        x = x + gt2 * _swiglu(h, p['ffn_w_gate_up'][i], p['ffn_w_down'][i])
    y = _rmsnorm(x, p['norm_final'])
    return y, jnp.stack(new_gdn_s), jnp.stack(new_gdn_c), jnp.stack(new_ssm_s), jnp.stack(new_ssm_c)


def setup_inputs(seed: int = 0) -> dict:
    key = jax.random.key(seed)
    ks = jax.random.split(key, 32)
    f32 = jnp.float32
    nrm = lambda k, shape, s: jax.random.normal(k, shape, f32) * s

    def dt_bias_init(k, shape):
        dt = jnp.exp(jax.random.uniform(k, shape, f32, np.log(1e-3), np.log(1e-1)))
        return dt + jnp.log(-jnp.expm1(-dt))

    def a_log_init(k, shape):
        return jnp.log(jax.random.uniform(k, shape, f32, 1.0, 16.0))

    return {
        'x_prompt': nrm(ks[0], (BATCH, SEQ, D_MODEL), 1.0),
        'x_sample': nrm(ks[1], (DEC_BATCH, DEC_SEQ, D_MODEL), 1.0),
        'c_prompt': nrm(ks[2], (BATCH, D_MODEL), 1.0),
        'c_sample': nrm(ks[3], (DEC_BATCH, D_MODEL), 1.0),
        'state_gdn': nrm(ks[4], (N_GDN_LAYERS, DEC_BATCH, GDN_V_HEADS, GDN_HEAD_DIM, GDN_HEAD_DIM), 0.3),
        'state_gdn_conv': nrm(ks[5], (N_GDN_LAYERS, DEC_BATCH, CONV_WIDTH - 1, GDN_CONV_DIM), 1.0),
        'state_ssm': nrm(ks[6], (N_SSM_LAYERS, DEC_BATCH, SSM_HEADS, SSM_HEAD_DIM, SSM_STATE), 0.3),
        'state_ssm_conv': nrm(ks[7], (N_SSM_LAYERS, DEC_BATCH, CONV_WIDTH - 1, SSM_CONV_DIM), 1.0),
        'w_mod': nrm(ks[8], (DEPTH, D_MODEL, 6 * D_MODEL), 0.5 * D_MODEL ** -0.5),
        'b_mod': nrm(ks[9], (DEPTH, 6 * D_MODEL), 0.02),
        'norm_mix': 1.0 + nrm(ks[10], (DEPTH, D_MODEL), 0.02),
        'norm_ffn': 1.0 + nrm(ks[11], (DEPTH, D_MODEL), 0.02),
        'norm_final': 1.0 + nrm(ks[12], (D_MODEL,), 0.02),
        'gdn_w_in': nrm(ks[13], (N_GDN_LAYERS, D_MODEL, GDN_PROJ), D_MODEL ** -0.5),
        'gdn_conv_w': nrm(ks[14], (N_GDN_LAYERS, CONV_WIDTH, GDN_CONV_DIM), CONV_WIDTH ** -0.5),
        'gdn_a_log': a_log_init(ks[15], (N_GDN_LAYERS, GDN_V_HEADS)),
        'gdn_dt_bias': dt_bias_init(ks[16], (N_GDN_LAYERS, GDN_V_HEADS)),
        'gdn_norm': 1.0 + nrm(ks[17], (N_GDN_LAYERS, GDN_HEAD_DIM), 0.02),
        'gdn_w_out': nrm(ks[18], (N_GDN_LAYERS, GDN_V_DIM, D_MODEL), GDN_V_DIM ** -0.5),
        'ssm_w_in': nrm(ks[19], (N_SSM_LAYERS, D_MODEL, SSM_PROJ), D_MODEL ** -0.5),
        'ssm_conv_w': nrm(ks[20], (N_SSM_LAYERS, CONV_WIDTH, SSM_CONV_DIM), CONV_WIDTH ** -0.5),
        'ssm_conv_b': nrm(ks[21], (N_SSM_LAYERS, SSM_CONV_DIM), 0.02),
        'ssm_a_log': a_log_init(ks[22], (N_SSM_LAYERS, SSM_HEADS)),
        'ssm_dt_bias': dt_bias_init(ks[23], (N_SSM_LAYERS, SSM_HEADS)),
        'ssm_d': 1.0 + nrm(ks[24], (N_SSM_LAYERS, SSM_HEADS), 0.1),
        'ssm_norm': 1.0 + nrm(ks[25], (N_SSM_LAYERS, SSM_D_INNER), 0.02),
        'ssm_w_out': nrm(ks[26], (N_SSM_LAYERS, SSM_D_INNER, D_MODEL), SSM_D_INNER ** -0.5),
        'ffn_w_gate_up': nrm(ks[27], (DEPTH, D_MODEL, 2 * FFN_HIDDEN), D_MODEL ** -0.5),
        'ffn_w_down': nrm(ks[28], (DEPTH, FFN_HIDDEN, D_MODEL), FFN_HIDDEN ** -0.5),
    }


def reference(x_prompt, x_sample, c_prompt, c_sample, state_gdn, state_gdn_conv, state_ssm, state_ssm_conv,
              w_mod, b_mod, norm_mix, norm_ffn, norm_final, gdn_w_in, gdn_conv_w, gdn_a_log, gdn_dt_bias, gdn_norm,
              gdn_w_out, ssm_w_in, ssm_conv_w, ssm_conv_b, ssm_a_log, ssm_dt_bias, ssm_d, ssm_norm, ssm_w_out,
              ffn_w_gate_up, ffn_w_down):
    p = dict(w_mod=w_mod, b_mod=b_mod, norm_mix=norm_mix, norm_ffn=norm_ffn, norm_final=norm_final,
             gdn_w_in=gdn_w_in, gdn_conv_w=gdn_conv_w, gdn_a_log=gdn_a_log, gdn_dt_bias=gdn_dt_bias,
             gdn_norm=gdn_norm, gdn_w_out=gdn_w_out, ssm_w_in=ssm_w_in, ssm_conv_w=ssm_conv_w,
             ssm_conv_b=ssm_conv_b, ssm_a_log=ssm_a_log, ssm_dt_bias=ssm_dt_bias, ssm_d=ssm_d, ssm_norm=ssm_norm,
             ssm_w_out=ssm_w_out, ffn_w_gate_up=ffn_w_gate_up, ffn_w_down=ffn_w_down)
    bp = x_prompt.shape[0]
    z_gdn = jnp.zeros((N_GDN_LAYERS, bp) + state_gdn.shape[2:], state_gdn.dtype)
    z_gdn_c = jnp.zeros((N_GDN_LAYERS, bp) + state_gdn_conv.shape[2:], state_gdn_conv.dtype)
    z_ssm = jnp.zeros((N_SSM_LAYERS, bp) + state_ssm.shape[2:], state_ssm.dtype)
    z_ssm_c = jnp.zeros((N_SSM_LAYERS, bp) + state_ssm_conv.shape[2:], state_ssm_conv.dtype)
    y_prompt, gdn_s_p, gdn_c_p, ssm_s_p, ssm_c_p = _trunk(x_prompt, c_prompt, z_gdn, z_gdn_c, z_ssm, z_ssm_c, p)
    y_sample, gdn_s_s, gdn_c_s, ssm_s_s, ssm_c_s = _trunk(x_sample, c_sample, state_gdn, state_gdn_conv,
                                                          state_ssm, state_ssm_conv, p)
    return (y_prompt, y_sample, gdn_s_p, gdn_c_p, ssm_s_p, ssm_c_p, gdn_s_s, gdn_c_s, ssm_s_s, ssm_c_s)
```

```python
import functools

import numpy as np
import jax
import jax.numpy as jnp
from jax import lax
from jax.experimental import pallas as pl
from jax.experimental.pallas import tpu as pltpu

F32 = jnp.float32
BF16 = jnp.bfloat16

D_MODEL = 1024
EPS = 1e-6
CONV_WIDTH = 4
CHUNK = 64
LANES = 128
HALF = LANES // 2

GDN_QK_HEADS = 8
GDN_V_HEADS = 16
GDN_HEAD_DIM = 128
GDN_QK_DIM = GDN_QK_HEADS * GDN_HEAD_DIM
GDN_V_DIM = GDN_V_HEADS * GDN_HEAD_DIM
GDN_CONV_DIM = 2 * GDN_QK_DIM + GDN_V_DIM
GDN_MAIN = GDN_CONV_DIM + GDN_V_DIM

SSM_D_INNER = 2 * D_MODEL
SSM_HEAD_DIM = 64
SSM_HEADS = SSM_D_INNER // SSM_HEAD_DIM
SSM_GROUPS = 4
SSM_STATE = 128
SSM_BC = SSM_GROUPS * SSM_STATE
SSM_CONV_DIM = SSM_D_INNER + 2 * SSM_BC
SSM_MAIN = SSM_D_INNER + SSM_CONV_DIM

FFN_HIDDEN = 2816

VMEM_LIMIT = 56 * 1024 * 1024
NEG_BIG = -1e30


def _cparams(*sem):
    return pltpu.CompilerParams(dimension_semantics=sem, vmem_limit_bytes=VMEM_LIMIT)


def _silu(x):
    return x * jax.nn.sigmoid(x)


def _softplus(x):
    return jnp.maximum(x, 0.0) + jnp.log1p(jnp.exp(-jnp.abs(x)))


def _dot(a, b):
    return jnp.dot(a, b, preferred_element_type=F32)


def _dot_nt(a, b):
    return lax.dot_general(a, b, (((1,), (1,)), ((), ())), preferred_element_type=F32)


def _tile_rows(v, rep):
    return v if rep == 1 else jnp.concatenate([v] * rep, axis=0)


def _mod_kernel(c_ref, w_ref, b_ref, o_ref):
    cs = _silu(c_ref[...]).astype(BF16)
    o_ref[...] = _dot(cs, w_ref[...]) + b_ref[...]


def _modulation(c, w_mod_bf, b_mod):
    depth, _, n = w_mod_bf.shape
    bc = c.shape[0]
    tn = 1536
    return pl.pallas_call(
        _mod_kernel,
        grid=(depth, n // tn),
        in_specs=[
            pl.BlockSpec((bc, D_MODEL), lambda l, j: (0, 0)),
            pl.BlockSpec((None, D_MODEL, tn), lambda l, j: (l, 0, j)),
            pl.BlockSpec((None, 1, tn), lambda l, j: (l, 0, j)),
        ],
        out_specs=pl.BlockSpec((None, bc, tn), lambda l, j: (l, 0, j)),
        out_shape=jax.ShapeDtypeStruct((depth, bc, n), F32),
        compiler_params=_cparams("parallel", "parallel"),
        name="adaln_mod",
    )(c, w_mod_bf, b_mod.reshape(depth, 1, n))


def _norm_mod(x, nw, sc, sh, rep):
    y = x * lax.rsqrt(jnp.mean(x * x, axis=-1, keepdims=True) + EPS) * nw
    return y * (1.0 + _tile_rows(sc, rep)) + _tile_rows(sh, rep)


def _in_proj_kernel(x_ref, nw_ref, sc_ref, sh_ref, w_ref, w2_ref, o_ref, o2_ref, h_ref, *, rep):
    @pl.when(pl.program_id(1) == 0)
    def _():
        h = _norm_mod(x_ref[...], nw_ref[...], sc_ref[...], sh_ref[...], rep).astype(BF16)
        h_ref[...] = h
        o2_ref[...] = _dot(h, w2_ref[...])

    o_ref[...] = _dot(h_ref[...], w_ref[...])


def _ffn_up_kernel(x_ref, nw_ref, sc_ref, sh_ref, wg_ref, wu_ref, o_ref, h_ref, *, rep):
    @pl.when(pl.program_id(1) == 0)
    def _():
        h_ref[...] = _norm_mod(x_ref[...], nw_ref[...], sc_ref[...], sh_ref[...], rep).astype(BF16)

    h = h_ref[...]
    o_ref[...] = (_silu(_dot(h, wg_ref[...])) * _dot(h, wu_ref[...])).astype(BF16)


class _Rows:
    def __init__(self, m, tm, group_rows, mod_rows):
        assert m % tm == 0
        self.m, self.tm = m, tm
        if mod_rows == 1:
            assert group_rows % tm == 0
            self.rep = 1
            self.gmap = lambda i: (i * tm) // group_rows
        else:
            assert tm % mod_rows == 0
            self.rep = tm // mod_rows
            self.gmap = lambda i: 0
        self.mod_rows = mod_rows

    def mod_spec(self, col_block, with_j):
        if with_j:
            return pl.BlockSpec((None, self.mod_rows, D_MODEL), lambda i, j: (self.gmap(i), 0, col_block))
        return pl.BlockSpec((None, self.mod_rows, D_MODEL), lambda i: (self.gmap(i), 0, col_block))


def _in_proj(x, nw, mod3, rows, sc_blk, sh_blk, w, w2, tn):
    m, tm = rows.m, rows.tm
    n = w.shape[1]
    assert n % tn == 0
    return pl.pallas_call(
        functools.partial(_in_proj_kernel, rep=rows.rep),
        grid=(m // tm, n // tn),
        in_specs=[
            pl.BlockSpec((tm, D_MODEL), lambda i, j: (i, 0)),
            pl.BlockSpec((1, D_MODEL), lambda i, j: (0, 0)),
            rows.mod_spec(sc_blk, True),
            rows.mod_spec(sh_blk, True),
            pl.BlockSpec((D_MODEL, tn), lambda i, j: (0, j)),
            pl.BlockSpec((D_MODEL, LANES), lambda i, j: (0, 0)),
        ],
        out_specs=[
            pl.BlockSpec((tm, tn), lambda i, j: (i, j)),
            pl.BlockSpec((tm, LANES), lambda i, j: (i, 0)),
        ],
        out_shape=[jax.ShapeDtypeStruct((m, n), F32), jax.ShapeDtypeStruct((m, LANES), F32)],
        scratch_shapes=[pltpu.VMEM((tm, D_MODEL), BF16)],
        compiler_params=_cparams("parallel", "arbitrary"),
        name="in_proj",
    )(x, nw.reshape(1, D_MODEL), mod3, mod3, w, w2)


def _ffn_up(x, nw, mod3, rows, wg, wu, th):
    m, tm = rows.m, rows.tm
    assert FFN_HIDDEN % th == 0
    return pl.pallas_call(
        functools.partial(_ffn_up_kernel, rep=rows.rep),
        grid=(m // tm, FFN_HIDDEN // th),
        in_specs=[
            pl.BlockSpec((tm, D_MODEL), lambda i, j: (i, 0)),
            pl.BlockSpec((1, D_MODEL), lambda i, j: (0, 0)),
            rows.mod_spec(4, True),
            rows.mod_spec(3, True),
            pl.BlockSpec((D_MODEL, th), lambda i, j: (0, j)),
            pl.BlockSpec((D_MODEL, th), lambda i, j: (0, j)),
        ],
        out_specs=pl.BlockSpec((tm, th), lambda i, j: (i, j)),
        out_shape=jax.ShapeDtypeStruct((m, FFN_HIDDEN), BF16),
        scratch_shapes=[pltpu.VMEM((tm, D_MODEL), BF16)],
        compiler_params=_cparams("parallel", "arbitrary"),
        name="ffn_up",
    )(x, nw.reshape(1, D_MODEL), mod3, mod3, wg, wu)


def _resid_store(acc, x_ref, gt_ref, o_ref, fnw_ref, y_ref, rep):
    xn = x_ref[...] + _tile_rows(gt_ref[...], rep) * acc
    o_ref[...] = xn
    if y_ref is not None:
        y_ref[...] = xn * lax.rsqrt(jnp.mean(xn * xn, axis=-1, keepdims=True) + EPS) * fnw_ref[...]


def _ffn_down_kernel(a_ref, x_ref, gt_ref, w_ref, *rest, rep, final):
    if final:
        fnw_ref, o_ref, y_ref = rest
    else:
        (o_ref,), fnw_ref, y_ref = rest, None, None
    _resid_store(_dot(a_ref[...], w_ref[...]), x_ref, gt_ref, o_ref, fnw_ref, y_ref, rep)


def _ffn_down(act, x, mod3, rows, w, fnw):
    m, tm = rows.m, rows.tm
    final = fnw is not None
    in_specs = [
        pl.BlockSpec((tm, FFN_HIDDEN), lambda i: (i, 0)),
        pl.BlockSpec((tm, D_MODEL), lambda i: (i, 0)),
        rows.mod_spec(5, False),
        pl.BlockSpec((FFN_HIDDEN, D_MODEL), lambda i: (0, 0)),
    ]
    args = [act, x, mod3, w]
    row_spec = pl.BlockSpec((tm, D_MODEL), lambda i: (i, 0))
    out_shape = jax.ShapeDtypeStruct((m, D_MODEL), F32)
    if final:
        in_specs.append(pl.BlockSpec((1, D_MODEL), lambda i: (0, 0)))
        args.append(fnw.reshape(1, D_MODEL))
        out_specs, out_shapes = [row_spec, row_spec], [out_shape, out_shape]
    else:
        out_specs, out_shapes = row_spec, out_shape
    return pl.pallas_call(
        functools.partial(_ffn_down_kernel, rep=rows.rep, final=final),
        grid=(m // tm,),
        in_specs=in_specs,
        out_specs=out_specs,
        out_shape=out_shapes,
        compiler_params=_cparams("parallel"),
        name="ffn_down",
    )(*args)


def _mixer_out_kernel(y_ref, z_ref, nw_ref, x_ref, gt_ref, w_ref, o_ref, a_ref, *, rep, group, gate_first):
    width = y_ref.shape[1]
    for s in range(0, width, group):
        y = y_ref[:, s:s + group]
        gate = _silu(z_ref[:, s:s + group])
        if gate_first:
            y = y * gate
        y = y * lax.rsqrt(jnp.mean(y * y, axis=-1, keepdims=True) + EPS) * nw_ref[:, s:s + group]
        if not gate_first:
            y = y * gate
        a_ref[:, s:s + group] = y.astype(BF16)
    _resid_store(_dot(a_ref[...], w_ref[...]), x_ref, gt_ref, o_ref, None, None, rep)


def _mixer_out(y, zsrc, z_blk, nw_full, x, mod3, rows, w, group, gate_first):
    m, tm = rows.m, rows.tm
    width = y.shape[1]
    return pl.pallas_call(
        functools.partial(_mixer_out_kernel, rep=rows.rep, group=group, gate_first=gate_first),
        grid=(m // tm,),
        in_specs=[
            pl.BlockSpec((tm, width), lambda i: (i, 0)),
            pl.BlockSpec((tm, width), lambda i: (i, z_blk)),
            pl.BlockSpec((1, width), lambda i: (0, 0)),
            pl.BlockSpec((tm, D_MODEL), lambda i: (i, 0)),
            rows.mod_spec(2, False),
            pl.BlockSpec((width, D_MODEL), lambda i: (0, 0)),
        ],
        out_specs=pl.BlockSpec((tm, D_MODEL), lambda i: (i, 0)),
        out_shape=jax.ShapeDtypeStruct((m, D_MODEL), F32),
        scratch_shapes=[pltpu.VMEM((tm, width), BF16)],
        compiler_params=_cparams("parallel"),
        name="mixer_out",
    )(y, zsrc, nw_full.reshape(1, width), x, mod3, w)


CONV_COLS = 1024
HALO = 8


def _post_conv(acc, o_ref, cb, n_l2, qscale):
    y = _silu(acc)
    if n_l2 == 0:
        o_ref[...] = y
        return

    @pl.when(cb < n_l2)
    def _():
        scale = jnp.where(cb == 0, qscale, 1.0).astype(F32)
        for s in range(0, CONV_COLS, GDN_HEAD_DIM):
            yh = y[:, s:s + GDN_HEAD_DIM]
            o_ref[:, s:s + GDN_HEAD_DIM] = yh * lax.rsqrt(jnp.sum(yh * yh, axis=-1, keepdims=True) + EPS) * scale

    @pl.when(cb >= n_l2)
    def _():
        o_ref[...] = y


def _conv_rows_kernel(u_ref, w_ref, b_ref, o_ref, ext_ref, *, tm, n_l2, qscale):
    cb, t = pl.program_id(1), pl.program_id(2)

    @pl.when(t == 0)
    def _():
        ext_ref[0:HALO, :] = jnp.zeros((HALO, CONV_COLS), F32)

    @pl.when(t > 0)
    def _():
        ext_ref[0:HALO, :] = ext_ref[tm:tm + HALO, :]

    u = u_ref[...]
    ext_ref[HALO:HALO + tm, :] = u
    acc = b_ref[...] + w_ref[3:4, :] * u
    for tap in range(CONV_WIDTH - 1):
        off = HALO - (CONV_WIDTH - 1) + tap
        acc = acc + w_ref[tap:tap + 1, :] * ext_ref[off:off + tm, :]
    _post_conv(acc, o_ref, cb, n_l2, qscale)


def _conv_rows(u, col_off, n_cols, conv_w, conv_b, nseq, seq_len, n_l2, qscale):
    tm = min(256, seq_len)
    assert seq_len % tm == 0 and n_cols % CONV_COLS == 0 and col_off % CONV_COLS == 0
    tiles = seq_len // tm
    cb0 = col_off // CONV_COLS
    return pl.pallas_call(
        functools.partial(_conv_rows_kernel, tm=tm, n_l2=n_l2, qscale=qscale),
        grid=(nseq, n_cols // CONV_COLS, tiles),
        in_specs=[
            pl.BlockSpec((tm, CONV_COLS), lambda b, c, t: (b * tiles + t, cb0 + c)),
            pl.BlockSpec((CONV_WIDTH, CONV_COLS), lambda b, c, t: (0, c)),
            pl.BlockSpec((1, CONV_COLS), lambda b, c, t: (0, c)),
        ],
        out_specs=pl.BlockSpec((tm, CONV_COLS), lambda b, c, t: (b * tiles + t, c)),
        out_shape=jax.ShapeDtypeStruct((nseq * seq_len, n_cols), F32),
        scratch_shapes=[pltpu.VMEM((tm + HALO, CONV_COLS), F32)],
        compiler_params=_cparams("parallel", "parallel", "arbitrary"),
        name="conv_rows",
    )(u, conv_w, conv_b.reshape(1, n_cols))


def _conv_steps_kernel(u_ref, hist_ref, w_ref, b_ref, o_ref, *, steps, n_l2, qscale):
    cb = pl.program_id(0)
    ext = [hist_ref[i] for i in range(CONV_WIDTH - 1)] + [u_ref[i] for i in range(steps)]
    for t in range(steps):
        acc = b_ref[...] + w_ref[0:1, :] * ext[t]
        for tap in range(1, CONV_WIDTH):
            acc = acc + w_ref[tap:tap + 1, :] * ext[t + tap]
        _post_conv(acc, o_ref.at[t], cb, n_l2, qscale)


def _conv_steps(u3, col_off, n_cols, hist3, conv_w, conv_b, n_l2, qscale):
    steps, nb, _ = u3.shape
    cb0 = col_off // CONV_COLS
    return pl.pallas_call(
        functools.partial(_conv_steps_kernel, steps=steps, n_l2=n_l2, qscale=qscale),
        grid=(n_cols // CONV_COLS,),
        in_specs=[
            pl.BlockSpec((steps, nb, CONV_COLS), lambda c: (0, 0, cb0 + c)),
            pl.BlockSpec((CONV_WIDTH - 1, nb, CONV_COLS), lambda c: (0, 0, c)),
            pl.BlockSpec((CONV_WIDTH, CONV_COLS), lambda c: (0, c)),
            pl.BlockSpec((1, CONV_COLS), lambda c: (0, c)),
        ],
        out_specs=pl.BlockSpec((steps, nb, CONV_COLS), lambda c: (0, 0, c)),
        out_shape=jax.ShapeDtypeStruct((steps, nb, n_cols), F32),
        compiler_params=_cparams("parallel"),
        name="conv_steps",
    )(u3, hist3, conv_w, conv_b.reshape(1, n_cols))


def _split3(x):
    hi = x.astype(BF16)
    r = x - hi.astype(F32)
    mid = r.astype(BF16)
    lo = (r - mid.astype(F32)).astype(BF16)
    return hi, mid, lo


def _cat3(x, axis):
    return jnp.concatenate(_split3(x), axis=axis)


def _pad_t(tile):
    return jnp.concatenate([tile, jnp.zeros_like(tile)], axis=0).T


def _pair_rows(t, n):
    return t[0:n] + pltpu.roll(t[n:2 * n], HALF, 1)


def _iotas(shape):
    return lax.broadcasted_iota(jnp.int32, shape, 0), lax.broadcasted_iota(jnp.int32, shape, 1)


def _block_diag(pair, bdmask):
    return jnp.where(bdmask, jnp.concatenate([pair, pair], axis=0), 0.0)


def _mm3(lhs, rhs):
    lh = lhs.astype(BF16)
    ll = (lhs - lh.astype(F32)).astype(BF16)
    rh = rhs.astype(BF16)
    rl = (rhs - rh.astype(F32)).astype(BF16)
    return _dot(jnp.concatenate([lh, lh, ll], axis=1), jnp.concatenate([rh, rl, rh], axis=0))


def _mm_hl(lhs, rhs_bf):
    lh = lhs.astype(BF16)
    ll = (lhs - lh.astype(F32)).astype(BF16)
    return _dot(jnp.concatenate([lh, ll], axis=1), jnp.concatenate([rhs_bf, rhs_bf], axis=0))


def _unit_lower_inverse(x, levels, bdmask, eye2):
    p = eye2 + x
    if levels <= 1:
        return p
    y = _mm3(x, _block_diag(x, bdmask))
    for _ in range(levels - 2):
        r = _mm3(jnp.concatenate([y, p], axis=0), _block_diag(y, bdmask))
        y, p = r[0:CHUNK], p + r[CHUNK:2 * CHUNK]
    return p + _mm3(p, _block_diag(y, bdmask))


def _two_blocks(a_bf, b_bf):
    z = jnp.zeros_like(a_bf)
    return jnp.concatenate([jnp.concatenate([a_bf, z], axis=1), jnp.concatenate([z, b_bf], axis=1)], axis=0)


def _recurrence_consts(n_pairs, chan_even0, chan_odd0):
    tril = np.tril(np.ones((CHUNK, CHUNK), np.float32))
    tril3 = np.concatenate([tril] * 3, axis=1)
    upper = np.zeros((LANES, 2 * LANES), np.float32)
    upper[:CHUNK, :CHUNK] = tril.T
    upper[:CHUNK, LANES:] = 1.0
    upper3 = np.concatenate([upper] * 3, axis=0)
    e = np.zeros((LANES, n_pairs * LANES), np.float32)
    for p in range(n_pairs):
        e[chan_even0 + p, p * LANES:p * LANES + HALF] = 1.0
        e[chan_odd0 + p, p * LANES + HALF:(p + 1) * LANES] = 1.0
    e3 = np.concatenate([e] * 3, axis=0)
    return jnp.asarray(tril3, BF16), jnp.asarray(upper3, BF16), jnp.asarray(e3, BF16)


def _levels(lr):
    return max(1, int(np.ceil(np.log2(lr))))


GDN_PAIRS = GDN_V_HEADS // 2


def _gdn_gates(raw, alog_row, dtb_row, lr):
    row, lane = _iotas((CHUNK, LANES))
    g = -jnp.exp(alog_row) * _softplus(raw + dtb_row)
    beta = jax.nn.sigmoid(raw)
    gt = jnp.where(lane < GDN_V_HEADS, g, jnp.where(lane < 2 * GDN_V_HEADS, beta, 0.0))
    if lr < CHUNK:
        gt = jnp.where(row < lr, gt, 0.0)
    return gt


def _gdn_block(act_ref, gt, s_in, s_out, o_ref, tril3_ref, upper3_ref, e3_ref, lr):
    np_ = GDN_PAIRS
    row, lane = _iotas((CHUNK, LANES))
    jl = jnp.bitwise_and(lane, HALF - 1)
    left = lane < HALF
    causal = row >= jl
    strict = row > jl
    eye2 = jnp.where(row == jl, 1.0, 0.0)
    r2, l2 = _iotas((LANES, LANES))
    bdmask = jnp.right_shift(r2, 6) == jnp.right_shift(l2, 6)
    _, lane_p = _iotas((np_, LANES))

    t1 = _pad_t(gt)
    cum_ext = _dot(_cat3(t1[0:2 * np_], 1), upper3_ref[...])
    cum_t, last_b = cum_ext[:, :LANES], cum_ext[:, LANES:]
    cum_rp = _pair_rows(cum_t, np_)
    beta_rp = _pair_rows(t1[2 * np_:4 * np_], np_)
    last_rp = jnp.where(lane_p < HALF, last_b[0:np_], last_b[np_:2 * np_])
    ecum_rp = jnp.exp(cum_rp)
    kdec_rp = jnp.exp(last_rp - cum_rp) * beta_rp
    elast = jnp.exp(last_b)
    cum = _dot(tril3_ref[...], _cat3(gt, 0))
    col_all = _dot(_cat3(cum, 1), e3_ref[...])

    for p in range(np_):
        q = act_ref[:, p * LANES:(p + 1) * LANES]
        k = act_ref[:, GDN_QK_DIM + p * LANES:GDN_QK_DIM + (p + 1) * LANES]
        v0 = act_ref[:, 2 * GDN_QK_DIM + 2 * p * LANES:2 * GDN_QK_DIM + (2 * p + 1) * LANES]
        v1 = act_ref[:, 2 * GDN_QK_DIM + (2 * p + 1) * LANES:2 * GDN_QK_DIM + (2 * p + 2) * LANES]
        kb, qb = k.astype(BF16), q.astype(BF16)
        gq = _dot_nt(jnp.concatenate([kb, qb], axis=0), jnp.concatenate([kb, kb], axis=0))
        colc = col_all[:, p * LANES:(p + 1) * LANES]
        dec = jnp.exp(jnp.where(causal, colc - cum_rp[p:p + 1], NEG_BIG))
        base = dec * beta_rp[p:p + 1]
        x = jnp.where(strict, -(gq[0:CHUNK] * base), 0.0)
        qkd = gq[CHUNK:2 * CHUNK] * base
        minv = _unit_lower_inverse(x, _levels(lr), bdmask, eye2)
        u = jnp.concatenate([v0, v1], axis=1) + _mm_hl(minv - eye2, _two_blocks(v0.astype(BF16), v1.astype(BF16)))
        w = _mm_hl(minv * ecum_rp[p:p + 1], _two_blocks(kb, kb))
        ecol = jnp.exp(colc)
        ecol_r = pltpu.roll(ecol, HALF, 1)
        efull = (jnp.where(left, ecol, ecol_r), jnp.where(left, ecol_r, ecol))
        deltas, qs, s_old = [], [], []
        for hh in range(2):
            s = s_in[2 * p + hh]
            r = _dot(jnp.concatenate([w[:, hh * LANES:(hh + 1) * LANES].astype(BF16), qb], axis=0), s.astype(BF16))
            deltas.append((u[:, hh * LANES:(hh + 1) * LANES] - r[0:CHUNK]).astype(BF16))
            qs.append(r[CHUNK:2 * CHUNK])
            s_old.append(s)
        bd_delta = _two_blocks(deltas[0], deltas[1])
        o_intra = _dot(qkd.astype(BF16), bd_delta)
        kt2 = jnp.concatenate([k, k], axis=0).T
        ds = _dot((kt2 * kdec_rp[p:p + 1]).astype(BF16), bd_delta)
        for hh in range(2):
            h = 2 * p + hh
            o_ref[:, h * LANES:(h + 1) * LANES] = efull[hh] * qs[hh] + o_intra[:, hh * LANES:(hh + 1) * LANES]
            e_h = elast[hh * np_ + p:hh * np_ + p + 1]
            s_out[h] = e_h * s_old[hh] + ds[:, hh * LANES:(hh + 1) * LANES]


def _gdn_seq_kernel(act_ref, graw_ref, alog_ref, dtb_ref, tril3_ref, upper3_ref, e3_ref, o_ref, s_ref):
    @pl.when(pl.program_id(1) == 0)
    def _():
        s_ref[...] = jnp.zeros(s_ref.shape, F32)

    gt = _gdn_gates(graw_ref[...], alog_ref[...], dtb_ref[...], CHUNK)
    _gdn_block(act_ref, gt, s_ref, s_ref, o_ref, tril3_ref, upper3_ref, e3_ref, CHUNK)


def _const_specs(consts, ngrid):
    zero = (lambda *_: (0, 0))
    return [pl.BlockSpec(c.shape, zero) for c in consts]


def _gdn_seq(act, graw, alog_row, dtb_row, consts, nseq, seq_len):
    nc = seq_len // CHUNK
    rowspec = pl.BlockSpec((1, LANES), lambda b, n: (0, 0))
    return pl.pallas_call(
        _gdn_seq_kernel,
        grid=(nseq, nc),
        in_specs=[
            pl.BlockSpec((CHUNK, GDN_CONV_DIM), lambda b, n: (b * nc + n, 0)),
            pl.BlockSpec((CHUNK, LANES), lambda b, n: (b * nc + n, 0)),
            rowspec, rowspec,
        ] + _const_specs(consts, 2),
        out_specs=[
            pl.BlockSpec((CHUNK, GDN_V_DIM), lambda b, n: (b * nc + n, 0)),
            pl.BlockSpec((None, GDN_V_HEADS, GDN_HEAD_DIM, GDN_HEAD_DIM), lambda b, n: (b, 0, 0, 0)),
        ],
        out_shape=[
            jax.ShapeDtypeStruct((nseq * seq_len, GDN_V_DIM), F32),
            jax.ShapeDtypeStruct((nseq, GDN_V_HEADS, GDN_HEAD_DIM, GDN_HEAD_DIM), F32),
        ],
        compiler_params=_cparams("parallel", "arbitrary"),
        name="gdn_seq",
    )(act, graw, alog_row, dtb_row, *consts)


def _load_padded(src_ref, pad_ref, lr):
    pad_ref[...] = jnp.zeros(pad_ref.shape, F32)
    for t in range(lr):
        pad_ref[t:t + 1, :] = src_ref[t]


def _gdn_step_kernel(act_ref, graw_ref, alog_ref, dtb_ref, tril3_ref, upper3_ref, e3_ref, s0_ref,
                     o_ref, s_ref, apad_ref, gpad_ref, opad_ref, *, lr):
    _load_padded(act_ref, apad_ref, lr)
    _load_padded(graw_ref, gpad_ref, lr)
    gt = _gdn_gates(gpad_ref[...], alog_ref[...], dtb_ref[...], lr)
    _gdn_block(apad_ref, gt, s0_ref, s_ref, opad_ref, tril3_ref, upper3_ref, e3_ref, lr)
    for t in range(lr):
        o_ref[t] = opad_ref[t:t + 1, :]


def _step_spec(steps, width):
    return pl.BlockSpec((steps, None, 1, width), lambda b: (0, b, 0, 0))


def _gdn_step(act3, graw3, alog_row, dtb_row, consts, s0):
    steps, nb, _ = act3.shape
    rowspec = pl.BlockSpec((1, LANES), lambda b: (0, 0))
    sspec = pl.BlockSpec((None, GDN_V_HEADS, GDN_HEAD_DIM, GDN_HEAD_DIM), lambda b: (b, 0, 0, 0))
    o, s = pl.pallas_call(
        functools.partial(_gdn_step_kernel, lr=steps),
        grid=(nb,),
        in_specs=[_step_spec(steps, GDN_CONV_DIM), _step_spec(steps, LANES), rowspec, rowspec]
        + _const_specs(consts, 1) + [sspec],
        out_specs=[_step_spec(steps, GDN_V_DIM), sspec],
        out_shape=[
            jax.ShapeDtypeStruct((steps, nb, 1, GDN_V_DIM), F32),
            jax.ShapeDtypeStruct(s0.shape, F32),
        ],
        scratch_shapes=[
            pltpu.VMEM((CHUNK, GDN_CONV_DIM), F32),
            pltpu.VMEM((CHUNK, LANES), F32),
            pltpu.VMEM((CHUNK, GDN_V_DIM), F32),
        ],
        compiler_params=_cparams("parallel"),
        name="gdn_step",
    )(act3.reshape(steps, nb, 1, GDN_CONV_DIM), graw3.reshape(steps, nb, 1, LANES), alog_row, dtb_row, *consts, s0)
    return o.reshape(steps * nb, GDN_V_DIM), s


SSM_PAIRS = SSM_HEADS // 2
PAIRS_PER_GROUP = SSM_PAIRS // SSM_GROUPS


def _ssm_gates(raw, alog_row, dtb_row, lr):
    row, lane = _iotas((CHUNK, LANES))
    dt = _softplus(raw + dtb_row)
    tile = jnp.where(lane < SSM_HEADS, dt, jnp.where(lane < 2 * SSM_HEADS, -jnp.exp(alog_row) * dt, 0.0))
    if lr < CHUNK:
        tile = jnp.where(row < lr, tile, 0.0)
    return tile


def _ssd_block(act_ref, tile, h_in, h_out, y_ref, dskip_ref, tril3_ref, upper3_ref, e3_ref):
    np_ = SSM_PAIRS
    row, lane = _iotas((CHUNK, LANES))
    causal = row >= jnp.bitwise_and(lane, HALF - 1)
    r2, l2 = _iotas((LANES, LANES))
    bdmask = jnp.right_shift(r2, 6) == jnp.right_shift(l2, 6)
    top = r2 < HALF
    _, lane_p = _iotas((np_, LANES))

    t1 = _pad_t(tile)
    cum_ext = _dot(_cat3(t1[2 * np_:4 * np_], 1), upper3_ref[...])
    cum_t, last_b = cum_ext[:, :LANES], cum_ext[:, LANES:]
    cum_rp = _pair_rows(cum_t, np_)
    dt_rp = _pair_rows(t1[0:2 * np_], np_)
    last_rp = jnp.where(lane_p < HALF, last_b[0:np_], last_b[np_:2 * np_])
    coef_rp = jnp.exp(last_rp - cum_rp) * dt_rp
    elast = jnp.exp(last_b)
    cum = _dot(tril3_ref[...], _cat3(tile, 0))
    col_all = _dot(_cat3(cum, 1), e3_ref[...])

    for g in range(SSM_GROUPS):
        bg = act_ref[:, SSM_D_INNER + g * SSM_STATE:SSM_D_INNER + (g + 1) * SSM_STATE].astype(BF16)
        cg = act_ref[:, SSM_D_INNER + SSM_BC + g * SSM_STATE:SSM_D_INNER + SSM_BC + (g + 1) * SSM_STATE].astype(BF16)
        bb = jnp.concatenate([bg, bg], axis=0)
        cb2 = _dot_nt(cg, bb)
        for pp in range(PAIRS_PER_GROUP):
            p = g * PAIRS_PER_GROUP + pp
            xp = act_ref[:, p * LANES:(p + 1) * LANES]
            colc = col_all[:, p * LANES:(p + 1) * LANES]
            dec = jnp.exp(jnp.where(causal, colc - cum_rp[p:p + 1], NEG_BIG))
            lm = cb2 * dec * dt_rp[p:p + 1]
            x2 = jnp.concatenate([xp, xp], axis=0)
            y_diag = _dot(lm.astype(BF16), jnp.where(bdmask, x2, 0.0).astype(BF16))
            hp = h_in[p]
            y_off = jnp.exp(colc) * _dot_nt(cg, hp.astype(BF16))
            y_ref[:, p * LANES:(p + 1) * LANES] = y_diag + y_off + dskip_ref[:, p * LANES:(p + 1) * LANES] * xp
            lhs = jnp.where(bdmask, x2.T * coef_rp[p:p + 1], 0.0)
            dh = _dot(lhs.astype(BF16), bb)
            e_rows = jnp.where(top, elast[p:p + 1], elast[np_ + p:np_ + p + 1])
            h_out[p] = e_rows * hp + dh


def _ssd_seq_kernel(act_ref, graw_ref, alog_ref, dtb_ref, dskip_ref, tril3_ref, upper3_ref, e3_ref, y_ref, h_ref):
    @pl.when(pl.program_id(1) == 0)
    def _():
        h_ref[...] = jnp.zeros(h_ref.shape, F32)

    tile = _ssm_gates(graw_ref[...], alog_ref[...], dtb_ref[...], CHUNK)
    _ssd_block(act_ref, tile, h_ref, h_ref, y_ref, dskip_ref, tril3_ref, upper3_ref, e3_ref)


def _ssd_seq(act, graw, alog_row, dtb_row, dskip_row, consts, nseq, seq_len):
    nc = seq_len // CHUNK
    rowspec = pl.BlockSpec((1, LANES), lambda b, n: (0, 0))
    return pl.pallas_call(
        _ssd_seq_kernel,
        grid=(nseq, nc),
        in_specs=[
            pl.BlockSpec((CHUNK, SSM_CONV_DIM), lambda b, n: (b * nc + n, 0)),
            pl.BlockSpec((CHUNK, LANES), lambda b, n: (b * nc + n, 0)),
            rowspec, rowspec,
            pl.BlockSpec((1, SSM_D_INNER), lambda b, n: (0, 0)),
        ] + _const_specs(consts, 2),
        out_specs=[
            pl.BlockSpec((CHUNK, SSM_D_INNER), lambda b, n: (b * nc + n, 0)),
            pl.BlockSpec((None, SSM_PAIRS, LANES, SSM_STATE), lambda b, n: (b, 0, 0, 0)),
        ],
        out_shape=[
            jax.ShapeDtypeStruct((nseq * seq_len, SSM_D_INNER), F32),
            jax.ShapeDtypeStruct((nseq, SSM_PAIRS, LANES, SSM_STATE), F32),
        ],
        compiler_params=_cparams("parallel", "arbitrary"),
        name="ssd_seq",
    )(act, graw, alog_row, dtb_row, dskip_row, *consts)


def _ssd_step_kernel(act_ref, graw_ref, alog_ref, dtb_ref, dskip_ref, tril3_ref, upper3_ref, e3_ref, h0_ref,
                     y_ref, h_ref, apad_ref, gpad_ref, ypad_ref, *, lr):
    _load_padded(act_ref, apad_ref, lr)
    _load_padded(graw_ref, gpad_ref, lr)
    tile = _ssm_gates(gpad_ref[...], alog_ref[...], dtb_ref[...], lr)
    _ssd_block(apad_ref, tile, h0_ref, h_ref, ypad_ref, dskip_ref, tril3_ref, upper3_ref, e3_ref)
    for t in range(lr):
        y_ref[t] = ypad_ref[t:t + 1, :]


def _ssd_step(act3, graw3, alog_row, dtb_row, dskip_row, consts, h0):
    steps, nb, _ = act3.shape
    rowspec = pl.BlockSpec((1, LANES), lambda b: (0, 0))
    hspec = pl.BlockSpec((None, SSM_PAIRS, LANES, SSM_STATE), lambda b: (b, 0, 0, 0))
    y, h = pl.pallas_call(
        functools.partial(_ssd_step_kernel, lr=steps),
        grid=(nb,),
        in_specs=[_step_spec(steps, SSM_CONV_DIM), _step_spec(steps, LANES), rowspec, rowspec,
                  pl.BlockSpec((1, SSM_D_INNER), lambda b: (0, 0))] + _const_specs(consts, 1) + [hspec],
        out_specs=[_step_spec(steps, SSM_D_INNER), hspec],
        out_shape=[
            jax.ShapeDtypeStruct((steps, nb, 1, SSM_D_INNER), F32),
            jax.ShapeDtypeStruct(h0.shape, F32),
        ],
        scratch_shapes=[
            pltpu.VMEM((CHUNK, SSM_CONV_DIM), F32),
            pltpu.VMEM((CHUNK, LANES), F32),
            pltpu.VMEM((CHUNK, SSM_D_INNER), F32),
        ],
        compiler_params=_cparams("parallel"),
        name="ssd_step",
    )(act3.reshape(steps, nb, 1, SSM_CONV_DIM), graw3.reshape(steps, nb, 1, LANES), alog_row, dtb_row, dskip_row,
      *consts, h0)
    return y.reshape(steps * nb, SSM_D_INNER), h


FFN_TILE = FFN_HIDDEN // 2
PROJ_TILE = 1024


def _lane_row(pieces):
    row = jnp.zeros((1, LANES), F32)
    for off, vec in pieces:
        row = row.at[0, off:off + vec.shape[0]].set(vec.astype(F32))
    return row


def _stage_params(w_mod, b_mod, norm_mix, norm_ffn, norm_final, gdn_w_in, gdn_conv_w, gdn_a_log, gdn_dt_bias,
                  gdn_norm, gdn_w_out, ssm_w_in, ssm_conv_w, ssm_conv_b, ssm_a_log, ssm_dt_bias, ssm_d, ssm_norm,
                  ssm_w_out, ffn_w_gate_up, ffn_w_down):
    perm_g = np.concatenate([np.arange(0, GDN_V_HEADS, 2), np.arange(1, GDN_V_HEADS, 2)])
    perm_s = np.concatenate([np.arange(0, SSM_HEADS, 2), np.arange(1, SSM_HEADS, 2)])
    g_in, s_in = gdn_w_in[0], ssm_w_in[0]
    beta_cols = g_in[:, GDN_MAIN:GDN_MAIN + GDN_V_HEADS][:, perm_g]
    a_cols = g_in[:, GDN_MAIN + GDN_V_HEADS:GDN_MAIN + 2 * GDN_V_HEADS][:, perm_g]
    gdn_small = jnp.concatenate([a_cols, beta_cols, jnp.zeros((D_MODEL, LANES - 2 * GDN_V_HEADS), F32)], axis=1)
    dt_cols = s_in[:, SSM_MAIN:SSM_MAIN + SSM_HEADS][:, perm_s]
    ssm_small = jnp.concatenate([dt_cols, dt_cols, jnp.zeros((D_MODEL, LANES - 2 * SSM_HEADS), F32)], axis=1)
    return dict(
        w_mod=w_mod.astype(BF16), b_mod=b_mod, norm_mix=norm_mix, norm_ffn=norm_ffn, norm_final=norm_final,
        gdn_main=g_in[:, :GDN_MAIN].astype(BF16), gdn_small=gdn_small.astype(BF16),
        gdn_conv_w=gdn_conv_w[0], gdn_conv_b=jnp.zeros((GDN_CONV_DIM,), F32),
        gdn_alog_row=_lane_row([(0, gdn_a_log[0][perm_g])]), gdn_dtb_row=_lane_row([(0, gdn_dt_bias[0][perm_g])]),
        gdn_norm=jnp.tile(gdn_norm[0], GDN_V_HEADS), gdn_w_out=gdn_w_out[0].astype(BF16),
        gdn_consts=_recurrence_consts(GDN_PAIRS, 0, GDN_PAIRS),
        ssm_main=s_in[:, :SSM_MAIN].astype(BF16), ssm_small=ssm_small.astype(BF16),
        ssm_conv_w=ssm_conv_w[0], ssm_conv_b=ssm_conv_b[0],
        ssm_alog_row=_lane_row([(SSM_HEADS, ssm_a_log[0][perm_s])]),
        ssm_dtb_row=_lane_row([(0, ssm_dt_bias[0][perm_s]), (SSM_HEADS, ssm_dt_bias[0][perm_s])]),
        ssm_dskip_row=jnp.repeat(ssm_d[0], SSM_HEAD_DIM).reshape(1, SSM_D_INNER),
        ssm_norm=ssm_norm[0], ssm_w_out=ssm_w_out[0].astype(BF16),
        ssm_consts=_recurrence_consts(SSM_PAIRS, SSM_HEADS, SSM_HEADS + SSM_PAIRS),
        wg=[ffn_w_gate_up[i][:, :FFN_HIDDEN].astype(BF16) for i in range(2)],
        wu=[ffn_w_gate_up[i][:, FFN_HIDDEN:].astype(BF16) for i in range(2)],
        wd=[ffn_w_down[i].astype(BF16) for i in range(2)],
    )


def _ffn(x, layer, mod3, rows_up, rows_down, p, final_w):
    act = _ffn_up(x, p["norm_ffn"][layer], mod3, rows_up, p["wg"][layer], p["wu"][layer], FFN_TILE)
    return _ffn_down(act, x, mod3, rows_down, p["wd"][layer], final_w)


QSCALE = GDN_HEAD_DIM ** -0.5


def _trunk_seq(x3, c, p):
    nseq, seq_len, _ = x3.shape
    m = nseq * seq_len
    x = x3.reshape(m, D_MODEL)
    mod = _modulation(c, p["w_mod"], p["b_mod"])
    mod3 = [mod[l].reshape(nseq, 1, 6 * D_MODEL) for l in range(2)]
    rows_a = _Rows(m, min(1024, seq_len), seq_len, 1)
    rows_b = _Rows(m, min(512, seq_len), seq_len, 1)

    pm, ps = _in_proj(x, p["norm_mix"][0], mod3[0], rows_a, 1, 0, p["gdn_main"], p["gdn_small"], PROJ_TILE)
    act = _conv_rows(pm, 0, GDN_CONV_DIM, p["gdn_conv_w"], p["gdn_conv_b"], nseq, seq_len, 2, QSCALE)
    o, gdn_s = _gdn_seq(act, ps, p["gdn_alog_row"], p["gdn_dtb_row"], p["gdn_consts"], nseq, seq_len)
    gdn_c = pm.reshape(nseq, seq_len, GDN_MAIN)[:, seq_len - (CONV_WIDTH - 1):, :GDN_CONV_DIM]
    x = _mixer_out(o, pm, 2, p["gdn_norm"], x, mod3[0], rows_b, p["gdn_w_out"], GDN_HEAD_DIM, False)
    x = _ffn(x, 0, mod3[0], rows_a, rows_b, p, None)

    pm, ps = _in_proj(x, p["norm_mix"][1], mod3[1], rows_a, 1, 0, p["ssm_main"], p["ssm_small"], PROJ_TILE)
    act = _conv_rows(pm, SSM_D_INNER, SSM_CONV_DIM, p["ssm_conv_w"], p["ssm_conv_b"], nseq, seq_len, 0, 1.0)
    y, ssm_h = _ssd_seq(act, ps, p["ssm_alog_row"], p["ssm_dtb_row"], p["ssm_dskip_row"], p["ssm_consts"],
                        nseq, seq_len)
    ssm_c = pm.reshape(nseq, seq_len, SSM_MAIN)[:, seq_len - (CONV_WIDTH - 1):, SSM_D_INNER:]
    x = _mixer_out(y, pm, 0, p["ssm_norm"], x, mod3[1], rows_b, p["ssm_w_out"], SSM_D_INNER // SSM_GROUPS, True)
    _, y_out = _ffn(x, 1, mod3[1], rows_a, rows_b, p, p["norm_final"])

    return (y_out.reshape(nseq, seq_len, D_MODEL), gdn_s[None], gdn_c[None],
            ssm_h.reshape(nseq, SSM_HEADS, SSM_HEAD_DIM, SSM_STATE)[None], ssm_c[None])


def _trunk_step(x3, c, st_gdn, cv_gdn, st_ssm, cv_ssm, p):
    nb, steps, _ = x3.shape
    assert steps >= CONV_WIDTH - 1
    m = nb * steps
    x = jnp.transpose(x3, (1, 0, 2)).reshape(m, D_MODEL)
    mod = _modulation(c, p["w_mod"], p["b_mod"])
    mod3 = [mod[l].reshape(1, nb, 6 * D_MODEL) for l in range(2)]
    rows = _Rows(m, m, None, nb)
    tok = lambda a: jnp.transpose(a, (1, 0, 2))

    pm, ps = _in_proj(x, p["norm_mix"][0], mod3[0], rows, 1, 0, p["gdn_main"], p["gdn_small"], PROJ_TILE)
    u3 = pm.reshape(steps, nb, GDN_MAIN)
    act3 = _conv_steps(u3, 0, GDN_CONV_DIM, tok(cv_gdn[0]), p["gdn_conv_w"], p["gdn_conv_b"], 2, QSCALE)
    o, gdn_s = _gdn_step(act3, ps.reshape(steps, nb, LANES), p["gdn_alog_row"], p["gdn_dtb_row"], p["gdn_consts"],
                         st_gdn[0])
    gdn_c = tok(u3[steps - (CONV_WIDTH - 1):, :, :GDN_CONV_DIM])
    x = _mixer_out(o, pm, 2, p["gdn_norm"], x, mod3[0], rows, p["gdn_w_out"], GDN_HEAD_DIM, False)
    x = _ffn(x, 0, mod3[0], rows, rows, p, None)

    pm, ps = _in_proj(x, p["norm_mix"][1], mod3[1], rows, 1, 0, p["ssm_main"], p["ssm_small"], PROJ_TILE)
    u3 = pm.reshape(steps, nb, SSM_MAIN)
    act3 = _conv_steps(u3, SSM_D_INNER, SSM_CONV_DIM, tok(cv_ssm[0]), p["ssm_conv_w"], p["ssm_conv_b"], 0, 1.0)
    h0 = st_ssm[0].reshape(nb, SSM_PAIRS, LANES, SSM_STATE)
    y, ssm_h = _ssd_step(act3, ps.reshape(steps, nb, LANES), p["ssm_alog_row"], p["ssm_dtb_row"],
                         p["ssm_dskip_row"], p["ssm_consts"], h0)
    ssm_c = tok(u3[steps - (CONV_WIDTH - 1):, :, SSM_D_INNER:])
    x = _mixer_out(y, pm, 0, p["ssm_norm"], x, mod3[1], rows, p["ssm_w_out"], SSM_D_INNER // SSM_GROUPS, True)
    _, y_out = _ffn(x, 1, mod3[1], rows, rows, p, p["norm_final"])

    return (tok(y_out.reshape(steps, nb, D_MODEL)), gdn_s[None], gdn_c[None],
            ssm_h.reshape(nb, SSM_HEADS, SSM_HEAD_DIM, SSM_STATE)[None], ssm_c[None])


def kernel(x_prompt, x_sample, c_prompt, c_sample, state_gdn, state_gdn_conv, state_ssm, state_ssm_conv, w_mod, b_mod,
           norm_mix, norm_ffn, norm_final, gdn_w_in, gdn_conv_w, gdn_a_log, gdn_dt_bias, gdn_norm, gdn_w_out, ssm_w_in,
           ssm_conv_w, ssm_conv_b, ssm_a_log, ssm_dt_bias, ssm_d, ssm_norm, ssm_w_out, ffn_w_gate_up, ffn_w_down):
    p = _stage_params(w_mod, b_mod, norm_mix, norm_ffn, norm_final, gdn_w_in, gdn_conv_w, gdn_a_log, gdn_dt_bias,
                      gdn_norm, gdn_w_out, ssm_w_in, ssm_conv_w, ssm_conv_b, ssm_a_log, ssm_dt_bias, ssm_d, ssm_norm,
                      ssm_w_out, ffn_w_gate_up, ffn_w_down)
    y_p, gs_p, gc_p, ss_p, sc_p = _trunk_seq(x_prompt, c_prompt, p)
    y_s, gs_s, gc_s, ss_s, sc_s = _trunk_step(x_sample, c_sample, state_gdn, state_gdn_conv, state_ssm,
                                              state_ssm_conv, p)
    return (y_p, y_s, gs_p, gc_p, ss_p, sc_p, gs_s, gc_s, ss_s, sc_s)
```

```python
import functools

import numpy as np
import jax
import jax.numpy as jnp
from jax import lax
from jax.experimental import pallas as pl
from jax.experimental.pallas import tpu as pltpu

F32 = jnp.float32
BF16 = jnp.bfloat16

D_MODEL = 1024
EPS = 1e-6
CONV_WIDTH = 4
CHUNK = 64
LANES = 128
HALF = LANES // 2

GDN_QK_HEADS = 8
GDN_V_HEADS = 16
GDN_HEAD_DIM = 128
GDN_QK_DIM = GDN_QK_HEADS * GDN_HEAD_DIM
GDN_V_DIM = GDN_V_HEADS * GDN_HEAD_DIM
GDN_CONV_DIM = 2 * GDN_QK_DIM + GDN_V_DIM
GDN_MAIN = GDN_CONV_DIM + GDN_V_DIM

SSM_D_INNER = 2 * D_MODEL
SSM_HEAD_DIM = 64
SSM_HEADS = SSM_D_INNER // SSM_HEAD_DIM
SSM_GROUPS = 4
SSM_STATE = 128
SSM_BC = SSM_GROUPS * SSM_STATE
SSM_CONV_DIM = SSM_D_INNER + 2 * SSM_BC
SSM_MAIN = SSM_D_INNER + SSM_CONV_DIM

FFN_HIDDEN = 2816

VMEM_LIMIT = 56 * 1024 * 1024
NEG_BIG = -1e30


def _cparams(*sem):
    return pltpu.CompilerParams(dimension_semantics=sem, vmem_limit_bytes=VMEM_LIMIT)


def _silu(x):
    return x * jax.nn.sigmoid(x)


def _softplus(x):
    return jnp.maximum(x, 0.0) + jnp.log1p(jnp.exp(-jnp.abs(x)))


def _dot(a, b):
    return jnp.dot(a, b, preferred_element_type=F32)


def _dot_nt(a, b):
    return lax.dot_general(a, b, (((1,), (1,)), ((), ())), preferred_element_type=F32)


def _tile_rows(v, rep):
    return v if rep == 1 else jnp.concatenate([v] * rep, axis=0)


def _mod_kernel(c_ref, w_ref, b_ref, o_ref):
    cs = _silu(c_ref[...]).astype(BF16)
    o_ref[...] = _dot(cs, w_ref[...]) + b_ref[...]


def _modulation(c, w_mod_bf, b_mod):
    depth, _, n = w_mod_bf.shape
    bc = c.shape[0]
    tn = 1536
    return pl.pallas_call(
        _mod_kernel,
        grid=(depth, n // tn),
        in_specs=[
            pl.BlockSpec((bc, D_MODEL), lambda l, j: (0, 0)),
            pl.BlockSpec((None, D_MODEL, tn), lambda l, j: (l, 0, j)),
            pl.BlockSpec((None, 1, tn), lambda l, j: (l, 0, j)),
        ],
        out_specs=pl.BlockSpec((None, bc, tn), lambda l, j: (l, 0, j)),
        out_shape=jax.ShapeDtypeStruct((depth, bc, n), F32),
        compiler_params=_cparams("parallel", "parallel"),
        name="adaln_mod",
    )(c, w_mod_bf, b_mod.reshape(depth, 1, n))


def _norm_mod(x, nw, sc, sh, rep):
    y = x * lax.rsqrt(jnp.mean(x * x, axis=-1, keepdims=True) + EPS) * nw
    return y * (1.0 + _tile_rows(sc, rep)) + _tile_rows(sh, rep)


def _in_proj_kernel(x_ref, nw_ref, sc_ref, sh_ref, w_ref, w2_ref, o_ref, o2_ref, h_ref, *, rep):
    @pl.when(pl.program_id(1) == 0)
    def _():
        h = _norm_mod(x_ref[...], nw_ref[...], sc_ref[...], sh_ref[...], rep).astype(BF16)
        h_ref[...] = h
        o2_ref[...] = _dot(h, w2_ref[...])

    o_ref[...] = _dot(h_ref[...], w_ref[...])


def _ffn_up_kernel(x_ref, nw_ref, sc_ref, sh_ref, wg_ref, wu_ref, o_ref, h_ref, *, rep):
    @pl.when(pl.program_id(1) == 0)
    def _():
        h_ref[...] = _norm_mod(x_ref[...], nw_ref[...], sc_ref[...], sh_ref[...], rep).astype(BF16)

    h = h_ref[...]
    o_ref[...] = (_silu(_dot(h, wg_ref[...])) * _dot(h, wu_ref[...])).astype(BF16)


class _Rows:
    def __init__(self, m, tm, group_rows, mod_rows):
        assert m % tm == 0
        self.m, self.tm = m, tm
        if mod_rows == 1:
            assert group_rows % tm == 0
            self.rep = 1
            self.gmap = lambda i: (i * tm) // group_rows
        else:
            assert tm % mod_rows == 0
            self.rep = tm // mod_rows
            self.gmap = lambda i: 0
        self.mod_rows = mod_rows

    def mod_spec(self, col_block, with_j):
        if with_j:
            return pl.BlockSpec((None, self.mod_rows, D_MODEL), lambda i, j: (self.gmap(i), 0, col_block))
        return pl.BlockSpec((None, self.mod_rows, D_MODEL), lambda i: (self.gmap(i), 0, col_block))


def _in_proj(x, nw, mod3, rows, sc_blk, sh_blk, w, w2, tn):
    m, tm = rows.m, rows.tm
    n = w.shape[1]
    assert n % tn == 0
    return pl.pallas_call(
        functools.partial(_in_proj_kernel, rep=rows.rep),
        grid=(m // tm, n // tn),
        in_specs=[
            pl.BlockSpec((tm, D_MODEL), lambda i, j: (i, 0)),
            pl.BlockSpec((1, D_MODEL), lambda i, j: (0, 0)),
            rows.mod_spec(sc_blk, True),
            rows.mod_spec(sh_blk, True),
            pl.BlockSpec((D_MODEL, tn), lambda i, j: (0, j)),
            pl.BlockSpec((D_MODEL, LANES), lambda i, j: (0, 0)),
        ],
        out_specs=[
            pl.BlockSpec((tm, tn), lambda i, j: (i, j)),
            pl.BlockSpec((tm, LANES), lambda i, j: (i, 0)),
        ],
        out_shape=[jax.ShapeDtypeStruct((m, n), F32), jax.ShapeDtypeStruct((m, LANES), F32)],
        scratch_shapes=[pltpu.VMEM((tm, D_MODEL), BF16)],
        compiler_params=_cparams("parallel", "arbitrary"),
        name="in_proj",
    )(x, nw.reshape(1, D_MODEL), mod3, mod3, w, w2)


def _ffn_up(x, nw, mod3, rows, wg, wu, th):
    m, tm = rows.m, rows.tm
    assert FFN_HIDDEN % th == 0
    return pl.pallas_call(
        functools.partial(_ffn_up_kernel, rep=rows.rep),
        grid=(m // tm, FFN_HIDDEN // th),
        in_specs=[
            pl.BlockSpec((tm, D_MODEL), lambda i, j: (i, 0)),
            pl.BlockSpec((1, D_MODEL), lambda i, j: (0, 0)),
            rows.mod_spec(4, True),
            rows.mod_spec(3, True),
            pl.BlockSpec((D_MODEL, th), lambda i, j: (0, j)),
            pl.BlockSpec((D_MODEL, th), lambda i, j: (0, j)),
        ],
        out_specs=pl.BlockSpec((tm, th), lambda i, j: (i, j)),
        out_shape=jax.ShapeDtypeStruct((m, FFN_HIDDEN), BF16),
        scratch_shapes=[pltpu.VMEM((tm, D_MODEL), BF16)],
        compiler_params=_cparams("parallel", "arbitrary"),
        name="ffn_up",
    )(x, nw.reshape(1, D_MODEL), mod3, mod3, wg, wu)


def _resid_store(acc, x_ref, gt_ref, o_ref, fnw_ref, y_ref, rep):
    xn = x_ref[...] + _tile_rows(gt_ref[...], rep) * acc
    o_ref[...] = xn
    if y_ref is not None:
        y_ref[...] = xn * lax.rsqrt(jnp.mean(xn * xn, axis=-1, keepdims=True) + EPS) * fnw_ref[...]


def _ffn_down_kernel(a_ref, x_ref, gt_ref, w_ref, *rest, rep, final):
    if final:
        fnw_ref, o_ref, y_ref = rest
    else:
        (o_ref,), fnw_ref, y_ref = rest, None, None
    _resid_store(_dot(a_ref[...], w_ref[...]), x_ref, gt_ref, o_ref, fnw_ref, y_ref, rep)


def _ffn_down(act, x, mod3, rows, w, fnw):
    m, tm = rows.m, rows.tm
    final = fnw is not None
    in_specs = [
        pl.BlockSpec((tm, FFN_HIDDEN), lambda i: (i, 0)),
        pl.BlockSpec((tm, D_MODEL), lambda i: (i, 0)),
        rows.mod_spec(5, False),
        pl.BlockSpec((FFN_HIDDEN, D_MODEL), lambda i: (0, 0)),
    ]
    args = [act, x, mod3, w]
    row_spec = pl.BlockSpec((tm, D_MODEL), lambda i: (i, 0))
    out_shape = jax.ShapeDtypeStruct((m, D_MODEL), F32)
    if final:
        in_specs.append(pl.BlockSpec((1, D_MODEL), lambda i: (0, 0)))
        args.append(fnw.reshape(1, D_MODEL))
        out_specs, out_shapes = [row_spec, row_spec], [out_shape, out_shape]
    else:
        out_specs, out_shapes = row_spec, out_shape
    return pl.pallas_call(
        functools.partial(_ffn_down_kernel, rep=rows.rep, final=final),
        grid=(m // tm,),
        in_specs=in_specs,
        out_specs=out_specs,
        out_shape=out_shapes,
        compiler_params=_cparams("parallel"),
        name="ffn_down",
    )(*args)


def _mixer_out_kernel(y_ref, z_ref, nw_ref, x_ref, gt_ref, w_ref, o_ref, a_ref, *, rep, group, gate_first):
    width = y_ref.shape[1]
    for s in range(0, width, group):
        y = y_ref[:, s:s + group]
        gate = _silu(z_ref[:, s:s + group])
        if gate_first:
            y = y * gate
        y = y * lax.rsqrt(jnp.mean(y * y, axis=-1, keepdims=True) + EPS) * nw_ref[:, s:s + group]
        if not gate_first:
            y = y * gate
        a_ref[:, s:s + group] = y.astype(BF16)
    _resid_store(_dot(a_ref[...], w_ref[...]), x_ref, gt_ref, o_ref, None, None, rep)


def _mixer_out(y, zsrc, z_blk, nw_full, x, mod3, rows, w, group, gate_first):
    m, tm = rows.m, rows.tm
    width = y.shape[1]
    return pl.pallas_call(
        functools.partial(_mixer_out_kernel, rep=rows.rep, group=group, gate_first=gate_first),
        grid=(m // tm,),
        in_specs=[
            pl.BlockSpec((tm, width), lambda i: (i, 0)),
            pl.BlockSpec((tm, width), lambda i: (i, z_blk)),
            pl.BlockSpec((1, width), lambda i: (0, 0)),
            pl.BlockSpec((tm, D_MODEL), lambda i: (i, 0)),
            rows.mod_spec(2, False),
            pl.BlockSpec((width, D_MODEL), lambda i: (0, 0)),
        ],
        out_specs=pl.BlockSpec((tm, D_MODEL), lambda i: (i, 0)),
        out_shape=jax.ShapeDtypeStruct((m, D_MODEL), F32),
        scratch_shapes=[pltpu.VMEM((tm, width), BF16)],
        compiler_params=_cparams("parallel"),
        name="mixer_out",
    )(y, zsrc, nw_full.reshape(1, width), x, mod3, w)


CONV_COLS = 1024
HALO = 8


def _post_conv(acc, o_ref, cb, n_l2, qscale):
    y = _silu(acc)
    if n_l2 == 0:
        o_ref[...] = y
        return

    @pl.when(cb < n_l2)
    def _():
        scale = jnp.where(cb == 0, qscale, 1.0).astype(F32)
        for s in range(0, CONV_COLS, GDN_HEAD_DIM):
            yh = y[:, s:s + GDN_HEAD_DIM]
            o_ref[:, s:s + GDN_HEAD_DIM] = yh * lax.rsqrt(jnp.sum(yh * yh, axis=-1, keepdims=True) + EPS) * scale

    @pl.when(cb >= n_l2)
    def _():
        o_ref[...] = y


def _conv_rows_kernel(u_ref, w_ref, b_ref, o_ref, ext_ref, *, tm, n_l2, qscale):
    cb, t = pl.program_id(1), pl.program_id(2)

    @pl.when(t == 0)
    def _():
        ext_ref[0:HALO, :] = jnp.zeros((HALO, CONV_COLS), F32)

    @pl.when(t > 0)
    def _():
        ext_ref[0:HALO, :] = ext_ref[tm:tm + HALO, :]

    u = u_ref[...]
    ext_ref[HALO:HALO + tm, :] = u
    acc = b_ref[...] + w_ref[3:4, :] * u
    for tap in range(CONV_WIDTH - 1):
        off = HALO - (CONV_WIDTH - 1) + tap
        acc = acc + w_ref[tap:tap + 1, :] * ext_ref[off:off + tm, :]
    _post_conv(acc, o_ref, cb, n_l2, qscale)


def _conv_rows(u, col_off, n_cols, conv_w, conv_b, nseq, seq_len, n_l2, qscale):
    tm = min(256, seq_len)
    assert seq_len % tm == 0 and n_cols % CONV_COLS == 0 and col_off % CONV_COLS == 0
    tiles = seq_len // tm
    cb0 = col_off // CONV_COLS
    return pl.pallas_call(
        functools.partial(_conv_rows_kernel, tm=tm, n_l2=n_l2, qscale=qscale),
        grid=(nseq, n_cols // CONV_COLS, tiles),
        in_specs=[
            pl.BlockSpec((tm, CONV_COLS), lambda b, c, t: (b * tiles + t, cb0 + c)),
            pl.BlockSpec((CONV_WIDTH, CONV_COLS), lambda b, c, t: (0, c)),
            pl.BlockSpec((1, CONV_COLS), lambda b, c, t: (0, c)),
        ],
        out_specs=pl.BlockSpec((tm, CONV_COLS), lambda b, c, t: (b * tiles + t, c)),
        out_shape=jax.ShapeDtypeStruct((nseq * seq_len, n_cols), F32),
        scratch_shapes=[pltpu.VMEM((tm + HALO, CONV_COLS), F32)],
        compiler_params=_cparams("parallel", "parallel", "arbitrary"),
        name="conv_rows",
    )(u, conv_w, conv_b.reshape(1, n_cols))


def _conv_steps_kernel(u_ref, hist_ref, w_ref, b_ref, o_ref, *, steps, n_l2, qscale):
    cb = pl.program_id(0)
    ext = [hist_ref[i] for i in range(CONV_WIDTH - 1)] + [u_ref[i] for i in range(steps)]
    for t in range(steps):
        acc = b_ref[...] + w_ref[0:1, :] * ext[t]
        for tap in range(1, CONV_WIDTH):
            acc = acc + w_ref[tap:tap + 1, :] * ext[t + tap]
        _post_conv(acc, o_ref.at[t], cb, n_l2, qscale)


def _conv_steps(u3, col_off, n_cols, hist3, conv_w, conv_b, n_l2, qscale):
    steps, nb, _ = u3.shape
    cb0 = col_off // CONV_COLS
    return pl.pallas_call(
        functools.partial(_conv_steps_kernel, steps=steps, n_l2=n_l2, qscale=qscale),
        grid=(n_cols // CONV_COLS,),
        in_specs=[
            pl.BlockSpec((steps, nb, CONV_COLS), lambda c: (0, 0, cb0 + c)),
            pl.BlockSpec((CONV_WIDTH - 1, nb, CONV_COLS), lambda c: (0, 0, c)),
            pl.BlockSpec((CONV_WIDTH, CONV_COLS), lambda c: (0, c)),
            pl.BlockSpec((1, CONV_COLS), lambda c: (0, c)),
        ],
        out_specs=pl.BlockSpec((steps, nb, CONV_COLS), lambda c: (0, 0, c)),
        out_shape=jax.ShapeDtypeStruct((steps, nb, n_cols), F32),
        compiler_params=_cparams("parallel"),
        name="conv_steps",
    )(u3, hist3, conv_w, conv_b.reshape(1, n_cols))


def _split3(x):
    hi = x.astype(BF16)
    r = x - hi.astype(F32)
    mid = r.astype(BF16)
    lo = (r - mid.astype(F32)).astype(BF16)
    return hi, mid, lo


def _cat3(x, axis):
    return jnp.concatenate(_split3(x), axis=axis)


def _pad_t(tile):
    return jnp.concatenate([tile, jnp.zeros_like(tile)], axis=0).T


def _pair_rows(t, n):
    return t[0:n] + pltpu.roll(t[n:2 * n], HALF, 1)


def _iotas(shape):
    return lax.broadcasted_iota(jnp.int32, shape, 0), lax.broadcasted_iota(jnp.int32, shape, 1)


def _block_diag(pair, bdmask):
    return jnp.where(bdmask, jnp.concatenate([pair, pair], axis=0), 0.0)


def _mm_pair(lhs, rhs):
    lh = lhs.astype(BF16)
    ll = (lhs - lh.astype(F32)).astype(BF16)
    rh = rhs.astype(BF16)
    rl = (rhs - rh.astype(F32)).astype(BF16)
    return _dot(jnp.concatenate([lh, lh, ll], axis=1), jnp.concatenate([rh, rl, rh], axis=0))


def _mm_hl(lhs, rhs_bf):
    lh = lhs.astype(BF16)
    ll = (lhs - lh.astype(F32)).astype(BF16)
    return _dot(jnp.concatenate([lh, ll], axis=1), jnp.concatenate([rhs_bf, rhs_bf], axis=0))


def _unit_lower_inverse(xs, levels, bdmask, eye2):
    ps = [eye2 + x for x in xs]
    if levels <= 1:
        return ps
    ys = [_mm_pair(x, _block_diag(x, bdmask)) for x in xs]
    for _ in range(levels - 2):
        rs = [_mm_pair(jnp.concatenate([y, p], axis=0), _block_diag(y, bdmask)) for y, p in zip(ys, ps)]
        ys = [r[0:CHUNK] for r in rs]
        ps = [p + r[CHUNK:2 * CHUNK] for p, r in zip(ps, rs)]
    return [p + _mm_pair(p, _block_diag(y, bdmask)) for p, y in zip(ps, ys)]


def _two_blocks(a_bf, b_bf):
    z = jnp.zeros_like(a_bf)
    return jnp.concatenate([jnp.concatenate([a_bf, z], axis=1), jnp.concatenate([z, b_bf], axis=1)], axis=0)


def _recurrence_consts(n_pairs, chan_even0, chan_odd0):
    tril = np.tril(np.ones((CHUNK, CHUNK), np.float32))
    tril3 = np.concatenate([tril] * 3, axis=1)
    upper = np.zeros((LANES, 2 * LANES), np.float32)
    upper[:CHUNK, :CHUNK] = tril.T
    upper[:CHUNK, LANES:] = 1.0
    upper3 = np.concatenate([upper] * 3, axis=0)
    e = np.zeros((LANES, n_pairs * LANES), np.float32)
    for p in range(n_pairs):
        e[chan_even0 + p, p * LANES:p * LANES + HALF] = 1.0
        e[chan_odd0 + p, p * LANES + HALF:(p + 1) * LANES] = 1.0
    e3 = np.concatenate([e] * 3, axis=0)
    return jnp.asarray(tril3, BF16), jnp.asarray(upper3, BF16), jnp.asarray(e3, BF16)


def _levels(lr):
    return max(1, int(np.ceil(np.log2(lr))))


GDN_PAIRS = GDN_V_HEADS // 2


def _gdn_gates(raw, alog_row, dtb_row, lr):
    row, lane = _iotas((CHUNK, LANES))
    g = -jnp.exp(alog_row) * _softplus(raw + dtb_row)
    beta = jax.nn.sigmoid(raw)
    gt = jnp.where(lane < GDN_V_HEADS, g, jnp.where(lane < 2 * GDN_V_HEADS, beta, 0.0))
    if lr < CHUNK:
        gt = jnp.where(row < lr, gt, 0.0)
    return gt


def _gdn_block(act_ref, gt, s_in, s_out, o_ref, tril3_ref, upper3_ref, e3_ref, lr):
    np_ = GDN_PAIRS
    row, lane = _iotas((CHUNK, LANES))
    jl = jnp.bitwise_and(lane, HALF - 1)
    left = lane < HALF
    causal = row >= jl
    strict = row > jl
    eye2 = jnp.where(row == jl, 1.0, 0.0)
    r2, l2 = _iotas((LANES, LANES))
    bdmask = jnp.right_shift(r2, 6) == jnp.right_shift(l2, 6)
    _, lane_p = _iotas((np_, LANES))

    t1 = _pad_t(gt)
    cum_ext = _dot(_cat3(t1[0:2 * np_], 1), upper3_ref[...])
    cum_t, last_b = cum_ext[:, :LANES], cum_ext[:, LANES:]
    cum_rp = _pair_rows(cum_t, np_)
    beta_rp = _pair_rows(t1[2 * np_:4 * np_], np_)
    last_rp = jnp.where(lane_p < HALF, last_b[0:np_], last_b[np_:2 * np_])
    ecum_rp = jnp.exp(cum_rp)
    kdec_rp = jnp.exp(last_rp - cum_rp) * beta_rp
    elast = jnp.exp(last_b)
    cum = _dot(tril3_ref[...], _cat3(gt, 0))
    col_all = _dot(_cat3(cum, 1), e3_ref[...])

    pairs = range(np_)
    sl = lambda a, i: a[:, i * LANES:(i + 1) * LANES]
    q = [sl(act_ref, p) for p in pairs]
    k = [sl(act_ref, GDN_QK_HEADS + p) for p in pairs]
    v = [sl(act_ref, 2 * GDN_QK_HEADS + h) for h in range(2 * np_)]
    kb = [a.astype(BF16) for a in k]
    qb = [a.astype(BF16) for a in q]
    gq = [_dot_nt(jnp.concatenate([kb[p], qb[p]], axis=0), jnp.concatenate([kb[p], kb[p]], axis=0)) for p in pairs]
    colc = [sl(col_all, p) for p in pairs]
    base = [jnp.exp(jnp.where(causal, colc[p] - cum_rp[p:p + 1], NEG_BIG)) * beta_rp[p:p + 1] for p in pairs]
    x = [jnp.where(strict, -(gq[p][0:CHUNK] * base[p]), 0.0) for p in pairs]
    qkd = [(gq[p][CHUNK:2 * CHUNK] * base[p]).astype(BF16) for p in pairs]
    minv = _unit_lower_inverse(x, _levels(lr), bdmask, eye2)
    u = [jnp.concatenate([v[2 * p], v[2 * p + 1]], axis=1)
         + _mm_hl(minv[p] - eye2, _two_blocks(v[2 * p].astype(BF16), v[2 * p + 1].astype(BF16))) for p in pairs]
    w = [_mm_hl(minv[p] * ecum_rp[p:p + 1], _two_blocks(kb[p], kb[p])) for p in pairs]
    s_old = [s_in[h] for h in range(2 * np_)]
    r = [_dot(jnp.concatenate([sl(w[h // 2], h % 2).astype(BF16), qb[h // 2]], axis=0), s_old[h].astype(BF16))
         for h in range(2 * np_)]
    delta = [(sl(u[h // 2], h % 2) - r[h][0:CHUNK]).astype(BF16) for h in range(2 * np_)]
    bd_delta = [_two_blocks(delta[2 * p], delta[2 * p + 1]) for p in pairs]
    o_intra = [_dot(qkd[p], bd_delta[p]) for p in pairs]
    ds = [_dot((jnp.concatenate([k[p], k[p]], axis=0).T * kdec_rp[p:p + 1]).astype(BF16), bd_delta[p]) for p in pairs]
    for p in pairs:
        ecol = jnp.exp(colc[p])
        ecol_r = pltpu.roll(ecol, HALF, 1)
        efull = (jnp.where(left, ecol, ecol_r), jnp.where(left, ecol_r, ecol))
        for hh in range(2):
            h = 2 * p + hh
            o_ref[:, h * LANES:(h + 1) * LANES] = efull[hh] * r[h][CHUNK:2 * CHUNK] + sl(o_intra[p], hh)
            e_h = elast[hh * np_ + p:hh * np_ + p + 1]
            s_out[h] = e_h * s_old[h] + sl(ds[p], hh)


def _gdn_seq_kernel(act_ref, graw_ref, alog_ref, dtb_ref, tril3_ref, upper3_ref, e3_ref, o_ref, s_ref):
    @pl.when(pl.program_id(1) == 0)
    def _():
        s_ref[...] = jnp.zeros(s_ref.shape, F32)

    gt = _gdn_gates(graw_ref[...], alog_ref[...], dtb_ref[...], CHUNK)
    _gdn_block(act_ref, gt, s_ref, s_ref, o_ref, tril3_ref, upper3_ref, e3_ref, CHUNK)


def _const_specs(consts, ngrid):
    zero = (lambda *_: (0, 0))
    return [pl.BlockSpec(c.shape, zero) for c in consts]


def _gdn_seq(act, graw, alog_row, dtb_row, consts, nseq, seq_len):
    nc = seq_len // CHUNK
    rowspec = pl.BlockSpec((1, LANES), lambda b, n: (0, 0))
    return pl.pallas_call(
        _gdn_seq_kernel,
        grid=(nseq, nc),
        in_specs=[
            pl.BlockSpec((CHUNK, GDN_CONV_DIM), lambda b, n: (b * nc + n, 0)),
            pl.BlockSpec((CHUNK, LANES), lambda b, n: (b * nc + n, 0)),
            rowspec, rowspec,
        ] + _const_specs(consts, 2),
        out_specs=[
            pl.BlockSpec((CHUNK, GDN_V_DIM), lambda b, n: (b * nc + n, 0)),
            pl.BlockSpec((None, GDN_V_HEADS, GDN_HEAD_DIM, GDN_HEAD_DIM), lambda b, n: (b, 0, 0, 0)),
        ],
        out_shape=[
            jax.ShapeDtypeStruct((nseq * seq_len, GDN_V_DIM), F32),
            jax.ShapeDtypeStruct((nseq, GDN_V_HEADS, GDN_HEAD_DIM, GDN_HEAD_DIM), F32),
        ],
        compiler_params=_cparams("parallel", "arbitrary"),
        name="gdn_seq",
    )(act, graw, alog_row, dtb_row, *consts)


def _load_padded(src_ref, pad_ref, lr):
    pad_ref[...] = jnp.zeros(pad_ref.shape, F32)
    for t in range(lr):
        pad_ref[t:t + 1, :] = src_ref[t]


def _gdn_step_kernel(act_ref, graw_ref, alog_ref, dtb_ref, tril3_ref, upper3_ref, e3_ref, s0_ref,
                     o_ref, s_ref, apad_ref, gpad_ref, opad_ref, *, lr):
    _load_padded(act_ref, apad_ref, lr)
    _load_padded(graw_ref, gpad_ref, lr)
    gt = _gdn_gates(gpad_ref[...], alog_ref[...], dtb_ref[...], lr)
    _gdn_block(apad_ref, gt, s0_ref, s_ref, opad_ref, tril3_ref, upper3_ref, e3_ref, lr)
    for t in range(lr):
        o_ref[t] = opad_ref[t:t + 1, :]


def _step_spec(steps, width):
    return pl.BlockSpec((steps, None, 1, width), lambda b: (0, b, 0, 0))


def _gdn_step(act3, graw3, alog_row, dtb_row, consts, s0):
    steps, nb, _ = act3.shape
    rowspec = pl.BlockSpec((1, LANES), lambda b: (0, 0))
    sspec = pl.BlockSpec((None, GDN_V_HEADS, GDN_HEAD_DIM, GDN_HEAD_DIM), lambda b: (b, 0, 0, 0))
    o, s = pl.pallas_call(
        functools.partial(_gdn_step_kernel, lr=steps),
        grid=(nb,),
        in_specs=[_step_spec(steps, GDN_CONV_DIM), _step_spec(steps, LANES), rowspec, rowspec]
        + _const_specs(consts, 1) + [sspec],
        out_specs=[_step_spec(steps, GDN_V_DIM), sspec],
        out_shape=[
            jax.ShapeDtypeStruct((steps, nb, 1, GDN_V_DIM), F32),
            jax.ShapeDtypeStruct(s0.shape, F32),
        ],
        scratch_shapes=[
            pltpu.VMEM((CHUNK, GDN_CONV_DIM), F32),
            pltpu.VMEM((CHUNK, LANES), F32),
            pltpu.VMEM((CHUNK, GDN_V_DIM), F32),
        ],
        compiler_params=_cparams("parallel"),
        name="gdn_step",
    )(act3.reshape(steps, nb, 1, GDN_CONV_DIM), graw3.reshape(steps, nb, 1, LANES), alog_row, dtb_row, *consts, s0)
    return o.reshape(steps * nb, GDN_V_DIM), s


SSM_PAIRS = SSM_HEADS // 2
PAIRS_PER_GROUP = SSM_PAIRS // SSM_GROUPS


def _ssm_gates(raw, alog_row, dtb_row, lr):
    row, lane = _iotas((CHUNK, LANES))
    dt = _softplus(raw + dtb_row)
    tile = jnp.where(lane < SSM_HEADS, dt, jnp.where(lane < 2 * SSM_HEADS, -jnp.exp(alog_row) * dt, 0.0))
    if lr < CHUNK:
        tile = jnp.where(row < lr, tile, 0.0)
    return tile


def _ssd_block(act_ref, tile, h_in, h_out, y_ref, dskip_ref, tril3_ref, upper3_ref, e3_ref):
    np_ = SSM_PAIRS
    row, lane = _iotas((CHUNK, LANES))
    causal = row >= jnp.bitwise_and(lane, HALF - 1)
    r2, l2 = _iotas((LANES, LANES))
    bdmask = jnp.right_shift(r2, 6) == jnp.right_shift(l2, 6)
    top = r2 < HALF
    _, lane_p = _iotas((np_, LANES))

    t1 = _pad_t(tile)
    cum_ext = _dot(_cat3(t1[2 * np_:4 * np_], 1), upper3_ref[...])
    cum_t, last_b = cum_ext[:, :LANES], cum_ext[:, LANES:]
    cum_rp = _pair_rows(cum_t, np_)
    dt_rp = _pair_rows(t1[0:2 * np_], np_)
    last_rp = jnp.where(lane_p < HALF, last_b[0:np_], last_b[np_:2 * np_])
    coef_rp = jnp.exp(last_rp - cum_rp) * dt_rp
    elast = jnp.exp(last_b)
    cum = _dot(tril3_ref[...], _cat3(tile, 0))
    col_all = _dot(_cat3(cum, 1), e3_ref[...])

    for g in range(SSM_GROUPS):
        bg = act_ref[:, SSM_D_INNER + g * SSM_STATE:SSM_D_INNER + (g + 1) * SSM_STATE].astype(BF16)
        cg = act_ref[:, SSM_D_INNER + SSM_BC + g * SSM_STATE:SSM_D_INNER + SSM_BC + (g + 1) * SSM_STATE].astype(BF16)
        bb = jnp.concatenate([bg, bg], axis=0)
        cb2 = _dot_nt(cg, bb)
        for pp in range(PAIRS_PER_GROUP):
            p = g * PAIRS_PER_GROUP + pp
            xp = act_ref[:, p * LANES:(p + 1) * LANES]
            colc = col_all[:, p * LANES:(p + 1) * LANES]
            dec = jnp.exp(jnp.where(causal, colc - cum_rp[p:p + 1], NEG_BIG))
            lm = cb2 * dec * dt_rp[p:p + 1]
            x2 = jnp.concatenate([xp, xp], axis=0)
            y_diag = _dot(lm.astype(BF16), jnp.where(bdmask, x2, 0.0).astype(BF16))
            hp = h_in[p]
            y_off = jnp.exp(colc) * _dot_nt(cg, hp.astype(BF16))
            y_ref[:, p * LANES:(p + 1) * LANES] = y_diag + y_off + dskip_ref[:, p * LANES:(p + 1) * LANES] * xp
            lhs = jnp.where(bdmask, x2.T * coef_rp[p:p + 1], 0.0)
            dh = _dot(lhs.astype(BF16), bb)
            e_rows = jnp.where(top, elast[p:p + 1], elast[np_ + p:np_ + p + 1])
            h_out[p] = e_rows * hp + dh


def _ssd_seq_kernel(act_ref, graw_ref, alog_ref, dtb_ref, dskip_ref, tril3_ref, upper3_ref, e3_ref, y_ref, h_ref):
    @pl.when(pl.program_id(1) == 0)
    def _():
        h_ref[...] = jnp.zeros(h_ref.shape, F32)

    tile = _ssm_gates(graw_ref[...], alog_ref[...], dtb_ref[...], CHUNK)
    _ssd_block(act_ref, tile, h_ref, h_ref, y_ref, dskip_ref, tril3_ref, upper3_ref, e3_ref)


def _ssd_seq(act, graw, alog_row, dtb_row, dskip_row, consts, nseq, seq_len):
    nc = seq_len // CHUNK
    rowspec = pl.BlockSpec((1, LANES), lambda b, n: (0, 0))
    return pl.pallas_call(
        _ssd_seq_kernel,
        grid=(nseq, nc),
        in_specs=[
            pl.BlockSpec((CHUNK, SSM_CONV_DIM), lambda b, n: (b * nc + n, 0)),
            pl.BlockSpec((CHUNK, LANES), lambda b, n: (b * nc + n, 0)),
            rowspec, rowspec,
            pl.BlockSpec((1, SSM_D_INNER), lambda b, n: (0, 0)),
        ] + _const_specs(consts, 2),
        out_specs=[
            pl.BlockSpec((CHUNK, SSM_D_INNER), lambda b, n: (b * nc + n, 0)),
            pl.BlockSpec((None, SSM_PAIRS, LANES, SSM_STATE), lambda b, n: (b, 0, 0, 0)),
        ],
        out_shape=[
            jax.ShapeDtypeStruct((nseq * seq_len, SSM_D_INNER), F32),
            jax.ShapeDtypeStruct((nseq, SSM_PAIRS, LANES, SSM_STATE), F32),
        ],
        compiler_params=_cparams("parallel", "arbitrary"),
        name="ssd_seq",
    )(act, graw, alog_row, dtb_row, dskip_row, *consts)


def _ssd_step_kernel(act_ref, graw_ref, alog_ref, dtb_ref, dskip_ref, tril3_ref, upper3_ref, e3_ref, h0_ref,
                     y_ref, h_ref, apad_ref, gpad_ref, ypad_ref, *, lr):
    _load_padded(act_ref, apad_ref, lr)
    _load_padded(graw_ref, gpad_ref, lr)
    tile = _ssm_gates(gpad_ref[...], alog_ref[...], dtb_ref[...], lr)
    _ssd_block(apad_ref, tile, h0_ref, h_ref, ypad_ref, dskip_ref, tril3_ref, upper3_ref, e3_ref)
    for t in range(lr):
        y_ref[t] = ypad_ref[t:t + 1, :]


def _ssd_step(act3, graw3, alog_row, dtb_row, dskip_row, consts, h0):
    steps, nb, _ = act3.shape
    rowspec = pl.BlockSpec((1, LANES), lambda b: (0, 0))
    hspec = pl.BlockSpec((None, SSM_PAIRS, LANES, SSM_STATE), lambda b: (b, 0, 0, 0))
    y, h = pl.pallas_call(
        functools.partial(_ssd_step_kernel, lr=steps),
        grid=(nb,),
        in_specs=[_step_spec(steps, SSM_CONV_DIM), _step_spec(steps, LANES), rowspec, rowspec,
                  pl.BlockSpec((1, SSM_D_INNER), lambda b: (0, 0))] + _const_specs(consts, 1) + [hspec],
        out_specs=[_step_spec(steps, SSM_D_INNER), hspec],
        out_shape=[
            jax.ShapeDtypeStruct((steps, nb, 1, SSM_D_INNER), F32),
            jax.ShapeDtypeStruct(h0.shape, F32),
        ],
        scratch_shapes=[
            pltpu.VMEM((CHUNK, SSM_CONV_DIM), F32),
            pltpu.VMEM((CHUNK, LANES), F32),
            pltpu.VMEM((CHUNK, SSM_D_INNER), F32),
        ],
        compiler_params=_cparams("parallel"),
        name="ssd_step",
    )(act3.reshape(steps, nb, 1, SSM_CONV_DIM), graw3.reshape(steps, nb, 1, LANES), alog_row, dtb_row, dskip_row,
      *consts, h0)
    return y.reshape(steps * nb, SSM_D_INNER), h


FFN_TILE = FFN_HIDDEN // 2
PROJ_TILE = 1024


def _lane_row(pieces):
    row = jnp.zeros((1, LANES), F32)
    for off, vec in pieces:
        row = row.at[0, off:off + vec.shape[0]].set(vec.astype(F32))
    return row


def _stage_params(w_mod, b_mod, norm_mix, norm_ffn, norm_final, gdn_w_in, gdn_conv_w, gdn_a_log, gdn_dt_bias,
                  gdn_norm, gdn_w_out, ssm_w_in, ssm_conv_w, ssm_conv_b, ssm_a_log, ssm_dt_bias, ssm_d, ssm_norm,
                  ssm_w_out, ffn_w_gate_up, ffn_w_down):
    perm_g = np.concatenate([np.arange(0, GDN_V_HEADS, 2), np.arange(1, GDN_V_HEADS, 2)])
    perm_s = np.concatenate([np.arange(0, SSM_HEADS, 2), np.arange(1, SSM_HEADS, 2)])
    g_in, s_in = gdn_w_in[0], ssm_w_in[0]
    beta_cols = g_in[:, GDN_MAIN:GDN_MAIN + GDN_V_HEADS][:, perm_g]
    a_cols = g_in[:, GDN_MAIN + GDN_V_HEADS:GDN_MAIN + 2 * GDN_V_HEADS][:, perm_g]
    gdn_small = jnp.concatenate([a_cols, beta_cols, jnp.zeros((D_MODEL, LANES - 2 * GDN_V_HEADS), F32)], axis=1)
    dt_cols = s_in[:, SSM_MAIN:SSM_MAIN + SSM_HEADS][:, perm_s]
    ssm_small = jnp.concatenate([dt_cols, dt_cols, jnp.zeros((D_MODEL, LANES - 2 * SSM_HEADS), F32)], axis=1)
    return dict(
        w_mod=w_mod.astype(BF16), b_mod=b_mod, norm_mix=norm_mix, norm_ffn=norm_ffn, norm_final=norm_final,
        gdn_main=g_in[:, :GDN_MAIN].astype(BF16), gdn_small=gdn_small.astype(BF16),
        gdn_conv_w=gdn_conv_w[0], gdn_conv_b=jnp.zeros((GDN_CONV_DIM,), F32),
        gdn_alog_row=_lane_row([(0, gdn_a_log[0][perm_g])]), gdn_dtb_row=_lane_row([(0, gdn_dt_bias[0][perm_g])]),
        gdn_norm=jnp.tile(gdn_norm[0], GDN_V_HEADS), gdn_w_out=gdn_w_out[0].astype(BF16),
        gdn_consts=_recurrence_consts(GDN_PAIRS, 0, GDN_PAIRS),
        ssm_main=s_in[:, :SSM_MAIN].astype(BF16), ssm_small=ssm_small.astype(BF16),
        ssm_conv_w=ssm_conv_w[0], ssm_conv_b=ssm_conv_b[0],
        ssm_alog_row=_lane_row([(SSM_HEADS, ssm_a_log[0][perm_s])]),
        ssm_dtb_row=_lane_row([(0, ssm_dt_bias[0][perm_s]), (SSM_HEADS, ssm_dt_bias[0][perm_s])]),
        ssm_dskip_row=jnp.repeat(ssm_d[0], SSM_HEAD_DIM).reshape(1, SSM_D_INNER),
        ssm_norm=ssm_norm[0], ssm_w_out=ssm_w_out[0].astype(BF16),
        ssm_consts=_recurrence_consts(SSM_PAIRS, SSM_HEADS, SSM_HEADS + SSM_PAIRS),
        wg=[ffn_w_gate_up[i][:, :FFN_HIDDEN].astype(BF16) for i in range(2)],
        wu=[ffn_w_gate_up[i][:, FFN_HIDDEN:].astype(BF16) for i in range(2)],
        wd=[ffn_w_down[i].astype(BF16) for i in range(2)],
    )


def _ffn(x, layer, mod3, rows_up, rows_down, p, final_w):
    act = _ffn_up(x, p["norm_ffn"][layer], mod3, rows_up, p["wg"][layer], p["wu"][layer], FFN_TILE)
    return _ffn_down(act, x, mod3, rows_down, p["wd"][layer], final_w)


QSCALE = GDN_HEAD_DIM ** -0.5


def _trunk_seq(x3, c, p):
    nseq, seq_len, _ = x3.shape
    m = nseq * seq_len
    x = x3.reshape(m, D_MODEL)
    mod = _modulation(c, p["w_mod"], p["b_mod"])
    mod3 = [mod[l].reshape(nseq, 1, 6 * D_MODEL) for l in range(2)]
    rows_a = _Rows(m, min(1024, seq_len), seq_len, 1)
    rows_b = _Rows(m, min(512, seq_len), seq_len, 1)

    pm, ps = _in_proj(x, p["norm_mix"][0], mod3[0], rows_a, 1, 0, p["gdn_main"], p["gdn_small"], PROJ_TILE)
    act = _conv_rows(pm, 0, GDN_CONV_DIM, p["gdn_conv_w"], p["gdn_conv_b"], nseq, seq_len, 2, QSCALE)
    o, gdn_s = _gdn_seq(act, ps, p["gdn_alog_row"], p["gdn_dtb_row"], p["gdn_consts"], nseq, seq_len)
    gdn_c = pm.reshape(nseq, seq_len, GDN_MAIN)[:, seq_len - (CONV_WIDTH - 1):, :GDN_CONV_DIM]
    x = _mixer_out(o, pm, 2, p["gdn_norm"], x, mod3[0], rows_b, p["gdn_w_out"], GDN_HEAD_DIM, False)
    x = _ffn(x, 0, mod3[0], rows_a, rows_b, p, None)

    pm, ps = _in_proj(x, p["norm_mix"][1], mod3[1], rows_a, 1, 0, p["ssm_main"], p["ssm_small"], PROJ_TILE)
    act = _conv_rows(pm, SSM_D_INNER, SSM_CONV_DIM, p["ssm_conv_w"], p["ssm_conv_b"], nseq, seq_len, 0, 1.0)
    y, ssm_h = _ssd_seq(act, ps, p["ssm_alog_row"], p["ssm_dtb_row"], p["ssm_dskip_row"], p["ssm_consts"],
                        nseq, seq_len)
    ssm_c = pm.reshape(nseq, seq_len, SSM_MAIN)[:, seq_len - (CONV_WIDTH - 1):, SSM_D_INNER:]
    x = _mixer_out(y, pm, 0, p["ssm_norm"], x, mod3[1], rows_b, p["ssm_w_out"], SSM_D_INNER // SSM_GROUPS, True)
    _, y_out = _ffn(x, 1, mod3[1], rows_a, rows_b, p, p["norm_final"])

    return (y_out.reshape(nseq, seq_len, D_MODEL), gdn_s[None], gdn_c[None],
            ssm_h.reshape(nseq, SSM_HEADS, SSM_HEAD_DIM, SSM_STATE)[None], ssm_c[None])


def _trunk_step(x3, c, st_gdn, cv_gdn, st_ssm, cv_ssm, p):
    nb, steps, _ = x3.shape
    assert steps >= CONV_WIDTH - 1
    m = nb * steps
    x = jnp.transpose(x3, (1, 0, 2)).reshape(m, D_MODEL)
    mod = _modulation(c, p["w_mod"], p["b_mod"])
    mod3 = [mod[l].reshape(1, nb, 6 * D_MODEL) for l in range(2)]
    rows = _Rows(m, m, None, nb)
    tok = lambda a: jnp.transpose(a, (1, 0, 2))

    pm, ps = _in_proj(x, p["norm_mix"][0], mod3[0], rows, 1, 0, p["gdn_main"], p["gdn_small"], PROJ_TILE)
    u3 = pm.reshape(steps, nb, GDN_MAIN)
    act3 = _conv_steps(u3, 0, GDN_CONV_DIM, tok(cv_gdn[0]), p["gdn_conv_w"], p["gdn_conv_b"], 2, QSCALE)
    o, gdn_s = _gdn_step(act3, ps.reshape(steps, nb, LANES), p["gdn_alog_row"], p["gdn_dtb_row"], p["gdn_consts"],
                         st_gdn[0])
    gdn_c = tok(u3[steps - (CONV_WIDTH - 1):, :, :GDN_CONV_DIM])
    x = _mixer_out(o, pm, 2, p["gdn_norm"], x, mod3[0], rows, p["gdn_w_out"], GDN_HEAD_DIM, False)
    x = _ffn(x, 0, mod3[0], rows, rows, p, None)

    pm, ps = _in_proj(x, p["norm_mix"][1], mod3[1], rows, 1, 0, p["ssm_main"], p["ssm_small"], PROJ_TILE)
    u3 = pm.reshape(steps, nb, SSM_MAIN)
    act3 = _conv_steps(u3, SSM_D_INNER, SSM_CONV_DIM, tok(cv_ssm[0]), p["ssm_conv_w"], p["ssm_conv_b"], 0, 1.0)
    h0 = st_ssm[0].reshape(nb, SSM_PAIRS, LANES, SSM_STATE)
    y, ssm_h = _ssd_step(act3, ps.reshape(steps, nb, LANES), p["ssm_alog_row"], p["ssm_dtb_row"],
                         p["ssm_dskip_row"], p["ssm_consts"], h0)
    ssm_c = tok(u3[steps - (CONV_WIDTH - 1):, :, SSM_D_INNER:])
    x = _mixer_out(y, pm, 0, p["ssm_norm"], x, mod3[1], rows, p["ssm_w_out"], SSM_D_INNER // SSM_GROUPS, True)
    _, y_out = _ffn(x, 1, mod3[1], rows, rows, p, p["norm_final"])

    return (tok(y_out.reshape(steps, nb, D_MODEL)), gdn_s[None], gdn_c[None],
            ssm_h.reshape(nb, SSM_HEADS, SSM_HEAD_DIM, SSM_STATE)[None], ssm_c[None])


def kernel(x_prompt, x_sample, c_prompt, c_sample, state_gdn, state_gdn_conv, state_ssm, state_ssm_conv, w_mod, b_mod,
           norm_mix, norm_ffn, norm_final, gdn_w_in, gdn_conv_w, gdn_a_log, gdn_dt_bias, gdn_norm, gdn_w_out, ssm_w_in,
           ssm_conv_w, ssm_conv_b, ssm_a_log, ssm_dt_bias, ssm_d, ssm_norm, ssm_w_out, ffn_w_gate_up, ffn_w_down):
    p = _stage_params(w_mod, b_mod, norm_mix, norm_ffn, norm_final, gdn_w_in, gdn_conv_w, gdn_a_log, gdn_dt_bias,
                      gdn_norm, gdn_w_out, ssm_w_in, ssm_conv_w, ssm_conv_b, ssm_a_log, ssm_dt_bias, ssm_d, ssm_norm,
                      ssm_w_out, ffn_w_gate_up, ffn_w_down)
    y_p, gs_p, gc_p, ss_p, sc_p = _trunk_seq(x_prompt, c_prompt, p)
    y_s, gs_s, gc_s, ss_s, sc_s = _trunk_step(x_sample, c_sample, state_gdn, state_gdn_conv, state_ssm,
                                              state_ssm_conv, p)
    return (y_p, y_s, gs_p, gc_p, ss_p, sc_p, gs_s, gc_s, ss_s, sc_s)
```

```python
import functools

import numpy as np
import jax
import jax.numpy as jnp
from jax import lax
from jax.experimental import pallas as pl
from jax.experimental.pallas import tpu as pltpu

F32 = jnp.float32
BF16 = jnp.bfloat16

D_MODEL = 1024
EPS = 1e-6
CONV_WIDTH = 4
CHUNK = 64
LANES = 128
HALF = LANES // 2

GDN_QK_HEADS = 8
GDN_V_HEADS = 16
GDN_HEAD_DIM = 128
GDN_QK_DIM = GDN_QK_HEADS * GDN_HEAD_DIM
GDN_V_DIM = GDN_V_HEADS * GDN_HEAD_DIM
GDN_CONV_DIM = 2 * GDN_QK_DIM + GDN_V_DIM
GDN_MAIN = GDN_CONV_DIM + GDN_V_DIM

SSM_D_INNER = 2 * D_MODEL
SSM_HEAD_DIM = 64
SSM_HEADS = SSM_D_INNER // SSM_HEAD_DIM
SSM_GROUPS = 4
SSM_STATE = 128
SSM_BC = SSM_GROUPS * SSM_STATE
SSM_CONV_DIM = SSM_D_INNER + 2 * SSM_BC
SSM_MAIN = SSM_D_INNER + SSM_CONV_DIM

FFN_HIDDEN = 2816

VMEM_LIMIT = 56 * 1024 * 1024
NEG_BIG = -1e30


def _cparams(*sem):
    return pltpu.CompilerParams(dimension_semantics=sem, vmem_limit_bytes=VMEM_LIMIT)


def _silu(x):
    return x * jax.nn.sigmoid(x)


def _softplus(x):
    return jnp.maximum(x, 0.0) + jnp.log1p(jnp.exp(-jnp.abs(x)))


def _dot(a, b):
    return jnp.dot(a, b, preferred_element_type=F32)


def _dot_nt(a, b):
    return lax.dot_general(a, b, (((1,), (1,)), ((), ())), preferred_element_type=F32)


def _tile_rows(v, rep):
    return v if rep == 1 else jnp.concatenate([v] * rep, axis=0)


def _mod_kernel(c_ref, w_ref, b_ref, o_ref):
    cs = _silu(c_ref[...]).astype(BF16)
    o_ref[...] = _dot(cs, w_ref[...]) + b_ref[...]


def _modulation(c, w_mod_bf, b_mod):
    depth, _, n = w_mod_bf.shape
    bc = c.shape[0]
    tn = 1536
    return pl.pallas_call(
        _mod_kernel,
        grid=(depth, n // tn),
        in_specs=[
            pl.BlockSpec((bc, D_MODEL), lambda l, j: (0, 0)),
            pl.BlockSpec((None, D_MODEL, tn), lambda l, j: (l, 0, j)),
            pl.BlockSpec((None, 1, tn), lambda l, j: (l, 0, j)),
        ],
        out_specs=pl.BlockSpec((None, bc, tn), lambda l, j: (l, 0, j)),
        out_shape=jax.ShapeDtypeStruct((depth, bc, n), F32),
        compiler_params=_cparams("parallel", "parallel"),
        name="adaln_mod",
    )(c, w_mod_bf, b_mod.reshape(depth, 1, n))


def _norm_mod(x, nw, sc, sh, rep):
    y = x * lax.rsqrt(jnp.mean(x * x, axis=-1, keepdims=True) + EPS) * nw
    return y * (1.0 + _tile_rows(sc, rep)) + _tile_rows(sh, rep)


def _in_proj_kernel(x_ref, nw_ref, sc_ref, sh_ref, w_ref, w2_ref, o_ref, o2_ref, h_ref, *, rep):
    @pl.when(pl.program_id(1) == 0)
    def _():
        h = _norm_mod(x_ref[...], nw_ref[...], sc_ref[...], sh_ref[...], rep).astype(BF16)
        h_ref[...] = h
        o2_ref[...] = _dot(h, w2_ref[...])

    o_ref[...] = _dot(h_ref[...], w_ref[...]).astype(o_ref.dtype)


def _ffn_up_kernel(x_ref, nw_ref, sc_ref, sh_ref, wg_ref, wu_ref, o_ref, h_ref, *, rep):
    @pl.when(pl.program_id(1) == 0)
    def _():
        h_ref[...] = _norm_mod(x_ref[...], nw_ref[...], sc_ref[...], sh_ref[...], rep).astype(BF16)

    h = h_ref[...]
    o_ref[...] = (_silu(_dot(h, wg_ref[...])) * _dot(h, wu_ref[...])).astype(BF16)


class _Rows:
    def __init__(self, m, tm, group_rows, mod_rows):
        assert m % tm == 0
        self.m, self.tm = m, tm
        if mod_rows == 1:
            assert group_rows % tm == 0
            self.rep = 1
            self.gmap = lambda i: (i * tm) // group_rows
        else:
            assert tm % mod_rows == 0
            self.rep = tm // mod_rows
            self.gmap = lambda i: 0
        self.mod_rows = mod_rows

    def mod_spec(self, col_block, with_j):
        if with_j:
            return pl.BlockSpec((None, self.mod_rows, D_MODEL), lambda i, j: (self.gmap(i), 0, col_block))
        return pl.BlockSpec((None, self.mod_rows, D_MODEL), lambda i: (self.gmap(i), 0, col_block))


def _in_proj(x, nw, mod3, rows, sc_blk, sh_blk, w, w2, tn):
    m, tm = rows.m, rows.tm
    n = w.shape[1]
    assert n % tn == 0
    return pl.pallas_call(
        functools.partial(_in_proj_kernel, rep=rows.rep),
        grid=(m // tm, n // tn),
        in_specs=[
            pl.BlockSpec((tm, D_MODEL), lambda i, j: (i, 0)),
            pl.BlockSpec((1, D_MODEL), lambda i, j: (0, 0)),
            rows.mod_spec(sc_blk, True),
            rows.mod_spec(sh_blk, True),
            pl.BlockSpec((D_MODEL, tn), lambda i, j: (0, j)),
            pl.BlockSpec((D_MODEL, LANES), lambda i, j: (0, 0)),
        ],
        out_specs=[
            pl.BlockSpec((tm, tn), lambda i, j: (i, j)),
            pl.BlockSpec((tm, LANES), lambda i, j: (i, 0)),
        ],
        out_shape=[jax.ShapeDtypeStruct((m, n), BF16), jax.ShapeDtypeStruct((m, LANES), F32)],
        scratch_shapes=[pltpu.VMEM((tm, D_MODEL), BF16)],
        compiler_params=_cparams("parallel", "arbitrary"),
        name="in_proj",
    )(x, nw.reshape(1, D_MODEL), mod3, mod3, w, w2)


def _ffn_up(x, nw, mod3, rows, wg, wu, th):
    m, tm = rows.m, rows.tm
    assert FFN_HIDDEN % th == 0
    return pl.pallas_call(
        functools.partial(_ffn_up_kernel, rep=rows.rep),
        grid=(m // tm, FFN_HIDDEN // th),
        in_specs=[
            pl.BlockSpec((tm, D_MODEL), lambda i, j: (i, 0)),
            pl.BlockSpec((1, D_MODEL), lambda i, j: (0, 0)),
            rows.mod_spec(4, True),
            rows.mod_spec(3, True),
            pl.BlockSpec((D_MODEL, th), lambda i, j: (0, j)),
            pl.BlockSpec((D_MODEL, th), lambda i, j: (0, j)),
        ],
        out_specs=pl.BlockSpec((tm, th), lambda i, j: (i, j)),
        out_shape=jax.ShapeDtypeStruct((m, FFN_HIDDEN), BF16),
        scratch_shapes=[pltpu.VMEM((tm, D_MODEL), BF16)],
        compiler_params=_cparams("parallel", "arbitrary"),
        name="ffn_up",
    )(x, nw.reshape(1, D_MODEL), mod3, mod3, wg, wu)


def _resid_store(acc, x_ref, gt_ref, o_ref, fnw_ref, y_ref, rep):
    xn = x_ref[...] + _tile_rows(gt_ref[...], rep) * acc
    o_ref[...] = xn
    if y_ref is not None:
        y_ref[...] = xn * lax.rsqrt(jnp.mean(xn * xn, axis=-1, keepdims=True) + EPS) * fnw_ref[...]


def _ffn_down_kernel(a_ref, x_ref, gt_ref, w_ref, *rest, rep, final):
    if final:
        fnw_ref, o_ref, y_ref = rest
    else:
        (o_ref,), fnw_ref, y_ref = rest, None, None
    _resid_store(_dot(a_ref[...], w_ref[...]), x_ref, gt_ref, o_ref, fnw_ref, y_ref, rep)


def _ffn_down(act, x, mod3, rows, w, fnw):
    m, tm = rows.m, rows.tm
    final = fnw is not None
    in_specs = [
        pl.BlockSpec((tm, FFN_HIDDEN), lambda i: (i, 0)),
        pl.BlockSpec((tm, D_MODEL), lambda i: (i, 0)),
        rows.mod_spec(5, False),
        pl.BlockSpec((FFN_HIDDEN, D_MODEL), lambda i: (0, 0)),
    ]
    args = [act, x, mod3, w]
    row_spec = pl.BlockSpec((tm, D_MODEL), lambda i: (i, 0))
    out_shape = jax.ShapeDtypeStruct((m, D_MODEL), F32)
    if final:
        in_specs.append(pl.BlockSpec((1, D_MODEL), lambda i: (0, 0)))
        args.append(fnw.reshape(1, D_MODEL))
        out_specs, out_shapes = [row_spec, row_spec], [out_shape, out_shape]
    else:
        out_specs, out_shapes = row_spec, out_shape
    return pl.pallas_call(
        functools.partial(_ffn_down_kernel, rep=rows.rep, final=final),
        grid=(m // tm,),
        in_specs=in_specs,
        out_specs=out_specs,
        out_shape=out_shapes,
        compiler_params=_cparams("parallel"),
        name="ffn_down",
    )(*args)


def _mixer_out_kernel(y_ref, z_ref, nw_ref, x_ref, gt_ref, w_ref, o_ref, a_ref, *, rep, group, gate_first):
    width = y_ref.shape[1]
    for s in range(0, width, group):
        y = y_ref[:, s:s + group]
        gate = _silu(z_ref[:, s:s + group].astype(F32))
        if gate_first:
            y = y * gate
        y = y * lax.rsqrt(jnp.mean(y * y, axis=-1, keepdims=True) + EPS) * nw_ref[:, s:s + group]
        if not gate_first:
            y = y * gate
        a_ref[:, s:s + group] = y.astype(BF16)
    _resid_store(_dot(a_ref[...], w_ref[...]), x_ref, gt_ref, o_ref, None, None, rep)


def _mixer_out(y, zsrc, z_blk, nw_full, x, mod3, rows, w, group, gate_first):
    m, tm = rows.m, rows.tm
    width = y.shape[1]
    return pl.pallas_call(
        functools.partial(_mixer_out_kernel, rep=rows.rep, group=group, gate_first=gate_first),
        grid=(m // tm,),
        in_specs=[
            pl.BlockSpec((tm, width), lambda i: (i, 0)),
            pl.BlockSpec((tm, width), lambda i: (i, z_blk)),
            pl.BlockSpec((1, width), lambda i: (0, 0)),
            pl.BlockSpec((tm, D_MODEL), lambda i: (i, 0)),
            rows.mod_spec(2, False),
            pl.BlockSpec((width, D_MODEL), lambda i: (0, 0)),
        ],
        out_specs=pl.BlockSpec((tm, D_MODEL), lambda i: (i, 0)),
        out_shape=jax.ShapeDtypeStruct((m, D_MODEL), F32),
        scratch_shapes=[pltpu.VMEM((tm, width), BF16)],
        compiler_params=_cparams("parallel"),
        name="mixer_out",
    )(y, zsrc, nw_full.reshape(1, width), x, mod3, w)


CONV_COLS = 1024


def _post_conv(acc, o_ref, cb, n_l2, qscale):
    y = _silu(acc)
    if n_l2 == 0:
        o_ref[...] = y
        return

    @pl.when(cb < n_l2)
    def _():
        scale = jnp.where(cb == 0, qscale, 1.0).astype(F32)
        for s in range(0, CONV_COLS, GDN_HEAD_DIM):
            yh = y[:, s:s + GDN_HEAD_DIM]
            o_ref[:, s:s + GDN_HEAD_DIM] = yh * lax.rsqrt(jnp.sum(yh * yh, axis=-1, keepdims=True) + EPS) * scale

    @pl.when(cb >= n_l2)
    def _():
        o_ref[...] = y


BHALO = 16
CONV_GROUP = 2 * LANES


def _shift_matrix():
    s = np.zeros(((CONV_WIDTH - 1) * CHUNK, BHALO + CHUNK), np.float32)
    for tap in range(CONV_WIDTH - 1):
        for r in range(CHUNK):
            s[tap * CHUNK + r, BHALO - (CONV_WIDTH - 1) + tap + r] = 1.0
    return jnp.asarray(s, BF16)


def _conv_block(raw_refs, shift_ref, w_ref, b_ref, ext_ref, act_ref, seq_start, l2_cols, q_cols, qscale):
    width = ext_ref.shape[1]
    if seq_start:
        ext_ref[0:BHALO, :] = jnp.zeros((BHALO, width), BF16)
    else:
        ext_ref[0:BHALO, :] = ext_ref[CHUNK:CHUNK + BHALO, :]

    off = 0
    for ref in raw_refs:
        ext_ref[BHALO:BHALO + CHUNK, off:off + ref.shape[1]] = ref[...]
        off += ref.shape[1]

    def lane_group(s):
        cols = slice(s, s + CONV_GROUP)
        sh = _dot(shift_ref[...], ext_ref[:, cols])
        acc = b_ref[:, cols] + w_ref[CONV_WIDTH - 1:CONV_WIDTH, cols] * ext_ref[BHALO:BHALO + CHUNK, cols].astype(F32)
        for tap in range(CONV_WIDTH - 1):
            acc = acc + w_ref[tap:tap + 1, cols] * sh[tap * CHUNK:(tap + 1) * CHUNK]
        y = _silu(acc)
        for h in range(s, s + CONV_GROUP, LANES):
            yh = y[:, h - s:h - s + LANES]
            if h < l2_cols:
                yh = yh * lax.rsqrt(jnp.sum(yh * yh, axis=-1, keepdims=True) + EPS)
                if h < q_cols:
                    yh = yh * qscale
            act_ref[:, h:h + LANES] = yh

    return [functools.partial(lane_group, s) for s in range(0, width, CONV_GROUP)]


class _Filler:
    def __init__(self, tasks, shares):
        self.tasks, self.per = list(tasks), -(-len(tasks) // shares)

    def emit(self):
        for task in self.tasks[:self.per]:
            task()
        self.tasks = self.tasks[self.per:]

    def flush(self):
        for task in self.tasks:
            task()
        self.tasks = []


_NO_FILL = _Filler([], 1)


def _conv_steps_kernel(u_ref, hist_ref, w_ref, b_ref, o_ref, *, steps, n_l2, qscale):
    cb = pl.program_id(0)
    ext = [hist_ref[i] for i in range(CONV_WIDTH - 1)] + [u_ref[i].astype(F32) for i in range(steps)]
    for t in range(steps):
        acc = b_ref[...] + w_ref[0:1, :] * ext[t]
        for tap in range(1, CONV_WIDTH):
            acc = acc + w_ref[tap:tap + 1, :] * ext[t + tap]
        _post_conv(acc, o_ref.at[t], cb, n_l2, qscale)


def _conv_steps(u3, skip_at, skip, n_cols, hist3, conv_w, conv_b, n_l2, qscale):
    steps, nb, _ = u3.shape
    return pl.pallas_call(
        functools.partial(_conv_steps_kernel, steps=steps, n_l2=n_l2, qscale=qscale),
        grid=(n_cols // CONV_COLS,),
        in_specs=[
            pl.BlockSpec((steps, nb, CONV_COLS), lambda c: (0, 0, c + skip * (c >= skip_at))),
            pl.BlockSpec((CONV_WIDTH - 1, nb, CONV_COLS), lambda c: (0, 0, c)),
            pl.BlockSpec((CONV_WIDTH, CONV_COLS), lambda c: (0, c)),
            pl.BlockSpec((1, CONV_COLS), lambda c: (0, c)),
        ],
        out_specs=pl.BlockSpec((steps, nb, CONV_COLS), lambda c: (0, 0, c)),
        out_shape=jax.ShapeDtypeStruct((steps, nb, n_cols), F32),
        compiler_params=_cparams("parallel"),
        name="conv_steps",
    )(u3, hist3, conv_w, conv_b.reshape(1, n_cols))


def _split3(x):
    hi = x.astype(BF16)
    r = x - hi.astype(F32)
    mid = r.astype(BF16)
    lo = (r - mid.astype(F32)).astype(BF16)
    return hi, mid, lo


def _cat3(x, axis):
    return jnp.concatenate(_split3(x), axis=axis)


def _pad_t(tile):
    return jnp.concatenate([tile, jnp.zeros_like(tile)], axis=0).T


def _pair_rows(t, n):
    return t[0:n] + pltpu.roll(t[n:2 * n], HALF, 1)


def _iotas(shape):
    return lax.broadcasted_iota(jnp.int32, shape, 0), lax.broadcasted_iota(jnp.int32, shape, 1)


def _block_diag(pair, bdmask):
    return jnp.where(bdmask, jnp.concatenate([pair, pair], axis=0), 0.0)


def _mm_pair(lhs, rhs):
    lh = lhs.astype(BF16)
    ll = (lhs - lh.astype(F32)).astype(BF16)
    rh = rhs.astype(BF16)
    rl = (rhs - rh.astype(F32)).astype(BF16)
    return _dot(jnp.concatenate([lh, lh, ll], axis=1), jnp.concatenate([rh, rl, rh], axis=0))


def _mm_hl(lhs, rhs_bf):
    lh = lhs.astype(BF16)
    ll = (lhs - lh.astype(F32)).astype(BF16)
    return _dot(jnp.concatenate([lh, ll], axis=1), jnp.concatenate([rhs_bf, rhs_bf], axis=0))


def _unit_lower_inverse(xs, levels, bdmask, eye2, fill):
    ps = [eye2 + x for x in xs]
    if levels <= 1:
        return ps
    ys = [_mm_pair(x, _block_diag(x, bdmask)) for x in xs]
    fill.emit()
    for _ in range(levels - 2):
        rs = [_mm_pair(jnp.concatenate([y, p], axis=0), _block_diag(y, bdmask)) for y, p in zip(ys, ps)]
        fill.emit()
        ys = [r[0:CHUNK] for r in rs]
        ps = [p + r[CHUNK:2 * CHUNK] for p, r in zip(ps, rs)]
    return [p + _mm_pair(p, _block_diag(y, bdmask)) for p, y in zip(ps, ys)]


def _two_blocks(a_bf, b_bf):
    z = jnp.zeros_like(a_bf)
    return jnp.concatenate([jnp.concatenate([a_bf, z], axis=1), jnp.concatenate([z, b_bf], axis=1)], axis=0)


def _recurrence_consts(n_pairs, chan_even0, chan_odd0):
    tril = np.tril(np.ones((CHUNK, CHUNK), np.float32))
    tril3 = np.concatenate([tril] * 3, axis=1)
    upper = np.zeros((LANES, 2 * LANES), np.float32)
    upper[:CHUNK, :CHUNK] = tril.T
    upper[:CHUNK, LANES:] = 1.0
    upper3 = np.concatenate([upper] * 3, axis=0)
    e = np.zeros((LANES, n_pairs * LANES), np.float32)
    for p in range(n_pairs):
        e[chan_even0 + p, p * LANES:p * LANES + HALF] = 1.0
        e[chan_odd0 + p, p * LANES + HALF:(p + 1) * LANES] = 1.0
    e3 = np.concatenate([e] * 3, axis=0)
    return jnp.asarray(tril3, BF16), jnp.asarray(upper3, BF16), jnp.asarray(e3, BF16)


def _levels(lr):
    return max(1, int(np.ceil(np.log2(lr))))


GDN_PAIRS = GDN_V_HEADS // 2
GDN_FILL_SHARES = 11


def _gdn_gates(raw, alog_row, dtb_row, lr):
    row, lane = _iotas((CHUNK, LANES))
    g = -jnp.exp(alog_row) * _softplus(raw + dtb_row)
    beta = jax.nn.sigmoid(raw)
    gt = jnp.where(lane < GDN_V_HEADS, g, jnp.where(lane < 2 * GDN_V_HEADS, beta, 0.0))
    if lr < CHUNK:
        gt = jnp.where(row < lr, gt, 0.0)
    return gt


def _gdn_block(act_ref, gt, s_in, s_out, o_ref, tril3_ref, upper3_ref, e3_ref, lr, fill=_NO_FILL):
    np_ = GDN_PAIRS
    row, lane = _iotas((CHUNK, LANES))
    jl = jnp.bitwise_and(lane, HALF - 1)
    left = lane < HALF
    causal = row >= jl
    strict = row > jl
    eye2 = jnp.where(row == jl, 1.0, 0.0)
    r2, l2 = _iotas((LANES, LANES))
    bdmask = jnp.right_shift(r2, 6) == jnp.right_shift(l2, 6)
    _, lane_p = _iotas((np_, LANES))

    t1 = _pad_t(gt)
    cum_ext = _dot(_cat3(t1[0:2 * np_], 1), upper3_ref[...])
    cum_t, last_b = cum_ext[:, :LANES], cum_ext[:, LANES:]
    cum_rp = _pair_rows(cum_t, np_)
    beta_rp = _pair_rows(t1[2 * np_:4 * np_], np_)
    last_rp = jnp.where(lane_p < HALF, last_b[0:np_], last_b[np_:2 * np_])
    ecum_rp = jnp.exp(cum_rp)
    kdec_rp = jnp.exp(last_rp - cum_rp) * beta_rp
    elast = jnp.exp(last_b)
    cum = _dot(tril3_ref[...], _cat3(gt, 0))
    col_all = _dot(_cat3(cum, 1), e3_ref[...])

    pairs = range(np_)
    sl = lambda a, i: a[:, i * LANES:(i + 1) * LANES]
    q = [sl(act_ref, p) for p in pairs]
    k = [sl(act_ref, GDN_QK_HEADS + p) for p in pairs]
    v = [sl(act_ref, 2 * GDN_QK_HEADS + h) for h in range(2 * np_)]
    kb = [a.astype(BF16) for a in k]
    qb = [a.astype(BF16) for a in q]
    gq = [_dot_nt(jnp.concatenate([kb[p], qb[p]], axis=0), jnp.concatenate([kb[p], kb[p]], axis=0)) for p in pairs]
    fill.emit()
    colc = [sl(col_all, p) for p in pairs]
    base = [jnp.exp(jnp.where(causal, colc[p] - cum_rp[p:p + 1], NEG_BIG)) * beta_rp[p:p + 1] for p in pairs]
    x = [jnp.where(strict, -(gq[p][0:CHUNK] * base[p]), 0.0) for p in pairs]
    qkd = [(gq[p][CHUNK:2 * CHUNK] * base[p]).astype(BF16) for p in pairs]
    minv = _unit_lower_inverse(x, _levels(lr), bdmask, eye2, fill)
    fill.emit()
    u = [jnp.concatenate([v[2 * p], v[2 * p + 1]], axis=1)
         + _mm_hl(minv[p] - eye2, _two_blocks(v[2 * p].astype(BF16), v[2 * p + 1].astype(BF16))) for p in pairs]
    fill.emit()
    w = [_mm_hl(minv[p] * ecum_rp[p:p + 1], _two_blocks(kb[p], kb[p])) for p in pairs]
    fill.emit()
    s_old = [s_in[h] for h in range(2 * np_)]
    r = [_dot(jnp.concatenate([sl(w[h // 2], h % 2).astype(BF16), qb[h // 2]], axis=0), s_old[h].astype(BF16))
         for h in range(2 * np_)]
    fill.emit()
    delta = [(sl(u[h // 2], h % 2) - r[h][0:CHUNK]).astype(BF16) for h in range(2 * np_)]
    bd_delta = [_two_blocks(delta[2 * p], delta[2 * p + 1]) for p in pairs]
    o_intra = [_dot(qkd[p], bd_delta[p]) for p in pairs]
    fill.emit()
    ds = [_dot((jnp.concatenate([k[p], k[p]], axis=0).T * kdec_rp[p:p + 1]).astype(BF16), bd_delta[p]) for p in pairs]
    fill.flush()
    for p in pairs:
        ecol = jnp.exp(colc[p])
        ecol_r = pltpu.roll(ecol, HALF, 1)
        efull = (jnp.where(left, ecol, ecol_r), jnp.where(left, ecol_r, ecol))
        for hh in range(2):
            h = 2 * p + hh
            o_ref[:, h * LANES:(h + 1) * LANES] = efull[hh] * r[h][CHUNK:2 * CHUNK] + sl(o_intra[p], hh)
            e_h = elast[hh * np_ + p:hh * np_ + p + 1]
            s_out[h] = e_h * s_old[h] + sl(ds[p], hh)


def _gdn_seq_kernel(raw0_ref, rawn_ref, graw_ref, shift_ref, cw_ref, cb_ref, alog_ref, dtb_ref, tril3_ref, upper3_ref,
                    e3_ref, o_ref, s_ref, ext_ref, act_ref):
    n = pl.program_id(1)
    conv = functools.partial(_conv_block, shift_ref=shift_ref, w_ref=cw_ref, b_ref=cb_ref, ext_ref=ext_ref,
                             l2_cols=2 * GDN_QK_DIM, q_cols=GDN_QK_DIM, qscale=QSCALE)

    @pl.when(n == 0)
    def _():
        s_ref[...] = jnp.zeros(s_ref.shape, F32)
        _Filler(conv([raw0_ref], act_ref=act_ref.at[0], seq_start=True), 1).flush()

    @pl.when(n > 0)
    def _():
        act_ref[0] = act_ref[1]

    gt = _gdn_gates(graw_ref[...], alog_ref[...], dtb_ref[...], CHUNK)
    fill = _Filler(conv([rawn_ref], act_ref=act_ref.at[1], seq_start=False), GDN_FILL_SHARES)
    _gdn_block(act_ref.at[0], gt, s_ref, s_ref, o_ref, tril3_ref, upper3_ref, e3_ref, CHUNK, fill)


def _const_specs(consts, ngrid):
    zero = (lambda *_: (0, 0))
    return [pl.BlockSpec(c.shape, zero) for c in consts]


def _gdn_seq(pm, graw, conv_w, conv_b, alog_row, dtb_row, consts, nseq, seq_len):
    nc = seq_len // CHUNK
    last = nseq * nc - 1
    rowspec = pl.BlockSpec((1, LANES), lambda b, n: (0, 0))
    return pl.pallas_call(
        _gdn_seq_kernel,
        grid=(nseq, nc),
        in_specs=[
            pl.BlockSpec((CHUNK, GDN_CONV_DIM), lambda b, n: (b * nc, 0)),
            pl.BlockSpec((CHUNK, GDN_CONV_DIM), lambda b, n: (jnp.minimum(b * nc + n + 1, last), 0)),
            pl.BlockSpec((CHUNK, LANES), lambda b, n: (b * nc + n, 0)),
            pl.BlockSpec(((CONV_WIDTH - 1) * CHUNK, BHALO + CHUNK), lambda b, n: (0, 0)),
            pl.BlockSpec((CONV_WIDTH, GDN_CONV_DIM), lambda b, n: (0, 0)),
            pl.BlockSpec((1, GDN_CONV_DIM), lambda b, n: (0, 0)),
            rowspec, rowspec,
        ] + _const_specs(consts, 2),
        out_specs=[
            pl.BlockSpec((CHUNK, GDN_V_DIM), lambda b, n: (b * nc + n, 0)),
            pl.BlockSpec((None, GDN_V_HEADS, GDN_HEAD_DIM, GDN_HEAD_DIM), lambda b, n: (b, 0, 0, 0)),
        ],
        out_shape=[
            jax.ShapeDtypeStruct((nseq * seq_len, GDN_V_DIM), F32),
            jax.ShapeDtypeStruct((nseq, GDN_V_HEADS, GDN_HEAD_DIM, GDN_HEAD_DIM), F32),
        ],
        scratch_shapes=[
            pltpu.VMEM((CHUNK + BHALO, GDN_CONV_DIM), BF16),
            pltpu.VMEM((2, CHUNK, GDN_CONV_DIM), F32),
        ],
        compiler_params=_cparams("parallel", "arbitrary"),
        name="gdn_seq",
    )(pm, pm, graw, _shift_matrix(), conv_w, conv_b.reshape(1, GDN_CONV_DIM), alog_row, dtb_row, *consts)


def _load_padded(src_ref, pad_ref, lr):
    pad_ref[...] = jnp.zeros(pad_ref.shape, F32)
    for t in range(lr):
        pad_ref[t:t + 1, :] = src_ref[t]


def _gdn_step_kernel(act_ref, graw_ref, alog_ref, dtb_ref, tril3_ref, upper3_ref, e3_ref, s0_ref,
                     o_ref, s_ref, apad_ref, gpad_ref, opad_ref, *, lr):
    _load_padded(act_ref, apad_ref, lr)
    _load_padded(graw_ref, gpad_ref, lr)
    gt = _gdn_gates(gpad_ref[...], alog_ref[...], dtb_ref[...], lr)
    _gdn_block(apad_ref, gt, s0_ref, s_ref, opad_ref, tril3_ref, upper3_ref, e3_ref, lr)
    for t in range(lr):
        o_ref[t] = opad_ref[t:t + 1, :]


def _step_spec(steps, width):
    return pl.BlockSpec((steps, None, 1, width), lambda b: (0, b, 0, 0))


def _gdn_step(act3, graw3, alog_row, dtb_row, consts, s0):
    steps, nb, _ = act3.shape
    rowspec = pl.BlockSpec((1, LANES), lambda b: (0, 0))
    sspec = pl.BlockSpec((None, GDN_V_HEADS, GDN_HEAD_DIM, GDN_HEAD_DIM), lambda b: (b, 0, 0, 0))
    o, s = pl.pallas_call(
        functools.partial(_gdn_step_kernel, lr=steps),
        grid=(nb,),
        in_specs=[_step_spec(steps, GDN_CONV_DIM), _step_spec(steps, LANES), rowspec, rowspec]
        + _const_specs(consts, 1) + [sspec],
        out_specs=[_step_spec(steps, GDN_V_DIM), sspec],
        out_shape=[
            jax.ShapeDtypeStruct((steps, nb, 1, GDN_V_DIM), F32),
            jax.ShapeDtypeStruct(s0.shape, F32),
        ],
        scratch_shapes=[
            pltpu.VMEM((CHUNK, GDN_CONV_DIM), F32),
            pltpu.VMEM((CHUNK, LANES), F32),
            pltpu.VMEM((CHUNK, GDN_V_DIM), F32),
        ],
        compiler_params=_cparams("parallel"),
        name="gdn_step",
    )(act3.reshape(steps, nb, 1, GDN_CONV_DIM), graw3.reshape(steps, nb, 1, LANES), alog_row, dtb_row, *consts, s0)
    return o.reshape(steps * nb, GDN_V_DIM), s


SSM_PAIRS = SSM_HEADS // 2
PAIRS_PER_GROUP = SSM_PAIRS // SSM_GROUPS


def _ssm_gates(raw, alog_row, dtb_row, lr):
    row, lane = _iotas((CHUNK, LANES))
    dt = _softplus(raw + dtb_row)
    tile = jnp.where(lane < SSM_HEADS, dt, jnp.where(lane < 2 * SSM_HEADS, -jnp.exp(alog_row) * dt, 0.0))
    if lr < CHUNK:
        tile = jnp.where(row < lr, tile, 0.0)
    return tile


def _ssd_block(act_ref, tile, h_in, h_out, y_ref, dskip_ref, tril3_ref, upper3_ref, e3_ref, fill=_NO_FILL):
    np_ = SSM_PAIRS
    row, lane = _iotas((CHUNK, LANES))
    causal = row >= jnp.bitwise_and(lane, HALF - 1)
    r2, l2 = _iotas((LANES, LANES))
    bdmask = jnp.right_shift(r2, 6) == jnp.right_shift(l2, 6)
    top = r2 < HALF
    _, lane_p = _iotas((np_, LANES))

    t1 = _pad_t(tile)
    cum_ext = _dot(_cat3(t1[2 * np_:4 * np_], 1), upper3_ref[...])
    cum_t, last_b = cum_ext[:, :LANES], cum_ext[:, LANES:]
    cum_rp = _pair_rows(cum_t, np_)
    dt_rp = _pair_rows(t1[0:2 * np_], np_)
    last_rp = jnp.where(lane_p < HALF, last_b[0:np_], last_b[np_:2 * np_])
    coef_rp = jnp.exp(last_rp - cum_rp) * dt_rp
    elast = jnp.exp(last_b)
    cum = _dot(tril3_ref[...], _cat3(tile, 0))
    col_all = _dot(_cat3(cum, 1), e3_ref[...])

    for g in range(SSM_GROUPS):
        bg = act_ref[:, SSM_D_INNER + g * SSM_STATE:SSM_D_INNER + (g + 1) * SSM_STATE].astype(BF16)
        cg = act_ref[:, SSM_D_INNER + SSM_BC + g * SSM_STATE:SSM_D_INNER + SSM_BC + (g + 1) * SSM_STATE].astype(BF16)
        bb = jnp.concatenate([bg, bg], axis=0)
        cb2 = _dot_nt(cg, bb)
        for pp in range(PAIRS_PER_GROUP):
            p = g * PAIRS_PER_GROUP + pp
            xp = act_ref[:, p * LANES:(p + 1) * LANES]
            colc = col_all[:, p * LANES:(p + 1) * LANES]
            dec = jnp.exp(jnp.where(causal, colc - cum_rp[p:p + 1], NEG_BIG))
            lm = cb2 * dec * dt_rp[p:p + 1]
            x2 = jnp.concatenate([xp, xp], axis=0)
            y_diag = _dot(lm.astype(BF16), jnp.where(bdmask, x2, 0.0).astype(BF16))
            hp = h_in[p]
            y_off = jnp.exp(colc) * _dot_nt(cg, hp.astype(BF16))
            y_ref[:, p * LANES:(p + 1) * LANES] = y_diag + y_off + dskip_ref[:, p * LANES:(p + 1) * LANES] * xp
            lhs = jnp.where(bdmask, x2.T * coef_rp[p:p + 1], 0.0)
            dh = _dot(lhs.astype(BF16), bb)
            e_rows = jnp.where(top, elast[p:p + 1], elast[np_ + p:np_ + p + 1])
            h_out[p] = e_rows * hp + dh
            fill.emit()
    fill.flush()


def _ssd_seq_kernel(x0_ref, bc0_ref, xn_ref, bcn_ref, graw_ref, shift_ref, cw_ref, cb_ref, alog_ref, dtb_ref,
                    dskip_ref, tril3_ref, upper3_ref, e3_ref, y_ref, h_ref, ext_ref, act_ref):
    n = pl.program_id(1)
    conv = functools.partial(_conv_block, shift_ref=shift_ref, w_ref=cw_ref, b_ref=cb_ref, ext_ref=ext_ref,
                             l2_cols=0, q_cols=0, qscale=1.0)

    @pl.when(n == 0)
    def _():
        h_ref[...] = jnp.zeros(h_ref.shape, F32)
        _Filler(conv([x0_ref, bc0_ref], act_ref=act_ref.at[0], seq_start=True), 1).flush()

    @pl.when(n > 0)
    def _():
        act_ref[0] = act_ref[1]

    tile = _ssm_gates(graw_ref[...], alog_ref[...], dtb_ref[...], CHUNK)
    fill = _Filler(conv([xn_ref, bcn_ref], act_ref=act_ref.at[1], seq_start=False), SSM_PAIRS)
    _ssd_block(act_ref.at[0], tile, h_ref, h_ref, y_ref, dskip_ref, tril3_ref, upper3_ref, e3_ref, fill)


def _ssd_seq(pm, graw, conv_w, conv_b, alog_row, dtb_row, dskip_row, consts, nseq, seq_len):
    nc = seq_len // CHUNK
    last = nseq * nc - 1
    bc_blk = 2 * SSM_D_INNER // (2 * SSM_BC)
    nxt = lambda b, n: jnp.minimum(b * nc + n + 1, last)
    rowspec = pl.BlockSpec((1, LANES), lambda b, n: (0, 0))
    return pl.pallas_call(
        _ssd_seq_kernel,
        grid=(nseq, nc),
        in_specs=[
            pl.BlockSpec((CHUNK, SSM_D_INNER), lambda b, n: (b * nc, 0)),
            pl.BlockSpec((CHUNK, 2 * SSM_BC), lambda b, n: (b * nc, bc_blk)),
            pl.BlockSpec((CHUNK, SSM_D_INNER), lambda b, n: (nxt(b, n), 0)),
            pl.BlockSpec((CHUNK, 2 * SSM_BC), lambda b, n: (nxt(b, n), bc_blk)),
            pl.BlockSpec((CHUNK, LANES), lambda b, n: (b * nc + n, 0)),
            pl.BlockSpec(((CONV_WIDTH - 1) * CHUNK, BHALO + CHUNK), lambda b, n: (0, 0)),
            pl.BlockSpec((CONV_WIDTH, SSM_CONV_DIM), lambda b, n: (0, 0)),
            pl.BlockSpec((1, SSM_CONV_DIM), lambda b, n: (0, 0)),
            rowspec, rowspec,
            pl.BlockSpec((1, SSM_D_INNER), lambda b, n: (0, 0)),
        ] + _const_specs(consts, 2),
        out_specs=[
            pl.BlockSpec((CHUNK, SSM_D_INNER), lambda b, n: (b * nc + n, 0)),
            pl.BlockSpec((None, SSM_PAIRS, LANES, SSM_STATE), lambda b, n: (b, 0, 0, 0)),
        ],
        out_shape=[
            jax.ShapeDtypeStruct((nseq * seq_len, SSM_D_INNER), F32),
            jax.ShapeDtypeStruct((nseq, SSM_PAIRS, LANES, SSM_STATE), F32),
        ],
        scratch_shapes=[
            pltpu.VMEM((CHUNK + BHALO, SSM_CONV_DIM), BF16),
            pltpu.VMEM((2, CHUNK, SSM_CONV_DIM), F32),
        ],
        compiler_params=_cparams("parallel", "arbitrary"),
        name="ssd_seq",
    )(pm, pm, pm, pm, graw, _shift_matrix(), conv_w, conv_b.reshape(1, SSM_CONV_DIM), alog_row, dtb_row, dskip_row, *consts)


def _ssd_step_kernel(act_ref, graw_ref, alog_ref, dtb_ref, dskip_ref, tril3_ref, upper3_ref, e3_ref, h0_ref,
                     y_ref, h_ref, apad_ref, gpad_ref, ypad_ref, *, lr):
    _load_padded(act_ref, apad_ref, lr)
    _load_padded(graw_ref, gpad_ref, lr)
    tile = _ssm_gates(gpad_ref[...], alog_ref[...], dtb_ref[...], lr)
    _ssd_block(apad_ref, tile, h0_ref, h_ref, ypad_ref, dskip_ref, tril3_ref, upper3_ref, e3_ref)
    for t in range(lr):
        y_ref[t] = ypad_ref[t:t + 1, :]


def _ssd_step(act3, graw3, alog_row, dtb_row, dskip_row, consts, h0):
    steps, nb, _ = act3.shape
    rowspec = pl.BlockSpec((1, LANES), lambda b: (0, 0))
    hspec = pl.BlockSpec((None, SSM_PAIRS, LANES, SSM_STATE), lambda b: (b, 0, 0, 0))
    y, h = pl.pallas_call(
        functools.partial(_ssd_step_kernel, lr=steps),
        grid=(nb,),
        in_specs=[_step_spec(steps, SSM_CONV_DIM), _step_spec(steps, LANES), rowspec, rowspec,
                  pl.BlockSpec((1, SSM_D_INNER), lambda b: (0, 0))] + _const_specs(consts, 1) + [hspec],
        out_specs=[_step_spec(steps, SSM_D_INNER), hspec],
        out_shape=[
            jax.ShapeDtypeStruct((steps, nb, 1, SSM_D_INNER), F32),
            jax.ShapeDtypeStruct(h0.shape, F32),
        ],
        scratch_shapes=[
            pltpu.VMEM((CHUNK, SSM_CONV_DIM), F32),
            pltpu.VMEM((CHUNK, LANES), F32),
            pltpu.VMEM((CHUNK, SSM_D_INNER), F32),
        ],
        compiler_params=_cparams("parallel"),
        name="ssd_step",
    )(act3.reshape(steps, nb, 1, SSM_CONV_DIM), graw3.reshape(steps, nb, 1, LANES), alog_row, dtb_row, dskip_row,
      *consts, h0)
    return y.reshape(steps * nb, SSM_D_INNER), h


FFN_TILE = FFN_HIDDEN // 2
PROJ_TILE = 1024


def _lane_row(pieces):
    row = jnp.zeros((1, LANES), F32)
    for off, vec in pieces:
        row = row.at[0, off:off + vec.shape[0]].set(vec.astype(F32))
    return row


def _stage_params(w_mod, b_mod, norm_mix, norm_ffn, norm_final, gdn_w_in, gdn_conv_w, gdn_a_log, gdn_dt_bias,
                  gdn_norm, gdn_w_out, ssm_w_in, ssm_conv_w, ssm_conv_b, ssm_a_log, ssm_dt_bias, ssm_d, ssm_norm,
                  ssm_w_out, ffn_w_gate_up, ffn_w_down):
    perm_g = np.concatenate([np.arange(0, GDN_V_HEADS, 2), np.arange(1, GDN_V_HEADS, 2)])
    perm_s = np.concatenate([np.arange(0, SSM_HEADS, 2), np.arange(1, SSM_HEADS, 2)])
    g_in, s_in = gdn_w_in[0], ssm_w_in[0]
    beta_cols = g_in[:, GDN_MAIN:GDN_MAIN + GDN_V_HEADS][:, perm_g]
    a_cols = g_in[:, GDN_MAIN + GDN_V_HEADS:GDN_MAIN + 2 * GDN_V_HEADS][:, perm_g]
    gdn_small = jnp.concatenate([a_cols, beta_cols, jnp.zeros((D_MODEL, LANES - 2 * GDN_V_HEADS), F32)], axis=1)
    dt_cols = s_in[:, SSM_MAIN:SSM_MAIN + SSM_HEADS][:, perm_s]
    ssm_small = jnp.concatenate([dt_cols, dt_cols, jnp.zeros((D_MODEL, LANES - 2 * SSM_HEADS), F32)], axis=1)
    return dict(
        w_mod=w_mod.astype(BF16), b_mod=b_mod, norm_mix=norm_mix, norm_ffn=norm_ffn, norm_final=norm_final,
        gdn_main=g_in[:, :GDN_MAIN].astype(BF16), gdn_small=gdn_small.astype(BF16),
        gdn_conv_w=gdn_conv_w[0], gdn_conv_b=jnp.zeros((GDN_CONV_DIM,), F32),
        gdn_alog_row=_lane_row([(0, gdn_a_log[0][perm_g])]), gdn_dtb_row=_lane_row([(0, gdn_dt_bias[0][perm_g])]),
        gdn_norm=jnp.tile(gdn_norm[0], GDN_V_HEADS), gdn_w_out=gdn_w_out[0].astype(BF16),
        gdn_consts=_recurrence_consts(GDN_PAIRS, 0, GDN_PAIRS),
        ssm_main=jnp.concatenate(
            [s_in[:, SSM_D_INNER:2 * SSM_D_INNER], s_in[:, :SSM_D_INNER], s_in[:, 2 * SSM_D_INNER:SSM_MAIN]],
            axis=1).astype(BF16),
        ssm_small=ssm_small.astype(BF16),
        ssm_conv_w=ssm_conv_w[0], ssm_conv_b=ssm_conv_b[0],
        ssm_alog_row=_lane_row([(SSM_HEADS, ssm_a_log[0][perm_s])]),
        ssm_dtb_row=_lane_row([(0, ssm_dt_bias[0][perm_s]), (SSM_HEADS, ssm_dt_bias[0][perm_s])]),
        ssm_dskip_row=jnp.repeat(ssm_d[0], SSM_HEAD_DIM).reshape(1, SSM_D_INNER),
        ssm_norm=ssm_norm[0], ssm_w_out=ssm_w_out[0].astype(BF16),
        ssm_consts=_recurrence_consts(SSM_PAIRS, SSM_HEADS, SSM_HEADS + SSM_PAIRS),
        wg=[ffn_w_gate_up[i][:, :FFN_HIDDEN].astype(BF16) for i in range(2)],
        wu=[ffn_w_gate_up[i][:, FFN_HIDDEN:].astype(BF16) for i in range(2)],
        wd=[ffn_w_down[i].astype(BF16) for i in range(2)],
    )


def _ffn(x, layer, mod3, rows_up, rows_down, p, final_w):
    act = _ffn_up(x, p["norm_ffn"][layer], mod3, rows_up, p["wg"][layer], p["wu"][layer], FFN_TILE)
    return _ffn_down(act, x, mod3, rows_down, p["wd"][layer], final_w)


QSCALE = GDN_HEAD_DIM ** -0.5


def _trunk_seq(x3, c, p):
    nseq, seq_len, _ = x3.shape
    m = nseq * seq_len
    x = x3.reshape(m, D_MODEL)
    mod = _modulation(c, p["w_mod"], p["b_mod"])
    mod3 = [mod[l].reshape(nseq, 1, 6 * D_MODEL) for l in range(2)]
    rows_a = _Rows(m, min(1024, seq_len), seq_len, 1)
    rows_b = _Rows(m, min(512, seq_len), seq_len, 1)

    pm, ps = _in_proj(x, p["norm_mix"][0], mod3[0], rows_a, 1, 0, p["gdn_main"], p["gdn_small"], PROJ_TILE)
    o, gdn_s = _gdn_seq(pm, ps, p["gdn_conv_w"], p["gdn_conv_b"], p["gdn_alog_row"], p["gdn_dtb_row"],
                        p["gdn_consts"], nseq, seq_len)
    tail = pm.reshape(nseq, seq_len, GDN_MAIN)[:, seq_len - (CONV_WIDTH - 1):].astype(F32)
    gdn_c = tail[..., :GDN_CONV_DIM]
    x = _mixer_out(o, pm, 2, p["gdn_norm"], x, mod3[0], rows_b, p["gdn_w_out"], GDN_HEAD_DIM, False)
    x = _ffn(x, 0, mod3[0], rows_a, rows_b, p, None)

    pm, ps = _in_proj(x, p["norm_mix"][1], mod3[1], rows_a, 1, 0, p["ssm_main"], p["ssm_small"], PROJ_TILE)
    y, ssm_h = _ssd_seq(pm, ps, p["ssm_conv_w"], p["ssm_conv_b"], p["ssm_alog_row"], p["ssm_dtb_row"],
                        p["ssm_dskip_row"], p["ssm_consts"], nseq, seq_len)
    tail = pm.reshape(nseq, seq_len, SSM_MAIN)[:, seq_len - (CONV_WIDTH - 1):].astype(F32)
    ssm_c = jnp.concatenate([tail[..., :SSM_D_INNER], tail[..., 2 * SSM_D_INNER:]], axis=-1)
    x = _mixer_out(y, pm, 1, p["ssm_norm"], x, mod3[1], rows_b, p["ssm_w_out"], SSM_D_INNER // SSM_GROUPS, True)
    _, y_out = _ffn(x, 1, mod3[1], rows_a, rows_b, p, p["norm_final"])

    return (y_out.reshape(nseq, seq_len, D_MODEL), gdn_s[None], gdn_c[None],
            ssm_h.reshape(nseq, SSM_HEADS, SSM_HEAD_DIM, SSM_STATE)[None], ssm_c[None])


def _trunk_step(x3, c, st_gdn, cv_gdn, st_ssm, cv_ssm, p):
    nb, steps, _ = x3.shape
    assert steps >= CONV_WIDTH - 1
    m = nb * steps
    x = jnp.transpose(x3, (1, 0, 2)).reshape(m, D_MODEL)
    mod = _modulation(c, p["w_mod"], p["b_mod"])
    mod3 = [mod[l].reshape(1, nb, 6 * D_MODEL) for l in range(2)]
    rows = _Rows(m, m, None, nb)
    tok = lambda a: jnp.transpose(a, (1, 0, 2))

    pm, ps = _in_proj(x, p["norm_mix"][0], mod3[0], rows, 1, 0, p["gdn_main"], p["gdn_small"], PROJ_TILE)
    u3 = pm.reshape(steps, nb, GDN_MAIN)
    act3 = _conv_steps(u3, 0, 0, GDN_CONV_DIM, tok(cv_gdn[0]), p["gdn_conv_w"], p["gdn_conv_b"], 2, QSCALE)
    o, gdn_s = _gdn_step(act3, ps.reshape(steps, nb, LANES), p["gdn_alog_row"], p["gdn_dtb_row"], p["gdn_consts"],
                         st_gdn[0])
    gdn_c = tok(u3[steps - (CONV_WIDTH - 1):, :, :GDN_CONV_DIM].astype(F32))
    x = _mixer_out(o, pm, 2, p["gdn_norm"], x, mod3[0], rows, p["gdn_w_out"], GDN_HEAD_DIM, False)
    x = _ffn(x, 0, mod3[0], rows, rows, p, None)

    pm, ps = _in_proj(x, p["norm_mix"][1], mod3[1], rows, 1, 0, p["ssm_main"], p["ssm_small"], PROJ_TILE)
    u3 = pm.reshape(steps, nb, SSM_MAIN)
    act3 = _conv_steps(u3, SSM_D_INNER // CONV_COLS, SSM_D_INNER // CONV_COLS, SSM_CONV_DIM, tok(cv_ssm[0]),
                       p["ssm_conv_w"], p["ssm_conv_b"], 0, 1.0)
    h0 = st_ssm[0].reshape(nb, SSM_PAIRS, LANES, SSM_STATE)
    y, ssm_h = _ssd_step(act3, ps.reshape(steps, nb, LANES), p["ssm_alog_row"], p["ssm_dtb_row"],
                         p["ssm_dskip_row"], p["ssm_consts"], h0)
    tail = u3[steps - (CONV_WIDTH - 1):].astype(F32)
    ssm_c = tok(jnp.concatenate([tail[..., :SSM_D_INNER], tail[..., 2 * SSM_D_INNER:]], axis=-1))
    x = _mixer_out(y, pm, 1, p["ssm_norm"], x, mod3[1], rows, p["ssm_w_out"], SSM_D_INNER // SSM_GROUPS, True)
    _, y_out = _ffn(x, 1, mod3[1], rows, rows, p, p["norm_final"])

    return (tok(y_out.reshape(steps, nb, D_MODEL)), gdn_s[None], gdn_c[None],
            ssm_h.reshape(nb, SSM_HEADS, SSM_HEAD_DIM, SSM_STATE)[None], ssm_c[None])


def kernel(x_prompt, x_sample, c_prompt, c_sample, state_gdn, state_gdn_conv, state_ssm, state_ssm_conv, w_mod, b_mod,
           norm_mix, norm_ffn, norm_final, gdn_w_in, gdn_conv_w, gdn_a_log, gdn_dt_bias, gdn_norm, gdn_w_out, ssm_w_in,
           ssm_conv_w, ssm_conv_b, ssm_a_log, ssm_dt_bias, ssm_d, ssm_norm, ssm_w_out, ffn_w_gate_up, ffn_w_down):
    p = _stage_params(w_mod, b_mod, norm_mix, norm_ffn, norm_final, gdn_w_in, gdn_conv_w, gdn_a_log, gdn_dt_bias,
                      gdn_norm, gdn_w_out, ssm_w_in, ssm_conv_w, ssm_conv_b, ssm_a_log, ssm_dt_bias, ssm_d, ssm_norm,
                      ssm_w_out, ffn_w_gate_up, ffn_w_down)
    y_p, gs_p, gc_p, ss_p, sc_p = _trunk_seq(x_prompt, c_prompt, p)
    y_s, gs_s, gc_s, ss_s, sc_s = _trunk_step(x_sample, c_sample, state_gdn, state_gdn_conv, state_ssm,
                                              state_ssm_conv, p)
    return (y_p, y_s, gs_p, gc_p, ss_p, sc_p, gs_s, gc_s, ss_s, sc_s)
```

```python
import functools

import numpy as np
import jax
import jax.numpy as jnp
from jax import lax
from jax.experimental import pallas as pl
from jax.experimental.pallas import tpu as pltpu

F32 = jnp.float32
BF16 = jnp.bfloat16

D_MODEL = 1024
EPS = 1e-6
CONV_WIDTH = 4
CHUNK = 64
LANES = 128
HALF = LANES // 2

GDN_QK_HEADS = 8
GDN_V_HEADS = 16
GDN_HEAD_DIM = 128
GDN_QK_DIM = GDN_QK_HEADS * GDN_HEAD_DIM
GDN_V_DIM = GDN_V_HEADS * GDN_HEAD_DIM
GDN_CONV_DIM = 2 * GDN_QK_DIM + GDN_V_DIM
GDN_MAIN = GDN_CONV_DIM + GDN_V_DIM

SSM_D_INNER = 2 * D_MODEL
SSM_HEAD_DIM = 64
SSM_HEADS = SSM_D_INNER // SSM_HEAD_DIM
SSM_GROUPS = 4
SSM_STATE = 128
SSM_BC = SSM_GROUPS * SSM_STATE
SSM_CONV_DIM = SSM_D_INNER + 2 * SSM_BC
SSM_MAIN = SSM_D_INNER + SSM_CONV_DIM

FFN_HIDDEN = 2816

VMEM_LIMIT = 56 * 1024 * 1024
NEG_BIG = -1e30


def _cparams(*sem):
    return pltpu.CompilerParams(dimension_semantics=sem, vmem_limit_bytes=VMEM_LIMIT)


def _silu(x):
    hx = 0.5 * x
    return hx + hx * jnp.tanh(hx)


def _softplus(x):
    return jnp.maximum(x, 0.0) + jnp.log1p(jnp.exp(-jnp.abs(x)))


def _dot(a, b):
    return jnp.dot(a, b, preferred_element_type=F32)


def _dot_nt(a, b):
    return lax.dot_general(a, b, (((1,), (1,)), ((), ())), preferred_element_type=F32)


def _tile_rows(v, rep):
    return v if rep == 1 else jnp.concatenate([v] * rep, axis=0)


def _mod_kernel(c_ref, w_ref, b_ref, o_ref):
    cs = _silu(c_ref[...]).astype(BF16)
    o_ref[...] = _dot(cs, w_ref[...]) + b_ref[...]


def _modulation(c, w_mod_bf, b_mod):
    depth, _, n = w_mod_bf.shape
    bc = c.shape[0]
    tn = 1536
    return pl.pallas_call(
        _mod_kernel,
        grid=(depth, n // tn),
        in_specs=[
            pl.BlockSpec((bc, D_MODEL), lambda l, j: (0, 0)),
            pl.BlockSpec((None, D_MODEL, tn), lambda l, j: (l, 0, j)),
            pl.BlockSpec((None, 1, tn), lambda l, j: (l, 0, j)),
        ],
        out_specs=pl.BlockSpec((None, bc, tn), lambda l, j: (l, 0, j)),
        out_shape=jax.ShapeDtypeStruct((depth, bc, n), F32),
        compiler_params=_cparams("parallel", "parallel"),
        name="adaln_mod",
    )(c, w_mod_bf, b_mod.reshape(depth, 1, n))


def _norm_mod(x, nw, sc, sh, rep):
    y = x * lax.rsqrt(jnp.mean(x * x, axis=-1, keepdims=True) + EPS) * nw
    return y * (1.0 + _tile_rows(sc, rep)) + _tile_rows(sh, rep)


def _in_proj_kernel(x_ref, nw_ref, sc_ref, sh_ref, w_ref, w2_ref, o_ref, o2_ref, h_ref, *, rep):
    @pl.when(pl.program_id(1) == 0)
    def _():
        h = _norm_mod(x_ref[...], nw_ref[...], sc_ref[...], sh_ref[...], rep).astype(BF16)
        h_ref[...] = h
        o2_ref[...] = _dot(h, w2_ref[...])

    o_ref[...] = _dot(h_ref[...], w_ref[...]).astype(o_ref.dtype)


def _ffn_up_kernel(x_ref, nw_ref, sc_ref, sh_ref, wg_ref, wu_ref, o_ref, h_ref, *, rep):
    @pl.when(pl.program_id(1) == 0)
    def _():
        h_ref[...] = _norm_mod(x_ref[...], nw_ref[...], sc_ref[...], sh_ref[...], rep).astype(BF16)

    h = h_ref[...]
    o_ref[...] = (_silu(_dot(h, wg_ref[...])) * _dot(h, wu_ref[...])).astype(BF16)


class _Rows:
    def __init__(self, m, tm, group_rows, mod_rows):
        assert m % tm == 0
        self.m, self.tm = m, tm
        if mod_rows == 1:
            assert group_rows % tm == 0
            self.rep = 1
            self.gmap = lambda i: (i * tm) // group_rows
        else:
            assert tm % mod_rows == 0
            self.rep = tm // mod_rows
            self.gmap = lambda i: 0
        self.mod_rows = mod_rows

    def mod_spec(self, col_block, with_j):
        if with_j:
            return pl.BlockSpec((None, self.mod_rows, D_MODEL), lambda i, j: (self.gmap(i), 0, col_block))
        return pl.BlockSpec((None, self.mod_rows, D_MODEL), lambda i: (self.gmap(i), 0, col_block))


def _in_proj(x, nw, mod3, rows, sc_blk, sh_blk, w, w2, tn):
    m, tm = rows.m, rows.tm
    n = w.shape[1]
    assert n % tn == 0
    return pl.pallas_call(
        functools.partial(_in_proj_kernel, rep=rows.rep),
        grid=(m // tm, n // tn),
        in_specs=[
            pl.BlockSpec((tm, D_MODEL), lambda i, j: (i, 0)),
            pl.BlockSpec((1, D_MODEL), lambda i, j: (0, 0)),
            rows.mod_spec(sc_blk, True),
            rows.mod_spec(sh_blk, True),
            pl.BlockSpec((D_MODEL, tn), lambda i, j: (0, j)),
            pl.BlockSpec((D_MODEL, LANES), lambda i, j: (0, 0)),
        ],
        out_specs=[
            pl.BlockSpec((tm, tn), lambda i, j: (i, j)),
            pl.BlockSpec((tm, LANES), lambda i, j: (i, 0)),
        ],
        out_shape=[jax.ShapeDtypeStruct((m, n), BF16), jax.ShapeDtypeStruct((m, LANES), F32)],
        scratch_shapes=[pltpu.VMEM((tm, D_MODEL), BF16)],
        compiler_params=_cparams("parallel", "arbitrary"),
        name="in_proj",
    )(x, nw.reshape(1, D_MODEL), mod3, mod3, w, w2)


def _ffn_up(x, nw, mod3, rows, wg, wu, th):
    m, tm = rows.m, rows.tm
    assert FFN_HIDDEN % th == 0
    return pl.pallas_call(
        functools.partial(_ffn_up_kernel, rep=rows.rep),
        grid=(m // tm, FFN_HIDDEN // th),
        in_specs=[
            pl.BlockSpec((tm, D_MODEL), lambda i, j: (i, 0)),
            pl.BlockSpec((1, D_MODEL), lambda i, j: (0, 0)),
            rows.mod_spec(4, True),
            rows.mod_spec(3, True),
            pl.BlockSpec((D_MODEL, th), lambda i, j: (0, j)),
            pl.BlockSpec((D_MODEL, th), lambda i, j: (0, j)),
        ],
        out_specs=pl.BlockSpec((tm, th), lambda i, j: (i, j)),
        out_shape=jax.ShapeDtypeStruct((m, FFN_HIDDEN), BF16),
        scratch_shapes=[pltpu.VMEM((tm, D_MODEL), BF16)],
        compiler_params=_cparams("parallel", "arbitrary"),
        name="ffn_up",
    )(x, nw.reshape(1, D_MODEL), mod3, mod3, wg, wu)


def _resid_store(acc, x_ref, gt_ref, o_ref, fnw_ref, y_ref, rep):
    xn = x_ref[...] + _tile_rows(gt_ref[...], rep) * acc
    o_ref[...] = xn
    if y_ref is not None:
        y_ref[...] = xn * lax.rsqrt(jnp.mean(xn * xn, axis=-1, keepdims=True) + EPS) * fnw_ref[...]


def _ffn_down_kernel(a_ref, x_ref, gt_ref, w_ref, *rest, rep, final):
    if final:
        fnw_ref, o_ref, y_ref = rest
    else:
        (o_ref,), fnw_ref, y_ref = rest, None, None
    _resid_store(_dot(a_ref[...], w_ref[...]), x_ref, gt_ref, o_ref, fnw_ref, y_ref, rep)


def _ffn_down(act, x, mod3, rows, w, fnw):
    m, tm = rows.m, rows.tm
    final = fnw is not None
    in_specs = [
        pl.BlockSpec((tm, FFN_HIDDEN), lambda i: (i, 0)),
        pl.BlockSpec((tm, D_MODEL), lambda i: (i, 0)),
        rows.mod_spec(5, False),
        pl.BlockSpec((FFN_HIDDEN, D_MODEL), lambda i: (0, 0)),
    ]
    args = [act, x, mod3, w]
    row_spec = pl.BlockSpec((tm, D_MODEL), lambda i: (i, 0))
    out_shape = jax.ShapeDtypeStruct((m, D_MODEL), F32)
    if final:
        in_specs.append(pl.BlockSpec((1, D_MODEL), lambda i: (0, 0)))
        args.append(fnw.reshape(1, D_MODEL))
        out_specs, out_shapes = [row_spec, row_spec], [out_shape, out_shape]
    else:
        out_specs, out_shapes = row_spec, out_shape
    return pl.pallas_call(
        functools.partial(_ffn_down_kernel, rep=rows.rep, final=final),
        grid=(m // tm,),
        in_specs=in_specs,
        out_specs=out_specs,
        out_shape=out_shapes,
        compiler_params=_cparams("parallel"),
        name="ffn_down",
    )(*args)


def _mixer_out_kernel(y_ref, z_ref, nw_ref, x_ref, gt_ref, w_ref, o_ref, a_ref, *, rep, group, gate_first):
    width = y_ref.shape[1]
    for s in range(0, width, group):
        y = y_ref[:, s:s + group]
        gate = _silu(z_ref[:, s:s + group].astype(F32))
        if gate_first:
            y = y * gate
        y = y * lax.rsqrt(jnp.mean(y * y, axis=-1, keepdims=True) + EPS) * nw_ref[:, s:s + group]
        if not gate_first:
            y = y * gate
        a_ref[:, s:s + group] = y.astype(BF16)
    _resid_store(_dot(a_ref[...], w_ref[...]), x_ref, gt_ref, o_ref, None, None, rep)


def _mixer_out(y, zsrc, z_blk, nw_full, x, mod3, rows, w, group, gate_first):
    m, tm = rows.m, rows.tm
    width = y.shape[1]
    return pl.pallas_call(
        functools.partial(_mixer_out_kernel, rep=rows.rep, group=group, gate_first=gate_first),
        grid=(m // tm,),
        in_specs=[
            pl.BlockSpec((tm, width), lambda i: (i, 0)),
            pl.BlockSpec((tm, width), lambda i: (i, z_blk)),
            pl.BlockSpec((1, width), lambda i: (0, 0)),
            pl.BlockSpec((tm, D_MODEL), lambda i: (i, 0)),
            rows.mod_spec(2, False),
            pl.BlockSpec((width, D_MODEL), lambda i: (0, 0)),
        ],
        out_specs=pl.BlockSpec((tm, D_MODEL), lambda i: (i, 0)),
        out_shape=jax.ShapeDtypeStruct((m, D_MODEL), F32),
        scratch_shapes=[pltpu.VMEM((tm, width), BF16)],
        compiler_params=_cparams("parallel"),
        name="mixer_out",
    )(y, zsrc, nw_full.reshape(1, width), x, mod3, w)


CONV_COLS = 1024


def _post_conv(acc, o_ref, cb, n_l2, qscale):
    y = _silu(acc)
    if n_l2 == 0:
        o_ref[...] = y
        return

    @pl.when(cb < n_l2)
    def _():
        scale = jnp.where(cb == 0, qscale, 1.0).astype(F32)
        for s in range(0, CONV_COLS, GDN_HEAD_DIM):
            yh = y[:, s:s + GDN_HEAD_DIM]
            o_ref[:, s:s + GDN_HEAD_DIM] = yh * lax.rsqrt(jnp.sum(yh * yh, axis=-1, keepdims=True) + EPS) * scale

    @pl.when(cb >= n_l2)
    def _():
        o_ref[...] = y


BHALO = 16
CONV_GROUP = 2 * LANES


def _shift_matrix():
    s = np.zeros(((CONV_WIDTH - 1) * CHUNK, BHALO + CHUNK), np.float32)
    for tap in range(CONV_WIDTH - 1):
        for r in range(CHUNK):
            s[tap * CHUNK + r, BHALO - (CONV_WIDTH - 1) + tap + r] = 1.0
    return jnp.asarray(s, BF16)


def _conv_block(raw_refs, shift_ref, w_ref, b_ref, ext_ref, act_ref, seq_start, l2_cols, q_cols, qscale):
    width = ext_ref.shape[1]
    if seq_start:
        ext_ref[0:BHALO, :] = jnp.zeros((BHALO, width), BF16)
    else:
        ext_ref[0:BHALO, :] = ext_ref[CHUNK:CHUNK + BHALO, :]

    off = 0
    for ref in raw_refs:
        ext_ref[BHALO:BHALO + CHUNK, off:off + ref.shape[1]] = ref[...]
        off += ref.shape[1]

    def lane_group(s):
        cols = slice(s, s + CONV_GROUP)
        sh = _dot(shift_ref[...], ext_ref[:, cols])
        acc = b_ref[:, cols] + w_ref[CONV_WIDTH - 1:CONV_WIDTH, cols] * ext_ref[BHALO:BHALO + CHUNK, cols].astype(F32)
        for tap in range(CONV_WIDTH - 1):
            acc = acc + w_ref[tap:tap + 1, cols] * sh[tap * CHUNK:(tap + 1) * CHUNK]
        y = _silu(acc)
        for h in range(s, s + CONV_GROUP, LANES):
            yh = y[:, h - s:h - s + LANES]
            if h < l2_cols:
                yh = yh * lax.rsqrt(jnp.sum(yh * yh, axis=-1, keepdims=True) + EPS)
                if h < q_cols:
                    yh = yh * qscale
            act_ref[:, h:h + LANES] = yh

    return [functools.partial(lane_group, s) for s in range(0, width, CONV_GROUP)]


class _Filler:
    def __init__(self, tasks, shares):
        self.tasks, self.per = list(tasks), -(-len(tasks) // shares)

    def emit(self):
        for task in self.tasks[:self.per]:
            task()
        self.tasks = self.tasks[self.per:]

    def flush(self):
        for task in self.tasks:
            task()
        self.tasks = []


_NO_FILL = _Filler([], 1)


def _conv_steps_kernel(u_ref, hist_ref, w_ref, b_ref, o_ref, *, steps, n_l2, qscale):
    cb = pl.program_id(0)
    ext = [hist_ref[i] for i in range(CONV_WIDTH - 1)] + [u_ref[i].astype(F32) for i in range(steps)]
    for t in range(steps):
        acc = b_ref[...] + w_ref[0:1, :] * ext[t]
        for tap in range(1, CONV_WIDTH):
            acc = acc + w_ref[tap:tap + 1, :] * ext[t + tap]
        _post_conv(acc, o_ref.at[t], cb, n_l2, qscale)


def _conv_steps(u3, skip_at, skip, n_cols, hist3, conv_w, conv_b, n_l2, qscale):
    steps, nb, _ = u3.shape
    return pl.pallas_call(
        functools.partial(_conv_steps_kernel, steps=steps, n_l2=n_l2, qscale=qscale),
        grid=(n_cols // CONV_COLS,),
        in_specs=[
            pl.BlockSpec((steps, nb, CONV_COLS), lambda c: (0, 0, c + skip * (c >= skip_at))),
            pl.BlockSpec((CONV_WIDTH - 1, nb, CONV_COLS), lambda c: (0, 0, c)),
            pl.BlockSpec((CONV_WIDTH, CONV_COLS), lambda c: (0, c)),
            pl.BlockSpec((1, CONV_COLS), lambda c: (0, c)),
        ],
        out_specs=pl.BlockSpec((steps, nb, CONV_COLS), lambda c: (0, 0, c)),
        out_shape=jax.ShapeDtypeStruct((steps, nb, n_cols), F32),
        compiler_params=_cparams("parallel"),
        name="conv_steps",
    )(u3, hist3, conv_w, conv_b.reshape(1, n_cols))


def _split3(x):
    hi = x.astype(BF16)
    r = x - hi.astype(F32)
    mid = r.astype(BF16)
    lo = (r - mid.astype(F32)).astype(BF16)
    return hi, mid, lo


def _cat3(x, axis):
    return jnp.concatenate(_split3(x), axis=axis)


def _pad_t(tile):
    return jnp.concatenate([tile, jnp.zeros_like(tile)], axis=0).T


def _pair_rows(t, n):
    return t[0:n] + pltpu.roll(t[n:2 * n], HALF, 1)


def _iotas(shape):
    return lax.broadcasted_iota(jnp.int32, shape, 0), lax.broadcasted_iota(jnp.int32, shape, 1)


def _pad_rows(a):
    if a.shape[0] == CHUNK:
        return a
    return jnp.concatenate([a, jnp.zeros((CHUNK - a.shape[0], a.shape[1]), a.dtype)], axis=0)


def _split2(x):
    hi = x.astype(BF16)
    return hi, (x - hi.astype(F32)).astype(BF16)


def _block_diag(pair_bf, bd_ones):
    pair_bf = _pad_rows(pair_bf)
    return jnp.concatenate([pair_bf, pair_bf], axis=0) * bd_ones


def _mm_pair(lhs_parts, rhs_parts, bd_ones):
    (lh, ll), (rh, rl) = lhs_parts, rhs_parts
    rh_bd = _block_diag(rh, bd_ones)
    return _dot(jnp.concatenate([lh, lh, ll], axis=1),
                jnp.concatenate([rh_bd, _block_diag(rl, bd_ones), rh_bd], axis=0))


def _mm_hl(lhs, rhs_bf):
    lh = lhs.astype(BF16)
    ll = (lhs - lh.astype(F32)).astype(BF16)
    return _dot(jnp.concatenate([lh, ll], axis=1), jnp.concatenate([rhs_bf, rhs_bf], axis=0))


def _unit_lower_inverse(xs, levels, bd_ones, eye2, fill):
    ps = [eye2 + x for x in xs]
    if levels <= 1:
        return ps
    rows = xs[0].shape[0]
    stack = lambda a, b: tuple(jnp.concatenate([s, t], axis=0) for s, t in zip(a, b))
    ysp = [_split2(x) for x in xs]
    ys = [_mm_pair(yp, yp, bd_ones) for yp in ysp]
    fill.emit()
    for _ in range(levels - 2):
        ysp = [_split2(y) for y in ys]
        rs = [_mm_pair(stack(yp, _split2(p)), yp, bd_ones) for yp, p in zip(ysp, ps)]
        fill.emit()
        ys = [r[0:rows] for r in rs]
        ps = [p + r[rows:2 * rows] for p, r in zip(ps, rs)]
    return [p + _mm_pair(_split2(p), _split2(y), bd_ones) for p, y in zip(ps, ys)]


def _two_blocks(a, b):
    a_bf, b_bf = _pad_rows(a).astype(BF16), _pad_rows(b).astype(BF16)
    z = jnp.zeros_like(a_bf)
    return jnp.concatenate([jnp.concatenate([a_bf, z], axis=1), jnp.concatenate([z, b_bf], axis=1)], axis=0)


def _recurrence_consts(n_pairs, chan_even0, chan_odd0):
    tril = np.tril(np.ones((CHUNK, CHUNK), np.float32))
    tril3 = np.concatenate([tril] * 3, axis=1)
    upper = np.zeros((LANES, 2 * LANES), np.float32)
    upper[:CHUNK, :CHUNK] = tril.T
    upper[:CHUNK, LANES:] = 1.0
    upper3 = np.concatenate([upper] * 3, axis=0)
    e = np.zeros((LANES, n_pairs * LANES), np.float32)
    for p in range(n_pairs):
        e[chan_even0 + p, p * LANES:p * LANES + HALF] = 1.0
        e[chan_odd0 + p, p * LANES + HALF:(p + 1) * LANES] = 1.0
    e3 = np.concatenate([e] * 3, axis=0)
    return jnp.asarray(tril3, BF16), jnp.asarray(upper3, BF16), jnp.asarray(e3, BF16)


def _levels(lr):
    return max(1, int(np.ceil(np.log2(lr))))


STEP_ROWS = 16


GDN_PAIRS = GDN_V_HEADS // 2
GDN_PAIR_BATCH = 8
GDN_FILL_SHARES = 11 * (GDN_PAIRS // GDN_PAIR_BATCH)


def _gdn_gates(raw, alog_row, dtb_row, lr):
    row, lane = _iotas((CHUNK, LANES))
    g = -jnp.exp(alog_row) * _softplus(raw + dtb_row)
    beta = jax.nn.sigmoid(raw)
    gt = jnp.where(lane < GDN_V_HEADS, g, jnp.where(lane < 2 * GDN_V_HEADS, beta, 0.0))
    if lr < CHUNK:
        gt = jnp.where(row < lr, gt, 0.0)
    return gt


def _gdn_block(act_ref, gt, s_in, s_out, o_ref, tril3_ref, upper3_ref, e3_ref, lr, fill=_NO_FILL):
    np_ = GDN_PAIRS
    rows = act_ref.shape[0]
    row, lane = _iotas((rows, LANES))
    jl = jnp.bitwise_and(lane, HALF - 1)
    left = lane < HALF
    causal = row >= jl
    strict = row > jl
    eye2 = jnp.where(row == jl, 1.0, 0.0)
    r2, l2 = _iotas((LANES, LANES))
    bdmask = jnp.right_shift(r2, 6) == jnp.right_shift(l2, 6)
    _, lane_p = _iotas((np_, LANES))

    t1 = _pad_t(gt)
    cum_ext = _dot(_cat3(t1[0:2 * np_], 1), upper3_ref[...])
    cum_t, last_b = cum_ext[:, :LANES], cum_ext[:, LANES:]
    cum_rp = _pair_rows(cum_t, np_)
    beta_rp = _pair_rows(t1[2 * np_:4 * np_], np_)
    last_rp = jnp.where(lane_p < HALF, last_b[0:np_], last_b[np_:2 * np_])
    ecum_rp = jnp.exp(cum_rp)
    kdec_rp = jnp.exp(last_rp - cum_rp) * beta_rp
    elast = jnp.exp(last_b)
    cum = _dot(tril3_ref[...], _cat3(gt, 0))
    col_all = _dot(_cat3(cum[0:rows], 1), e3_ref[...])

    bd_ones = jnp.where(bdmask, 1.0, 0.0).astype(BF16)
    sl = lambda a, i: a[:, i * LANES:(i + 1) * LANES]

    def run(pairs):
        heads = [2 * p + hh for p in pairs for hh in range(2)]
        q = {p: sl(act_ref, p) for p in pairs}
        k = {p: _pad_rows(sl(act_ref, GDN_QK_HEADS + p)) for p in pairs}
        v = {h: sl(act_ref, 2 * GDN_QK_HEADS + h) for h in heads}
        kb = {p: k[p].astype(BF16) for p in pairs}
        qb = {p: q[p].astype(BF16) for p in pairs}
        gq = {p: _dot_nt(jnp.concatenate([kb[p][0:rows], qb[p]], axis=0), jnp.concatenate([kb[p], kb[p]], axis=0))
              for p in pairs}
        fill.emit()
        colc = {p: sl(col_all, p) for p in pairs}
        base = {p: jnp.exp(jnp.where(causal, colc[p] - cum_rp[p:p + 1], NEG_BIG)) * beta_rp[p:p + 1] for p in pairs}
        x = [jnp.where(strict, -(gq[p][0:rows] * base[p]), 0.0) for p in pairs]
        qkd = {p: (gq[p][rows:2 * rows] * base[p]).astype(BF16) for p in pairs}
        minv = dict(zip(pairs, _unit_lower_inverse(x, _levels(lr), bd_ones, eye2, fill)))
        fill.emit()
        u = {p: jnp.concatenate([v[2 * p], v[2 * p + 1]], axis=1)
             + _mm_hl(minv[p] - eye2, _two_blocks(v[2 * p], v[2 * p + 1])) for p in pairs}
        fill.emit()
        w = {p: _mm_hl(minv[p] * ecum_rp[p:p + 1], _two_blocks(kb[p], kb[p])) for p in pairs}
        fill.emit()
        s_old = {h: s_in[h] for h in heads}
        r = {h: _dot(jnp.concatenate([sl(w[h // 2], h % 2).astype(BF16), qb[h // 2]], axis=0),
                     s_old[h].astype(BF16)) for h in heads}
        fill.emit()
        delta = {h: sl(u[h // 2], h % 2) - r[h][0:rows] for h in heads}
        bd_delta = {p: _two_blocks(delta[2 * p], delta[2 * p + 1]) for p in pairs}
        o_intra = {p: _dot(qkd[p], bd_delta[p]) for p in pairs}
        fill.emit()
        ds = {p: _dot((jnp.concatenate([k[p], k[p]], axis=0).T * kdec_rp[p:p + 1]).astype(BF16), bd_delta[p])
              for p in pairs}
        for p in pairs:
            ecol = jnp.exp(colc[p])
            ecol_r = pltpu.roll(ecol, HALF, 1)
            efull = (jnp.where(left, ecol, ecol_r), jnp.where(left, ecol_r, ecol))
            for hh in range(2):
                h = 2 * p + hh
                o_ref[:, h * LANES:(h + 1) * LANES] = efull[hh] * r[h][rows:2 * rows] + sl(o_intra[p], hh)
                e_h = elast[hh * np_ + p:hh * np_ + p + 1]
                s_out[h] = e_h * s_old[h] + sl(ds[p], hh)

    for b0 in range(0, np_, GDN_PAIR_BATCH):
        run(list(range(b0, b0 + GDN_PAIR_BATCH)))
    fill.flush()


def _gdn_seq_kernel(raw0_ref, rawn_ref, graw_ref, shift_ref, cw_ref, cb_ref, alog_ref, dtb_ref, tril3_ref, upper3_ref,
                    e3_ref, o_ref, s_ref, ext_ref, act_ref):
    n = pl.program_id(1)
    conv = functools.partial(_conv_block, shift_ref=shift_ref, w_ref=cw_ref, b_ref=cb_ref, ext_ref=ext_ref,
                             l2_cols=2 * GDN_QK_DIM, q_cols=GDN_QK_DIM, qscale=QSCALE)

    @pl.when(n == 0)
    def _():
        s_ref[...] = jnp.zeros(s_ref.shape, F32)
        _Filler(conv([raw0_ref], act_ref=act_ref.at[0], seq_start=True), 1).flush()

    @pl.when(n > 0)
    def _():
        act_ref[0] = act_ref[1]

    gt = _gdn_gates(graw_ref[...], alog_ref[...], dtb_ref[...], CHUNK)
    fill = _Filler(conv([rawn_ref], act_ref=act_ref.at[1], seq_start=False), GDN_FILL_SHARES)
    _gdn_block(act_ref.at[0], gt, s_ref, s_ref, o_ref, tril3_ref, upper3_ref, e3_ref, CHUNK, fill)


def _const_specs(consts, ngrid):
    zero = (lambda *_: (0, 0))
    return [pl.BlockSpec(c.shape, zero) for c in consts]


def _gdn_seq(pm, graw, conv_w, conv_b, alog_row, dtb_row, consts, nseq, seq_len):
    nc = seq_len // CHUNK
    last = nseq * nc - 1
    rowspec = pl.BlockSpec((1, LANES), lambda b, n: (0, 0))
    return pl.pallas_call(
        _gdn_seq_kernel,
        grid=(nseq, nc),
        in_specs=[
            pl.BlockSpec((CHUNK, GDN_CONV_DIM), lambda b, n: (b * nc, 0)),
            pl.BlockSpec((CHUNK, GDN_CONV_DIM), lambda b, n: (jnp.minimum(b * nc + n + 1, last), 0)),
            pl.BlockSpec((CHUNK, LANES), lambda b, n: (b * nc + n, 0)),
            pl.BlockSpec(((CONV_WIDTH - 1) * CHUNK, BHALO + CHUNK), lambda b, n: (0, 0)),
            pl.BlockSpec((CONV_WIDTH, GDN_CONV_DIM), lambda b, n: (0, 0)),
            pl.BlockSpec((1, GDN_CONV_DIM), lambda b, n: (0, 0)),
            rowspec, rowspec,
        ] + _const_specs(consts, 2),
        out_specs=[
            pl.BlockSpec((CHUNK, GDN_V_DIM), lambda b, n: (b * nc + n, 0)),
            pl.BlockSpec((None, GDN_V_HEADS, GDN_HEAD_DIM, GDN_HEAD_DIM), lambda b, n: (b, 0, 0, 0)),
        ],
        out_shape=[
            jax.ShapeDtypeStruct((nseq * seq_len, GDN_V_DIM), F32),
            jax.ShapeDtypeStruct((nseq, GDN_V_HEADS, GDN_HEAD_DIM, GDN_HEAD_DIM), F32),
        ],
        scratch_shapes=[
            pltpu.VMEM((CHUNK + BHALO, GDN_CONV_DIM), BF16),
            pltpu.VMEM((2, CHUNK, GDN_CONV_DIM), F32),
        ],
        compiler_params=_cparams("parallel", "arbitrary"),
        name="gdn_seq",
    )(pm, pm, graw, _shift_matrix(), conv_w, conv_b.reshape(1, GDN_CONV_DIM), alog_row, dtb_row, *consts)


def _load_padded(src_ref, pad_ref, lr):
    pad_ref[...] = jnp.zeros(pad_ref.shape, F32)
    for t in range(lr):
        pad_ref[t:t + 1, :] = src_ref[t]


def _gdn_step_kernel(act_ref, graw_ref, alog_ref, dtb_ref, tril3_ref, upper3_ref, e3_ref, s0_ref,
                     o_ref, s_ref, apad_ref, gpad_ref, opad_ref, *, lr):
    _load_padded(act_ref, apad_ref, lr)
    _load_padded(graw_ref, gpad_ref, lr)
    gt = _gdn_gates(gpad_ref[...], alog_ref[...], dtb_ref[...], lr)
    _gdn_block(apad_ref, gt, s0_ref, s_ref, opad_ref, tril3_ref, upper3_ref, e3_ref, lr)
    for t in range(lr):
        o_ref[t] = opad_ref[t:t + 1, :]


def _step_spec(steps, width):
    return pl.BlockSpec((steps, None, 1, width), lambda b: (0, b, 0, 0))


def _gdn_step(act3, graw3, alog_row, dtb_row, consts, s0):
    steps, nb, _ = act3.shape
    rowspec = pl.BlockSpec((1, LANES), lambda b: (0, 0))
    sspec = pl.BlockSpec((None, GDN_V_HEADS, GDN_HEAD_DIM, GDN_HEAD_DIM), lambda b: (b, 0, 0, 0))
    o, s = pl.pallas_call(
        functools.partial(_gdn_step_kernel, lr=steps),
        grid=(nb,),
        in_specs=[_step_spec(steps, GDN_CONV_DIM), _step_spec(steps, LANES), rowspec, rowspec]
        + _const_specs(consts, 1) + [sspec],
        out_specs=[_step_spec(steps, GDN_V_DIM), sspec],
        out_shape=[
            jax.ShapeDtypeStruct((steps, nb, 1, GDN_V_DIM), F32),
            jax.ShapeDtypeStruct(s0.shape, F32),
        ],
        scratch_shapes=[
            pltpu.VMEM((STEP_ROWS, GDN_CONV_DIM), F32),
            pltpu.VMEM((CHUNK, LANES), F32),
            pltpu.VMEM((STEP_ROWS, GDN_V_DIM), F32),
        ],
        compiler_params=_cparams("parallel"),
        name="gdn_step",
    )(act3.reshape(steps, nb, 1, GDN_CONV_DIM), graw3.reshape(steps, nb, 1, LANES), alog_row, dtb_row, *consts, s0)
    return o.reshape(steps * nb, GDN_V_DIM), s


SSM_PAIRS = SSM_HEADS // 2
PAIRS_PER_GROUP = SSM_PAIRS // SSM_GROUPS


def _ssm_gates(raw, alog_row, dtb_row, lr):
    row, lane = _iotas((CHUNK, LANES))
    dt = _softplus(raw + dtb_row)
    tile = jnp.where(lane < SSM_HEADS, dt, jnp.where(lane < 2 * SSM_HEADS, -jnp.exp(alog_row) * dt, 0.0))
    if lr < CHUNK:
        tile = jnp.where(row < lr, tile, 0.0)
    return tile


def _ssd_block(act_ref, tile, h_in, h_out, y_ref, dskip_ref, tril3_ref, upper3_ref, e3_ref, fill=_NO_FILL):
    np_ = SSM_PAIRS
    rows = act_ref.shape[0]
    row, lane = _iotas((rows, LANES))
    causal = row >= jnp.bitwise_and(lane, HALF - 1)
    r2, l2 = _iotas((LANES, LANES))
    bdmask = jnp.right_shift(r2, 6) == jnp.right_shift(l2, 6)
    top = r2 < HALF
    _, lane_p = _iotas((np_, LANES))

    t1 = _pad_t(tile)
    cum_ext = _dot(_cat3(t1[2 * np_:4 * np_], 1), upper3_ref[...])
    cum_t, last_b = cum_ext[:, :LANES], cum_ext[:, LANES:]
    cum_rp = _pair_rows(cum_t, np_)
    dt_rp = _pair_rows(t1[0:2 * np_], np_)
    last_rp = jnp.where(lane_p < HALF, last_b[0:np_], last_b[np_:2 * np_])
    coef_rp = jnp.exp(last_rp - cum_rp) * dt_rp
    elast = jnp.exp(last_b)
    cum = _dot(tril3_ref[...], _cat3(tile, 0))
    col_all = _dot(_cat3(cum[0:rows], 1), e3_ref[...])

    for g in range(SSM_GROUPS):
        bg = _pad_rows(act_ref[:, SSM_D_INNER + g * SSM_STATE:SSM_D_INNER + (g + 1) * SSM_STATE]).astype(BF16)
        cg = act_ref[:, SSM_D_INNER + SSM_BC + g * SSM_STATE:SSM_D_INNER + SSM_BC + (g + 1) * SSM_STATE].astype(BF16)
        bb = jnp.concatenate([bg, bg], axis=0)
        cb2 = _dot_nt(cg, bb)
        for pp in range(PAIRS_PER_GROUP):
            p = g * PAIRS_PER_GROUP + pp
            xp = act_ref[:, p * LANES:(p + 1) * LANES]
            colc = col_all[:, p * LANES:(p + 1) * LANES]
            dec = jnp.exp(jnp.where(causal, colc - cum_rp[p:p + 1], NEG_BIG))
            lm = cb2 * dec * dt_rp[p:p + 1]
            xpad = _pad_rows(xp)
            x2 = jnp.concatenate([xpad, xpad], axis=0)
            y_diag = _dot(lm.astype(BF16), jnp.where(bdmask, x2, 0.0).astype(BF16))
            hp = h_in[p]
            y_off = jnp.exp(colc) * _dot_nt(cg, hp.astype(BF16))
            y_ref[:, p * LANES:(p + 1) * LANES] = y_diag + y_off + dskip_ref[:, p * LANES:(p + 1) * LANES] * xp
            lhs = jnp.where(bdmask, x2.T * coef_rp[p:p + 1], 0.0)
            dh = _dot(lhs.astype(BF16), bb)
            e_rows = jnp.where(top, elast[p:p + 1], elast[np_ + p:np_ + p + 1])
            h_out[p] = e_rows * hp + dh
            fill.emit()
    fill.flush()


def _ssd_seq_kernel(x0_ref, bc0_ref, xn_ref, bcn_ref, graw_ref, shift_ref, cw_ref, cb_ref, alog_ref, dtb_ref,
                    dskip_ref, tril3_ref, upper3_ref, e3_ref, y_ref, h_ref, ext_ref, act_ref):
    n = pl.program_id(1)
    conv = functools.partial(_conv_block, shift_ref=shift_ref, w_ref=cw_ref, b_ref=cb_ref, ext_ref=ext_ref,
                             l2_cols=0, q_cols=0, qscale=1.0)

    @pl.when(n == 0)
    def _():
        h_ref[...] = jnp.zeros(h_ref.shape, F32)
        _Filler(conv([x0_ref, bc0_ref], act_ref=act_ref.at[0], seq_start=True), 1).flush()

    @pl.when(n > 0)
    def _():
        act_ref[0] = act_ref[1]

    tile = _ssm_gates(graw_ref[...], alog_ref[...], dtb_ref[...], CHUNK)
    fill = _Filler(conv([xn_ref, bcn_ref], act_ref=act_ref.at[1], seq_start=False), SSM_PAIRS)
    _ssd_block(act_ref.at[0], tile, h_ref, h_ref, y_ref, dskip_ref, tril3_ref, upper3_ref, e3_ref, fill)


def _ssd_seq(pm, graw, conv_w, conv_b, alog_row, dtb_row, dskip_row, consts, nseq, seq_len):
    nc = seq_len // CHUNK
    last = nseq * nc - 1
    bc_blk = 2 * SSM_D_INNER // (2 * SSM_BC)
    nxt = lambda b, n: jnp.minimum(b * nc + n + 1, last)
    rowspec = pl.BlockSpec((1, LANES), lambda b, n: (0, 0))
    return pl.pallas_call(
        _ssd_seq_kernel,
        grid=(nseq, nc),
        in_specs=[
            pl.BlockSpec((CHUNK, SSM_D_INNER), lambda b, n: (b * nc, 0)),
            pl.BlockSpec((CHUNK, 2 * SSM_BC), lambda b, n: (b * nc, bc_blk)),
            pl.BlockSpec((CHUNK, SSM_D_INNER), lambda b, n: (nxt(b, n), 0)),
            pl.BlockSpec((CHUNK, 2 * SSM_BC), lambda b, n: (nxt(b, n), bc_blk)),
            pl.BlockSpec((CHUNK, LANES), lambda b, n: (b * nc + n, 0)),
            pl.BlockSpec(((CONV_WIDTH - 1) * CHUNK, BHALO + CHUNK), lambda b, n: (0, 0)),
            pl.BlockSpec((CONV_WIDTH, SSM_CONV_DIM), lambda b, n: (0, 0)),
            pl.BlockSpec((1, SSM_CONV_DIM), lambda b, n: (0, 0)),
            rowspec, rowspec,
            pl.BlockSpec((1, SSM_D_INNER), lambda b, n: (0, 0)),
        ] + _const_specs(consts, 2),
        out_specs=[
            pl.BlockSpec((CHUNK, SSM_D_INNER), lambda b, n: (b * nc + n, 0)),
            pl.BlockSpec((None, SSM_PAIRS, LANES, SSM_STATE), lambda b, n: (b, 0, 0, 0)),
        ],
        out_shape=[
            jax.ShapeDtypeStruct((nseq * seq_len, SSM_D_INNER), F32),
            jax.ShapeDtypeStruct((nseq, SSM_PAIRS, LANES, SSM_STATE), F32),
        ],
        scratch_shapes=[
            pltpu.VMEM((CHUNK + BHALO, SSM_CONV_DIM), BF16),
            pltpu.VMEM((2, CHUNK, SSM_CONV_DIM), F32),
        ],
        compiler_params=_cparams("parallel", "arbitrary"),
        name="ssd_seq",
    )(pm, pm, pm, pm, graw, _shift_matrix(), conv_w, conv_b.reshape(1, SSM_CONV_DIM), alog_row, dtb_row, dskip_row, *consts)


def _ssd_step_kernel(act_ref, graw_ref, alog_ref, dtb_ref, dskip_ref, tril3_ref, upper3_ref, e3_ref, h0_ref,
                     y_ref, h_ref, apad_ref, gpad_ref, ypad_ref, *, lr):
    _load_padded(act_ref, apad_ref, lr)
    _load_padded(graw_ref, gpad_ref, lr)
    tile = _ssm_gates(gpad_ref[...], alog_ref[...], dtb_ref[...], lr)
    _ssd_block(apad_ref, tile, h0_ref, h_ref, ypad_ref, dskip_ref, tril3_ref, upper3_ref, e3_ref)
    for t in range(lr):
        y_ref[t] = ypad_ref[t:t + 1, :]


def _ssd_step(act3, graw3, alog_row, dtb_row, dskip_row, consts, h0):
    steps, nb, _ = act3.shape
    rowspec = pl.BlockSpec((1, LANES), lambda b: (0, 0))
    hspec = pl.BlockSpec((None, SSM_PAIRS, LANES, SSM_STATE), lambda b: (b, 0, 0, 0))
    y, h = pl.pallas_call(
        functools.partial(_ssd_step_kernel, lr=steps),
        grid=(nb,),
        in_specs=[_step_spec(steps, SSM_CONV_DIM), _step_spec(steps, LANES), rowspec, rowspec,
                  pl.BlockSpec((1, SSM_D_INNER), lambda b: (0, 0))] + _const_specs(consts, 1) + [hspec],
        out_specs=[_step_spec(steps, SSM_D_INNER), hspec],
        out_shape=[
            jax.ShapeDtypeStruct((steps, nb, 1, SSM_D_INNER), F32),
            jax.ShapeDtypeStruct(h0.shape, F32),
        ],
        scratch_shapes=[
            pltpu.VMEM((STEP_ROWS, SSM_CONV_DIM), F32),
            pltpu.VMEM((CHUNK, LANES), F32),
            pltpu.VMEM((STEP_ROWS, SSM_D_INNER), F32),
        ],
        compiler_params=_cparams("parallel"),
        name="ssd_step",
    )(act3.reshape(steps, nb, 1, SSM_CONV_DIM), graw3.reshape(steps, nb, 1, LANES), alog_row, dtb_row, dskip_row,
      *consts, h0)
    return y.reshape(steps * nb, SSM_D_INNER), h


FFN_TILE = FFN_HIDDEN // 2
GDN_PROJ_TILE = GDN_MAIN // 3
SSM_PROJ_TILE = SSM_MAIN // 2


def _lane_row(pieces):
    row = jnp.zeros((1, LANES), F32)
    for off, vec in pieces:
        row = row.at[0, off:off + vec.shape[0]].set(vec.astype(F32))
    return row


def _stage_params(w_mod, b_mod, norm_mix, norm_ffn, norm_final, gdn_w_in, gdn_conv_w, gdn_a_log, gdn_dt_bias,
                  gdn_norm, gdn_w_out, ssm_w_in, ssm_conv_w, ssm_conv_b, ssm_a_log, ssm_dt_bias, ssm_d, ssm_norm,
                  ssm_w_out, ffn_w_gate_up, ffn_w_down):
    perm_g = np.concatenate([np.arange(0, GDN_V_HEADS, 2), np.arange(1, GDN_V_HEADS, 2)])
    perm_s = np.concatenate([np.arange(0, SSM_HEADS, 2), np.arange(1, SSM_HEADS, 2)])
    g_in, s_in = gdn_w_in[0], ssm_w_in[0]
    beta_cols = g_in[:, GDN_MAIN:GDN_MAIN + GDN_V_HEADS][:, perm_g]
    a_cols = g_in[:, GDN_MAIN + GDN_V_HEADS:GDN_MAIN + 2 * GDN_V_HEADS][:, perm_g]
    gdn_small = jnp.concatenate([a_cols, beta_cols, jnp.zeros((D_MODEL, LANES - 2 * GDN_V_HEADS), F32)], axis=1)
    dt_cols = s_in[:, SSM_MAIN:SSM_MAIN + SSM_HEADS][:, perm_s]
    ssm_small = jnp.concatenate([dt_cols, dt_cols, jnp.zeros((D_MODEL, LANES - 2 * SSM_HEADS), F32)], axis=1)
    return dict(
        w_mod=w_mod.astype(BF16), b_mod=b_mod, norm_mix=norm_mix, norm_ffn=norm_ffn, norm_final=norm_final,
        gdn_main=g_in[:, :GDN_MAIN].astype(BF16), gdn_small=gdn_small.astype(BF16),
        gdn_conv_w=gdn_conv_w[0], gdn_conv_b=jnp.zeros((GDN_CONV_DIM,), F32),
        gdn_alog_row=_lane_row([(0, gdn_a_log[0][perm_g])]), gdn_dtb_row=_lane_row([(0, gdn_dt_bias[0][perm_g])]),
        gdn_norm=jnp.tile(gdn_norm[0], GDN_V_HEADS), gdn_w_out=gdn_w_out[0].astype(BF16),
        gdn_consts=_recurrence_consts(GDN_PAIRS, 0, GDN_PAIRS),
        ssm_main=jnp.concatenate(
            [s_in[:, SSM_D_INNER:2 * SSM_D_INNER], s_in[:, :SSM_D_INNER], s_in[:, 2 * SSM_D_INNER:SSM_MAIN]],
            axis=1).astype(BF16),
        ssm_small=ssm_small.astype(BF16),
        ssm_conv_w=ssm_conv_w[0], ssm_conv_b=ssm_conv_b[0],
        ssm_alog_row=_lane_row([(SSM_HEADS, ssm_a_log[0][perm_s])]),
        ssm_dtb_row=_lane_row([(0, ssm_dt_bias[0][perm_s]), (SSM_HEADS, ssm_dt_bias[0][perm_s])]),
        ssm_dskip_row=jnp.repeat(ssm_d[0], SSM_HEAD_DIM).reshape(1, SSM_D_INNER),
        ssm_norm=ssm_norm[0], ssm_w_out=ssm_w_out[0].astype(BF16),
        ssm_consts=_recurrence_consts(SSM_PAIRS, SSM_HEADS, SSM_HEADS + SSM_PAIRS),
        wg=[ffn_w_gate_up[i][:, :FFN_HIDDEN].astype(BF16) for i in range(2)],
        wu=[ffn_w_gate_up[i][:, FFN_HIDDEN:].astype(BF16) for i in range(2)],
        wd=[ffn_w_down[i].astype(BF16) for i in range(2)],
    )


def _ffn(x, layer, mod3, rows_up, rows_down, p, final_w):
    act = _ffn_up(x, p["norm_ffn"][layer], mod3, rows_up, p["wg"][layer], p["wu"][layer], FFN_TILE)
    return _ffn_down(act, x, mod3, rows_down, p["wd"][layer], final_w)


QSCALE = GDN_HEAD_DIM ** -0.5


def _trunk_seq(x3, c, p):
    nseq, seq_len, _ = x3.shape
    m = nseq * seq_len
    x = x3.reshape(m, D_MODEL)
    mod = _modulation(c, p["w_mod"], p["b_mod"])
    mod3 = [mod[l].reshape(nseq, 1, 6 * D_MODEL) for l in range(2)]
    rows_a = _Rows(m, min(1024, seq_len), seq_len, 1)
    rows_b = _Rows(m, min(512, seq_len), seq_len, 1)

    pm, ps = _in_proj(x, p["norm_mix"][0], mod3[0], rows_a, 1, 0, p["gdn_main"], p["gdn_small"], GDN_PROJ_TILE)
    o, gdn_s = _gdn_seq(pm, ps, p["gdn_conv_w"], p["gdn_conv_b"], p["gdn_alog_row"], p["gdn_dtb_row"],
                        p["gdn_consts"], nseq, seq_len)
    tail = pm.reshape(nseq, seq_len, GDN_MAIN)[:, seq_len - (CONV_WIDTH - 1):].astype(F32)
    gdn_c = tail[..., :GDN_CONV_DIM]
    x = _mixer_out(o, pm, 2, p["gdn_norm"], x, mod3[0], rows_b, p["gdn_w_out"], GDN_HEAD_DIM, False)
    x = _ffn(x, 0, mod3[0], rows_a, rows_b, p, None)

    pm, ps = _in_proj(x, p["norm_mix"][1], mod3[1], rows_a, 1, 0, p["ssm_main"], p["ssm_small"], SSM_PROJ_TILE)
    y, ssm_h = _ssd_seq(pm, ps, p["ssm_conv_w"], p["ssm_conv_b"], p["ssm_alog_row"], p["ssm_dtb_row"],
                        p["ssm_dskip_row"], p["ssm_consts"], nseq, seq_len)
    tail = pm.reshape(nseq, seq_len, SSM_MAIN)[:, seq_len - (CONV_WIDTH - 1):].astype(F32)
    ssm_c = jnp.concatenate([tail[..., :SSM_D_INNER], tail[..., 2 * SSM_D_INNER:]], axis=-1)
    x = _mixer_out(y, pm, 1, p["ssm_norm"], x, mod3[1], rows_b, p["ssm_w_out"], SSM_D_INNER // SSM_GROUPS, True)
    _, y_out = _ffn(x, 1, mod3[1], rows_a, rows_b, p, p["norm_final"])

    return (y_out.reshape(nseq, seq_len, D_MODEL), gdn_s[None], gdn_c[None],
            ssm_h.reshape(nseq, SSM_HEADS, SSM_HEAD_DIM, SSM_STATE)[None], ssm_c[None])


def _trunk_step(x3, c, st_gdn, cv_gdn, st_ssm, cv_ssm, p):
    nb, steps, _ = x3.shape
    assert steps >= CONV_WIDTH - 1
    m = nb * steps
    x = jnp.transpose(x3, (1, 0, 2)).reshape(m, D_MODEL)
    mod = _modulation(c, p["w_mod"], p["b_mod"])
    mod3 = [mod[l].reshape(1, nb, 6 * D_MODEL) for l in range(2)]
    rows = _Rows(m, m, None, nb)
    tok = lambda a: jnp.transpose(a, (1, 0, 2))

    pm, ps = _in_proj(x, p["norm_mix"][0], mod3[0], rows, 1, 0, p["gdn_main"], p["gdn_small"], GDN_PROJ_TILE)
    u3 = pm.reshape(steps, nb, GDN_MAIN)
    act3 = _conv_steps(u3, 0, 0, GDN_CONV_DIM, tok(cv_gdn[0]), p["gdn_conv_w"], p["gdn_conv_b"], 2, QSCALE)
    o, gdn_s = _gdn_step(act3, ps.reshape(steps, nb, LANES), p["gdn_alog_row"], p["gdn_dtb_row"], p["gdn_consts"],
                         st_gdn[0])
    gdn_c = tok(u3[steps - (CONV_WIDTH - 1):, :, :GDN_CONV_DIM].astype(F32))
    x = _mixer_out(o, pm, 2, p["gdn_norm"], x, mod3[0], rows, p["gdn_w_out"], GDN_HEAD_DIM, False)
    x = _ffn(x, 0, mod3[0], rows, rows, p, None)

    pm, ps = _in_proj(x, p["norm_mix"][1], mod3[1], rows, 1, 0, p["ssm_main"], p["ssm_small"], SSM_PROJ_TILE)
    u3 = pm.reshape(steps, nb, SSM_MAIN)
    act3 = _conv_steps(u3, SSM_D_INNER // CONV_COLS, SSM_D_INNER // CONV_COLS, SSM_CONV_DIM, tok(cv_ssm[0]),
                       p["ssm_conv_w"], p["ssm_conv_b"], 0, 1.0)
    h0 = st_ssm[0].reshape(nb, SSM_PAIRS, LANES, SSM_STATE)
    y, ssm_h = _ssd_step(act3, ps.reshape(steps, nb, LANES), p["ssm_alog_row"], p["ssm_dtb_row"],
                         p["ssm_dskip_row"], p["ssm_consts"], h0)
    tail = u3[steps - (CONV_WIDTH - 1):].astype(F32)
    ssm_c = tok(jnp.concatenate([tail[..., :SSM_D_INNER], tail[..., 2 * SSM_D_INNER:]], axis=-1))
    x = _mixer_out(y, pm, 1, p["ssm_norm"], x, mod3[1], rows, p["ssm_w_out"], SSM_D_INNER // SSM_GROUPS, True)
    _, y_out = _ffn(x, 1, mod3[1], rows, rows, p, p["norm_final"])

    return (tok(y_out.reshape(steps, nb, D_MODEL)), gdn_s[None], gdn_c[None],
            ssm_h.reshape(nb, SSM_HEADS, SSM_HEAD_DIM, SSM_STATE)[None], ssm_c[None])


def kernel(x_prompt, x_sample, c_prompt, c_sample, state_gdn, state_gdn_conv, state_ssm, state_ssm_conv, w_mod, b_mod,
           norm_mix, norm_ffn, norm_final, gdn_w_in, gdn_conv_w, gdn_a_log, gdn_dt_bias, gdn_norm, gdn_w_out, ssm_w_in,
           ssm_conv_w, ssm_conv_b, ssm_a_log, ssm_dt_bias, ssm_d, ssm_norm, ssm_w_out, ffn_w_gate_up, ffn_w_down):
    p = _stage_params(w_mod, b_mod, norm_mix, norm_ffn, norm_final, gdn_w_in, gdn_conv_w, gdn_a_log, gdn_dt_bias,
                      gdn_norm, gdn_w_out, ssm_w_in, ssm_conv_w, ssm_conv_b, ssm_a_log, ssm_dt_bias, ssm_d, ssm_norm,
                      ssm_w_out, ffn_w_gate_up, ffn_w_down)
    y_p, gs_p, gc_p, ss_p, sc_p = _trunk_seq(x_prompt, c_prompt, p)
    y_s, gs_s, gc_s, ss_s, sc_s = _trunk_step(x_sample, c_sample, state_gdn, state_gdn_conv, state_ssm,
                                              state_ssm_conv, p)
    return (y_p, y_s, gs_p, gc_p, ss_p, sc_p, gs_s, gc_s, ss_s, sc_s)
```

```python
import functools

import numpy as np
import jax
import jax.numpy as jnp
from jax import lax
from jax.experimental import pallas as pl
from jax.experimental.pallas import tpu as pltpu

F32 = jnp.float32
BF16 = jnp.bfloat16

D_MODEL = 1024
EPS = 1e-6
CONV_WIDTH = 4
CHUNK = 64
LANES = 128
HALF = LANES // 2

GDN_QK_HEADS = 8
GDN_V_HEADS = 16
GDN_HEAD_DIM = 128
GDN_QK_DIM = GDN_QK_HEADS * GDN_HEAD_DIM
GDN_V_DIM = GDN_V_HEADS * GDN_HEAD_DIM
GDN_CONV_DIM = 2 * GDN_QK_DIM + GDN_V_DIM
GDN_MAIN = GDN_CONV_DIM + GDN_V_DIM

SSM_D_INNER = 2 * D_MODEL
SSM_HEAD_DIM = 64
SSM_HEADS = SSM_D_INNER // SSM_HEAD_DIM
SSM_GROUPS = 4
SSM_STATE = 128
SSM_BC = SSM_GROUPS * SSM_STATE
SSM_CONV_DIM = SSM_D_INNER + 2 * SSM_BC
SSM_MAIN = SSM_D_INNER + SSM_CONV_DIM

FFN_HIDDEN = 2816

VMEM_LIMIT = 56 * 1024 * 1024
NEG_BIG = -1e30


def _cparams(*sem):
    return pltpu.CompilerParams(dimension_semantics=sem, vmem_limit_bytes=VMEM_LIMIT)


def _silu(x):
    hx = 0.5 * x
    return hx + hx * jnp.tanh(hx)


def _softplus(x):
    return jnp.maximum(x, 0.0) + jnp.log1p(jnp.exp(-jnp.abs(x)))


def _dot(a, b):
    return jnp.dot(a, b, preferred_element_type=F32)


def _dot_nt(a, b):
    return lax.dot_general(a, b, (((1,), (1,)), ((), ())), preferred_element_type=F32)


def _tile_rows(v, rep):
    return v if rep == 1 else jnp.concatenate([v] * rep, axis=0)


def _mod_kernel(c_ref, w_ref, b_ref, o_ref):
    cs = _silu(c_ref[...]).astype(BF16)
    o_ref[...] = _dot(cs, w_ref[...].astype(BF16)) + b_ref[...]


def _modulation(c, w_mod, b_mod):
    depth, _, n = w_mod.shape
    bc = c.shape[0]
    tn = 1536
    return pl.pallas_call(
        _mod_kernel,
        grid=(depth, n // tn),
        in_specs=[
            pl.BlockSpec((bc, D_MODEL), lambda l, j: (0, 0)),
            pl.BlockSpec((None, D_MODEL, tn), lambda l, j: (l, 0, j)),
            pl.BlockSpec((None, 1, tn), lambda l, j: (l, 0, j)),
        ],
        out_specs=pl.BlockSpec((None, bc, tn), lambda l, j: (l, 0, j)),
        out_shape=jax.ShapeDtypeStruct((depth, bc, n), F32),
        compiler_params=_cparams("parallel", "parallel"),
        name="adaln_mod",
    )(c, w_mod, b_mod.reshape(depth, 1, n))


def _norm_mod(x, nw, sc, sh, rep):
    y = x * lax.rsqrt(jnp.mean(x * x, axis=-1, keepdims=True) + EPS) * nw
    return y * (1.0 + _tile_rows(sc, rep)) + _tile_rows(sh, rep)


def _in_proj_kernel(x_ref, nw_ref, sc_ref, sh_ref, w_ref, w2_ref, o_ref, o2_ref, h_ref, *, rep):
    @pl.when(pl.program_id(1) == 0)
    def _():
        h = _norm_mod(x_ref[...], nw_ref[...], sc_ref[...], sh_ref[...], rep).astype(BF16)
        h_ref[...] = h
        o2_ref[...] = _dot(h, w2_ref[...])

    o_ref[...] = _dot(h_ref[...], w_ref[...]).astype(o_ref.dtype)


def _ffn_up_kernel(x_ref, nw_ref, sc_ref, sh_ref, wg_ref, wu_ref, o_ref, h_ref, *, rep):
    @pl.when(pl.program_id(1) == 0)
    def _():
        h_ref[...] = _norm_mod(x_ref[...], nw_ref[...], sc_ref[...], sh_ref[...], rep).astype(BF16)

    h = h_ref[...]
    o_ref[...] = (_silu(_dot(h, wg_ref[...])) * _dot(h, wu_ref[...])).astype(BF16)


class _Rows:
    def __init__(self, m, tm, group_rows, mod_rows):
        assert m % tm == 0
        self.m, self.tm = m, tm
        if mod_rows == 1:
            assert group_rows % tm == 0
            self.rep = 1
            self.gmap = lambda i: (i * tm) // group_rows
        else:
            assert tm % mod_rows == 0
            self.rep = tm // mod_rows
            self.gmap = lambda i: 0
        self.mod_rows = mod_rows

    def mod_spec(self, col_block, with_j):
        if with_j:
            return pl.BlockSpec((None, self.mod_rows, D_MODEL), lambda i, j: (self.gmap(i), 0, col_block))
        return pl.BlockSpec((None, self.mod_rows, D_MODEL), lambda i: (self.gmap(i), 0, col_block))


def _in_proj(x, nw, mod3, rows, sc_blk, sh_blk, w, w2, tn):
    m, tm = rows.m, rows.tm
    n = w.shape[1]
    assert n % tn == 0
    return pl.pallas_call(
        functools.partial(_in_proj_kernel, rep=rows.rep),
        grid=(m // tm, n // tn),
        in_specs=[
            pl.BlockSpec((tm, D_MODEL), lambda i, j: (i, 0)),
            pl.BlockSpec((1, D_MODEL), lambda i, j: (0, 0)),
            rows.mod_spec(sc_blk, True),
            rows.mod_spec(sh_blk, True),
            pl.BlockSpec((D_MODEL, tn), lambda i, j: (0, j)),
            pl.BlockSpec((D_MODEL, LANES), lambda i, j: (0, 0)),
        ],
        out_specs=[
            pl.BlockSpec((tm, tn), lambda i, j: (i, j)),
            pl.BlockSpec((tm, LANES), lambda i, j: (i, 0)),
        ],
        out_shape=[jax.ShapeDtypeStruct((m, n), BF16), jax.ShapeDtypeStruct((m, LANES), F32)],
        scratch_shapes=[pltpu.VMEM((tm, D_MODEL), BF16)],
        compiler_params=_cparams("parallel", "arbitrary"),
        name="in_proj",
    )(x, nw.reshape(1, D_MODEL), mod3, mod3, w, w2)


def _ffn_up(x, nw, mod3, rows, wg, wu, th):
    m, tm = rows.m, rows.tm
    assert FFN_HIDDEN % th == 0
    return pl.pallas_call(
        functools.partial(_ffn_up_kernel, rep=rows.rep),
        grid=(m // tm, FFN_HIDDEN // th),
        in_specs=[
            pl.BlockSpec((tm, D_MODEL), lambda i, j: (i, 0)),
            pl.BlockSpec((1, D_MODEL), lambda i, j: (0, 0)),
            rows.mod_spec(4, True),
            rows.mod_spec(3, True),
            pl.BlockSpec((D_MODEL, th), lambda i, j: (0, j)),
            pl.BlockSpec((D_MODEL, th), lambda i, j: (0, j)),
        ],
        out_specs=pl.BlockSpec((tm, th), lambda i, j: (i, j)),
        out_shape=jax.ShapeDtypeStruct((m, FFN_HIDDEN), BF16),
        scratch_shapes=[pltpu.VMEM((tm, D_MODEL), BF16)],
        compiler_params=_cparams("parallel", "arbitrary"),
        name="ffn_up",
    )(x, nw.reshape(1, D_MODEL), mod3, mod3, wg, wu)


def _resid_store(acc, x_ref, gt_ref, o_ref, fnw_ref, y_ref, rep):
    xn = x_ref[...] + _tile_rows(gt_ref[...], rep) * acc
    o_ref[...] = xn
    if y_ref is not None:
        y_ref[...] = xn * lax.rsqrt(jnp.mean(xn * xn, axis=-1, keepdims=True) + EPS) * fnw_ref[...]


def _ffn_down_kernel(a_ref, x_ref, gt_ref, w_ref, *rest, rep, final):
    if final:
        fnw_ref, o_ref, y_ref = rest
    else:
        (o_ref,), fnw_ref, y_ref = rest, None, None
    _resid_store(_dot(a_ref[...], w_ref[...]), x_ref, gt_ref, o_ref, fnw_ref, y_ref, rep)


def _ffn_down(act, x, mod3, rows, w, fnw):
    m, tm = rows.m, rows.tm
    final = fnw is not None
    in_specs = [
        pl.BlockSpec((tm, FFN_HIDDEN), lambda i: (i, 0)),
        pl.BlockSpec((tm, D_MODEL), lambda i: (i, 0)),
        rows.mod_spec(5, False),
        pl.BlockSpec((FFN_HIDDEN, D_MODEL), lambda i: (0, 0)),
    ]
    args = [act, x, mod3, w]
    row_spec = pl.BlockSpec((tm, D_MODEL), lambda i: (i, 0))
    out_shape = jax.ShapeDtypeStruct((m, D_MODEL), F32)
    if final:
        in_specs.append(pl.BlockSpec((1, D_MODEL), lambda i: (0, 0)))
        args.append(fnw.reshape(1, D_MODEL))
        out_specs, out_shapes = [row_spec, row_spec], [out_shape, out_shape]
    else:
        out_specs, out_shapes = row_spec, out_shape
    return pl.pallas_call(
        functools.partial(_ffn_down_kernel, rep=rows.rep, final=final),
        grid=(m // tm,),
        in_specs=in_specs,
        out_specs=out_specs,
        out_shape=out_shapes,
        compiler_params=_cparams("parallel"),
        name="ffn_down",
    )(*args)


def _mixer_out_kernel(y_ref, z_ref, nw_ref, x_ref, gt_ref, w_ref, o_ref, a_ref, *, rep, group, gate_first):
    width = y_ref.shape[1]
    for s in range(0, width, group):
        y = y_ref[:, s:s + group]
        gate = _silu(z_ref[:, s:s + group].astype(F32))
        if gate_first:
            y = y * gate
        y = y * lax.rsqrt(jnp.mean(y * y, axis=-1, keepdims=True) + EPS) * nw_ref[:, s:s + group]
        if not gate_first:
            y = y * gate
        a_ref[:, s:s + group] = y.astype(BF16)
    _resid_store(_dot(a_ref[...], w_ref[...]), x_ref, gt_ref, o_ref, None, None, rep)


def _mixer_out(y, zsrc, z_blk, nw_full, x, mod3, rows, w, group, gate_first):
    m, tm = rows.m, rows.tm
    width = y.shape[1]
    return pl.pallas_call(
        functools.partial(_mixer_out_kernel, rep=rows.rep, group=group, gate_first=gate_first),
        grid=(m // tm,),
        in_specs=[
            pl.BlockSpec((tm, width), lambda i: (i, 0)),
            pl.BlockSpec((tm, width), lambda i: (i, z_blk)),
            pl.BlockSpec((1, width), lambda i: (0, 0)),
            pl.BlockSpec((tm, D_MODEL), lambda i: (i, 0)),
            rows.mod_spec(2, False),
            pl.BlockSpec((width, D_MODEL), lambda i: (0, 0)),
        ],
        out_specs=pl.BlockSpec((tm, D_MODEL), lambda i: (i, 0)),
        out_shape=jax.ShapeDtypeStruct((m, D_MODEL), F32),
        scratch_shapes=[pltpu.VMEM((tm, width), BF16)],
        compiler_params=_cparams("parallel"),
        name="mixer_out",
    )(y, zsrc, nw_full.reshape(1, width), x, mod3, w)


CONV_COLS = 1024


def _post_conv(acc, o_ref, cb, n_l2, qscale):
    y = _silu(acc)
    if n_l2 == 0:
        o_ref[...] = y
        return

    @pl.when(cb < n_l2)
    def _():
        scale = jnp.where(cb == 0, qscale, 1.0).astype(F32)
        for s in range(0, CONV_COLS, GDN_HEAD_DIM):
            yh = y[:, s:s + GDN_HEAD_DIM]
            o_ref[:, s:s + GDN_HEAD_DIM] = yh * lax.rsqrt(jnp.sum(yh * yh, axis=-1, keepdims=True) + EPS) * scale

    @pl.when(cb >= n_l2)
    def _():
        o_ref[...] = y


BHALO = 16
CONV_GROUP = 2 * LANES


def _shift_matrix():
    s = np.zeros(((CONV_WIDTH - 1) * CHUNK, BHALO + CHUNK), np.float32)
    for tap in range(CONV_WIDTH - 1):
        for r in range(CHUNK):
            s[tap * CHUNK + r, BHALO - (CONV_WIDTH - 1) + tap + r] = 1.0
    return jnp.asarray(s, BF16)


def _conv_block(raw_refs, shift_ref, w_ref, b_ref, ext_ref, act_ref, seq_start, l2_cols, q_cols, qscale):
    width = ext_ref.shape[1]
    if seq_start:
        ext_ref[0:BHALO, :] = jnp.zeros((BHALO, width), BF16)
    else:
        ext_ref[0:BHALO, :] = ext_ref[CHUNK:CHUNK + BHALO, :]

    off = 0
    for ref in raw_refs:
        ext_ref[BHALO:BHALO + CHUNK, off:off + ref.shape[1]] = ref[...]
        off += ref.shape[1]

    def lane_group(s):
        cols = slice(s, s + CONV_GROUP)
        sh = _dot(shift_ref[...], ext_ref[:, cols])
        acc = b_ref[:, cols] + w_ref[CONV_WIDTH - 1:CONV_WIDTH, cols] * ext_ref[BHALO:BHALO + CHUNK, cols].astype(F32)
        for tap in range(CONV_WIDTH - 1):
            acc = acc + w_ref[tap:tap + 1, cols] * sh[tap * CHUNK:(tap + 1) * CHUNK]
        y = _silu(acc)
        for h in range(s, s + CONV_GROUP, LANES):
            yh = y[:, h - s:h - s + LANES]
            if h < l2_cols:
                yh = yh * lax.rsqrt(jnp.sum(yh * yh, axis=-1, keepdims=True) + EPS)
                if h < q_cols:
                    yh = yh * qscale
            act_ref[:, h:h + LANES] = yh

    return [functools.partial(lane_group, s) for s in range(0, width, CONV_GROUP)]


class _Filler:
    def __init__(self, tasks, shares):
        self.tasks, self.per = list(tasks), -(-len(tasks) // shares)

    def emit(self):
        for task in self.tasks[:self.per]:
            task()
        self.tasks = self.tasks[self.per:]

    def flush(self):
        for task in self.tasks:
            task()
        self.tasks = []


_NO_FILL = _Filler([], 1)


def _conv_steps_kernel(u_ref, hist_ref, w_ref, b_ref, o_ref, *, steps, n_l2, qscale):
    cb = pl.program_id(0)
    ext = [hist_ref[i] for i in range(CONV_WIDTH - 1)] + [u_ref[i].astype(F32) for i in range(steps)]
    for t in range(steps):
        acc = b_ref[...] + w_ref[0:1, :] * ext[t]
        for tap in range(1, CONV_WIDTH):
            acc = acc + w_ref[tap:tap + 1, :] * ext[t + tap]
        _post_conv(acc, o_ref.at[t], cb, n_l2, qscale)


def _conv_steps(u3, skip_at, skip, n_cols, hist3, conv_w, conv_b, n_l2, qscale):
    steps, nb, _ = u3.shape
    return pl.pallas_call(
        functools.partial(_conv_steps_kernel, steps=steps, n_l2=n_l2, qscale=qscale),
        grid=(n_cols // CONV_COLS,),
        in_specs=[
            pl.BlockSpec((steps, nb, CONV_COLS), lambda c: (0, 0, c + skip * (c >= skip_at))),
            pl.BlockSpec((CONV_WIDTH - 1, nb, CONV_COLS), lambda c: (0, 0, c)),
            pl.BlockSpec((CONV_WIDTH, CONV_COLS), lambda c: (0, c)),
            pl.BlockSpec((1, CONV_COLS), lambda c: (0, c)),
        ],
        out_specs=pl.BlockSpec((steps, nb, CONV_COLS), lambda c: (0, 0, c)),
        out_shape=jax.ShapeDtypeStruct((steps, nb, n_cols), F32),
        compiler_params=_cparams("parallel"),
        name="conv_steps",
    )(u3, hist3, conv_w, conv_b.reshape(1, n_cols))


def _split3(x):
    hi = x.astype(BF16)
    r = x - hi.astype(F32)
    mid = r.astype(BF16)
    lo = (r - mid.astype(F32)).astype(BF16)
    return hi, mid, lo


def _cat3(x, axis):
    return jnp.concatenate(_split3(x), axis=axis)


def _pad_t(tile):
    return jnp.concatenate([tile, jnp.zeros_like(tile)], axis=0).T


def _pair_rows(t, n):
    return t[0:n] + pltpu.roll(t[n:2 * n], HALF, 1)


def _iotas(shape):
    return lax.broadcasted_iota(jnp.int32, shape, 0), lax.broadcasted_iota(jnp.int32, shape, 1)


def _pad_rows(a):
    if a.shape[0] == CHUNK:
        return a
    return jnp.concatenate([a, jnp.zeros((CHUNK - a.shape[0], a.shape[1]), a.dtype)], axis=0)


def _split2(x):
    hi = x.astype(BF16)
    return hi, (x - hi.astype(F32)).astype(BF16)


def _block_diag(pair_bf, bd_ones):
    pair_bf = _pad_rows(pair_bf)
    return jnp.concatenate([pair_bf, pair_bf], axis=0) * bd_ones


def _mm_pairs(lhs_parts, rhs_parts, bd_ones):
    outs = []
    for (lh, ll), (rh, rl) in zip(lhs_parts, rhs_parts):
        rh_bd = _block_diag(rh, bd_ones)
        outs.append(_dot(jnp.concatenate([lh, lh, ll], axis=1),
                         jnp.concatenate([rh_bd, _block_diag(rl, bd_ones), rh_bd], axis=0)))
    return outs


def _mm_hl(lhs, rhs_bf):
    lh = lhs.astype(BF16)
    ll = (lhs - lh.astype(F32)).astype(BF16)
    return _dot(jnp.concatenate([lh, ll], axis=1), jnp.concatenate([rhs_bf, rhs_bf], axis=0))


def _unit_lower_inverse(xs, levels, bd_ones, eye2, fill):
    ps = [eye2 + x for x in xs]
    if levels <= 1:
        return ps
    rows = xs[0].shape[0]
    stack = lambda a, b: tuple(jnp.concatenate([s, t], axis=0) for s, t in zip(a, b))
    ysp = [_split2(x) for x in xs]
    ys = _mm_pairs(ysp, ysp, bd_ones)
    fill.emit()
    for _ in range(levels - 2):
        ysp = [_split2(y) for y in ys]
        rs = _mm_pairs([stack(yp, _split2(p)) for yp, p in zip(ysp, ps)], ysp, bd_ones)
        fill.emit()
        ys = [r[0:rows] for r in rs]
        ps = [p + r[rows:2 * rows] for p, r in zip(ps, rs)]
    last = _mm_pairs([_split2(p) for p in ps], [_split2(y) for y in ys], bd_ones)
    return [p + t for p, t in zip(ps, last)]


def _two_blocks(a, b):
    a_bf, b_bf = _pad_rows(a).astype(BF16), _pad_rows(b).astype(BF16)
    z = jnp.zeros_like(a_bf)
    return jnp.concatenate([jnp.concatenate([a_bf, z], axis=1), jnp.concatenate([z, b_bf], axis=1)], axis=0)


def _recurrence_consts(n_pairs, chan_even0, chan_odd0):
    tril = np.tril(np.ones((CHUNK, CHUNK), np.float32))
    tril3 = np.concatenate([tril] * 3, axis=1)
    upper = np.zeros((LANES, 2 * LANES), np.float32)
    upper[:CHUNK, :CHUNK] = tril.T
    upper[:CHUNK, LANES:] = 1.0
    upper3 = np.concatenate([upper] * 3, axis=0)
    e = np.zeros((LANES, n_pairs * LANES), np.float32)
    for p in range(n_pairs):
        e[chan_even0 + p, p * LANES:p * LANES + HALF] = 1.0
        e[chan_odd0 + p, p * LANES + HALF:(p + 1) * LANES] = 1.0
    e3 = np.concatenate([e] * 3, axis=0)
    return jnp.asarray(tril3, BF16), jnp.asarray(upper3, BF16), jnp.asarray(e3, BF16)


def _levels(lr):
    return max(1, int(np.ceil(np.log2(lr))))


STEP_ROWS = 16


GDN_PAIRS = GDN_V_HEADS // 2
GDN_PAIR_BATCH = 8
GDN_FILL_SHARES = 11 * (GDN_PAIRS // GDN_PAIR_BATCH)


def _gdn_gates(raw, alog_row, dtb_row, lr):
    row, lane = _iotas((CHUNK, LANES))
    g = -jnp.exp(alog_row) * _softplus(raw + dtb_row)
    beta = jax.nn.sigmoid(raw)
    gt = jnp.where(lane < GDN_V_HEADS, g, jnp.where(lane < 2 * GDN_V_HEADS, beta, 0.0))
    if lr < CHUNK:
        gt = jnp.where(row < lr, gt, 0.0)
    return gt


def _gdn_block(act_ref, gt, s_in, s_out, o_ref, tril3_ref, upper3_ref, e3_ref, lr, fill=_NO_FILL):
    np_ = GDN_PAIRS
    rows = act_ref.shape[0]
    row, lane = _iotas((rows, LANES))
    jl = jnp.bitwise_and(lane, HALF - 1)
    left = lane < HALF
    causal = row >= jl
    strict = row > jl
    eye2 = jnp.where(row == jl, 1.0, 0.0)
    r2, l2 = _iotas((LANES, LANES))
    bdmask = jnp.right_shift(r2, 6) == jnp.right_shift(l2, 6)
    _, lane_p = _iotas((np_, LANES))

    t1 = _pad_t(gt)
    cum_ext = _dot(_cat3(t1[0:2 * np_], 1), upper3_ref[...])
    cum_t, last_b = cum_ext[:, :LANES], cum_ext[:, LANES:]
    cum_rp = _pair_rows(cum_t, np_)
    beta_rp = _pair_rows(t1[2 * np_:4 * np_], np_)
    last_rp = jnp.where(lane_p < HALF, last_b[0:np_], last_b[np_:2 * np_])
    ecum_rp = jnp.exp(cum_rp)
    kdec_rp = jnp.exp(last_rp - cum_rp) * beta_rp
    elast = jnp.exp(last_b)
    cum = _dot(tril3_ref[...], _cat3(gt, 0))
    col_all = _dot(_cat3(cum[0:rows], 1), e3_ref[...])

    bd_ones = jnp.where(bdmask, 1.0, 0.0).astype(BF16)
    sl = lambda a, i: a[:, i * LANES:(i + 1) * LANES]

    def run(pairs):
        heads = [2 * p + hh for p in pairs for hh in range(2)]
        q = {p: sl(act_ref, p) for p in pairs}
        k = {p: _pad_rows(sl(act_ref, GDN_QK_HEADS + p)) for p in pairs}
        v = {h: sl(act_ref, 2 * GDN_QK_HEADS + h) for h in heads}
        kb = {p: k[p].astype(BF16) for p in pairs}
        qb = {p: q[p].astype(BF16) for p in pairs}
        gq = {p: _dot_nt(jnp.concatenate([kb[p][0:rows], qb[p]], axis=0), jnp.concatenate([kb[p], kb[p]], axis=0))
              for p in pairs}
        fill.emit()
        colc = {p: sl(col_all, p) for p in pairs}
        base = {p: jnp.exp(jnp.where(causal, colc[p] - cum_rp[p:p + 1], NEG_BIG)) * beta_rp[p:p + 1] for p in pairs}
        x = [jnp.where(strict, -(gq[p][0:rows] * base[p]), 0.0) for p in pairs]
        qkd = {p: (gq[p][rows:2 * rows] * base[p]).astype(BF16) for p in pairs}
        minv = dict(zip(pairs, _unit_lower_inverse(x, _levels(lr), bd_ones, eye2, fill)))
        fill.emit()
        u = {p: jnp.concatenate([v[2 * p], v[2 * p + 1]], axis=1)
             + _mm_hl(minv[p] - eye2, _two_blocks(v[2 * p], v[2 * p + 1])) for p in pairs}
        fill.emit()
        w = {p: _mm_hl(minv[p] * ecum_rp[p:p + 1], _two_blocks(kb[p], kb[p])) for p in pairs}
        fill.emit()
        s_old = {h: s_in[h] for h in heads}
        r = {h: _dot(jnp.concatenate([sl(w[h // 2], h % 2).astype(BF16), qb[h // 2]], axis=0),
                     s_old[h].astype(BF16)) for h in heads}
        fill.emit()
        delta = {h: sl(u[h // 2], h % 2) - r[h][0:rows] for h in heads}
        bd_delta = {p: _two_blocks(delta[2 * p], delta[2 * p + 1]) for p in pairs}
        o_intra = {p: _dot(qkd[p], bd_delta[p]) for p in pairs}
        fill.emit()
        ds = {p: _dot((jnp.concatenate([k[p], k[p]], axis=0).T * kdec_rp[p:p + 1]).astype(BF16), bd_delta[p])
              for p in pairs}
        for p in pairs:
            ecol = jnp.exp(colc[p])
            ecol_r = pltpu.roll(ecol, HALF, 1)
            efull = (jnp.where(left, ecol, ecol_r), jnp.where(left, ecol_r, ecol))
            for hh in range(2):
                h = 2 * p + hh
                o_ref[:, h * LANES:(h + 1) * LANES] = efull[hh] * r[h][rows:2 * rows] + sl(o_intra[p], hh)
                e_h = elast[hh * np_ + p:hh * np_ + p + 1]
                s_out[h] = e_h * s_old[h] + sl(ds[p], hh)

    for b0 in range(0, np_, GDN_PAIR_BATCH):
        run(list(range(b0, b0 + GDN_PAIR_BATCH)))
    fill.flush()


def _gdn_seq_kernel(raw0_ref, rawn_ref, graw_ref, shift_ref, cw_ref, cb_ref, alog_ref, dtb_ref, tril3_ref, upper3_ref,
                    e3_ref, o_ref, s_ref, ext_ref, act_ref):
    n = pl.program_id(1)
    conv = functools.partial(_conv_block, shift_ref=shift_ref, w_ref=cw_ref, b_ref=cb_ref, ext_ref=ext_ref,
                             l2_cols=2 * GDN_QK_DIM, q_cols=GDN_QK_DIM, qscale=QSCALE)

    @pl.when(n == 0)
    def _():
        s_ref[...] = jnp.zeros(s_ref.shape, F32)
        _Filler(conv([raw0_ref], act_ref=act_ref.at[0], seq_start=True), 1).flush()

    @pl.when(n > 0)
    def _():
        act_ref[0] = act_ref[1]

    gt = _gdn_gates(graw_ref[...], alog_ref[...], dtb_ref[...], CHUNK)
    fill = _Filler(conv([rawn_ref], act_ref=act_ref.at[1], seq_start=False), GDN_FILL_SHARES)
    _gdn_block(act_ref.at[0], gt, s_ref, s_ref, o_ref, tril3_ref, upper3_ref, e3_ref, CHUNK, fill)


def _const_specs(consts, ngrid):
    zero = (lambda *_: (0, 0))
    return [pl.BlockSpec(c.shape, zero) for c in consts]


def _gdn_seq(pm, graw, conv_w, conv_b, alog_row, dtb_row, consts, nseq, seq_len):
    nc = seq_len // CHUNK
    last = nseq * nc - 1
    rowspec = pl.BlockSpec((1, LANES), lambda b, n: (0, 0))
    return pl.pallas_call(
        _gdn_seq_kernel,
        grid=(nseq, nc),
        in_specs=[
            pl.BlockSpec((CHUNK, GDN_CONV_DIM), lambda b, n: (b * nc, 0)),
            pl.BlockSpec((CHUNK, GDN_CONV_DIM), lambda b, n: (jnp.minimum(b * nc + n + 1, last), 0)),
            pl.BlockSpec((CHUNK, LANES), lambda b, n: (b * nc + n, 0)),
            pl.BlockSpec(((CONV_WIDTH - 1) * CHUNK, BHALO + CHUNK), lambda b, n: (0, 0)),
            pl.BlockSpec((CONV_WIDTH, GDN_CONV_DIM), lambda b, n: (0, 0)),
            pl.BlockSpec((1, GDN_CONV_DIM), lambda b, n: (0, 0)),
            rowspec, rowspec,
        ] + _const_specs(consts, 2),
        out_specs=[
            pl.BlockSpec((CHUNK, GDN_V_DIM), lambda b, n: (b * nc + n, 0)),
            pl.BlockSpec((None, GDN_V_HEADS, GDN_HEAD_DIM, GDN_HEAD_DIM), lambda b, n: (b, 0, 0, 0)),
        ],
        out_shape=[
            jax.ShapeDtypeStruct((nseq * seq_len, GDN_V_DIM), F32),
            jax.ShapeDtypeStruct((nseq, GDN_V_HEADS, GDN_HEAD_DIM, GDN_HEAD_DIM), F32),
        ],
        scratch_shapes=[
            pltpu.VMEM((CHUNK + BHALO, GDN_CONV_DIM), BF16),
            pltpu.VMEM((2, CHUNK, GDN_CONV_DIM), F32),
        ],
        compiler_params=_cparams("parallel", "arbitrary"),
        name="gdn_seq",
    )(pm, pm, graw, _shift_matrix(), conv_w, conv_b.reshape(1, GDN_CONV_DIM), alog_row, dtb_row, *consts)


SEQ_PER_STEP = 8


def _load_padded(src_ref, s, pad_ref, lr):
    pad_ref[...] = jnp.zeros(pad_ref.shape, F32)
    for t in range(lr):
        pad_ref[t:t + 1, :] = src_ref[t, pl.ds(s, 1), :]


def _store_tokens(pad_ref, dst_ref, s, lr):
    for t in range(lr):
        dst_ref[t, pl.ds(s, 1), :] = pad_ref[t:t + 1, :]


def _gdn_step_kernel(act_ref, graw_ref, alog_ref, dtb_ref, tril3_ref, upper3_ref, e3_ref, s0_ref,
                     o_ref, s_ref, apad_ref, gpad_ref, opad_ref, *, lr):
    def one_sequence(s, carry):
        _load_padded(act_ref, s, apad_ref, lr)
        _load_padded(graw_ref, s, gpad_ref, lr)
        gt = _gdn_gates(gpad_ref[...], alog_ref[...], dtb_ref[...], lr)
        _gdn_block(apad_ref, gt, s0_ref.at[s], s_ref.at[s], opad_ref, tril3_ref, upper3_ref, e3_ref, lr)
        _store_tokens(opad_ref, o_ref, s, lr)
        return carry

    lax.fori_loop(0, SEQ_PER_STEP, one_sequence, 0)


def _step_spec(steps, width):
    return pl.BlockSpec((steps, SEQ_PER_STEP, width), lambda g: (0, g, 0))


def _gdn_step(act3, graw3, alog_row, dtb_row, consts, s0):
    steps, nb, _ = act3.shape
    assert nb % SEQ_PER_STEP == 0 and steps <= STEP_ROWS
    rowspec = pl.BlockSpec((1, LANES), lambda b: (0, 0))
    sspec = pl.BlockSpec((SEQ_PER_STEP, GDN_V_HEADS, GDN_HEAD_DIM, GDN_HEAD_DIM), lambda g: (g, 0, 0, 0))
    o, s = pl.pallas_call(
        functools.partial(_gdn_step_kernel, lr=steps),
        grid=(nb // SEQ_PER_STEP,),
        in_specs=[_step_spec(steps, GDN_CONV_DIM), _step_spec(steps, LANES), rowspec, rowspec]
        + _const_specs(consts, 1) + [sspec],
        out_specs=[_step_spec(steps, GDN_V_DIM), sspec],
        out_shape=[
            jax.ShapeDtypeStruct((steps, nb, GDN_V_DIM), F32),
            jax.ShapeDtypeStruct(s0.shape, F32),
        ],
        scratch_shapes=[
            pltpu.VMEM((STEP_ROWS, GDN_CONV_DIM), F32),
            pltpu.VMEM((CHUNK, LANES), F32),
            pltpu.VMEM((STEP_ROWS, GDN_V_DIM), F32),
        ],
        compiler_params=_cparams("parallel"),
        name="gdn_step",
    )(act3, graw3, alog_row, dtb_row, *consts, s0)
    return o.reshape(steps * nb, GDN_V_DIM), s


SSM_PAIRS = SSM_HEADS // 2
PAIRS_PER_GROUP = SSM_PAIRS // SSM_GROUPS


def _ssm_gates(raw, alog_row, dtb_row, lr):
    row, lane = _iotas((CHUNK, LANES))
    dt = _softplus(raw + dtb_row)
    tile = jnp.where(lane < SSM_HEADS, dt, jnp.where(lane < 2 * SSM_HEADS, -jnp.exp(alog_row) * dt, 0.0))
    if lr < CHUNK:
        tile = jnp.where(row < lr, tile, 0.0)
    return tile


def _ssd_block(act_ref, tile, h_in, h_out, y_ref, dskip_ref, tril3_ref, upper3_ref, e3_ref, fill=_NO_FILL):
    np_ = SSM_PAIRS
    rows = act_ref.shape[0]
    row, lane = _iotas((rows, LANES))
    causal = row >= jnp.bitwise_and(lane, HALF - 1)
    r2, l2 = _iotas((LANES, LANES))
    bdmask = jnp.right_shift(r2, 6) == jnp.right_shift(l2, 6)
    top = r2 < HALF
    _, lane_p = _iotas((np_, LANES))

    t1 = _pad_t(tile)
    cum_ext = _dot(_cat3(t1[2 * np_:4 * np_], 1), upper3_ref[...])
    cum_t, last_b = cum_ext[:, :LANES], cum_ext[:, LANES:]
    cum_rp = _pair_rows(cum_t, np_)
    dt_rp = _pair_rows(t1[0:2 * np_], np_)
    last_rp = jnp.where(lane_p < HALF, last_b[0:np_], last_b[np_:2 * np_])
    coef_rp = jnp.exp(last_rp - cum_rp) * dt_rp
    elast = jnp.exp(last_b)
    cum = _dot(tril3_ref[...], _cat3(tile, 0))
    col_all = _dot(_cat3(cum[0:rows], 1), e3_ref[...])

    pairs = range(np_)
    sl = lambda a, i: a[:, i * LANES:(i + 1) * LANES]
    grp = lambda p: p // PAIRS_PER_GROUP
    bg = [_pad_rows(sl(act_ref, SSM_D_INNER // LANES + g)).astype(BF16) for g in range(SSM_GROUPS)]
    cg = [sl(act_ref, (SSM_D_INNER + SSM_BC) // LANES + g).astype(BF16) for g in range(SSM_GROUPS)]
    bb = [jnp.concatenate([b, b], axis=0) for b in bg]
    cb2 = [_dot_nt(cg[g], bb[g]) for g in range(SSM_GROUPS)]
    fill.emit()
    xp = [sl(act_ref, p) for p in pairs]
    colc = [sl(col_all, p) for p in pairs]
    x2 = [jnp.concatenate([_pad_rows(a), _pad_rows(a)], axis=0) for a in xp]
    lm = [cb2[grp(p)] * jnp.exp(jnp.where(causal, colc[p] - cum_rp[p:p + 1], NEG_BIG)) * dt_rp[p:p + 1] for p in pairs]
    y_diag = []
    for p in pairs:
        y_diag.append(_dot(lm[p].astype(BF16), jnp.where(bdmask, x2[p], 0.0).astype(BF16)))
        if p % 4 == 3:
            fill.emit()
    hp = [h_in[p] for p in pairs]
    y_off = []
    for p in pairs:
        y_off.append(_dot_nt(cg[grp(p)], hp[p].astype(BF16)))
        if p % 4 == 3:
            fill.emit()
    for p in pairs:
        y_ref[:, p * LANES:(p + 1) * LANES] = (y_diag[p] + jnp.exp(colc[p]) * y_off[p]
                                               + dskip_ref[:, p * LANES:(p + 1) * LANES] * xp[p])
    dh = []
    for p in pairs:
        lhs = jnp.where(bdmask, x2[p].T * coef_rp[p:p + 1], 0.0)
        dh.append(_dot(lhs.astype(BF16), bb[grp(p)]))
        if p % 4 == 3:
            fill.emit()
    for p in pairs:
        e_rows = jnp.where(top, elast[p:p + 1], elast[np_ + p:np_ + p + 1])
        h_out[p] = e_rows * hp[p] + dh[p]
    fill.flush()


def _ssd_seq_kernel(x0_ref, bc0_ref, xn_ref, bcn_ref, graw_ref, shift_ref, cw_ref, cb_ref, alog_ref, dtb_ref,
                    dskip_ref, tril3_ref, upper3_ref, e3_ref, y_ref, h_ref, ext_ref, act_ref):
    n = pl.program_id(1)
    conv = functools.partial(_conv_block, shift_ref=shift_ref, w_ref=cw_ref, b_ref=cb_ref, ext_ref=ext_ref,
                             l2_cols=0, q_cols=0, qscale=1.0)

    @pl.when(n == 0)
    def _():
        h_ref[...] = jnp.zeros(h_ref.shape, F32)
        _Filler(conv([x0_ref, bc0_ref], act_ref=act_ref.at[0], seq_start=True), 1).flush()

    @pl.when(n > 0)
    def _():
        act_ref[0] = act_ref[1]

    tile = _ssm_gates(graw_ref[...], alog_ref[...], dtb_ref[...], CHUNK)
    fill = _Filler(conv([xn_ref, bcn_ref], act_ref=act_ref.at[1], seq_start=False), SSM_PAIRS)
    _ssd_block(act_ref.at[0], tile, h_ref, h_ref, y_ref, dskip_ref, tril3_ref, upper3_ref, e3_ref, fill)


def _ssd_seq(pm, graw, conv_w, conv_b, alog_row, dtb_row, dskip_row, consts, nseq, seq_len):
    nc = seq_len // CHUNK
    last = nseq * nc - 1
    bc_blk = 2 * SSM_D_INNER // (2 * SSM_BC)
    nxt = lambda b, n: jnp.minimum(b * nc + n + 1, last)
    rowspec = pl.BlockSpec((1, LANES), lambda b, n: (0, 0))
    return pl.pallas_call(
        _ssd_seq_kernel,
        grid=(nseq, nc),
        in_specs=[
            pl.BlockSpec((CHUNK, SSM_D_INNER), lambda b, n: (b * nc, 0)),
            pl.BlockSpec((CHUNK, 2 * SSM_BC), lambda b, n: (b * nc, bc_blk)),
            pl.BlockSpec((CHUNK, SSM_D_INNER), lambda b, n: (nxt(b, n), 0)),
            pl.BlockSpec((CHUNK, 2 * SSM_BC), lambda b, n: (nxt(b, n), bc_blk)),
            pl.BlockSpec((CHUNK, LANES), lambda b, n: (b * nc + n, 0)),
            pl.BlockSpec(((CONV_WIDTH - 1) * CHUNK, BHALO + CHUNK), lambda b, n: (0, 0)),
            pl.BlockSpec((CONV_WIDTH, SSM_CONV_DIM), lambda b, n: (0, 0)),
            pl.BlockSpec((1, SSM_CONV_DIM), lambda b, n: (0, 0)),
            rowspec, rowspec,
            pl.BlockSpec((1, SSM_D_INNER), lambda b, n: (0, 0)),
        ] + _const_specs(consts, 2),
        out_specs=[
            pl.BlockSpec((CHUNK, SSM_D_INNER), lambda b, n: (b * nc + n, 0)),
            pl.BlockSpec((None, SSM_PAIRS, LANES, SSM_STATE), lambda b, n: (b, 0, 0, 0)),
        ],
        out_shape=[
            jax.ShapeDtypeStruct((nseq * seq_len, SSM_D_INNER), F32),
            jax.ShapeDtypeStruct((nseq, SSM_PAIRS, LANES, SSM_STATE), F32),
        ],
        scratch_shapes=[
            pltpu.VMEM((CHUNK + BHALO, SSM_CONV_DIM), BF16),
            pltpu.VMEM((2, CHUNK, SSM_CONV_DIM), F32),
        ],
        compiler_params=_cparams("parallel", "arbitrary"),
        name="ssd_seq",
    )(pm, pm, pm, pm, graw, _shift_matrix(), conv_w, conv_b.reshape(1, SSM_CONV_DIM), alog_row, dtb_row, dskip_row, *consts)


def _ssd_step_kernel(act_ref, graw_ref, alog_ref, dtb_ref, dskip_ref, tril3_ref, upper3_ref, e3_ref, h0_ref,
                     y_ref, h_ref, apad_ref, gpad_ref, ypad_ref, *, lr):
    def one_sequence(s, carry):
        _load_padded(act_ref, s, apad_ref, lr)
        _load_padded(graw_ref, s, gpad_ref, lr)
        tile = _ssm_gates(gpad_ref[...], alog_ref[...], dtb_ref[...], lr)
        _ssd_block(apad_ref, tile, h0_ref.at[s], h_ref.at[s], ypad_ref, dskip_ref, tril3_ref, upper3_ref, e3_ref)
        _store_tokens(ypad_ref, y_ref, s, lr)
        return carry

    lax.fori_loop(0, SEQ_PER_STEP, one_sequence, 0)


def _ssd_step(act3, graw3, alog_row, dtb_row, dskip_row, consts, h0):
    steps, nb, _ = act3.shape
    assert nb % SEQ_PER_STEP == 0 and steps <= STEP_ROWS
    rowspec = pl.BlockSpec((1, LANES), lambda b: (0, 0))
    hspec = pl.BlockSpec((SEQ_PER_STEP, SSM_PAIRS, LANES, SSM_STATE), lambda g: (g, 0, 0, 0))
    y, h = pl.pallas_call(
        functools.partial(_ssd_step_kernel, lr=steps),
        grid=(nb // SEQ_PER_STEP,),
        in_specs=[_step_spec(steps, SSM_CONV_DIM), _step_spec(steps, LANES), rowspec, rowspec,
                  pl.BlockSpec((1, SSM_D_INNER), lambda b: (0, 0))] + _const_specs(consts, 1) + [hspec],
        out_specs=[_step_spec(steps, SSM_D_INNER), hspec],
        out_shape=[
            jax.ShapeDtypeStruct((steps, nb, SSM_D_INNER), F32),
            jax.ShapeDtypeStruct(h0.shape, F32),
        ],
        scratch_shapes=[
            pltpu.VMEM((STEP_ROWS, SSM_CONV_DIM), F32),
            pltpu.VMEM((CHUNK, LANES), F32),
            pltpu.VMEM((STEP_ROWS, SSM_D_INNER), F32),
        ],
        compiler_params=_cparams("parallel"),
        name="ssd_step",
    )(act3, graw3, alog_row, dtb_row, dskip_row, *consts, h0)
    return y.reshape(steps * nb, SSM_D_INNER), h


FFN_TILE = FFN_HIDDEN // 2
GDN_PROJ_TILE = GDN_MAIN // 3
SSM_PROJ_TILE = SSM_MAIN // 2


def _lane_row(pieces):
    row = jnp.zeros((1, LANES), F32)
    for off, vec in pieces:
        row = row.at[0, off:off + vec.shape[0]].set(vec.astype(F32))
    return row


def _stage_params(w_mod, b_mod, norm_mix, norm_ffn, norm_final, gdn_w_in, gdn_conv_w, gdn_a_log, gdn_dt_bias,
                  gdn_norm, gdn_w_out, ssm_w_in, ssm_conv_w, ssm_conv_b, ssm_a_log, ssm_dt_bias, ssm_d, ssm_norm,
                  ssm_w_out, ffn_w_gate_up, ffn_w_down):
    perm_g = np.concatenate([np.arange(0, GDN_V_HEADS, 2), np.arange(1, GDN_V_HEADS, 2)])
    perm_s = np.concatenate([np.arange(0, SSM_HEADS, 2), np.arange(1, SSM_HEADS, 2)])
    g_in, s_in = gdn_w_in[0], ssm_w_in[0]
    beta_cols = g_in[:, GDN_MAIN:GDN_MAIN + GDN_V_HEADS][:, perm_g]
    a_cols = g_in[:, GDN_MAIN + GDN_V_HEADS:GDN_MAIN + 2 * GDN_V_HEADS][:, perm_g]
    gdn_small = jnp.concatenate([a_cols, beta_cols, jnp.zeros((D_MODEL, LANES - 2 * GDN_V_HEADS), F32)], axis=1)
    dt_cols = s_in[:, SSM_MAIN:SSM_MAIN + SSM_HEADS][:, perm_s]
    ssm_small = jnp.concatenate([dt_cols, dt_cols, jnp.zeros((D_MODEL, LANES - 2 * SSM_HEADS), F32)], axis=1)
    return dict(
        w_mod=w_mod, b_mod=b_mod, norm_mix=norm_mix, norm_ffn=norm_ffn, norm_final=norm_final,
        gdn_main=g_in[:, :GDN_MAIN].astype(BF16), gdn_small=gdn_small.astype(BF16),
        gdn_conv_w=gdn_conv_w[0], gdn_conv_b=jnp.zeros((GDN_CONV_DIM,), F32),
        gdn_alog_row=_lane_row([(0, gdn_a_log[0][perm_g])]), gdn_dtb_row=_lane_row([(0, gdn_dt_bias[0][perm_g])]),
        gdn_norm=jnp.tile(gdn_norm[0], GDN_V_HEADS), gdn_w_out=gdn_w_out[0].astype(BF16),
        gdn_consts=_recurrence_consts(GDN_PAIRS, 0, GDN_PAIRS),
        ssm_main=jnp.concatenate(
            [s_in[:, SSM_D_INNER:2 * SSM_D_INNER], s_in[:, :SSM_D_INNER], s_in[:, 2 * SSM_D_INNER:SSM_MAIN]],
            axis=1).astype(BF16),
        ssm_small=ssm_small.astype(BF16),
        ssm_conv_w=ssm_conv_w[0], ssm_conv_b=ssm_conv_b[0],
        ssm_alog_row=_lane_row([(SSM_HEADS, ssm_a_log[0][perm_s])]),
        ssm_dtb_row=_lane_row([(0, ssm_dt_bias[0][perm_s]), (SSM_HEADS, ssm_dt_bias[0][perm_s])]),
        ssm_dskip_row=jnp.repeat(ssm_d[0], SSM_HEAD_DIM).reshape(1, SSM_D_INNER),
        ssm_norm=ssm_norm[0], ssm_w_out=ssm_w_out[0].astype(BF16),
        ssm_consts=_recurrence_consts(SSM_PAIRS, SSM_HEADS, SSM_HEADS + SSM_PAIRS),
        wg=[ffn_w_gate_up[i][:, :FFN_HIDDEN].astype(BF16) for i in range(2)],
        wu=[ffn_w_gate_up[i][:, FFN_HIDDEN:].astype(BF16) for i in range(2)],
        wd=[ffn_w_down[i].astype(BF16) for i in range(2)],
    )


def _ffn(x, layer, mod3, rows_up, rows_down, p, final_w):
    act = _ffn_up(x, p["norm_ffn"][layer], mod3, rows_up, p["wg"][layer], p["wu"][layer], FFN_TILE)
    return _ffn_down(act, x, mod3, rows_down, p["wd"][layer], final_w)


QSCALE = GDN_HEAD_DIM ** -0.5


def _trunk_seq(x3, mod, p):
    nseq, seq_len, _ = x3.shape
    m = nseq * seq_len
    x = x3.reshape(m, D_MODEL)
    mod3 = [mod[l].reshape(nseq, 1, 6 * D_MODEL) for l in range(2)]
    rows_a = _Rows(m, min(1024, seq_len), seq_len, 1)
    rows_b = _Rows(m, min(512, seq_len), seq_len, 1)

    pm, ps = _in_proj(x, p["norm_mix"][0], mod3[0], rows_a, 1, 0, p["gdn_main"], p["gdn_small"], GDN_PROJ_TILE)
    o, gdn_s = _gdn_seq(pm, ps, p["gdn_conv_w"], p["gdn_conv_b"], p["gdn_alog_row"], p["gdn_dtb_row"],
                        p["gdn_consts"], nseq, seq_len)
    tail = pm.reshape(nseq, seq_len, GDN_MAIN)[:, seq_len - (CONV_WIDTH - 1):].astype(F32)
    gdn_c = tail[..., :GDN_CONV_DIM]
    x = _mixer_out(o, pm, 2, p["gdn_norm"], x, mod3[0], rows_b, p["gdn_w_out"], GDN_HEAD_DIM, False)
    x = _ffn(x, 0, mod3[0], rows_a, rows_b, p, None)

    pm, ps = _in_proj(x, p["norm_mix"][1], mod3[1], rows_a, 1, 0, p["ssm_main"], p["ssm_small"], SSM_PROJ_TILE)
    y, ssm_h = _ssd_seq(pm, ps, p["ssm_conv_w"], p["ssm_conv_b"], p["ssm_alog_row"], p["ssm_dtb_row"],
                        p["ssm_dskip_row"], p["ssm_consts"], nseq, seq_len)
    tail = pm.reshape(nseq, seq_len, SSM_MAIN)[:, seq_len - (CONV_WIDTH - 1):].astype(F32)
    ssm_c = jnp.concatenate([tail[..., :SSM_D_INNER], tail[..., 2 * SSM_D_INNER:]], axis=-1)
    x = _mixer_out(y, pm, 1, p["ssm_norm"], x, mod3[1], rows_b, p["ssm_w_out"], SSM_D_INNER // SSM_GROUPS, True)
    _, y_out = _ffn(x, 1, mod3[1], rows_a, rows_b, p, p["norm_final"])

    return (y_out.reshape(nseq, seq_len, D_MODEL), gdn_s[None], gdn_c[None],
            ssm_h.reshape(nseq, SSM_HEADS, SSM_HEAD_DIM, SSM_STATE)[None], ssm_c[None])


def _trunk_step(x3, mod, st_gdn, cv_gdn, st_ssm, cv_ssm, p):
    nb, steps, _ = x3.shape
    assert steps >= CONV_WIDTH - 1
    m = nb * steps
    x = jnp.transpose(x3, (1, 0, 2)).reshape(m, D_MODEL)
    mod3 = [mod[l].reshape(1, nb, 6 * D_MODEL) for l in range(2)]
    rows = _Rows(m, m, None, nb)
    tok = lambda a: jnp.transpose(a, (1, 0, 2))

    pm, ps = _in_proj(x, p["norm_mix"][0], mod3[0], rows, 1, 0, p["gdn_main"], p["gdn_small"], GDN_PROJ_TILE)
    u3 = pm.reshape(steps, nb, GDN_MAIN)
    act3 = _conv_steps(u3, 0, 0, GDN_CONV_DIM, tok(cv_gdn[0]), p["gdn_conv_w"], p["gdn_conv_b"], 2, QSCALE)
    o, gdn_s = _gdn_step(act3, ps.reshape(steps, nb, LANES), p["gdn_alog_row"], p["gdn_dtb_row"], p["gdn_consts"],
                         st_gdn[0])
    gdn_c = tok(u3[steps - (CONV_WIDTH - 1):, :, :GDN_CONV_DIM].astype(F32))
    x = _mixer_out(o, pm, 2, p["gdn_norm"], x, mod3[0], rows, p["gdn_w_out"], GDN_HEAD_DIM, False)
    x = _ffn(x, 0, mod3[0], rows, rows, p, None)

    pm, ps = _in_proj(x, p["norm_mix"][1], mod3[1], rows, 1, 0, p["ssm_main"], p["ssm_small"], SSM_PROJ_TILE)
    u3 = pm.reshape(steps, nb, SSM_MAIN)
    act3 = _conv_steps(u3, SSM_D_INNER // CONV_COLS, SSM_D_INNER // CONV_COLS, SSM_CONV_DIM, tok(cv_ssm[0]),
                       p["ssm_conv_w"], p["ssm_conv_b"], 0, 1.0)
    h0 = st_ssm[0].reshape(nb, SSM_PAIRS, LANES, SSM_STATE)
    y, ssm_h = _ssd_step(act3, ps.reshape(steps, nb, LANES), p["ssm_alog_row"], p["ssm_dtb_row"],
                         p["ssm_dskip_row"], p["ssm_consts"], h0)
    tail = u3[steps - (CONV_WIDTH - 1):].astype(F32)
    ssm_c = tok(jnp.concatenate([tail[..., :SSM_D_INNER], tail[..., 2 * SSM_D_INNER:]], axis=-1))
    x = _mixer_out(y, pm, 1, p["ssm_norm"], x, mod3[1], rows, p["ssm_w_out"], SSM_D_INNER // SSM_GROUPS, True)
    _, y_out = _ffn(x, 1, mod3[1], rows, rows, p, p["norm_final"])

    return (tok(y_out.reshape(steps, nb, D_MODEL)), gdn_s[None], gdn_c[None],
            ssm_h.reshape(nb, SSM_HEADS, SSM_HEAD_DIM, SSM_STATE)[None], ssm_c[None])


def kernel(x_prompt, x_sample, c_prompt, c_sample, state_gdn, state_gdn_conv, state_ssm, state_ssm_conv, w_mod, b_mod,
           norm_mix, norm_ffn, norm_final, gdn_w_in, gdn_conv_w, gdn_a_log, gdn_dt_bias, gdn_norm, gdn_w_out, ssm_w_in,
           ssm_conv_w, ssm_conv_b, ssm_a_log, ssm_dt_bias, ssm_d, ssm_norm, ssm_w_out, ffn_w_gate_up, ffn_w_down):
    p = _stage_params(w_mod, b_mod, norm_mix, norm_ffn, norm_final, gdn_w_in, gdn_conv_w, gdn_a_log, gdn_dt_bias,
                      gdn_norm, gdn_w_out, ssm_w_in, ssm_conv_w, ssm_conv_b, ssm_a_log, ssm_dt_bias, ssm_d, ssm_norm,
                      ssm_w_out, ffn_w_gate_up, ffn_w_down)
    n_prompt = x_prompt.shape[0]
    mod = _modulation(jnp.concatenate([c_prompt, c_sample], axis=0), p["w_mod"], p["b_mod"])
    y_p, gs_p, gc_p, ss_p, sc_p = _trunk_seq(x_prompt, mod[:, :n_prompt], p)
    y_s, gs_s, gc_s, ss_s, sc_s = _trunk_step(x_sample, mod[:, n_prompt:], state_gdn, state_gdn_conv, state_ssm,
                                              state_ssm_conv, p)
    return (y_p, y_s, gs_p, gc_p, ss_p, sc_p, gs_s, gc_s, ss_s, sc_s)
```

```python
import functools

import numpy as np
import jax
import jax.numpy as jnp
from jax import lax
from jax.experimental import pallas as pl
from jax.experimental.pallas import tpu as pltpu

F32 = jnp.float32
BF16 = jnp.bfloat16

D_MODEL = 1024
EPS = 1e-6
CONV_WIDTH = 4
CHUNK = 64
LANES = 128
HALF = LANES // 2

GDN_QK_HEADS = 8
GDN_V_HEADS = 16
GDN_HEAD_DIM = 128
GDN_QK_DIM = GDN_QK_HEADS * GDN_HEAD_DIM
GDN_V_DIM = GDN_V_HEADS * GDN_HEAD_DIM
GDN_CONV_DIM = 2 * GDN_QK_DIM + GDN_V_DIM
GDN_MAIN = GDN_CONV_DIM + GDN_V_DIM

SSM_D_INNER = 2 * D_MODEL
SSM_HEAD_DIM = 64
SSM_HEADS = SSM_D_INNER // SSM_HEAD_DIM
SSM_GROUPS = 4
SSM_STATE = 128
SSM_BC = SSM_GROUPS * SSM_STATE
SSM_CONV_DIM = SSM_D_INNER + 2 * SSM_BC
SSM_MAIN = SSM_D_INNER + SSM_CONV_DIM

FFN_HIDDEN = 2816

VMEM_LIMIT = 56 * 1024 * 1024
NEG_BIG = -1e30


def _cparams(*sem):
    return pltpu.CompilerParams(dimension_semantics=sem, vmem_limit_bytes=VMEM_LIMIT)


def _silu(x):
    hx = 0.5 * x
    return hx + hx * jnp.tanh(hx)


def _softplus(x):
    return jnp.maximum(x, 0.0) + jnp.log1p(jnp.exp(-jnp.abs(x)))


def _dot(a, b):
    return jnp.dot(a, b, preferred_element_type=F32)


def _dot_nt(a, b):
    return lax.dot_general(a, b, (((1,), (1,)), ((), ())), preferred_element_type=F32)


def _tile_rows(v, rep):
    return v if rep == 1 else jnp.concatenate([v] * rep, axis=0)


def _mod_kernel(c_ref, w_ref, b_ref, o_ref):
    cs = _silu(c_ref[...]).astype(BF16)
    o_ref[...] = _dot(cs, w_ref[...].astype(BF16)) + b_ref[...]


def _modulation(c, w_mod, b_mod):
    depth, _, n = w_mod.shape
    bc = c.shape[0]
    tn = 1536
    return pl.pallas_call(
        _mod_kernel,
        grid=(depth, n // tn),
        in_specs=[
            pl.BlockSpec((bc, D_MODEL), lambda l, j: (0, 0)),
            pl.BlockSpec((None, D_MODEL, tn), lambda l, j: (l, 0, j)),
            pl.BlockSpec((None, 1, tn), lambda l, j: (l, 0, j)),
        ],
        out_specs=pl.BlockSpec((None, bc, tn), lambda l, j: (l, 0, j)),
        out_shape=jax.ShapeDtypeStruct((depth, bc, n), F32),
        compiler_params=_cparams("parallel", "parallel"),
        name="adaln_mod",
    )(c, w_mod, b_mod.reshape(depth, 1, n))


def _norm_mod(x, nw, sc, sh, rep):
    y = x * lax.rsqrt(jnp.mean(x * x, axis=-1, keepdims=True) + EPS) * nw
    return y * (1.0 + _tile_rows(sc, rep)) + _tile_rows(sh, rep)


def _in_proj_kernel(x_ref, nw_ref, sc_ref, sh_ref, w_ref, w2_ref, o_ref, o2_ref, h_ref, *, rep):
    @pl.when(pl.program_id(1) == 0)
    def _():
        h = _norm_mod(x_ref[...], nw_ref[...], sc_ref[...], sh_ref[...], rep).astype(BF16)
        h_ref[...] = h
        o2_ref[...] = _dot(h, w2_ref[...])

    o_ref[...] = _dot(h_ref[...], w_ref[...].astype(BF16)).astype(o_ref.dtype)


def _ffn_up_kernel(x_ref, nw_ref, sc_ref, sh_ref, wg_ref, wu_ref, o_ref, h_ref, *, rep):
    @pl.when(pl.program_id(1) == 0)
    def _():
        h_ref[...] = _norm_mod(x_ref[...], nw_ref[...], sc_ref[...], sh_ref[...], rep).astype(BF16)

    h = h_ref[...]
    o_ref[...] = (_silu(_dot(h, wg_ref[...].astype(BF16))) * _dot(h, wu_ref[...].astype(BF16))).astype(BF16)


class _Rows:
    def __init__(self, m, tm, group_rows, mod_rows):
        assert m % tm == 0
        self.m, self.tm = m, tm
        if mod_rows == 1:
            assert group_rows % tm == 0
            self.rep = 1
            self.gmap = lambda i: (i * tm) // group_rows
        else:
            assert tm % mod_rows == 0
            self.rep = tm // mod_rows
            self.gmap = lambda i: 0
        self.mod_rows = mod_rows

    def mod_spec(self, col_block, with_j):
        if with_j:
            return pl.BlockSpec((None, self.mod_rows, D_MODEL), lambda i, j: (self.gmap(i), 0, col_block))
        return pl.BlockSpec((None, self.mod_rows, D_MODEL), lambda i: (self.gmap(i), 0, col_block))


def _in_proj(x, nw, mod3, rows, sc_blk, sh_blk, w, wmap, n, w2, tn):
    m, tm = rows.m, rows.tm
    assert n % tn == 0
    return pl.pallas_call(
        functools.partial(_in_proj_kernel, rep=rows.rep),
        grid=(m // tm, n // tn),
        in_specs=[
            pl.BlockSpec((tm, D_MODEL), lambda i, j: (i, 0)),
            pl.BlockSpec((1, D_MODEL), lambda i, j: (0, 0)),
            rows.mod_spec(sc_blk, True),
            rows.mod_spec(sh_blk, True),
            pl.BlockSpec((D_MODEL, tn), lambda i, j: (0, wmap(j))),
            pl.BlockSpec((D_MODEL, LANES), lambda i, j: (0, 0)),
        ],
        out_specs=[
            pl.BlockSpec((tm, tn), lambda i, j: (i, j)),
            pl.BlockSpec((tm, LANES), lambda i, j: (i, 0)),
        ],
        out_shape=[jax.ShapeDtypeStruct((m, n), BF16), jax.ShapeDtypeStruct((m, LANES), F32)],
        scratch_shapes=[pltpu.VMEM((tm, D_MODEL), BF16)],
        compiler_params=_cparams("parallel", "arbitrary"),
        name="in_proj",
    )(x, nw.reshape(1, D_MODEL), mod3, mod3, w, w2)


def _ffn_up(x, nw, mod3, rows, w_gate_up, th):
    m, tm = rows.m, rows.tm
    assert FFN_HIDDEN % th == 0
    nblk = FFN_HIDDEN // th
    return pl.pallas_call(
        functools.partial(_ffn_up_kernel, rep=rows.rep),
        grid=(m // tm, FFN_HIDDEN // th),
        in_specs=[
            pl.BlockSpec((tm, D_MODEL), lambda i, j: (i, 0)),
            pl.BlockSpec((1, D_MODEL), lambda i, j: (0, 0)),
            rows.mod_spec(4, True),
            rows.mod_spec(3, True),
            pl.BlockSpec((D_MODEL, th), lambda i, j: (0, j)),
            pl.BlockSpec((D_MODEL, th), lambda i, j: (0, j + nblk)),
        ],
        out_specs=pl.BlockSpec((tm, th), lambda i, j: (i, j)),
        out_shape=jax.ShapeDtypeStruct((m, FFN_HIDDEN), BF16),
        scratch_shapes=[pltpu.VMEM((tm, D_MODEL), BF16)],
        compiler_params=_cparams("parallel", "arbitrary"),
        name="ffn_up",
    )(x, nw.reshape(1, D_MODEL), mod3, mod3, w_gate_up, w_gate_up)


def _resid_store(acc, x_ref, gt_ref, o_ref, fnw_ref, y_ref, rep):
    xn = x_ref[...] + _tile_rows(gt_ref[...], rep) * acc
    o_ref[...] = xn
    if y_ref is not None:
        y_ref[...] = xn * lax.rsqrt(jnp.mean(xn * xn, axis=-1, keepdims=True) + EPS) * fnw_ref[...]


def _cast_weight_once(w_ref, wbf_ref):
    @pl.when(pl.program_id(0) == 0)
    def _():
        wbf_ref[...] = w_ref[...].astype(BF16)


def _ffn_down_kernel(a_ref, x_ref, gt_ref, w_ref, *rest, rep, final):
    if final:
        fnw_ref, o_ref, y_ref, wbf_ref = rest
    else:
        (o_ref, wbf_ref), fnw_ref, y_ref = rest, None, None
    _cast_weight_once(w_ref, wbf_ref)
    _resid_store(_dot(a_ref[...], wbf_ref[...]), x_ref, gt_ref, o_ref, fnw_ref, y_ref, rep)


def _ffn_down(act, x, mod3, rows, w, fnw):
    m, tm = rows.m, rows.tm
    final = fnw is not None
    in_specs = [
        pl.BlockSpec((tm, FFN_HIDDEN), lambda i: (i, 0)),
        pl.BlockSpec((tm, D_MODEL), lambda i: (i, 0)),
        rows.mod_spec(5, False),
        pl.BlockSpec((FFN_HIDDEN, D_MODEL), lambda i: (0, 0)),
    ]
    args = [act, x, mod3, w]
    row_spec = pl.BlockSpec((tm, D_MODEL), lambda i: (i, 0))
    out_shape = jax.ShapeDtypeStruct((m, D_MODEL), F32)
    if final:
        in_specs.append(pl.BlockSpec((1, D_MODEL), lambda i: (0, 0)))
        args.append(fnw.reshape(1, D_MODEL))
        out_specs, out_shapes = [row_spec, row_spec], [out_shape, out_shape]
    else:
        out_specs, out_shapes = row_spec, out_shape
    return pl.pallas_call(
        functools.partial(_ffn_down_kernel, rep=rows.rep, final=final),
        grid=(m // tm,),
        in_specs=in_specs,
        out_specs=out_specs,
        out_shape=out_shapes,
        scratch_shapes=[pltpu.VMEM((FFN_HIDDEN, D_MODEL), BF16)],
        compiler_params=_cparams("arbitrary"),
        name="ffn_down",
    )(*args)


def _mixer_out_kernel(y_ref, z_ref, nw_ref, x_ref, gt_ref, w_ref, o_ref, a_ref, wbf_ref, *, rep, group, gate_first):
    _cast_weight_once(w_ref, wbf_ref)
    width = y_ref.shape[1]
    for s in range(0, width, group):
        y = y_ref[:, s:s + group]
        gate = _silu(z_ref[:, s:s + group].astype(F32))
        if gate_first:
            y = y * gate
        y = y * lax.rsqrt(jnp.mean(y * y, axis=-1, keepdims=True) + EPS) * nw_ref[:, s:s + group]
        if not gate_first:
            y = y * gate
        a_ref[:, s:s + group] = y.astype(BF16)
    _resid_store(_dot(a_ref[...], wbf_ref[...]), x_ref, gt_ref, o_ref, None, None, rep)


def _mixer_out(y, zsrc, z_blk, nw_full, x, mod3, rows, w, group, gate_first):
    m, tm = rows.m, rows.tm
    width = y.shape[1]
    return pl.pallas_call(
        functools.partial(_mixer_out_kernel, rep=rows.rep, group=group, gate_first=gate_first),
        grid=(m // tm,),
        in_specs=[
            pl.BlockSpec((tm, width), lambda i: (i, 0)),
            pl.BlockSpec((tm, width), lambda i: (i, z_blk)),
            pl.BlockSpec((1, width), lambda i: (0, 0)),
            pl.BlockSpec((tm, D_MODEL), lambda i: (i, 0)),
            rows.mod_spec(2, False),
            pl.BlockSpec((width, D_MODEL), lambda i: (0, 0)),
        ],
        out_specs=pl.BlockSpec((tm, D_MODEL), lambda i: (i, 0)),
        out_shape=jax.ShapeDtypeStruct((m, D_MODEL), F32),
        scratch_shapes=[pltpu.VMEM((tm, width), BF16), pltpu.VMEM((width, D_MODEL), BF16)],
        compiler_params=_cparams("arbitrary"),
        name="mixer_out",
    )(y, zsrc, nw_full.reshape(1, width), x, mod3, w)


CONV_COLS = 1024


def _post_conv(acc, o_ref, cb, n_l2, qscale):
    y = _silu(acc)
    if n_l2 == 0:
        o_ref[...] = y
        return

    @pl.when(cb < n_l2)
    def _():
        scale = jnp.where(cb == 0, qscale, 1.0).astype(F32)
        for s in range(0, CONV_COLS, GDN_HEAD_DIM):
            yh = y[:, s:s + GDN_HEAD_DIM]
            o_ref[:, s:s + GDN_HEAD_DIM] = yh * lax.rsqrt(jnp.sum(yh * yh, axis=-1, keepdims=True) + EPS) * scale

    @pl.when(cb >= n_l2)
    def _():
        o_ref[...] = y


BHALO = 16
CONV_GROUP = 2 * LANES


def _shift_matrix():
    s = np.zeros(((CONV_WIDTH - 1) * CHUNK, BHALO + CHUNK), np.float32)
    for tap in range(CONV_WIDTH - 1):
        for r in range(CHUNK):
            s[tap * CHUNK + r, BHALO - (CONV_WIDTH - 1) + tap + r] = 1.0
    return jnp.asarray(s, BF16)


def _conv_block(raw_refs, shift_ref, w_ref, b_ref, ext_ref, act_ref, seq_start, l2_cols, q_cols, qscale):
    width = ext_ref.shape[1]
    if seq_start:
        ext_ref[0:BHALO, :] = jnp.zeros((BHALO, width), BF16)
    else:
        ext_ref[0:BHALO, :] = ext_ref[CHUNK:CHUNK + BHALO, :]

    off = 0
    for ref in raw_refs:
        ext_ref[BHALO:BHALO + CHUNK, off:off + ref.shape[1]] = ref[...]
        off += ref.shape[1]

    def lane_group(s):
        cols = slice(s, s + CONV_GROUP)
        sh = _dot(shift_ref[...], ext_ref[:, cols])
        acc = b_ref[:, cols] + w_ref[CONV_WIDTH - 1:CONV_WIDTH, cols] * ext_ref[BHALO:BHALO + CHUNK, cols].astype(F32)
        for tap in range(CONV_WIDTH - 1):
            acc = acc + w_ref[tap:tap + 1, cols] * sh[tap * CHUNK:(tap + 1) * CHUNK]
        y = _silu(acc)
        for h in range(s, s + CONV_GROUP, LANES):
            yh = y[:, h - s:h - s + LANES]
            if h < l2_cols:
                yh = yh * lax.rsqrt(jnp.sum(yh * yh, axis=-1, keepdims=True) + EPS)
                if h < q_cols:
                    yh = yh * qscale
            act_ref[:, h:h + LANES] = yh

    return [functools.partial(lane_group, s) for s in range(0, width, CONV_GROUP)]


class _Filler:
    def __init__(self, tasks, shares):
        self.tasks, self.per = list(tasks), -(-len(tasks) // shares)

    def emit(self):
        for task in self.tasks[:self.per]:
            task()
        self.tasks = self.tasks[self.per:]

    def flush(self):
        for task in self.tasks:
            task()
        self.tasks = []


_NO_FILL = _Filler([], 1)


def _conv_steps_kernel(u_ref, hist_ref, w_ref, b_ref, o_ref, *, steps, n_l2, qscale):
    cb = pl.program_id(0)
    ext = [hist_ref[i] for i in range(CONV_WIDTH - 1)] + [u_ref[i].astype(F32) for i in range(steps)]
    for t in range(steps):
        acc = b_ref[...] + w_ref[0:1, :] * ext[t]
        for tap in range(1, CONV_WIDTH):
            acc = acc + w_ref[tap:tap + 1, :] * ext[t + tap]
        _post_conv(acc, o_ref.at[t], cb, n_l2, qscale)


def _conv_steps(u3, skip_at, skip, n_cols, hist3, conv_w, conv_b, n_l2, qscale):
    steps, nb, _ = u3.shape
    return pl.pallas_call(
        functools.partial(_conv_steps_kernel, steps=steps, n_l2=n_l2, qscale=qscale),
        grid=(n_cols // CONV_COLS,),
        in_specs=[
            pl.BlockSpec((steps, nb, CONV_COLS), lambda c: (0, 0, c + skip * (c >= skip_at))),
            pl.BlockSpec((CONV_WIDTH - 1, nb, CONV_COLS), lambda c: (0, 0, c)),
            pl.BlockSpec((CONV_WIDTH, CONV_COLS), lambda c: (0, c)),
            pl.BlockSpec((1, CONV_COLS), lambda c: (0, c)),
        ],
        out_specs=pl.BlockSpec((steps, nb, CONV_COLS), lambda c: (0, 0, c)),
        out_shape=jax.ShapeDtypeStruct((steps, nb, n_cols), F32),
        compiler_params=_cparams("parallel"),
        name="conv_steps",
    )(u3, hist3, conv_w, conv_b.reshape(1, n_cols))


def _split3(x):
    hi = x.astype(BF16)
    r = x - hi.astype(F32)
    mid = r.astype(BF16)
    lo = (r - mid.astype(F32)).astype(BF16)
    return hi, mid, lo


def _cat3(x, axis):
    return jnp.concatenate(_split3(x), axis=axis)


def _pad_t(tile):
    return jnp.concatenate([tile, jnp.zeros_like(tile)], axis=0).T


def _pair_rows(t, n):
    return t[0:n] + pltpu.roll(t[n:2 * n], HALF, 1)


def _iotas(shape):
    return lax.broadcasted_iota(jnp.int32, shape, 0), lax.broadcasted_iota(jnp.int32, shape, 1)


def _pad_rows(a):
    if a.shape[0] == CHUNK:
        return a
    return jnp.concatenate([a, jnp.zeros((CHUNK - a.shape[0], a.shape[1]), a.dtype)], axis=0)


def _split2(x):
    hi = x.astype(BF16)
    return hi, (x - hi.astype(F32)).astype(BF16)


def _block_diag(pair_bf, bd_ones):
    pair_bf = _pad_rows(pair_bf)
    return jnp.concatenate([pair_bf, pair_bf], axis=0) * bd_ones


def _mm_pairs(lhs_parts, rhs_parts, bd_ones):
    outs = []
    for (lh, ll), (rh, rl) in zip(lhs_parts, rhs_parts):
        rh_bd = _block_diag(rh, bd_ones)
        outs.append(_dot(jnp.concatenate([lh, lh, ll], axis=1),
                         jnp.concatenate([rh_bd, _block_diag(rl, bd_ones), rh_bd], axis=0)))
    return outs


def _mm_hl(lhs, rhs_bf):
    lh = lhs.astype(BF16)
    ll = (lhs - lh.astype(F32)).astype(BF16)
    return _dot(jnp.concatenate([lh, ll], axis=1), jnp.concatenate([rhs_bf, rhs_bf], axis=0))


def _unit_lower_inverse(xs, levels, bd_ones, eye2, fill):
    ps = [eye2 + x for x in xs]
    if levels <= 1:
        return ps
    rows = xs[0].shape[0]
    stack = lambda a, b: tuple(jnp.concatenate([s, t], axis=0) for s, t in zip(a, b))
    ysp = [_split2(x) for x in xs]
    ys = _mm_pairs(ysp, ysp, bd_ones)
    fill.emit()
    for _ in range(levels - 2):
        ysp = [_split2(y) for y in ys]
        rs = _mm_pairs([stack(yp, _split2(p)) for yp, p in zip(ysp, ps)], ysp, bd_ones)
        fill.emit()
        ys = [r[0:rows] for r in rs]
        ps = [p + r[rows:2 * rows] for p, r in zip(ps, rs)]
    last = _mm_pairs([_split2(p) for p in ps], [_split2(y) for y in ys], bd_ones)
    return [p + t for p, t in zip(ps, last)]


def _two_blocks(a, b):
    a_bf, b_bf = _pad_rows(a).astype(BF16), _pad_rows(b).astype(BF16)
    z = jnp.zeros_like(a_bf)
    return jnp.concatenate([jnp.concatenate([a_bf, z], axis=1), jnp.concatenate([z, b_bf], axis=1)], axis=0)


def _recurrence_consts(n_pairs, chan_even0, chan_odd0):
    tril = np.tril(np.ones((CHUNK, CHUNK), np.float32))
    tril3 = np.concatenate([tril] * 3, axis=1)
    upper = np.zeros((LANES, 2 * LANES), np.float32)
    upper[:CHUNK, :CHUNK] = tril.T
    upper[:CHUNK, LANES:] = 1.0
    upper3 = np.concatenate([upper] * 3, axis=0)
    e = np.zeros((LANES, n_pairs * LANES), np.float32)
    for p in range(n_pairs):
        e[chan_even0 + p, p * LANES:p * LANES + HALF] = 1.0
        e[chan_odd0 + p, p * LANES + HALF:(p + 1) * LANES] = 1.0
    e3 = np.concatenate([e] * 3, axis=0)
    return jnp.asarray(tril3, BF16), jnp.asarray(upper3, BF16), jnp.asarray(e3, BF16)


def _levels(lr):
    return max(1, int(np.ceil(np.log2(lr))))


STEP_ROWS = 16


GDN_PAIRS = GDN_V_HEADS // 2
GDN_PAIR_BATCH = 8
GDN_FILL_SHARES = 11 * (GDN_PAIRS // GDN_PAIR_BATCH)


def _gdn_gates(raw, alog_row, dtb_row, lr):
    row, lane = _iotas((CHUNK, LANES))
    g = -jnp.exp(alog_row) * _softplus(raw + dtb_row)
    beta = jax.nn.sigmoid(raw)
    gt = jnp.where(lane < GDN_V_HEADS, g, jnp.where(lane < 2 * GDN_V_HEADS, beta, 0.0))
    if lr < CHUNK:
        gt = jnp.where(row < lr, gt, 0.0)
    return gt


def _gdn_block(act_ref, gt, s_in, s_out, o_ref, tril3_ref, upper3_ref, e3_ref, lr, fill=_NO_FILL):
    np_ = GDN_PAIRS
    rows = act_ref.shape[0]
    row, lane = _iotas((rows, LANES))
    jl = jnp.bitwise_and(lane, HALF - 1)
    left = lane < HALF
    causal = row >= jl
    strict = row > jl
    eye2 = jnp.where(row == jl, 1.0, 0.0)
    r2, l2 = _iotas((LANES, LANES))
    bdmask = jnp.right_shift(r2, 6) == jnp.right_shift(l2, 6)
    _, lane_p = _iotas((np_, LANES))

    t1 = _pad_t(gt)
    cum_ext = _dot(_cat3(t1[0:2 * np_], 1), upper3_ref[...])
    cum_t, last_b = cum_ext[:, :LANES], cum_ext[:, LANES:]
    cum_rp = _pair_rows(cum_t, np_)
    beta_rp = _pair_rows(t1[2 * np_:4 * np_], np_)
    last_rp = jnp.where(lane_p < HALF, last_b[0:np_], last_b[np_:2 * np_])
    ecum_rp = jnp.exp(cum_rp)
    kdec_rp = jnp.exp(last_rp - cum_rp) * beta_rp
    elast = jnp.exp(last_b)
    cum = _dot(tril3_ref[...], _cat3(gt, 0))
    col_all = _dot(_cat3(cum[0:rows], 1), e3_ref[...])

    bd_ones = jnp.where(bdmask, 1.0, 0.0).astype(BF16)
    sl = lambda a, i: a[:, i * LANES:(i + 1) * LANES]

    def run(pairs):
        heads = [2 * p + hh for p in pairs for hh in range(2)]
        q = {p: sl(act_ref, p) for p in pairs}
        k = {p: _pad_rows(sl(act_ref, GDN_QK_HEADS + p)) for p in pairs}
        v = {h: sl(act_ref, 2 * GDN_QK_HEADS + h) for h in heads}
        kb = {p: k[p].astype(BF16) for p in pairs}
        qb = {p: q[p].astype(BF16) for p in pairs}
        gq = {p: _dot_nt(jnp.concatenate([kb[p][0:rows], qb[p]], axis=0), jnp.concatenate([kb[p], kb[p]], axis=0))
              for p in pairs}
        fill.emit()
        colc = {p: sl(col_all, p) for p in pairs}
        base = {p: jnp.exp(jnp.where(causal, colc[p] - cum_rp[p:p + 1], NEG_BIG)) * beta_rp[p:p + 1] for p in pairs}
        x = [jnp.where(strict, -(gq[p][0:rows] * base[p]), 0.0) for p in pairs]
        qkd = {p: (gq[p][rows:2 * rows] * base[p]).astype(BF16) for p in pairs}
        minv = dict(zip(pairs, _unit_lower_inverse(x, _levels(lr), bd_ones, eye2, fill)))
        fill.emit()
        u = {p: jnp.concatenate([v[2 * p], v[2 * p + 1]], axis=1)
             + _mm_hl(minv[p] - eye2, _two_blocks(v[2 * p], v[2 * p + 1])) for p in pairs}
        fill.emit()
        w = {p: _mm_hl(minv[p] * ecum_rp[p:p + 1], _two_blocks(kb[p], kb[p])) for p in pairs}
        fill.emit()
        s_old = {h: s_in[h] for h in heads}
        r = {h: _dot(jnp.concatenate([sl(w[h // 2], h % 2).astype(BF16), qb[h // 2]], axis=0),
                     s_old[h].astype(BF16)) for h in heads}
        fill.emit()
        delta = {h: sl(u[h // 2], h % 2) - r[h][0:rows] for h in heads}
        bd_delta = {p: _two_blocks(delta[2 * p], delta[2 * p + 1]) for p in pairs}
        fill.emit()
        od = {p: _dot(jnp.concatenate(
            [qkd[p], (jnp.concatenate([k[p], k[p]], axis=0).T * kdec_rp[p:p + 1]).astype(BF16)], axis=0), bd_delta[p])
            for p in pairs}
        o_intra = {p: od[p][0:rows] for p in pairs}
        ds = {p: od[p][rows:rows + LANES] for p in pairs}
        for p in pairs:
            ecol = jnp.exp(colc[p])
            ecol_r = pltpu.roll(ecol, HALF, 1)
            efull = (jnp.where(left, ecol, ecol_r), jnp.where(left, ecol_r, ecol))
            for hh in range(2):
                h = 2 * p + hh
                o_ref[:, h * LANES:(h + 1) * LANES] = efull[hh] * r[h][rows:2 * rows] + sl(o_intra[p], hh)
                e_h = elast[hh * np_ + p:hh * np_ + p + 1]
                s_out[h] = e_h * s_old[h] + sl(ds[p], hh)

    for b0 in range(0, np_, GDN_PAIR_BATCH):
        run(list(range(b0, b0 + GDN_PAIR_BATCH)))
    fill.flush()


def _gdn_seq_kernel(raw0_ref, rawn_ref, graw_ref, shift_ref, cw_ref, cb_ref, alog_ref, dtb_ref, tril3_ref, upper3_ref,
                    e3_ref, o_ref, s_ref, ext_ref, act_ref):
    n = pl.program_id(1)
    conv = functools.partial(_conv_block, shift_ref=shift_ref, w_ref=cw_ref, b_ref=cb_ref, ext_ref=ext_ref,
                             l2_cols=2 * GDN_QK_DIM, q_cols=GDN_QK_DIM, qscale=QSCALE)

    @pl.when(n == 0)
    def _():
        s_ref[...] = jnp.zeros(s_ref.shape, F32)
        _Filler(conv([raw0_ref], act_ref=act_ref.at[0], seq_start=True), 1).flush()

    @pl.when(n > 0)
    def _():
        act_ref[0] = act_ref[1]

    gt = _gdn_gates(graw_ref[...], alog_ref[...], dtb_ref[...], CHUNK)
    fill = _Filler(conv([rawn_ref], act_ref=act_ref.at[1], seq_start=False), GDN_FILL_SHARES)
    _gdn_block(act_ref.at[0], gt, s_ref, s_ref, o_ref, tril3_ref, upper3_ref, e3_ref, CHUNK, fill)


def _const_specs(consts, ngrid):
    zero = (lambda *_: (0, 0))
    return [pl.BlockSpec(c.shape, zero) for c in consts]


def _gdn_seq(pm, graw, conv_w, conv_b, alog_row, dtb_row, consts, nseq, seq_len):
    nc = seq_len // CHUNK
    last = nseq * nc - 1
    rowspec = pl.BlockSpec((1, LANES), lambda b, n: (0, 0))
    return pl.pallas_call(
        _gdn_seq_kernel,
        grid=(nseq, nc),
        in_specs=[
            pl.BlockSpec((CHUNK, GDN_CONV_DIM), lambda b, n: (b * nc, 0)),
            pl.BlockSpec((CHUNK, GDN_CONV_DIM), lambda b, n: (jnp.minimum(b * nc + n + 1, last), 0)),
            pl.BlockSpec((CHUNK, LANES), lambda b, n: (b * nc + n, 0)),
            pl.BlockSpec(((CONV_WIDTH - 1) * CHUNK, BHALO + CHUNK), lambda b, n: (0, 0)),
            pl.BlockSpec((CONV_WIDTH, GDN_CONV_DIM), lambda b, n: (0, 0)),
            pl.BlockSpec((1, GDN_CONV_DIM), lambda b, n: (0, 0)),
            rowspec, rowspec,
        ] + _const_specs(consts, 2),
        out_specs=[
            pl.BlockSpec((CHUNK, GDN_V_DIM), lambda b, n: (b * nc + n, 0)),
            pl.BlockSpec((None, GDN_V_HEADS, GDN_HEAD_DIM, GDN_HEAD_DIM), lambda b, n: (b, 0, 0, 0)),
        ],
        out_shape=[
            jax.ShapeDtypeStruct((nseq * seq_len, GDN_V_DIM), F32),
            jax.ShapeDtypeStruct((nseq, GDN_V_HEADS, GDN_HEAD_DIM, GDN_HEAD_DIM), F32),
        ],
        scratch_shapes=[
            pltpu.VMEM((CHUNK + BHALO, GDN_CONV_DIM), BF16),
            pltpu.VMEM((2, CHUNK, GDN_CONV_DIM), F32),
        ],
        compiler_params=_cparams("parallel", "arbitrary"),
        name="gdn_seq",
    )(pm, pm, graw, _shift_matrix(), conv_w, conv_b.reshape(1, GDN_CONV_DIM), alog_row, dtb_row, *consts)


SEQ_PER_STEP = 8


def _load_padded(src_ref, s, pad_ref, lr):
    pad_ref[...] = jnp.zeros(pad_ref.shape, F32)
    for t in range(lr):
        pad_ref[t:t + 1, :] = src_ref[t, pl.ds(s, 1), :]


def _store_tokens(pad_ref, dst_ref, s, lr):
    for t in range(lr):
        dst_ref[t, pl.ds(s, 1), :] = pad_ref[t:t + 1, :]


def _gdn_step_kernel(act_ref, graw_ref, alog_ref, dtb_ref, tril3_ref, upper3_ref, e3_ref, s0_ref,
                     o_ref, s_ref, apad_ref, gpad_ref, opad_ref, *, lr):
    def one_sequence(s, carry):
        _load_padded(act_ref, s, apad_ref, lr)
        _load_padded(graw_ref, s, gpad_ref, lr)
        gt = _gdn_gates(gpad_ref[...], alog_ref[...], dtb_ref[...], lr)
        _gdn_block(apad_ref, gt, s0_ref.at[s], s_ref.at[s], opad_ref, tril3_ref, upper3_ref, e3_ref, lr)
        _store_tokens(opad_ref, o_ref, s, lr)
        return carry

    lax.fori_loop(0, SEQ_PER_STEP, one_sequence, 0)


def _step_spec(steps, width):
    return pl.BlockSpec((steps, SEQ_PER_STEP, width), lambda g: (0, g, 0))


def _gdn_step(act3, graw3, alog_row, dtb_row, consts, s0):
    steps, nb, _ = act3.shape
    assert nb % SEQ_PER_STEP == 0 and steps <= STEP_ROWS
    rowspec = pl.BlockSpec((1, LANES), lambda b: (0, 0))
    sspec = pl.BlockSpec((SEQ_PER_STEP, GDN_V_HEADS, GDN_HEAD_DIM, GDN_HEAD_DIM), lambda g: (g, 0, 0, 0))
    o, s = pl.pallas_call(
        functools.partial(_gdn_step_kernel, lr=steps),
        grid=(nb // SEQ_PER_STEP,),
        in_specs=[_step_spec(steps, GDN_CONV_DIM), _step_spec(steps, LANES), rowspec, rowspec]
        + _const_specs(consts, 1) + [sspec],
        out_specs=[_step_spec(steps, GDN_V_DIM), sspec],
        out_shape=[
            jax.ShapeDtypeStruct((steps, nb, GDN_V_DIM), F32),
            jax.ShapeDtypeStruct(s0.shape, F32),
        ],
        scratch_shapes=[
            pltpu.VMEM((STEP_ROWS, GDN_CONV_DIM), F32),
            pltpu.VMEM((CHUNK, LANES), F32),
            pltpu.VMEM((STEP_ROWS, GDN_V_DIM), F32),
        ],
        compiler_params=_cparams("parallel"),
        name="gdn_step",
    )(act3, graw3, alog_row, dtb_row, *consts, s0)
    return o.reshape(steps * nb, GDN_V_DIM), s


SSM_PAIRS = SSM_HEADS // 2
PAIRS_PER_GROUP = SSM_PAIRS // SSM_GROUPS


def _ssm_gates(raw, alog_row, dtb_row, lr):
    row, lane = _iotas((CHUNK, LANES))
    dt = _softplus(raw + dtb_row)
    tile = jnp.where(lane < SSM_HEADS, dt, jnp.where(lane < 2 * SSM_HEADS, -jnp.exp(alog_row) * dt, 0.0))
    if lr < CHUNK:
        tile = jnp.where(row < lr, tile, 0.0)
    return tile


def _ssd_block(act_ref, tile, h_in, h_out, y_ref, dskip_ref, tril3_ref, upper3_ref, e3_ref, fill=_NO_FILL):
    np_ = SSM_PAIRS
    rows = act_ref.shape[0]
    row, lane = _iotas((rows, LANES))
    causal = row >= jnp.bitwise_and(lane, HALF - 1)
    r2, l2 = _iotas((LANES, LANES))
    bdmask = jnp.right_shift(r2, 6) == jnp.right_shift(l2, 6)
    top = r2 < HALF
    _, lane_p = _iotas((np_, LANES))

    t1 = _pad_t(tile)
    cum_ext = _dot(_cat3(t1[2 * np_:4 * np_], 1), upper3_ref[...])
    cum_t, last_b = cum_ext[:, :LANES], cum_ext[:, LANES:]
    cum_rp = _pair_rows(cum_t, np_)
    dt_rp = _pair_rows(t1[0:2 * np_], np_)
    last_rp = jnp.where(lane_p < HALF, last_b[0:np_], last_b[np_:2 * np_])
    coef_rp = jnp.exp(last_rp - cum_rp) * dt_rp
    elast = jnp.exp(last_b)
    cum = _dot(tril3_ref[...], _cat3(tile, 0))
    col_all = _dot(_cat3(cum[0:rows], 1), e3_ref[...])

    pairs = range(np_)
    sl = lambda a, i: a[:, i * LANES:(i + 1) * LANES]
    grp = lambda p: p // PAIRS_PER_GROUP
    bg = [_pad_rows(sl(act_ref, SSM_D_INNER // LANES + g)).astype(BF16) for g in range(SSM_GROUPS)]
    cg = [sl(act_ref, (SSM_D_INNER + SSM_BC) // LANES + g).astype(BF16) for g in range(SSM_GROUPS)]
    bb = [jnp.concatenate([b, b], axis=0) for b in bg]
    cb2 = [_dot_nt(cg[g], bb[g]) for g in range(SSM_GROUPS)]
    fill.emit()
    xp = [sl(act_ref, p) for p in pairs]
    colc = [sl(col_all, p) for p in pairs]
    x2 = [jnp.concatenate([_pad_rows(a), _pad_rows(a)], axis=0) for a in xp]
    lm = [cb2[grp(p)] * jnp.exp(jnp.where(causal, colc[p] - cum_rp[p:p + 1], NEG_BIG)) * dt_rp[p:p + 1] for p in pairs]
    y_diag = []
    for p in pairs:
        y_diag.append(_dot(lm[p].astype(BF16), jnp.where(bdmask, x2[p], 0.0).astype(BF16)))
        if p % 4 == 3:
            fill.emit()
    hp = [h_in[p] for p in pairs]
    y_off = []
    for p in pairs:
        y_off.append(_dot_nt(cg[grp(p)], hp[p].astype(BF16)))
        if p % 4 == 3:
            fill.emit()
    for p in pairs:
        y_ref[:, p * LANES:(p + 1) * LANES] = (y_diag[p] + jnp.exp(colc[p]) * y_off[p]
                                               + dskip_ref[:, p * LANES:(p + 1) * LANES] * xp[p])
    dh = []
    for p in pairs:
        lhs = jnp.where(bdmask, x2[p].T * coef_rp[p:p + 1], 0.0)
        dh.append(_dot(lhs.astype(BF16), bb[grp(p)]))
        if p % 4 == 3:
            fill.emit()
    for p in pairs:
        e_rows = jnp.where(top, elast[p:p + 1], elast[np_ + p:np_ + p + 1])
        h_out[p] = e_rows * hp[p] + dh[p]
    fill.flush()


def _ssd_seq_kernel(x0_ref, bc0_ref, xn_ref, bcn_ref, graw_ref, shift_ref, cw_ref, cb_ref, alog_ref, dtb_ref,
                    dskip_ref, tril3_ref, upper3_ref, e3_ref, y_ref, h_ref, ext_ref, act_ref):
    n = pl.program_id(1)
    conv = functools.partial(_conv_block, shift_ref=shift_ref, w_ref=cw_ref, b_ref=cb_ref, ext_ref=ext_ref,
                             l2_cols=0, q_cols=0, qscale=1.0)

    @pl.when(n == 0)
    def _():
        h_ref[...] = jnp.zeros(h_ref.shape, F32)
        _Filler(conv([x0_ref, bc0_ref], act_ref=act_ref.at[0], seq_start=True), 1).flush()

    @pl.when(n > 0)
    def _():
        act_ref[0] = act_ref[1]

    tile = _ssm_gates(graw_ref[...], alog_ref[...], dtb_ref[...], CHUNK)
    fill = _Filler(conv([xn_ref, bcn_ref], act_ref=act_ref.at[1], seq_start=False), SSM_PAIRS)
    _ssd_block(act_ref.at[0], tile, h_ref, h_ref, y_ref, dskip_ref, tril3_ref, upper3_ref, e3_ref, fill)


def _ssd_seq(pm, graw, conv_w, conv_b, alog_row, dtb_row, dskip_row, consts, nseq, seq_len):
    nc = seq_len // CHUNK
    last = nseq * nc - 1
    bc_blk = 2 * SSM_D_INNER // (2 * SSM_BC)
    nxt = lambda b, n: jnp.minimum(b * nc + n + 1, last)
    rowspec = pl.BlockSpec((1, LANES), lambda b, n: (0, 0))
    return pl.pallas_call(
        _ssd_seq_kernel,
        grid=(nseq, nc),
        in_specs=[
            pl.BlockSpec((CHUNK, SSM_D_INNER), lambda b, n: (b * nc, 0)),
            pl.BlockSpec((CHUNK, 2 * SSM_BC), lambda b, n: (b * nc, bc_blk)),
            pl.BlockSpec((CHUNK, SSM_D_INNER), lambda b, n: (nxt(b, n), 0)),
            pl.BlockSpec((CHUNK, 2 * SSM_BC), lambda b, n: (nxt(b, n), bc_blk)),
            pl.BlockSpec((CHUNK, LANES), lambda b, n: (b * nc + n, 0)),
            pl.BlockSpec(((CONV_WIDTH - 1) * CHUNK, BHALO + CHUNK), lambda b, n: (0, 0)),
            pl.BlockSpec((CONV_WIDTH, SSM_CONV_DIM), lambda b, n: (0, 0)),
            pl.BlockSpec((1, SSM_CONV_DIM), lambda b, n: (0, 0)),
            rowspec, rowspec,
            pl.BlockSpec((1, SSM_D_INNER), lambda b, n: (0, 0)),
        ] + _const_specs(consts, 2),
        out_specs=[
            pl.BlockSpec((CHUNK, SSM_D_INNER), lambda b, n: (b * nc + n, 0)),
            pl.BlockSpec((None, SSM_PAIRS, LANES, SSM_STATE), lambda b, n: (b, 0, 0, 0)),
        ],
        out_shape=[
            jax.ShapeDtypeStruct((nseq * seq_len, SSM_D_INNER), F32),
            jax.ShapeDtypeStruct((nseq, SSM_PAIRS, LANES, SSM_STATE), F32),
        ],
        scratch_shapes=[
            pltpu.VMEM((CHUNK + BHALO, SSM_CONV_DIM), BF16),
            pltpu.VMEM((2, CHUNK, SSM_CONV_DIM), F32),
        ],
        compiler_params=_cparams("parallel", "arbitrary"),
        name="ssd_seq",
    )(pm, pm, pm, pm, graw, _shift_matrix(), conv_w, conv_b.reshape(1, SSM_CONV_DIM), alog_row, dtb_row, dskip_row, *consts)


def _ssd_step_kernel(act_ref, graw_ref, alog_ref, dtb_ref, dskip_ref, tril3_ref, upper3_ref, e3_ref, h0_ref,
                     y_ref, h_ref, apad_ref, gpad_ref, ypad_ref, *, lr):
    def one_sequence(s, carry):
        _load_padded(act_ref, s, apad_ref, lr)
        _load_padded(graw_ref, s, gpad_ref, lr)
        tile = _ssm_gates(gpad_ref[...], alog_ref[...], dtb_ref[...], lr)
        _ssd_block(apad_ref, tile, h0_ref.at[s], h_ref.at[s], ypad_ref, dskip_ref, tril3_ref, upper3_ref, e3_ref)
        _store_tokens(ypad_ref, y_ref, s, lr)
        return carry

    lax.fori_loop(0, SEQ_PER_STEP, one_sequence, 0)


def _ssd_step(act3, graw3, alog_row, dtb_row, dskip_row, consts, h0):
    steps, nb, _ = act3.shape
    assert nb % SEQ_PER_STEP == 0 and steps <= STEP_ROWS
    rowspec = pl.BlockSpec((1, LANES), lambda b: (0, 0))
    hspec = pl.BlockSpec((SEQ_PER_STEP, SSM_PAIRS, LANES, SSM_STATE), lambda g: (g, 0, 0, 0))
    y, h = pl.pallas_call(
        functools.partial(_ssd_step_kernel, lr=steps),
        grid=(nb // SEQ_PER_STEP,),
        in_specs=[_step_spec(steps, SSM_CONV_DIM), _step_spec(steps, LANES), rowspec, rowspec,
                  pl.BlockSpec((1, SSM_D_INNER), lambda b: (0, 0))] + _const_specs(consts, 1) + [hspec],
        out_specs=[_step_spec(steps, SSM_D_INNER), hspec],
        out_shape=[
            jax.ShapeDtypeStruct((steps, nb, SSM_D_INNER), F32),
            jax.ShapeDtypeStruct(h0.shape, F32),
        ],
        scratch_shapes=[
            pltpu.VMEM((STEP_ROWS, SSM_CONV_DIM), F32),
            pltpu.VMEM((CHUNK, LANES), F32),
            pltpu.VMEM((STEP_ROWS, SSM_D_INNER), F32),
        ],
        compiler_params=_cparams("parallel"),
        name="ssd_step",
    )(act3, graw3, alog_row, dtb_row, dskip_row, *consts, h0)
    return y.reshape(steps * nb, SSM_D_INNER), h


FFN_TILE = FFN_HIDDEN // 2
PROJ_TILE = 1024


def _gdn_wmap(j):
    return j


def _ssm_wmap(j):
    half = SSM_D_INNER // PROJ_TILE
    return jnp.where(j < 2 * half, lax.rem(j + half, 2 * half), j)


def _lane_row(pieces):
    row = jnp.zeros((1, LANES), F32)
    for off, vec in pieces:
        row = row.at[0, off:off + vec.shape[0]].set(vec.astype(F32))
    return row


def _stage_params(w_mod, b_mod, norm_mix, norm_ffn, norm_final, gdn_w_in, gdn_conv_w, gdn_a_log, gdn_dt_bias,
                  gdn_norm, gdn_w_out, ssm_w_in, ssm_conv_w, ssm_conv_b, ssm_a_log, ssm_dt_bias, ssm_d, ssm_norm,
                  ssm_w_out, ffn_w_gate_up, ffn_w_down):
    perm_g = np.concatenate([np.arange(0, GDN_V_HEADS, 2), np.arange(1, GDN_V_HEADS, 2)])
    perm_s = np.concatenate([np.arange(0, SSM_HEADS, 2), np.arange(1, SSM_HEADS, 2)])
    g_in, s_in = gdn_w_in[0], ssm_w_in[0]
    beta_cols = g_in[:, GDN_MAIN:GDN_MAIN + GDN_V_HEADS][:, perm_g]
    a_cols = g_in[:, GDN_MAIN + GDN_V_HEADS:GDN_MAIN + 2 * GDN_V_HEADS][:, perm_g]
    gdn_small = jnp.concatenate([a_cols, beta_cols, jnp.zeros((D_MODEL, LANES - 2 * GDN_V_HEADS), F32)], axis=1)
    dt_cols = s_in[:, SSM_MAIN:SSM_MAIN + SSM_HEADS][:, perm_s]
    ssm_small = jnp.concatenate([dt_cols, dt_cols, jnp.zeros((D_MODEL, LANES - 2 * SSM_HEADS), F32)], axis=1)
    return dict(
        w_mod=w_mod, b_mod=b_mod, norm_mix=norm_mix, norm_ffn=norm_ffn, norm_final=norm_final,
        gdn_main=g_in, gdn_small=gdn_small.astype(BF16),
        gdn_conv_w=gdn_conv_w[0], gdn_conv_b=jnp.zeros((GDN_CONV_DIM,), F32),
        gdn_alog_row=_lane_row([(0, gdn_a_log[0][perm_g])]), gdn_dtb_row=_lane_row([(0, gdn_dt_bias[0][perm_g])]),
        gdn_norm=jnp.tile(gdn_norm[0], GDN_V_HEADS), gdn_w_out=gdn_w_out[0],
        gdn_consts=_recurrence_consts(GDN_PAIRS, 0, GDN_PAIRS),
        ssm_main=s_in, ssm_small=ssm_small.astype(BF16),
        ssm_conv_w=ssm_conv_w[0], ssm_conv_b=ssm_conv_b[0],
        ssm_alog_row=_lane_row([(SSM_HEADS, ssm_a_log[0][perm_s])]),
        ssm_dtb_row=_lane_row([(0, ssm_dt_bias[0][perm_s]), (SSM_HEADS, ssm_dt_bias[0][perm_s])]),
        ssm_dskip_row=jnp.repeat(ssm_d[0], SSM_HEAD_DIM).reshape(1, SSM_D_INNER),
        ssm_norm=ssm_norm[0], ssm_w_out=ssm_w_out[0],
        ssm_consts=_recurrence_consts(SSM_PAIRS, SSM_HEADS, SSM_HEADS + SSM_PAIRS),
        w_gate_up=ffn_w_gate_up, wd=ffn_w_down,
    )


def _ffn(x, layer, mod3, rows_up, rows_down, p, final_w):
    act = _ffn_up(x, p["norm_ffn"][layer], mod3, rows_up, p["w_gate_up"][layer], FFN_TILE)
    return _ffn_down(act, x, mod3, rows_down, p["wd"][layer], final_w)


QSCALE = GDN_HEAD_DIM ** -0.5


def _trunk_seq(x3, mod, p):
    nseq, seq_len, _ = x3.shape
    m = nseq * seq_len
    x = x3.reshape(m, D_MODEL)
    mod3 = [mod[l].reshape(nseq, 1, 6 * D_MODEL) for l in range(2)]
    rows_a = _Rows(m, min(1024, seq_len), seq_len, 1)
    rows_b = _Rows(m, min(512, seq_len), seq_len, 1)
    rows_p = _Rows(m, min(2048, seq_len), seq_len, 1)

    pm, ps = _in_proj(x, p["norm_mix"][0], mod3[0], rows_p, 1, 0, p["gdn_main"], _gdn_wmap, GDN_MAIN, p["gdn_small"], PROJ_TILE)
    o, gdn_s = _gdn_seq(pm, ps, p["gdn_conv_w"], p["gdn_conv_b"], p["gdn_alog_row"], p["gdn_dtb_row"],
                        p["gdn_consts"], nseq, seq_len)
    tail = pm.reshape(nseq, seq_len, GDN_MAIN)[:, seq_len - (CONV_WIDTH - 1):].astype(F32)
    gdn_c = tail[..., :GDN_CONV_DIM]
    x = _mixer_out(o, pm, 2, p["gdn_norm"], x, mod3[0], rows_b, p["gdn_w_out"], GDN_HEAD_DIM, False)
    x = _ffn(x, 0, mod3[0], rows_a, rows_b, p, None)

    pm, ps = _in_proj(x, p["norm_mix"][1], mod3[1], rows_p, 1, 0, p["ssm_main"], _ssm_wmap, SSM_MAIN, p["ssm_small"], PROJ_TILE)
    y, ssm_h = _ssd_seq(pm, ps, p["ssm_conv_w"], p["ssm_conv_b"], p["ssm_alog_row"], p["ssm_dtb_row"],
                        p["ssm_dskip_row"], p["ssm_consts"], nseq, seq_len)
    tail = pm.reshape(nseq, seq_len, SSM_MAIN)[:, seq_len - (CONV_WIDTH - 1):].astype(F32)
    ssm_c = jnp.concatenate([tail[..., :SSM_D_INNER], tail[..., 2 * SSM_D_INNER:]], axis=-1)
    x = _mixer_out(y, pm, 1, p["ssm_norm"], x, mod3[1], rows_b, p["ssm_w_out"], SSM_D_INNER // SSM_GROUPS, True)
    _, y_out = _ffn(x, 1, mod3[1], rows_a, rows_b, p, p["norm_final"])

    return (y_out.reshape(nseq, seq_len, D_MODEL), gdn_s[None], gdn_c[None],
            ssm_h.reshape(nseq, SSM_HEADS, SSM_HEAD_DIM, SSM_STATE)[None], ssm_c[None])


def _trunk_step(x3, mod, st_gdn, cv_gdn, st_ssm, cv_ssm, p):
    nb, steps, _ = x3.shape
    assert steps >= CONV_WIDTH - 1
    m = nb * steps
    x = jnp.transpose(x3, (1, 0, 2)).reshape(m, D_MODEL)
    mod3 = [mod[l].reshape(1, nb, 6 * D_MODEL) for l in range(2)]
    rows = _Rows(m, m, None, nb)
    tok = lambda a: jnp.transpose(a, (1, 0, 2))

    pm, ps = _in_proj(x, p["norm_mix"][0], mod3[0], rows, 1, 0, p["gdn_main"], _gdn_wmap, GDN_MAIN, p["gdn_small"], PROJ_TILE)
    u3 = pm.reshape(steps, nb, GDN_MAIN)
    act3 = _conv_steps(u3, 0, 0, GDN_CONV_DIM, tok(cv_gdn[0]), p["gdn_conv_w"], p["gdn_conv_b"], 2, QSCALE)
    o, gdn_s = _gdn_step(act3, ps.reshape(steps, nb, LANES), p["gdn_alog_row"], p["gdn_dtb_row"], p["gdn_consts"],
                         st_gdn[0])
    gdn_c = tok(u3[steps - (CONV_WIDTH - 1):, :, :GDN_CONV_DIM].astype(F32))
    x = _mixer_out(o, pm, 2, p["gdn_norm"], x, mod3[0], rows, p["gdn_w_out"], GDN_HEAD_DIM, False)
    x = _ffn(x, 0, mod3[0], rows, rows, p, None)

    pm, ps = _in_proj(x, p["norm_mix"][1], mod3[1], rows, 1, 0, p["ssm_main"], _ssm_wmap, SSM_MAIN, p["ssm_small"], PROJ_TILE)
    u3 = pm.reshape(steps, nb, SSM_MAIN)
    act3 = _conv_steps(u3, SSM_D_INNER // CONV_COLS, SSM_D_INNER // CONV_COLS, SSM_CONV_DIM, tok(cv_ssm[0]),
                       p["ssm_conv_w"], p["ssm_conv_b"], 0, 1.0)
    h0 = st_ssm[0].reshape(nb, SSM_PAIRS, LANES, SSM_STATE)
    y, ssm_h = _ssd_step(act3, ps.reshape(steps, nb, LANES), p["ssm_alog_row"], p["ssm_dtb_row"],
                         p["ssm_dskip_row"], p["ssm_consts"], h0)
    tail = u3[steps - (CONV_WIDTH - 1):].astype(F32)
    ssm_c = tok(jnp.concatenate([tail[..., :SSM_D_INNER], tail[..., 2 * SSM_D_INNER:]], axis=-1))
    x = _mixer_out(y, pm, 1, p["ssm_norm"], x, mod3[1], rows, p["ssm_w_out"], SSM_D_INNER // SSM_GROUPS, True)
    _, y_out = _ffn(x, 1, mod3[1], rows, rows, p, p["norm_final"])

    return (tok(y_out.reshape(steps, nb, D_MODEL)), gdn_s[None], gdn_c[None],
            ssm_h.reshape(nb, SSM_HEADS, SSM_HEAD_DIM, SSM_STATE)[None], ssm_c[None])


def kernel(x_prompt, x_sample, c_prompt, c_sample, state_gdn, state_gdn_conv, state_ssm, state_ssm_conv, w_mod, b_mod,
           norm_mix, norm_ffn, norm_final, gdn_w_in, gdn_conv_w, gdn_a_log, gdn_dt_bias, gdn_norm, gdn_w_out, ssm_w_in,
           ssm_conv_w, ssm_conv_b, ssm_a_log, ssm_dt_bias, ssm_d, ssm_norm, ssm_w_out, ffn_w_gate_up, ffn_w_down):
    p = _stage_params(w_mod, b_mod, norm_mix, norm_ffn, norm_final, gdn_w_in, gdn_conv_w, gdn_a_log, gdn_dt_bias,
                      gdn_norm, gdn_w_out, ssm_w_in, ssm_conv_w, ssm_conv_b, ssm_a_log, ssm_dt_bias, ssm_d, ssm_norm,
                      ssm_w_out, ffn_w_gate_up, ffn_w_down)
    n_prompt = x_prompt.shape[0]
    mod = _modulation(jnp.concatenate([c_prompt, c_sample], axis=0), p["w_mod"], p["b_mod"])
    y_p, gs_p, gc_p, ss_p, sc_p = _trunk_seq(x_prompt, mod[:, :n_prompt], p)
    y_s, gs_s, gc_s, ss_s, sc_s = _trunk_step(x_sample, mod[:, n_prompt:], state_gdn, state_gdn_conv, state_ssm,
                                              state_ssm_conv, p)
    return (y_p, y_s, gs_p, gc_p, ss_p, sc_p, gs_s, gc_s, ss_s, sc_s)
```

```python
import functools

import numpy as np
import jax
import jax.numpy as jnp
from jax import lax
from jax.experimental import pallas as pl
from jax.experimental.pallas import tpu as pltpu

F32 = jnp.float32
BF16 = jnp.bfloat16

D_MODEL = 1024
EPS = 1e-6
CONV_WIDTH = 4
CHUNK = 64
LANES = 128
HALF = LANES // 2

GDN_QK_HEADS = 8
GDN_V_HEADS = 16
GDN_HEAD_DIM = 128
GDN_QK_DIM = GDN_QK_HEADS * GDN_HEAD_DIM
GDN_V_DIM = GDN_V_HEADS * GDN_HEAD_DIM
GDN_CONV_DIM = 2 * GDN_QK_DIM + GDN_V_DIM
GDN_MAIN = GDN_CONV_DIM + GDN_V_DIM

SSM_D_INNER = 2 * D_MODEL
SSM_HEAD_DIM = 64
SSM_HEADS = SSM_D_INNER // SSM_HEAD_DIM
SSM_GROUPS = 4
SSM_STATE = 128
SSM_BC = SSM_GROUPS * SSM_STATE
SSM_CONV_DIM = SSM_D_INNER + 2 * SSM_BC
SSM_MAIN = SSM_D_INNER + SSM_CONV_DIM

FFN_HIDDEN = 2816

VMEM_LIMIT = 56 * 1024 * 1024
NEG_BIG = -1e30


def _cparams(*sem):
    return pltpu.CompilerParams(dimension_semantics=sem, vmem_limit_bytes=VMEM_LIMIT)


def _silu(x):
    hx = 0.5 * x
    return hx + hx * jnp.tanh(hx)


def _softplus(x):
    return jnp.maximum(x, 0.0) + jnp.log1p(jnp.exp(-jnp.abs(x)))


def _dot(a, b):
    return jnp.dot(a, b, preferred_element_type=F32)


def _dot_nt(a, b):
    return lax.dot_general(a, b, (((1,), (1,)), ((), ())), preferred_element_type=F32)


def _tile_rows(v, rep):
    return v if rep == 1 else jnp.concatenate([v] * rep, axis=0)


def _mod_kernel(c_ref, w_ref, b_ref, o_ref):
    cs = _silu(c_ref[...]).astype(BF16)
    o_ref[...] = _dot(cs, w_ref[...].astype(BF16)) + b_ref[...]


def _modulation(c, w_mod, b_mod):
    depth, _, n = w_mod.shape
    bc = c.shape[0]
    tn = 1536
    return pl.pallas_call(
        _mod_kernel,
        grid=(depth, n // tn),
        in_specs=[
            pl.BlockSpec((bc, D_MODEL), lambda l, j: (0, 0)),
            pl.BlockSpec((None, D_MODEL, tn), lambda l, j: (l, 0, j)),
            pl.BlockSpec((None, 1, tn), lambda l, j: (l, 0, j)),
        ],
        out_specs=pl.BlockSpec((None, bc, tn), lambda l, j: (l, 0, j)),
        out_shape=jax.ShapeDtypeStruct((depth, bc, n), F32),
        compiler_params=_cparams("parallel", "parallel"),
        name="adaln_mod",
    )(c, w_mod, b_mod.reshape(depth, 1, n))


def _norm_mod(x, nw, sc, sh, rep):
    y = x * lax.rsqrt(jnp.mean(x * x, axis=-1, keepdims=True) + EPS) * nw
    return y * (1.0 + _tile_rows(sc, rep)) + _tile_rows(sh, rep)


def _in_proj_kernel(x_ref, nw_ref, sc_ref, sh_ref, w_ref, w2_ref, o_ref, o2_ref, h_ref, *, rep):
    @pl.when(pl.program_id(1) == 0)
    def _():
        h = _norm_mod(x_ref[...], nw_ref[...], sc_ref[...], sh_ref[...], rep).astype(BF16)
        h_ref[...] = h
        o2_ref[...] = _dot(h, w2_ref[...])

    o_ref[...] = _dot(h_ref[...], w_ref[...]).astype(o_ref.dtype)


def _ffn_up_kernel(x_ref, nw_ref, sc_ref, sh_ref, wg_ref, wu_ref, o_ref, h_ref, *, rep):
    @pl.when(pl.program_id(1) == 0)
    def _():
        h_ref[...] = _norm_mod(x_ref[...], nw_ref[...], sc_ref[...], sh_ref[...], rep).astype(BF16)

    h = h_ref[...]
    o_ref[...] = (_silu(_dot(h, wg_ref[...])) * _dot(h, wu_ref[...])).astype(BF16)


class _Rows:
    def __init__(self, m, tm, group_rows, mod_rows):
        assert m % tm == 0
        self.m, self.tm = m, tm
        if mod_rows == 1:
            assert group_rows % tm == 0
            self.rep = 1
            self.gmap = lambda i: (i * tm) // group_rows
        else:
            assert tm % mod_rows == 0
            self.rep = tm // mod_rows
            self.gmap = lambda i: 0
        self.mod_rows = mod_rows

    def mod_spec(self, col_block, with_j):
        if with_j:
            return pl.BlockSpec((None, self.mod_rows, D_MODEL), lambda i, j: (self.gmap(i), 0, col_block))
        return pl.BlockSpec((None, self.mod_rows, D_MODEL), lambda i: (self.gmap(i), 0, col_block))


def _in_proj(x, nw, mod3, rows, sc_blk, sh_blk, w, w2, tn):
    m, tm = rows.m, rows.tm
    n = w.shape[1]
    assert n % tn == 0
    return pl.pallas_call(
        functools.partial(_in_proj_kernel, rep=rows.rep),
        grid=(m // tm, n // tn),
        in_specs=[
            pl.BlockSpec((tm, D_MODEL), lambda i, j: (i, 0)),
            pl.BlockSpec((1, D_MODEL), lambda i, j: (0, 0)),
            rows.mod_spec(sc_blk, True),
            rows.mod_spec(sh_blk, True),
            pl.BlockSpec((D_MODEL, tn), lambda i, j: (0, j)),
            pl.BlockSpec((D_MODEL, LANES), lambda i, j: (0, 0)),
        ],
        out_specs=[
            pl.BlockSpec((tm, tn), lambda i, j: (i, j)),
            pl.BlockSpec((tm, LANES), lambda i, j: (i, 0)),
        ],
        out_shape=[jax.ShapeDtypeStruct((m, n), BF16), jax.ShapeDtypeStruct((m, LANES), F32)],
        scratch_shapes=[pltpu.VMEM((tm, D_MODEL), BF16)],
        compiler_params=_cparams("parallel", "arbitrary"),
        name="in_proj",
    )(x, nw.reshape(1, D_MODEL), mod3, mod3, w, w2)


def _ffn_up(x, nw, mod3, rows, wg, wu, th):
    m, tm = rows.m, rows.tm
    assert FFN_HIDDEN % th == 0
    return pl.pallas_call(
        functools.partial(_ffn_up_kernel, rep=rows.rep),
        grid=(m // tm, FFN_HIDDEN // th),
        in_specs=[
            pl.BlockSpec((tm, D_MODEL), lambda i, j: (i, 0)),
            pl.BlockSpec((1, D_MODEL), lambda i, j: (0, 0)),
            rows.mod_spec(4, True),
            rows.mod_spec(3, True),
            pl.BlockSpec((D_MODEL, th), lambda i, j: (0, j)),
            pl.BlockSpec((D_MODEL, th), lambda i, j: (0, j)),
        ],
        out_specs=pl.BlockSpec((tm, th), lambda i, j: (i, j)),
        out_shape=jax.ShapeDtypeStruct((m, FFN_HIDDEN), BF16),
        scratch_shapes=[pltpu.VMEM((tm, D_MODEL), BF16)],
        compiler_params=_cparams("parallel", "arbitrary"),
        name="ffn_up",
    )(x, nw.reshape(1, D_MODEL), mod3, mod3, wg, wu)


def _resid_store(acc, x_ref, gt_ref, o_ref, fnw_ref, y_ref, rep):
    xn = x_ref[...] + _tile_rows(gt_ref[...], rep) * acc
    o_ref[...] = xn
    if y_ref is not None:
        y_ref[...] = xn * lax.rsqrt(jnp.mean(xn * xn, axis=-1, keepdims=True) + EPS) * fnw_ref[...]


def _ffn_down_kernel(a_ref, x_ref, gt_ref, w_ref, *rest, rep, final):
    if final:
        fnw_ref, o_ref, y_ref = rest
    else:
        (o_ref,), fnw_ref, y_ref = rest, None, None
    _resid_store(_dot(a_ref[...], w_ref[...]), x_ref, gt_ref, o_ref, fnw_ref, y_ref, rep)


def _ffn_down(act, x, mod3, rows, w, fnw):
    m, tm = rows.m, rows.tm
    final = fnw is not None
    in_specs = [
        pl.BlockSpec((tm, FFN_HIDDEN), lambda i: (i, 0)),
        pl.BlockSpec((tm, D_MODEL), lambda i: (i, 0)),
        rows.mod_spec(5, False),
        pl.BlockSpec((FFN_HIDDEN, D_MODEL), lambda i: (0, 0)),
    ]
    args = [act, x, mod3, w]
    row_spec = pl.BlockSpec((tm, D_MODEL), lambda i: (i, 0))
    out_shape = jax.ShapeDtypeStruct((m, D_MODEL), F32)
    if final:
        in_specs.append(pl.BlockSpec((1, D_MODEL), lambda i: (0, 0)))
        args.append(fnw.reshape(1, D_MODEL))
        out_specs, out_shapes = [row_spec, row_spec], [out_shape, out_shape]
    else:
        out_specs, out_shapes = row_spec, out_shape
    return pl.pallas_call(
        functools.partial(_ffn_down_kernel, rep=rows.rep, final=final),
        grid=(m // tm,),
        in_specs=in_specs,
        out_specs=out_specs,
        out_shape=out_shapes,
        compiler_params=_cparams("parallel"),
        name="ffn_down",
    )(*args)


def _mixer_out_kernel(y_ref, z_ref, nw_ref, x_ref, gt_ref, w_ref, o_ref, a_ref, *, rep, group, gate_first):
    width = y_ref.shape[1]
    for s in range(0, width, group):
        y = y_ref[:, s:s + group]
        gate = _silu(z_ref[:, s:s + group].astype(F32))
        if gate_first:
            y = y * gate
        y = y * lax.rsqrt(jnp.mean(y * y, axis=-1, keepdims=True) + EPS) * nw_ref[:, s:s + group]
        if not gate_first:
            y = y * gate
        a_ref[:, s:s + group] = y.astype(BF16)
    _resid_store(_dot(a_ref[...], w_ref[...]), x_ref, gt_ref, o_ref, None, None, rep)


def _mixer_out(y, zsrc, z_blk, nw_full, x, mod3, rows, w, group, gate_first):
    m, tm = rows.m, rows.tm
    width = y.shape[1]
    return pl.pallas_call(
        functools.partial(_mixer_out_kernel, rep=rows.rep, group=group, gate_first=gate_first),
        grid=(m // tm,),
        in_specs=[
            pl.BlockSpec((tm, width), lambda i: (i, 0)),
            pl.BlockSpec((tm, width), lambda i: (i, z_blk)),
            pl.BlockSpec((1, width), lambda i: (0, 0)),
            pl.BlockSpec((tm, D_MODEL), lambda i: (i, 0)),
            rows.mod_spec(2, False),
            pl.BlockSpec((width, D_MODEL), lambda i: (0, 0)),
        ],
        out_specs=pl.BlockSpec((tm, D_MODEL), lambda i: (i, 0)),
        out_shape=jax.ShapeDtypeStruct((m, D_MODEL), F32),
        scratch_shapes=[pltpu.VMEM((tm, width), BF16)],
        compiler_params=_cparams("parallel"),
        name="mixer_out",
    )(y, zsrc, nw_full.reshape(1, width), x, mod3, w)


CONV_COLS = 1024


def _post_conv(acc, o_ref, cb, n_l2, qscale):
    y = _silu(acc)
    if n_l2 == 0:
        o_ref[...] = y
        return

    @pl.when(cb < n_l2)
    def _():
        scale = jnp.where(cb == 0, qscale, 1.0).astype(F32)
        for s in range(0, CONV_COLS, GDN_HEAD_DIM):
            yh = y[:, s:s + GDN_HEAD_DIM]
            o_ref[:, s:s + GDN_HEAD_DIM] = yh * lax.rsqrt(jnp.sum(yh * yh, axis=-1, keepdims=True) + EPS) * scale

    @pl.when(cb >= n_l2)
    def _():
        o_ref[...] = y


BHALO = 16
CONV_GROUP = 2 * LANES


def _shift_matrix():
    s = np.zeros(((CONV_WIDTH - 1) * CHUNK, BHALO + CHUNK), np.float32)
    for tap in range(CONV_WIDTH - 1):
        for r in range(CHUNK):
            s[tap * CHUNK + r, BHALO - (CONV_WIDTH - 1) + tap + r] = 1.0
    return jnp.asarray(s, BF16)


def _conv_block(raw_refs, shift_ref, w_ref, b_ref, ext_ref, act_ref, seq_start, l2_cols, q_cols, qscale):
    width = ext_ref.shape[1]
    if seq_start:
        ext_ref[0:BHALO, :] = jnp.zeros((BHALO, width), BF16)
    else:
        ext_ref[0:BHALO, :] = ext_ref[CHUNK:CHUNK + BHALO, :]

    off = 0
    for ref in raw_refs:
        ext_ref[BHALO:BHALO + CHUNK, off:off + ref.shape[1]] = ref[...]
        off += ref.shape[1]

    def lane_group(s):
        cols = slice(s, s + CONV_GROUP)
        sh = _dot(shift_ref[...], ext_ref[:, cols])
        acc = b_ref[:, cols] + w_ref[CONV_WIDTH - 1:CONV_WIDTH, cols] * ext_ref[BHALO:BHALO + CHUNK, cols].astype(F32)
        for tap in range(CONV_WIDTH - 1):
            acc = acc + w_ref[tap:tap + 1, cols] * sh[tap * CHUNK:(tap + 1) * CHUNK]
        y = _silu(acc)
        for h in range(s, s + CONV_GROUP, LANES):
            yh = y[:, h - s:h - s + LANES]
            if h < l2_cols:
                yh = yh * lax.rsqrt(jnp.sum(yh * yh, axis=-1, keepdims=True) + EPS)
                if h < q_cols:
                    yh = yh * qscale
            act_ref[:, h:h + LANES] = yh

    return [functools.partial(lane_group, s) for s in range(0, width, CONV_GROUP)]


class _Filler:
    def __init__(self, tasks, shares):
        self.tasks, self.per = list(tasks), -(-len(tasks) // shares)

    def emit(self):
        for task in self.tasks[:self.per]:
            task()
        self.tasks = self.tasks[self.per:]

    def flush(self):
        for task in self.tasks:
            task()
        self.tasks = []


_NO_FILL = _Filler([], 1)


def _conv_steps_kernel(u_ref, hist_ref, w_ref, b_ref, o_ref, *, steps, n_l2, qscale):
    cb = pl.program_id(0)
    ext = [hist_ref[i] for i in range(CONV_WIDTH - 1)] + [u_ref[i].astype(F32) for i in range(steps)]
    for t in range(steps):
        acc = b_ref[...] + w_ref[0:1, :] * ext[t]
        for tap in range(1, CONV_WIDTH):
            acc = acc + w_ref[tap:tap + 1, :] * ext[t + tap]
        _post_conv(acc, o_ref.at[t], cb, n_l2, qscale)


def _conv_steps(u3, skip_at, skip, n_cols, hist3, conv_w, conv_b, n_l2, qscale):
    steps, nb, _ = u3.shape
    return pl.pallas_call(
        functools.partial(_conv_steps_kernel, steps=steps, n_l2=n_l2, qscale=qscale),
        grid=(n_cols // CONV_COLS,),
        in_specs=[
            pl.BlockSpec((steps, nb, CONV_COLS), lambda c: (0, 0, c + skip * (c >= skip_at))),
            pl.BlockSpec((CONV_WIDTH - 1, nb, CONV_COLS), lambda c: (0, 0, c)),
            pl.BlockSpec((CONV_WIDTH, CONV_COLS), lambda c: (0, c)),
            pl.BlockSpec((1, CONV_COLS), lambda c: (0, c)),
        ],
        out_specs=pl.BlockSpec((steps, nb, CONV_COLS), lambda c: (0, 0, c)),
        out_shape=jax.ShapeDtypeStruct((steps, nb, n_cols), F32),
        compiler_params=_cparams("parallel"),
        name="conv_steps",
    )(u3, hist3, conv_w, conv_b.reshape(1, n_cols))


def _split3(x):
    hi = x.astype(BF16)
    r = x - hi.astype(F32)
    mid = r.astype(BF16)
    lo = (r - mid.astype(F32)).astype(BF16)
    return hi, mid, lo


def _cat3(x, axis):
    return jnp.concatenate(_split3(x), axis=axis)


def _pad_t(tile):
    return jnp.concatenate([tile, jnp.zeros_like(tile)], axis=0).T


def _pair_rows(t, n):
    return t[0:n] + pltpu.roll(t[n:2 * n], HALF, 1)


def _iotas(shape):
    return lax.broadcasted_iota(jnp.int32, shape, 0), lax.broadcasted_iota(jnp.int32, shape, 1)


def _pad_rows(a):
    if a.shape[0] == CHUNK:
        return a
    return jnp.concatenate([a, jnp.zeros((CHUNK - a.shape[0], a.shape[1]), a.dtype)], axis=0)


def _split2(x):
    hi = x.astype(BF16)
    return hi, (x - hi.astype(F32)).astype(BF16)


def _block_diag(pair_bf, bd_ones):
    pair_bf = _pad_rows(pair_bf)
    return jnp.concatenate([pair_bf, pair_bf], axis=0) * bd_ones


def _mm3(lhs_parts, rhs_hi, rhs_lo):
    lh, ll = lhs_parts
    return _dot(jnp.concatenate([lh, lh, ll], axis=1), jnp.concatenate([rhs_hi, rhs_lo, rhs_hi], axis=0))


def _mm_pairs(lhs_parts, rhs_parts, bd_fn):
    return [_mm3(lp, bd_fn(rh), bd_fn(rl)) for lp, (rh, rl) in zip(lhs_parts, rhs_parts)]


def _mm_hl(lhs, rhs_bf):
    return _dot(lhs.astype(BF16), rhs_bf)


def _unit_lower_inverse(xs, levels, bd_fn, eye, fill):
    ps = [eye + x for x in xs]
    if levels <= 1:
        return ps
    rows = xs[0].shape[0]
    stack = lambda a, b: tuple(jnp.concatenate([s, t], axis=0) for s, t in zip(a, b))
    ysp = [_split2(x) for x in xs]
    ys = _mm_pairs(ysp, ysp, bd_fn)
    fill.emit()
    for _ in range(levels - 2):
        ysp = [_split2(y) for y in ys]
        rs = _mm_pairs([stack(yp, _split2(p)) for yp, p in zip(ysp, ps)], ysp, bd_fn)
        fill.emit()
        ys = [r[0:rows] for r in rs]
        ps = [p + r[rows:2 * rows] for p, r in zip(ps, rs)]
    last = _mm_pairs([_split2(p) for p in ps], [_split2(y) for y in ys], bd_fn)
    return [p + t for p, t in zip(ps, last)]


QUARTER = HALF // 2


def _unit_lower_inverse_blocked(xs, fill):
    row, lane = _iotas((QUARTER, LANES))
    r2, l2 = _iotas((LANES, LANES))
    even = jnp.bitwise_and(jnp.right_shift(lane, 5), 1) == 0
    eye_q = jnp.where(row == jnp.bitwise_and(lane, QUARTER - 1), 1.0, 0.0)
    bd4_ones = jnp.where(jnp.right_shift(r2, 5) == jnp.right_shift(l2, 5), 1.0, 0.0).astype(BF16)
    row_q = jnp.right_shift(r2, 5)
    place = jnp.where(((row_q == 1) & (jnp.right_shift(l2, 5) == 0)) | ((row_q == 3) & (jnp.right_shift(l2, 5) == 2)),
                      1.0, 0.0).astype(BF16)
    bd4 = lambda a: jnp.concatenate([a] * 4, axis=0) * bd4_ones

    tops = [x[0:QUARTER] for x in xs]
    bots = [x[QUARTER:CHUNK] for x in xs]
    diag = [jnp.where(even, t, b) for t, b in zip(tops, bots)]
    dinv = _unit_lower_inverse(diag, _levels(QUARTER), bd4, eye_q, fill)
    fill.emit()
    ai = [jnp.where(even, d, 0.0) for d in dinv]
    ci = [jnp.where(even, 0.0, d) for d in dinv]
    b_ai = _mm_pairs([_split2(jnp.where(even, b, 0.0)) for b in bots], [_split2(a) for a in ai], bd4)
    fill.emit()
    low = [_mm3(_split2(c), *[jnp.concatenate([part] * 4, axis=0) * place for part in _split2(e)])
           for c, e in zip(ci, b_ai)]
    return [jnp.concatenate([a, l + c], axis=0) for a, l, c in zip(ai, low, ci)]


def _two_blocks(a, b):
    a_bf, b_bf = _pad_rows(a).astype(BF16), _pad_rows(b).astype(BF16)
    z = jnp.zeros_like(a_bf)
    return jnp.concatenate([jnp.concatenate([a_bf, z], axis=1), jnp.concatenate([z, b_bf], axis=1)], axis=0)


def _recurrence_consts(n_pairs, chan_even0, chan_odd0):
    tril = np.tril(np.ones((CHUNK, CHUNK), np.float32))
    tril3 = np.concatenate([tril] * 3, axis=1)
    upper = np.zeros((LANES, 2 * LANES), np.float32)
    upper[:CHUNK, :CHUNK] = tril.T
    upper[:CHUNK, LANES:] = 1.0
    upper3 = np.concatenate([upper] * 3, axis=0)
    e = np.zeros((LANES, n_pairs * LANES), np.float32)
    for p in range(n_pairs):
        e[chan_even0 + p, p * LANES:p * LANES + HALF] = 1.0
        e[chan_odd0 + p, p * LANES + HALF:(p + 1) * LANES] = 1.0
    e3 = np.concatenate([e] * 3, axis=0)
    return jnp.asarray(tril3, BF16), jnp.asarray(upper3, BF16), jnp.asarray(e3, BF16)


def _levels(lr):
    return max(1, int(np.ceil(np.log2(lr))))


STEP_ROWS = 16


GDN_PAIRS = GDN_V_HEADS // 2
GDN_PAIR_BATCH = 8
GDN_FILL_SHARES = 12 * (GDN_PAIRS // GDN_PAIR_BATCH)


def _gdn_gates(raw, alog_row, dtb_row, lr):
    row, lane = _iotas((CHUNK, LANES))
    g = -jnp.exp(alog_row) * _softplus(raw + dtb_row)
    beta = jax.nn.sigmoid(raw)
    gt = jnp.where(lane < GDN_V_HEADS, g, jnp.where(lane < 2 * GDN_V_HEADS, beta, 0.0))
    if lr < CHUNK:
        gt = jnp.where(row < lr, gt, 0.0)
    return gt


def _gdn_block(act_ref, gt, s_in, s_out, o_ref, tril3_ref, upper3_ref, e3_ref, lr, fill=_NO_FILL):
    np_ = GDN_PAIRS
    rows = act_ref.shape[0]
    row, lane = _iotas((rows, LANES))
    jl = jnp.bitwise_and(lane, HALF - 1)
    left = lane < HALF
    causal = row >= jl
    strict = row > jl
    eye2 = jnp.where(row == jl, 1.0, 0.0)
    r2, l2 = _iotas((LANES, LANES))
    bdmask = jnp.right_shift(r2, 6) == jnp.right_shift(l2, 6)
    _, lane_p = _iotas((np_, LANES))

    t1 = _pad_t(gt)
    cum_ext = _dot(_cat3(t1[0:2 * np_], 1), upper3_ref[...])
    cum_t, last_b = cum_ext[:, :LANES], cum_ext[:, LANES:]
    cum_rp = _pair_rows(cum_t, np_)
    beta_rp = _pair_rows(t1[2 * np_:4 * np_], np_)
    last_rp = jnp.where(lane_p < HALF, last_b[0:np_], last_b[np_:2 * np_])
    ecum_rp = jnp.exp(cum_rp)
    kdec_rp = jnp.exp(last_rp - cum_rp) * beta_rp
    elast = jnp.exp(last_b)
    cum = _dot(tril3_ref[...], _cat3(gt, 0))
    col_all = _dot(_cat3(cum[0:rows], 1), e3_ref[...])

    bd_ones = jnp.where(bdmask, 1.0, 0.0).astype(BF16)
    sl = lambda a, i: a[:, i * LANES:(i + 1) * LANES]

    def run(pairs):
        heads = [2 * p + hh for p in pairs for hh in range(2)]
        q = {p: sl(act_ref, p) for p in pairs}
        k = {p: _pad_rows(sl(act_ref, GDN_QK_HEADS + p)) for p in pairs}
        v = {h: sl(act_ref, 2 * GDN_QK_HEADS + h) for h in heads}
        kb = {p: k[p].astype(BF16) for p in pairs}
        qb = {p: q[p].astype(BF16) for p in pairs}
        gq = {p: _dot_nt(jnp.concatenate([kb[p][0:rows], qb[p]], axis=0), jnp.concatenate([kb[p], kb[p]], axis=0))
              for p in pairs}
        fill.emit()
        colc = {p: sl(col_all, p) for p in pairs}
        base = {p: jnp.exp(jnp.where(causal, colc[p] - cum_rp[p:p + 1], NEG_BIG)) * beta_rp[p:p + 1] for p in pairs}
        x = [jnp.where(strict, -(gq[p][0:rows] * base[p]), 0.0) for p in pairs]
        qkd = {p: (gq[p][rows:2 * rows] * base[p]).astype(BF16) for p in pairs}
        if lr == CHUNK:
            minv = dict(zip(pairs, _unit_lower_inverse_blocked(x, fill)))
        else:
            minv = dict(zip(pairs, _unit_lower_inverse(x, _levels(lr), lambda a: _block_diag(a, bd_ones), eye2, fill)))
        fill.emit()
        u = {p: jnp.concatenate([v[2 * p], v[2 * p + 1]], axis=1)
             + _mm_hl(minv[p] - eye2, _two_blocks(v[2 * p], v[2 * p + 1])) for p in pairs}
        fill.emit()
        w = {p: _mm_hl(minv[p] * ecum_rp[p:p + 1], _two_blocks(kb[p], kb[p])) for p in pairs}
        fill.emit()
        s_old = {h: s_in[h] for h in heads}
        r = {h: _dot(jnp.concatenate([sl(w[h // 2], h % 2).astype(BF16), qb[h // 2]], axis=0),
                     s_old[h].astype(BF16)) for h in heads}
        fill.emit()
        delta = {h: sl(u[h // 2], h % 2) - r[h][0:rows] for h in heads}
        bd_delta = {p: _two_blocks(delta[2 * p], delta[2 * p + 1]) for p in pairs}
        fill.emit()
        od = {p: _dot(jnp.concatenate(
            [qkd[p], (jnp.concatenate([k[p], k[p]], axis=0).T * kdec_rp[p:p + 1]).astype(BF16)], axis=0), bd_delta[p])
            for p in pairs}
        o_intra = {p: od[p][0:rows] for p in pairs}
        ds = {p: od[p][rows:rows + LANES] for p in pairs}
        for p in pairs:
            ecol = jnp.exp(colc[p])
            ecol_r = pltpu.roll(ecol, HALF, 1)
            efull = (jnp.where(left, ecol, ecol_r), jnp.where(left, ecol_r, ecol))
            for hh in range(2):
                h = 2 * p + hh
                o_ref[:, h * LANES:(h + 1) * LANES] = efull[hh] * r[h][rows:2 * rows] + sl(o_intra[p], hh)
                e_h = elast[hh * np_ + p:hh * np_ + p + 1]
                s_out[h] = e_h * s_old[h] + sl(ds[p], hh)

    for b0 in range(0, np_, GDN_PAIR_BATCH):
        run(list(range(b0, b0 + GDN_PAIR_BATCH)))
    fill.flush()


def _gdn_seq_kernel(raw0_ref, rawn_ref, graw_ref, shift_ref, cw_ref, cb_ref, alog_ref, dtb_ref, tril3_ref, upper3_ref,
                    e3_ref, o_ref, s_ref, ext_ref, act_ref):
    n = pl.program_id(1)
    conv = functools.partial(_conv_block, shift_ref=shift_ref, w_ref=cw_ref, b_ref=cb_ref, ext_ref=ext_ref,
                             l2_cols=2 * GDN_QK_DIM, q_cols=GDN_QK_DIM, qscale=QSCALE)

    @pl.when(n == 0)
    def _():
        s_ref[...] = jnp.zeros(s_ref.shape, F32)
        _Filler(conv([raw0_ref], act_ref=act_ref.at[0], seq_start=True), 1).flush()

    @pl.when(n > 0)
    def _():
        act_ref[0] = act_ref[1]

    gt = _gdn_gates(graw_ref[...], alog_ref[...], dtb_ref[...], CHUNK)
    fill = _Filler(conv([rawn_ref], act_ref=act_ref.at[1], seq_start=False), GDN_FILL_SHARES)
    _gdn_block(act_ref.at[0], gt, s_ref, s_ref, o_ref, tril3_ref, upper3_ref, e3_ref, CHUNK, fill)


def _const_specs(consts, ngrid):
    zero = (lambda *_: (0, 0))
    return [pl.BlockSpec(c.shape, zero) for c in consts]


def _gdn_seq(pm, graw, conv_w, conv_b, alog_row, dtb_row, consts, nseq, seq_len):
    nc = seq_len // CHUNK
    last = nseq * nc - 1
    rowspec = pl.BlockSpec((1, LANES), lambda b, n: (0, 0))
    return pl.pallas_call(
        _gdn_seq_kernel,
        grid=(nseq, nc),
        in_specs=[
            pl.BlockSpec((CHUNK, GDN_CONV_DIM), lambda b, n: (b * nc, 0)),
            pl.BlockSpec((CHUNK, GDN_CONV_DIM), lambda b, n: (jnp.minimum(b * nc + n + 1, last), 0)),
            pl.BlockSpec((CHUNK, LANES), lambda b, n: (b * nc + n, 0)),
            pl.BlockSpec(((CONV_WIDTH - 1) * CHUNK, BHALO + CHUNK), lambda b, n: (0, 0)),
            pl.BlockSpec((CONV_WIDTH, GDN_CONV_DIM), lambda b, n: (0, 0)),
            pl.BlockSpec((1, GDN_CONV_DIM), lambda b, n: (0, 0)),
            rowspec, rowspec,
        ] + _const_specs(consts, 2),
        out_specs=[
            pl.BlockSpec((CHUNK, GDN_V_DIM), lambda b, n: (b * nc + n, 0)),
            pl.BlockSpec((None, GDN_V_HEADS, GDN_HEAD_DIM, GDN_HEAD_DIM), lambda b, n: (b, 0, 0, 0)),
        ],
        out_shape=[
            jax.ShapeDtypeStruct((nseq * seq_len, GDN_V_DIM), F32),
            jax.ShapeDtypeStruct((nseq, GDN_V_HEADS, GDN_HEAD_DIM, GDN_HEAD_DIM), F32),
        ],
        scratch_shapes=[
            pltpu.VMEM((CHUNK + BHALO, GDN_CONV_DIM), BF16),
            pltpu.VMEM((2, CHUNK, GDN_CONV_DIM), F32),
        ],
        compiler_params=_cparams("parallel", "arbitrary"),
        name="gdn_seq",
    )(pm, pm, graw, _shift_matrix(), conv_w, conv_b.reshape(1, GDN_CONV_DIM), alog_row, dtb_row, *consts)


SEQ_PER_STEP = 8


def _load_padded(src_ref, s, pad_ref, lr):
    pad_ref[...] = jnp.zeros(pad_ref.shape, F32)
    for t in range(lr):
        pad_ref[t:t + 1, :] = src_ref[t, pl.ds(s, 1), :]


def _store_tokens(pad_ref, dst_ref, s, lr):
    for t in range(lr):
        dst_ref[t, pl.ds(s, 1), :] = pad_ref[t:t + 1, :]


def _gdn_step_kernel(act_ref, graw_ref, alog_ref, dtb_ref, tril3_ref, upper3_ref, e3_ref, s0_ref,
                     o_ref, s_ref, apad_ref, gpad_ref, opad_ref, *, lr):
    def one_sequence(s, carry):
        _load_padded(act_ref, s, apad_ref, lr)
        _load_padded(graw_ref, s, gpad_ref, lr)
        gt = _gdn_gates(gpad_ref[...], alog_ref[...], dtb_ref[...], lr)
        _gdn_block(apad_ref, gt, s0_ref.at[s], s_ref.at[s], opad_ref, tril3_ref, upper3_ref, e3_ref, lr)
        _store_tokens(opad_ref, o_ref, s, lr)
        return carry

    lax.fori_loop(0, SEQ_PER_STEP, one_sequence, 0)


def _step_spec(steps, width):
    return pl.BlockSpec((steps, SEQ_PER_STEP, width), lambda g: (0, g, 0))


def _gdn_step(act3, graw3, alog_row, dtb_row, consts, s0):
    steps, nb, _ = act3.shape
    assert nb % SEQ_PER_STEP == 0 and steps <= STEP_ROWS
    rowspec = pl.BlockSpec((1, LANES), lambda b: (0, 0))
    sspec = pl.BlockSpec((SEQ_PER_STEP, GDN_V_HEADS, GDN_HEAD_DIM, GDN_HEAD_DIM), lambda g: (g, 0, 0, 0))
    o, s = pl.pallas_call(
        functools.partial(_gdn_step_kernel, lr=steps),
        grid=(nb // SEQ_PER_STEP,),
        in_specs=[_step_spec(steps, GDN_CONV_DIM), _step_spec(steps, LANES), rowspec, rowspec]
        + _const_specs(consts, 1) + [sspec],
        out_specs=[_step_spec(steps, GDN_V_DIM), sspec],
        out_shape=[
            jax.ShapeDtypeStruct((steps, nb, GDN_V_DIM), F32),
            jax.ShapeDtypeStruct(s0.shape, F32),
        ],
        scratch_shapes=[
            pltpu.VMEM((STEP_ROWS, GDN_CONV_DIM), F32),
            pltpu.VMEM((CHUNK, LANES), F32),
            pltpu.VMEM((STEP_ROWS, GDN_V_DIM), F32),
        ],
        compiler_params=_cparams("parallel"),
        name="gdn_step",
    )(act3, graw3, alog_row, dtb_row, *consts, s0)
    return o.reshape(steps * nb, GDN_V_DIM), s


SSM_PAIRS = SSM_HEADS // 2
PAIRS_PER_GROUP = SSM_PAIRS // SSM_GROUPS


def _ssm_gates(raw, alog_row, dtb_row, lr):
    row, lane = _iotas((CHUNK, LANES))
    dt = _softplus(raw + dtb_row)
    tile = jnp.where(lane < SSM_HEADS, dt, jnp.where(lane < 2 * SSM_HEADS, -jnp.exp(alog_row) * dt, 0.0))
    if lr < CHUNK:
        tile = jnp.where(row < lr, tile, 0.0)
    return tile


def _ssd_block(act_ref, tile, h_in, h_out, y_ref, dskip_ref, tril3_ref, upper3_ref, e3_ref, fill=_NO_FILL):
    np_ = SSM_PAIRS
    rows = act_ref.shape[0]
    row, lane = _iotas((rows, LANES))
    causal = row >= jnp.bitwise_and(lane, HALF - 1)
    r2, l2 = _iotas((LANES, LANES))
    bdmask = jnp.right_shift(r2, 6) == jnp.right_shift(l2, 6)
    top = r2 < HALF
    _, lane_p = _iotas((np_, LANES))

    t1 = _pad_t(tile)
    cum_ext = _dot(_cat3(t1[2 * np_:4 * np_], 1), upper3_ref[...])
    cum_t, last_b = cum_ext[:, :LANES], cum_ext[:, LANES:]
    cum_rp = _pair_rows(cum_t, np_)
    dt_rp = _pair_rows(t1[0:2 * np_], np_)
    last_rp = jnp.where(lane_p < HALF, last_b[0:np_], last_b[np_:2 * np_])
    coef_rp = jnp.exp(last_rp - cum_rp) * dt_rp
    elast = jnp.exp(last_b)
    cum = _dot(tril3_ref[...], _cat3(tile, 0))
    col_all = _dot(_cat3(cum[0:rows], 1), e3_ref[...])

    pairs = range(np_)
    sl = lambda a, i: a[:, i * LANES:(i + 1) * LANES]
    grp = lambda p: p // PAIRS_PER_GROUP
    bg = [_pad_rows(sl(act_ref, SSM_D_INNER // LANES + g)).astype(BF16) for g in range(SSM_GROUPS)]
    cg = [sl(act_ref, (SSM_D_INNER + SSM_BC) // LANES + g).astype(BF16) for g in range(SSM_GROUPS)]
    bb = [jnp.concatenate([b, b], axis=0) for b in bg]
    cb2 = [_dot_nt(cg[g], bb[g]) for g in range(SSM_GROUPS)]
    fill.emit()
    xp = [sl(act_ref, p) for p in pairs]
    colc = [sl(col_all, p) for p in pairs]
    x2 = [jnp.concatenate([_pad_rows(a), _pad_rows(a)], axis=0) for a in xp]
    lm = [cb2[grp(p)] * jnp.exp(jnp.where(causal, colc[p] - cum_rp[p:p + 1], NEG_BIG)) * dt_rp[p:p + 1] for p in pairs]
    y_diag = []
    for p in pairs:
        y_diag.append(_dot(lm[p].astype(BF16), jnp.where(bdmask, x2[p], 0.0).astype(BF16)))
        if p % 4 == 3:
            fill.emit()
    hp = [h_in[p] for p in pairs]
    y_off = []
    for p in pairs:
        y_off.append(_dot_nt(cg[grp(p)], hp[p].astype(BF16)))
        if p % 4 == 3:
            fill.emit()
    for p in pairs:
        y_ref[:, p * LANES:(p + 1) * LANES] = (y_diag[p] + jnp.exp(colc[p]) * y_off[p]
                                               + dskip_ref[:, p * LANES:(p + 1) * LANES] * xp[p])
    dh = []
    for p in pairs:
        lhs = jnp.where(bdmask, x2[p].T * coef_rp[p:p + 1], 0.0)
        dh.append(_dot(lhs.astype(BF16), bb[grp(p)]))
        if p % 4 == 3:
            fill.emit()
    for p in pairs:
        e_rows = jnp.where(top, elast[p:p + 1], elast[np_ + p:np_ + p + 1])
        h_out[p] = e_rows * hp[p] + dh[p]
    fill.flush()


def _ssd_seq_kernel(x0_ref, bc0_ref, xn_ref, bcn_ref, graw_ref, shift_ref, cw_ref, cb_ref, alog_ref, dtb_ref,
                    dskip_ref, tril3_ref, upper3_ref, e3_ref, y_ref, h_ref, ext_ref, act_ref):
    n = pl.program_id(1)
    conv = functools.partial(_conv_block, shift_ref=shift_ref, w_ref=cw_ref, b_ref=cb_ref, ext_ref=ext_ref,
                             l2_cols=0, q_cols=0, qscale=1.0)

    @pl.when(n == 0)
    def _():
        h_ref[...] = jnp.zeros(h_ref.shape, F32)
        _Filler(conv([x0_ref, bc0_ref], act_ref=act_ref.at[0], seq_start=True), 1).flush()

    @pl.when(n > 0)
    def _():
        act_ref[0] = act_ref[1]

    tile = _ssm_gates(graw_ref[...], alog_ref[...], dtb_ref[...], CHUNK)
    fill = _Filler(conv([xn_ref, bcn_ref], act_ref=act_ref.at[1], seq_start=False), SSM_PAIRS)
    _ssd_block(act_ref.at[0], tile, h_ref, h_ref, y_ref, dskip_ref, tril3_ref, upper3_ref, e3_ref, fill)


def _ssd_seq(pm, graw, conv_w, conv_b, alog_row, dtb_row, dskip_row, consts, nseq, seq_len):
    nc = seq_len // CHUNK
    last = nseq * nc - 1
    bc_blk = 2 * SSM_D_INNER // (2 * SSM_BC)
    nxt = lambda b, n: jnp.minimum(b * nc + n + 1, last)
    rowspec = pl.BlockSpec((1, LANES), lambda b, n: (0, 0))
    return pl.pallas_call(
        _ssd_seq_kernel,
        grid=(nseq, nc),
        in_specs=[
            pl.BlockSpec((CHUNK, SSM_D_INNER), lambda b, n: (b * nc, 0)),
            pl.BlockSpec((CHUNK, 2 * SSM_BC), lambda b, n: (b * nc, bc_blk)),
            pl.BlockSpec((CHUNK, SSM_D_INNER), lambda b, n: (nxt(b, n), 0)),
            pl.BlockSpec((CHUNK, 2 * SSM_BC), lambda b, n: (nxt(b, n), bc_blk)),
            pl.BlockSpec((CHUNK, LANES), lambda b, n: (b * nc + n, 0)),
            pl.BlockSpec(((CONV_WIDTH - 1) * CHUNK, BHALO + CHUNK), lambda b, n: (0, 0)),
            pl.BlockSpec((CONV_WIDTH, SSM_CONV_DIM), lambda b, n: (0, 0)),
            pl.BlockSpec((1, SSM_CONV_DIM), lambda b, n: (0, 0)),
            rowspec, rowspec,
            pl.BlockSpec((1, SSM_D_INNER), lambda b, n: (0, 0)),
        ] + _const_specs(consts, 2),
        out_specs=[
            pl.BlockSpec((CHUNK, SSM_D_INNER), lambda b, n: (b * nc + n, 0)),
            pl.BlockSpec((None, SSM_PAIRS, LANES, SSM_STATE), lambda b, n: (b, 0, 0, 0)),
        ],
        out_shape=[
            jax.ShapeDtypeStruct((nseq * seq_len, SSM_D_INNER), F32),
            jax.ShapeDtypeStruct((nseq, SSM_PAIRS, LANES, SSM_STATE), F32),
        ],
        scratch_shapes=[
            pltpu.VMEM((CHUNK + BHALO, SSM_CONV_DIM), BF16),
            pltpu.VMEM((2, CHUNK, SSM_CONV_DIM), F32),
        ],
        compiler_params=_cparams("parallel", "arbitrary"),
        name="ssd_seq",
    )(pm, pm, pm, pm, graw, _shift_matrix(), conv_w, conv_b.reshape(1, SSM_CONV_DIM), alog_row, dtb_row, dskip_row, *consts)


def _ssd_step_kernel(act_ref, graw_ref, alog_ref, dtb_ref, dskip_ref, tril3_ref, upper3_ref, e3_ref, h0_ref,
                     y_ref, h_ref, apad_ref, gpad_ref, ypad_ref, *, lr):
    def one_sequence(s, carry):
        _load_padded(act_ref, s, apad_ref, lr)
        _load_padded(graw_ref, s, gpad_ref, lr)
        tile = _ssm_gates(gpad_ref[...], alog_ref[...], dtb_ref[...], lr)
        _ssd_block(apad_ref, tile, h0_ref.at[s], h_ref.at[s], ypad_ref, dskip_ref, tril3_ref, upper3_ref, e3_ref)
        _store_tokens(ypad_ref, y_ref, s, lr)
        return carry

    lax.fori_loop(0, SEQ_PER_STEP, one_sequence, 0)


def _ssd_step(act3, graw3, alog_row, dtb_row, dskip_row, consts, h0):
    steps, nb, _ = act3.shape
    assert nb % SEQ_PER_STEP == 0 and steps <= STEP_ROWS
    rowspec = pl.BlockSpec((1, LANES), lambda b: (0, 0))
    hspec = pl.BlockSpec((SEQ_PER_STEP, SSM_PAIRS, LANES, SSM_STATE), lambda g: (g, 0, 0, 0))
    y, h = pl.pallas_call(
        functools.partial(_ssd_step_kernel, lr=steps),
        grid=(nb // SEQ_PER_STEP,),
        in_specs=[_step_spec(steps, SSM_CONV_DIM), _step_spec(steps, LANES), rowspec, rowspec,
                  pl.BlockSpec((1, SSM_D_INNER), lambda b: (0, 0))] + _const_specs(consts, 1) + [hspec],
        out_specs=[_step_spec(steps, SSM_D_INNER), hspec],
        out_shape=[
            jax.ShapeDtypeStruct((steps, nb, SSM_D_INNER), F32),
            jax.ShapeDtypeStruct(h0.shape, F32),
        ],
        scratch_shapes=[
            pltpu.VMEM((STEP_ROWS, SSM_CONV_DIM), F32),
            pltpu.VMEM((CHUNK, LANES), F32),
            pltpu.VMEM((STEP_ROWS, SSM_D_INNER), F32),
        ],
        compiler_params=_cparams("parallel"),
        name="ssd_step",
    )(act3, graw3, alog_row, dtb_row, dskip_row, *consts, h0)
    return y.reshape(steps * nb, SSM_D_INNER), h


FFN_TILE = FFN_HIDDEN // 2
GDN_PROJ_TILE = GDN_MAIN // 3
SSM_PROJ_TILE = SSM_MAIN // 2


def _lane_row(pieces):
    row = jnp.zeros((1, LANES), F32)
    for off, vec in pieces:
        row = row.at[0, off:off + vec.shape[0]].set(vec.astype(F32))
    return row


def _stage_params(w_mod, b_mod, norm_mix, norm_ffn, norm_final, gdn_w_in, gdn_conv_w, gdn_a_log, gdn_dt_bias,
                  gdn_norm, gdn_w_out, ssm_w_in, ssm_conv_w, ssm_conv_b, ssm_a_log, ssm_dt_bias, ssm_d, ssm_norm,
                  ssm_w_out, ffn_w_gate_up, ffn_w_down):
    perm_g = np.concatenate([np.arange(0, GDN_V_HEADS, 2), np.arange(1, GDN_V_HEADS, 2)])
    perm_s = np.concatenate([np.arange(0, SSM_HEADS, 2), np.arange(1, SSM_HEADS, 2)])
    g_in, s_in = gdn_w_in[0], ssm_w_in[0]
    beta_cols = g_in[:, GDN_MAIN:GDN_MAIN + GDN_V_HEADS][:, perm_g]
    a_cols = g_in[:, GDN_MAIN + GDN_V_HEADS:GDN_MAIN + 2 * GDN_V_HEADS][:, perm_g]
    gdn_small = jnp.concatenate([a_cols, beta_cols, jnp.zeros((D_MODEL, LANES - 2 * GDN_V_HEADS), F32)], axis=1)
    dt_cols = s_in[:, SSM_MAIN:SSM_MAIN + SSM_HEADS][:, perm_s]
    ssm_small = jnp.concatenate([dt_cols, dt_cols, jnp.zeros((D_MODEL, LANES - 2 * SSM_HEADS), F32)], axis=1)
    return dict(
        w_mod=w_mod, b_mod=b_mod, norm_mix=norm_mix, norm_ffn=norm_ffn, norm_final=norm_final,
        gdn_main=g_in[:, :GDN_MAIN].astype(BF16), gdn_small=gdn_small.astype(BF16),
        gdn_conv_w=gdn_conv_w[0], gdn_conv_b=jnp.zeros((GDN_CONV_DIM,), F32),
        gdn_alog_row=_lane_row([(0, gdn_a_log[0][perm_g])]), gdn_dtb_row=_lane_row([(0, gdn_dt_bias[0][perm_g])]),
        gdn_norm=jnp.tile(gdn_norm[0], GDN_V_HEADS), gdn_w_out=gdn_w_out[0].astype(BF16),
        gdn_consts=_recurrence_consts(GDN_PAIRS, 0, GDN_PAIRS),
        ssm_main=jnp.concatenate(
            [s_in[:, SSM_D_INNER:2 * SSM_D_INNER], s_in[:, :SSM_D_INNER], s_in[:, 2 * SSM_D_INNER:SSM_MAIN]],
            axis=1).astype(BF16),
        ssm_small=ssm_small.astype(BF16),
        ssm_conv_w=ssm_conv_w[0], ssm_conv_b=ssm_conv_b[0],
        ssm_alog_row=_lane_row([(SSM_HEADS, ssm_a_log[0][perm_s])]),
        ssm_dtb_row=_lane_row([(0, ssm_dt_bias[0][perm_s]), (SSM_HEADS, ssm_dt_bias[0][perm_s])]),
        ssm_dskip_row=jnp.repeat(ssm_d[0], SSM_HEAD_DIM).reshape(1, SSM_D_INNER),
        ssm_norm=ssm_norm[0], ssm_w_out=ssm_w_out[0].astype(BF16),
        ssm_consts=_recurrence_consts(SSM_PAIRS, SSM_HEADS, SSM_HEADS + SSM_PAIRS),
        wg=[ffn_w_gate_up[i][:, :FFN_HIDDEN].astype(BF16) for i in range(2)],
        wu=[ffn_w_gate_up[i][:, FFN_HIDDEN:].astype(BF16) for i in range(2)],
        wd=[ffn_w_down[i].astype(BF16) for i in range(2)],
    )


def _ffn(x, layer, mod3, rows_up, rows_down, p, final_w):
    act = _ffn_up(x, p["norm_ffn"][layer], mod3, rows_up, p["wg"][layer], p["wu"][layer], FFN_TILE)
    return _ffn_down(act, x, mod3, rows_down, p["wd"][layer], final_w)


QSCALE = GDN_HEAD_DIM ** -0.5


def _trunk_seq(x3, mod, p):
    nseq, seq_len, _ = x3.shape
    m = nseq * seq_len
    x = x3.reshape(m, D_MODEL)
    mod3 = [mod[l].reshape(nseq, 1, 6 * D_MODEL) for l in range(2)]
    rows_a = _Rows(m, min(1024, seq_len), seq_len, 1)
    rows_b = _Rows(m, min(512, seq_len), seq_len, 1)

    pm, ps = _in_proj(x, p["norm_mix"][0], mod3[0], rows_a, 1, 0, p["gdn_main"], p["gdn_small"], GDN_PROJ_TILE)
    o, gdn_s = _gdn_seq(pm, ps, p["gdn_conv_w"], p["gdn_conv_b"], p["gdn_alog_row"], p["gdn_dtb_row"],
                        p["gdn_consts"], nseq, seq_len)
    tail = pm.reshape(nseq, seq_len, GDN_MAIN)[:, seq_len - (CONV_WIDTH - 1):].astype(F32)
    gdn_c = tail[..., :GDN_CONV_DIM]
    x = _mixer_out(o, pm, 2, p["gdn_norm"], x, mod3[0], rows_b, p["gdn_w_out"], GDN_HEAD_DIM, False)
    x = _ffn(x, 0, mod3[0], rows_a, rows_b, p, None)

    pm, ps = _in_proj(x, p["norm_mix"][1], mod3[1], rows_a, 1, 0, p["ssm_main"], p["ssm_small"], SSM_PROJ_TILE)
    y, ssm_h = _ssd_seq(pm, ps, p["ssm_conv_w"], p["ssm_conv_b"], p["ssm_alog_row"], p["ssm_dtb_row"],
                        p["ssm_dskip_row"], p["ssm_consts"], nseq, seq_len)
    tail = pm.reshape(nseq, seq_len, SSM_MAIN)[:, seq_len - (CONV_WIDTH - 1):].astype(F32)
    ssm_c = jnp.concatenate([tail[..., :SSM_D_INNER], tail[..., 2 * SSM_D_INNER:]], axis=-1)
    x = _mixer_out(y, pm, 1, p["ssm_norm"], x, mod3[1], rows_b, p["ssm_w_out"], SSM_D_INNER // SSM_GROUPS, True)
    _, y_out = _ffn(x, 1, mod3[1], rows_a, rows_b, p, p["norm_final"])

    return (y_out.reshape(nseq, seq_len, D_MODEL), gdn_s[None], gdn_c[None],
            ssm_h.reshape(nseq, SSM_HEADS, SSM_HEAD_DIM, SSM_STATE)[None], ssm_c[None])


def _trunk_step(x3, mod, st_gdn, cv_gdn, st_ssm, cv_ssm, p):
    nb, steps, _ = x3.shape
    assert steps >= CONV_WIDTH - 1
    m = nb * steps
    x = jnp.transpose(x3, (1, 0, 2)).reshape(m, D_MODEL)
    mod3 = [mod[l].reshape(1, nb, 6 * D_MODEL) for l in range(2)]
    rows = _Rows(m, m, None, nb)
    tok = lambda a: jnp.transpose(a, (1, 0, 2))

    pm, ps = _in_proj(x, p["norm_mix"][0], mod3[0], rows, 1, 0, p["gdn_main"], p["gdn_small"], GDN_PROJ_TILE)
    u3 = pm.reshape(steps, nb, GDN_MAIN)
    act3 = _conv_steps(u3, 0, 0, GDN_CONV_DIM, tok(cv_gdn[0]), p["gdn_conv_w"], p["gdn_conv_b"], 2, QSCALE)
    o, gdn_s = _gdn_step(act3, ps.reshape(steps, nb, LANES), p["gdn_alog_row"], p["gdn_dtb_row"], p["gdn_consts"],
                         st_gdn[0])
    gdn_c = tok(u3[steps - (CONV_WIDTH - 1):, :, :GDN_CONV_DIM].astype(F32))
    x = _mixer_out(o, pm, 2, p["gdn_norm"], x, mod3[0], rows, p["gdn_w_out"], GDN_HEAD_DIM, False)
    x = _ffn(x, 0, mod3[0], rows, rows, p, None)

    pm, ps = _in_proj(x, p["norm_mix"][1], mod3[1], rows, 1, 0, p["ssm_main"], p["ssm_small"], SSM_PROJ_TILE)
    u3 = pm.reshape(steps, nb, SSM_MAIN)
    act3 = _conv_steps(u3, SSM_D_INNER // CONV_COLS, SSM_D_INNER // CONV_COLS, SSM_CONV_DIM, tok(cv_ssm[0]),
                       p["ssm_conv_w"], p["ssm_conv_b"], 0, 1.0)
    h0 = st_ssm[0].reshape(nb, SSM_PAIRS, LANES, SSM_STATE)
    y, ssm_h = _ssd_step(act3, ps.reshape(steps, nb, LANES), p["ssm_alog_row"], p["ssm_dtb_row"],
                         p["ssm_dskip_row"], p["ssm_consts"], h0)
    tail = u3[steps - (CONV_WIDTH - 1):].astype(F32)
    ssm_c = tok(jnp.concatenate([tail[..., :SSM_D_INNER], tail[..., 2 * SSM_D_INNER:]], axis=-1))
    x = _mixer_out(y, pm, 1, p["ssm_norm"], x, mod3[1], rows, p["ssm_w_out"], SSM_D_INNER // SSM_GROUPS, True)
    _, y_out = _ffn(x, 1, mod3[1], rows, rows, p, p["norm_final"])

    return (tok(y_out.reshape(steps, nb, D_MODEL)), gdn_s[None], gdn_c[None],
            ssm_h.reshape(nb, SSM_HEADS, SSM_HEAD_DIM, SSM_STATE)[None], ssm_c[None])


def kernel(x_prompt, x_sample, c_prompt, c_sample, state_gdn, state_gdn_conv, state_ssm, state_ssm_conv, w_mod, b_mod,
           norm_mix, norm_ffn, norm_final, gdn_w_in, gdn_conv_w, gdn_a_log, gdn_dt_bias, gdn_norm, gdn_w_out, ssm_w_in,
           ssm_conv_w, ssm_conv_b, ssm_a_log, ssm_dt_bias, ssm_d, ssm_norm, ssm_w_out, ffn_w_gate_up, ffn_w_down):
    p = _stage_params(w_mod, b_mod, norm_mix, norm_ffn, norm_final, gdn_w_in, gdn_conv_w, gdn_a_log, gdn_dt_bias,
                      gdn_norm, gdn_w_out, ssm_w_in, ssm_conv_w, ssm_conv_b, ssm_a_log, ssm_dt_bias, ssm_d, ssm_norm,
                      ssm_w_out, ffn_w_gate_up, ffn_w_down)
    n_prompt = x_prompt.shape[0]
    mod = _modulation(jnp.concatenate([c_prompt, c_sample], axis=0), p["w_mod"], p["b_mod"])
    y_p, gs_p, gc_p, ss_p, sc_p = _trunk_seq(x_prompt, mod[:, :n_prompt], p)
    y_s, gs_s, gc_s, ss_s, sc_s = _trunk_step(x_sample, mod[:, n_prompt:], state_gdn, state_gdn_conv, state_ssm,
                                              state_ssm_conv, p)
    return (y_p, y_s, gs_p, gc_p, ss_p, sc_p, gs_s, gc_s, ss_s, sc_s)
```

```python
import functools

import numpy as np
import jax
import jax.numpy as jnp
from jax import lax
from jax.experimental import pallas as pl
from jax.experimental.pallas import tpu as pltpu

F32 = jnp.float32
BF16 = jnp.bfloat16

D_MODEL = 1024
EPS = 1e-6
CONV_WIDTH = 4
CHUNK = 64
LANES = 128
HALF = LANES // 2

GDN_QK_HEADS = 8
GDN_V_HEADS = 16
GDN_HEAD_DIM = 128
GDN_QK_DIM = GDN_QK_HEADS * GDN_HEAD_DIM
GDN_V_DIM = GDN_V_HEADS * GDN_HEAD_DIM
GDN_CONV_DIM = 2 * GDN_QK_DIM + GDN_V_DIM
GDN_MAIN = GDN_CONV_DIM + GDN_V_DIM

SSM_D_INNER = 2 * D_MODEL
SSM_HEAD_DIM = 64
SSM_HEADS = SSM_D_INNER // SSM_HEAD_DIM
SSM_GROUPS = 4
SSM_STATE = 128
SSM_BC = SSM_GROUPS * SSM_STATE
SSM_CONV_DIM = SSM_D_INNER + 2 * SSM_BC
SSM_MAIN = SSM_D_INNER + SSM_CONV_DIM

FFN_HIDDEN = 2816

VMEM_LIMIT = 56 * 1024 * 1024
NEG_BIG = -1e30


def _cparams(*sem):
    return pltpu.CompilerParams(dimension_semantics=sem, vmem_limit_bytes=VMEM_LIMIT)


def _silu(x):
    hx = 0.5 * x
    return hx + hx * jnp.tanh(hx)


def _softplus(x):
    return jnp.maximum(x, 0.0) + jnp.log1p(jnp.exp(-jnp.abs(x)))


def _dot(a, b):
    return jnp.dot(a, b, preferred_element_type=F32)


def _dot_nt(a, b):
    return lax.dot_general(a, b, (((1,), (1,)), ((), ())), preferred_element_type=F32)


def _tile_rows(v, rep):
    return v if rep == 1 else jnp.concatenate([v] * rep, axis=0)


def _mod_kernel(c_ref, w_ref, b_ref, o_ref):
    cs = _silu(c_ref[...]).astype(BF16)
    o_ref[...] = _dot(cs, w_ref[...].astype(BF16)) + b_ref[...]


def _modulation(c, w_mod, b_mod):
    depth, _, n = w_mod.shape
    bc = c.shape[0]
    tn = 1536
    return pl.pallas_call(
        _mod_kernel,
        grid=(depth, n // tn),
        in_specs=[
            pl.BlockSpec((bc, D_MODEL), lambda l, j: (0, 0)),
            pl.BlockSpec((None, D_MODEL, tn), lambda l, j: (l, 0, j)),
            pl.BlockSpec((None, 1, tn), lambda l, j: (l, 0, j)),
        ],
        out_specs=pl.BlockSpec((None, bc, tn), lambda l, j: (l, 0, j)),
        out_shape=jax.ShapeDtypeStruct((depth, bc, n), F32),
        compiler_params=_cparams("parallel", "parallel"),
        name="adaln_mod",
    )(c, w_mod, b_mod.reshape(depth, 1, n))


def _norm_mod(x, nw, sc, sh, rep):
    y = x * lax.rsqrt(jnp.mean(x * x, axis=-1, keepdims=True) + EPS) * nw
    return y * (1.0 + _tile_rows(sc, rep)) + _tile_rows(sh, rep)


def _in_proj_kernel(x_ref, nw_ref, sc_ref, sh_ref, w_ref, w2_ref, o_ref, o2_ref, h_ref, *, rep):
    @pl.when(pl.program_id(1) == 0)
    def _():
        h = _norm_mod(x_ref[...], nw_ref[...], sc_ref[...], sh_ref[...], rep).astype(BF16)
        h_ref[...] = h
        o2_ref[...] = _dot(h, w2_ref[...])

    o_ref[...] = _dot(h_ref[...], w_ref[...]).astype(o_ref.dtype)


def _ffn_up_kernel(x_ref, nw_ref, sc_ref, sh_ref, wg_ref, wu_ref, o_ref, h_ref, *, rep):
    @pl.when(pl.program_id(1) == 0)
    def _():
        h_ref[...] = _norm_mod(x_ref[...], nw_ref[...], sc_ref[...], sh_ref[...], rep).astype(BF16)

    h = h_ref[...]
    o_ref[...] = (_silu(_dot(h, wg_ref[...])) * _dot(h, wu_ref[...])).astype(BF16)


class _Rows:
    def __init__(self, m, tm, group_rows, mod_rows):
        assert m % tm == 0
        self.m, self.tm = m, tm
        if mod_rows == 1:
            assert group_rows % tm == 0
            self.rep = 1
            self.gmap = lambda i: (i * tm) // group_rows
        else:
            assert tm % mod_rows == 0
            self.rep = tm // mod_rows
            self.gmap = lambda i: 0
        self.mod_rows = mod_rows

    def mod_spec(self, col_block, with_j):
        if with_j:
            return pl.BlockSpec((None, self.mod_rows, D_MODEL), lambda i, j: (self.gmap(i), 0, col_block))
        return pl.BlockSpec((None, self.mod_rows, D_MODEL), lambda i: (self.gmap(i), 0, col_block))


def _in_proj(x, nw, mod3, rows, sc_blk, sh_blk, w, w2, tn):
    m, tm = rows.m, rows.tm
    n = w.shape[1]
    assert n % tn == 0
    return pl.pallas_call(
        functools.partial(_in_proj_kernel, rep=rows.rep),
        grid=(m // tm, n // tn),
        in_specs=[
            pl.BlockSpec((tm, D_MODEL), lambda i, j: (i, 0)),
            pl.BlockSpec((1, D_MODEL), lambda i, j: (0, 0)),
            rows.mod_spec(sc_blk, True),
            rows.mod_spec(sh_blk, True),
            pl.BlockSpec((D_MODEL, tn), lambda i, j: (0, j)),
            pl.BlockSpec((D_MODEL, LANES), lambda i, j: (0, 0)),
        ],
        out_specs=[
            pl.BlockSpec((tm, tn), lambda i, j: (i, j)),
            pl.BlockSpec((tm, LANES), lambda i, j: (i, 0)),
        ],
        out_shape=[jax.ShapeDtypeStruct((m, n), BF16), jax.ShapeDtypeStruct((m, LANES), F32)],
        scratch_shapes=[pltpu.VMEM((tm, D_MODEL), BF16)],
        compiler_params=_cparams("parallel", "arbitrary"),
        name="in_proj",
    )(x, nw.reshape(1, D_MODEL), mod3, mod3, w, w2)


def _ffn_up(x, nw, mod3, rows, wg, wu, th):
    m, tm = rows.m, rows.tm
    assert FFN_HIDDEN % th == 0
    return pl.pallas_call(
        functools.partial(_ffn_up_kernel, rep=rows.rep),
        grid=(m // tm, FFN_HIDDEN // th),
        in_specs=[
            pl.BlockSpec((tm, D_MODEL), lambda i, j: (i, 0)),
            pl.BlockSpec((1, D_MODEL), lambda i, j: (0, 0)),
            rows.mod_spec(4, True),
            rows.mod_spec(3, True),
            pl.BlockSpec((D_MODEL, th), lambda i, j: (0, j)),
            pl.BlockSpec((D_MODEL, th), lambda i, j: (0, j)),
        ],
        out_specs=pl.BlockSpec((tm, th), lambda i, j: (i, j)),
        out_shape=jax.ShapeDtypeStruct((m, FFN_HIDDEN), BF16),
        scratch_shapes=[pltpu.VMEM((tm, D_MODEL), BF16)],
        compiler_params=_cparams("parallel", "arbitrary"),
        name="ffn_up",
    )(x, nw.reshape(1, D_MODEL), mod3, mod3, wg, wu)


def _resid_store(acc, x_ref, gt_ref, o_ref, fnw_ref, y_ref, rep):
    xn = x_ref[...] + _tile_rows(gt_ref[...], rep) * acc
    o_ref[...] = xn
    if y_ref is not None:
        y_ref[...] = xn * lax.rsqrt(jnp.mean(xn * xn, axis=-1, keepdims=True) + EPS) * fnw_ref[...]


def _ffn_down_kernel(a_ref, x_ref, gt_ref, w_ref, *rest, rep, final):
    if final:
        fnw_ref, o_ref, y_ref = rest
    else:
        (o_ref,), fnw_ref, y_ref = rest, None, None
    _resid_store(_dot(a_ref[...], w_ref[...]), x_ref, gt_ref, o_ref, fnw_ref, y_ref, rep)


def _ffn_down(act, x, mod3, rows, w, fnw):
    m, tm = rows.m, rows.tm
    final = fnw is not None
    in_specs = [
        pl.BlockSpec((tm, FFN_HIDDEN), lambda i: (i, 0)),
        pl.BlockSpec((tm, D_MODEL), lambda i: (i, 0)),
        rows.mod_spec(5, False),
        pl.BlockSpec((FFN_HIDDEN, D_MODEL), lambda i: (0, 0)),
    ]
    args = [act, x, mod3, w]
    row_spec = pl.BlockSpec((tm, D_MODEL), lambda i: (i, 0))
    out_shape = jax.ShapeDtypeStruct((m, D_MODEL), F32)
    if final:
        in_specs.append(pl.BlockSpec((1, D_MODEL), lambda i: (0, 0)))
        args.append(fnw.reshape(1, D_MODEL))
        out_specs, out_shapes = [row_spec, row_spec], [out_shape, out_shape]
    else:
        out_specs, out_shapes = row_spec, out_shape
    return pl.pallas_call(
        functools.partial(_ffn_down_kernel, rep=rows.rep, final=final),
        grid=(m // tm,),
        in_specs=in_specs,
        out_specs=out_specs,
        out_shape=out_shapes,
        compiler_params=_cparams("parallel"),
        name="ffn_down",
    )(*args)


def _mixer_out_kernel(y_ref, z_ref, nw_ref, x_ref, gt_ref, w_ref, o_ref, a_ref, *, rep, group, gate_first):
    width = y_ref.shape[1]
    for s in range(0, width, group):
        y = y_ref[:, s:s + group]
        gate = _silu(z_ref[:, s:s + group].astype(F32))
        if gate_first:
            y = y * gate
        y = y * lax.rsqrt(jnp.mean(y * y, axis=-1, keepdims=True) + EPS) * nw_ref[:, s:s + group]
        if not gate_first:
            y = y * gate
        a_ref[:, s:s + group] = y.astype(BF16)
    _resid_store(_dot(a_ref[...], w_ref[...]), x_ref, gt_ref, o_ref, None, None, rep)


def _mixer_out(y, zsrc, z_blk, nw_full, x, mod3, rows, w, group, gate_first):
    m, tm = rows.m, rows.tm
    width = y.shape[1]
    return pl.pallas_call(
        functools.partial(_mixer_out_kernel, rep=rows.rep, group=group, gate_first=gate_first),
        grid=(m // tm,),
        in_specs=[
            pl.BlockSpec((tm, width), lambda i: (i, 0)),
            pl.BlockSpec((tm, width), lambda i: (i, z_blk)),
            pl.BlockSpec((1, width), lambda i: (0, 0)),
            pl.BlockSpec((tm, D_MODEL), lambda i: (i, 0)),
            rows.mod_spec(2, False),
            pl.BlockSpec((width, D_MODEL), lambda i: (0, 0)),
        ],
        out_specs=pl.BlockSpec((tm, D_MODEL), lambda i: (i, 0)),
        out_shape=jax.ShapeDtypeStruct((m, D_MODEL), F32),
        scratch_shapes=[pltpu.VMEM((tm, width), BF16)],
        compiler_params=_cparams("parallel"),
        name="mixer_out",
    )(y, zsrc, nw_full.reshape(1, width), x, mod3, w)


CONV_COLS = 1024


def _post_conv(acc, o_ref, cb, n_l2, qscale):
    y = _silu(acc)
    if n_l2 == 0:
        o_ref[...] = y
        return

    @pl.when(cb < n_l2)
    def _():
        scale = jnp.where(cb == 0, qscale, 1.0).astype(F32)
        for s in range(0, CONV_COLS, GDN_HEAD_DIM):
            yh = y[:, s:s + GDN_HEAD_DIM]
            o_ref[:, s:s + GDN_HEAD_DIM] = yh * lax.rsqrt(jnp.sum(yh * yh, axis=-1, keepdims=True) + EPS) * scale

    @pl.when(cb >= n_l2)
    def _():
        o_ref[...] = y


BHALO = 16
CONV_GROUP = 2 * LANES


def _shift_matrix():
    s = np.zeros(((CONV_WIDTH - 1) * CHUNK, BHALO + CHUNK), np.float32)
    for tap in range(CONV_WIDTH - 1):
        for r in range(CHUNK):
            s[tap * CHUNK + r, BHALO - (CONV_WIDTH - 1) + tap + r] = 1.0
    return jnp.asarray(s, BF16)


def _conv_block(raw_refs, shift_ref, w_ref, b_ref, ext_ref, act_ref, seq_start, l2_cols, q_cols, qscale):
    width = ext_ref.shape[1]
    if seq_start:
        ext_ref[0:BHALO, :] = jnp.zeros((BHALO, width), BF16)
    else:
        ext_ref[0:BHALO, :] = ext_ref[CHUNK:CHUNK + BHALO, :]

    off = 0
    for ref in raw_refs:
        ext_ref[BHALO:BHALO + CHUNK, off:off + ref.shape[1]] = ref[...]
        off += ref.shape[1]

    def lane_group(s):
        cols = slice(s, s + CONV_GROUP)
        sh = _dot(shift_ref[...], ext_ref[:, cols])
        acc = b_ref[:, cols] + w_ref[CONV_WIDTH - 1:CONV_WIDTH, cols] * ext_ref[BHALO:BHALO + CHUNK, cols].astype(F32)
        for tap in range(CONV_WIDTH - 1):
            acc = acc + w_ref[tap:tap + 1, cols] * sh[tap * CHUNK:(tap + 1) * CHUNK]
        y = _silu(acc)
        for h in range(s, s + CONV_GROUP, LANES):
            yh = y[:, h - s:h - s + LANES]
            if h < l2_cols:
                yh = yh * lax.rsqrt(jnp.sum(yh * yh, axis=-1, keepdims=True) + EPS)
                if h < q_cols:
                    yh = yh * qscale
            act_ref[:, h:h + LANES] = yh

    return [functools.partial(lane_group, s) for s in range(0, width, CONV_GROUP)]


class _Filler:
    def __init__(self, tasks, shares):
        self.tasks, self.per = list(tasks), -(-len(tasks) // shares)

    def emit(self):
        for task in self.tasks[:self.per]:
            task()
        self.tasks = self.tasks[self.per:]

    def flush(self):
        for task in self.tasks:
            task()
        self.tasks = []


_NO_FILL = _Filler([], 1)


def _conv_steps_kernel(u_ref, hist_ref, w_ref, b_ref, o_ref, *, steps, n_l2, qscale):
    cb = pl.program_id(0)
    ext = [hist_ref[i] for i in range(CONV_WIDTH - 1)] + [u_ref[i].astype(F32) for i in range(steps)]
    for t in range(steps):
        acc = b_ref[...] + w_ref[0:1, :] * ext[t]
        for tap in range(1, CONV_WIDTH):
            acc = acc + w_ref[tap:tap + 1, :] * ext[t + tap]
        _post_conv(acc, o_ref.at[t], cb, n_l2, qscale)


def _conv_steps(u3, skip_at, skip, n_cols, hist3, conv_w, conv_b, n_l2, qscale):
    steps, nb, _ = u3.shape
    return pl.pallas_call(
        functools.partial(_conv_steps_kernel, steps=steps, n_l2=n_l2, qscale=qscale),
        grid=(n_cols // CONV_COLS,),
        in_specs=[
            pl.BlockSpec((steps, nb, CONV_COLS), lambda c: (0, 0, c + skip * (c >= skip_at))),
            pl.BlockSpec((CONV_WIDTH - 1, nb, CONV_COLS), lambda c: (0, 0, c)),
            pl.BlockSpec((CONV_WIDTH, CONV_COLS), lambda c: (0, c)),
            pl.BlockSpec((1, CONV_COLS), lambda c: (0, c)),
        ],
        out_specs=pl.BlockSpec((steps, nb, CONV_COLS), lambda c: (0, 0, c)),
        out_shape=jax.ShapeDtypeStruct((steps, nb, n_cols), F32),
        compiler_params=_cparams("parallel"),
        name="conv_steps",
    )(u3, hist3, conv_w, conv_b.reshape(1, n_cols))


def _split3(x):
    hi = x.astype(BF16)
    r = x - hi.astype(F32)
    mid = r.astype(BF16)
    lo = (r - mid.astype(F32)).astype(BF16)
    return hi, mid, lo


def _cat3(x, axis):
    return jnp.concatenate(_split3(x), axis=axis)


def _pad_t(tile):
    return jnp.concatenate([tile, jnp.zeros_like(tile)], axis=0).T


def _pair_rows(t, n):
    return t[0:n] + pltpu.roll(t[n:2 * n], HALF, 1)


def _iotas(shape):
    return lax.broadcasted_iota(jnp.int32, shape, 0), lax.broadcasted_iota(jnp.int32, shape, 1)


def _pad_rows(a):
    if a.shape[0] == CHUNK:
        return a
    return jnp.concatenate([a, jnp.zeros((CHUNK - a.shape[0], a.shape[1]), a.dtype)], axis=0)


def _split2(x):
    hi = x.astype(BF16)
    return hi, (x - hi.astype(F32)).astype(BF16)


def _block_diag(pair_bf, bd_ones):
    pair_bf = _pad_rows(pair_bf)
    return jnp.concatenate([pair_bf, pair_bf], axis=0) * bd_ones


def _mm3(lhs_parts, rhs_hi, rhs_lo):
    lh, ll = lhs_parts
    return _dot(jnp.concatenate([lh, lh, ll], axis=1), jnp.concatenate([rhs_hi, rhs_lo, rhs_hi], axis=0))


def _mm_pairs(lhs_parts, rhs_parts, bd_fn, terms=3):
    if terms == 1:
        return [_dot(lh, bd_fn(rh)) for (lh, _), (rh, _) in zip(lhs_parts, rhs_parts)]
    return [_mm3(lp, bd_fn(rh), bd_fn(rl)) for lp, (rh, rl) in zip(lhs_parts, rhs_parts)]


def _mm_hl(lhs, rhs_bf):
    return _dot(lhs.astype(BF16), rhs_bf)


def _unit_lower_inverse(xs, levels, bd_fn, eye, fill, terms=3):
    ps = [eye + x for x in xs]
    if levels <= 1:
        return ps
    rows = xs[0].shape[0]
    stack = lambda a, b: tuple(jnp.concatenate([s, t], axis=0) for s, t in zip(a, b))
    ysp = [_split2(x) for x in xs]
    ys = _mm_pairs(ysp, ysp, bd_fn, terms)
    fill.emit()
    for _ in range(levels - 2):
        ysp = [_split2(y) for y in ys]
        rs = _mm_pairs([stack(yp, _split2(p)) for yp, p in zip(ysp, ps)], ysp, bd_fn, terms)
        fill.emit()
        ys = [r[0:rows] for r in rs]
        ps = [p + r[rows:2 * rows] for p, r in zip(ps, rs)]
    last = _mm_pairs([_split2(p) for p in ps], [_split2(y) for y in ys], bd_fn, terms)
    return [p + t for p, t in zip(ps, last)]


QUARTER = HALF // 2


def _unit_lower_inverse_blocked(xs, fill):
    row, lane = _iotas((QUARTER, LANES))
    r2, l2 = _iotas((LANES, LANES))
    even = jnp.bitwise_and(jnp.right_shift(lane, 5), 1) == 0
    eye_q = jnp.where(row == jnp.bitwise_and(lane, QUARTER - 1), 1.0, 0.0)
    bd4_ones = jnp.where(jnp.right_shift(r2, 5) == jnp.right_shift(l2, 5), 1.0, 0.0).astype(BF16)
    row_q = jnp.right_shift(r2, 5)
    place = jnp.where(((row_q == 1) & (jnp.right_shift(l2, 5) == 0)) | ((row_q == 3) & (jnp.right_shift(l2, 5) == 2)),
                      1.0, 0.0).astype(BF16)
    bd4 = lambda a: jnp.concatenate([a] * 4, axis=0) * bd4_ones

    tops = [x[0:QUARTER] for x in xs]
    bots = [x[QUARTER:CHUNK] for x in xs]
    diag = [jnp.where(even, t, b) for t, b in zip(tops, bots)]
    dinv = _unit_lower_inverse(diag, _levels(QUARTER), bd4, eye_q, fill)
    fill.emit()
    ai = [jnp.where(even, d, 0.0) for d in dinv]
    ci = [jnp.where(even, 0.0, d) for d in dinv]
    b_ai = _mm_pairs([_split2(jnp.where(even, b, 0.0)) for b in bots], [_split2(a) for a in ai], bd4)
    fill.emit()
    low = [_mm3(_split2(c), *[jnp.concatenate([part] * 4, axis=0) * place for part in _split2(e)])
           for c, e in zip(ci, b_ai)]
    return [jnp.concatenate([a, l + c], axis=0) for a, l, c in zip(ai, low, ci)]


def _two_blocks(a, b):
    a_bf, b_bf = _pad_rows(a).astype(BF16), _pad_rows(b).astype(BF16)
    z = jnp.zeros_like(a_bf)
    return jnp.concatenate([jnp.concatenate([a_bf, z], axis=1), jnp.concatenate([z, b_bf], axis=1)], axis=0)


def _recurrence_consts(n_pairs, chan_even0, chan_odd0):
    tril = np.tril(np.ones((CHUNK, CHUNK), np.float32))
    tril3 = np.concatenate([tril] * 3, axis=1)
    upper = np.zeros((LANES, 2 * LANES), np.float32)
    upper[:CHUNK, :CHUNK] = tril.T
    upper[:CHUNK, LANES:] = 1.0
    upper3 = np.concatenate([upper] * 3, axis=0)
    e = np.zeros((LANES, n_pairs * LANES), np.float32)
    for p in range(n_pairs):
        e[chan_even0 + p, p * LANES:p * LANES + HALF] = 1.0
        e[chan_odd0 + p, p * LANES + HALF:(p + 1) * LANES] = 1.0
    e3 = np.concatenate([e] * 3, axis=0)
    return jnp.asarray(tril3, BF16), jnp.asarray(upper3, BF16), jnp.asarray(e3, BF16)


def _levels(lr):
    return max(1, int(np.ceil(np.log2(lr))))


STEP_ROWS = 16
DECODE_TOKENS_SINGLE_TERM = 4


GDN_PAIRS = GDN_V_HEADS // 2
GDN_PAIR_BATCH = 8
GDN_FILL_SHARES = 12 * (GDN_PAIRS // GDN_PAIR_BATCH)


def _gdn_gates(raw, alog_row, dtb_row, lr):
    row, lane = _iotas((CHUNK, LANES))
    g = -jnp.exp(alog_row) * _softplus(raw + dtb_row)
    beta = jax.nn.sigmoid(raw)
    gt = jnp.where(lane < GDN_V_HEADS, g, jnp.where(lane < 2 * GDN_V_HEADS, beta, 0.0))
    if lr < CHUNK:
        gt = jnp.where(row < lr, gt, 0.0)
    return gt


def _gdn_block(insts, tril3_ref, upper3_ref, e3_ref, lr, fill=_NO_FILL):
    np_ = GDN_PAIRS
    rows = insts[0][0].shape[0]
    row, lane = _iotas((rows, LANES))
    jl = jnp.bitwise_and(lane, HALF - 1)
    left = lane < HALF
    causal = row >= jl
    strict = row > jl
    eye2 = jnp.where(row == jl, 1.0, 0.0)
    r2, l2 = _iotas((LANES, LANES))
    bd_ones = jnp.where(jnp.right_shift(r2, 6) == jnp.right_shift(l2, 6), 1.0, 0.0).astype(BF16)
    _, lane_p = _iotas((np_, LANES))
    sl = lambda a, i: a[:, i * LANES:(i + 1) * LANES]

    pre = []
    for _, gt, _, _, _ in insts:
        t1 = _pad_t(gt)
        cum_ext = _dot(_cat3(t1[0:2 * np_], 1), upper3_ref[...])
        cum_t, last_b = cum_ext[:, :LANES], cum_ext[:, LANES:]
        cum_rp = _pair_rows(cum_t, np_)
        beta_rp = _pair_rows(t1[2 * np_:4 * np_], np_)
        last_rp = jnp.where(lane_p < HALF, last_b[0:np_], last_b[np_:2 * np_])
        cum = _dot(tril3_ref[...], _cat3(gt, 0))
        pre.append(dict(cum_rp=cum_rp, beta_rp=beta_rp, ecum_rp=jnp.exp(cum_rp),
                        kdec_rp=jnp.exp(last_rp - cum_rp) * beta_rp, elast=jnp.exp(last_b),
                        col_all=_dot(_cat3(cum[0:rows], 1), e3_ref[...])))

    def run(items):
        heads = [(i, 2 * p + hh) for i, p in items for hh in range(2)]
        act = lambda i: insts[i][0]
        prow = lambda name, it: pre[it[0]][name][it[1]:it[1] + 1]
        q = {it: sl(act(it[0]), it[1]) for it in items}
        k = {it: _pad_rows(sl(act(it[0]), GDN_QK_HEADS + it[1])) for it in items}
        v = {ih: sl(act(ih[0]), 2 * GDN_QK_HEADS + ih[1]) for ih in heads}
        kb = {it: k[it].astype(BF16) for it in items}
        qb = {it: q[it].astype(BF16) for it in items}
        gq = {it: _dot_nt(jnp.concatenate([kb[it][0:rows], qb[it]], axis=0), jnp.concatenate([kb[it], kb[it]], axis=0))
              for it in items}
        fill.emit()
        colc = {it: sl(pre[it[0]]["col_all"], it[1]) for it in items}
        base = {it: jnp.exp(jnp.where(causal, colc[it] - prow("cum_rp", it), NEG_BIG)) * prow("beta_rp", it)
                for it in items}
        x = [jnp.where(strict, -(gq[it][0:rows] * base[it]), 0.0) for it in items]
        qkd = {it: (gq[it][rows:2 * rows] * base[it]).astype(BF16) for it in items}
        if lr == CHUNK:
            minv = dict(zip(items, _unit_lower_inverse_blocked(x, fill)))
        else:
            minv = dict(zip(items, _unit_lower_inverse(x, _levels(lr), lambda a: _block_diag(a, bd_ones), eye2, fill,
                                                       terms=1 if lr <= DECODE_TOKENS_SINGLE_TERM else 3)))
        fill.emit()
        pair_of = lambda ih: (ih[0], ih[1] // 2)
        u = {it: jnp.concatenate([v[(it[0], 2 * it[1])], v[(it[0], 2 * it[1] + 1)]], axis=1)
             + _mm_hl(minv[it] - eye2, _two_blocks(v[(it[0], 2 * it[1])], v[(it[0], 2 * it[1] + 1)])) for it in items}
        fill.emit()
        w = {it: _mm_hl(minv[it] * prow("ecum_rp", it), _two_blocks(kb[it], kb[it])) for it in items}
        fill.emit()
        s_old = {ih: insts[ih[0]][2][ih[1]] for ih in heads}
        r = {ih: _dot(jnp.concatenate([sl(w[pair_of(ih)], ih[1] % 2).astype(BF16), qb[pair_of(ih)]], axis=0),
                      s_old[ih].astype(BF16)) for ih in heads}
        fill.emit()
        delta = {ih: sl(u[pair_of(ih)], ih[1] % 2) - r[ih][0:rows] for ih in heads}
        bd_delta = {it: _two_blocks(delta[(it[0], 2 * it[1])], delta[(it[0], 2 * it[1] + 1)]) for it in items}
        fill.emit()
        od = {it: _dot(jnp.concatenate(
            [qkd[it], (jnp.concatenate([k[it], k[it]], axis=0).T * prow("kdec_rp", it)).astype(BF16)], axis=0),
            bd_delta[it]) for it in items}
        for it in items:
            i, p = it
            ecol = jnp.exp(colc[it])
            ecol_r = pltpu.roll(ecol, HALF, 1)
            efull = (jnp.where(left, ecol, ecol_r), jnp.where(left, ecol_r, ecol))
            for hh in range(2):
                h = 2 * p + hh
                insts[i][4][:, h * LANES:(h + 1) * LANES] = (efull[hh] * r[(i, h)][rows:2 * rows]
                                                             + sl(od[it][0:rows], hh))
                e_h = pre[i]["elast"][hh * np_ + p:hh * np_ + p + 1]
                insts[i][3][h] = e_h * s_old[(i, h)] + sl(od[it][rows:rows + LANES], hh)

    for b0 in range(0, np_, GDN_PAIR_BATCH):
        run([(i, p) for i in range(len(insts)) for p in range(b0, b0 + GDN_PAIR_BATCH)])
    fill.flush()


def _gdn_seq_kernel(raw0_ref, rawn_ref, graw_ref, shift_ref, cw_ref, cb_ref, alog_ref, dtb_ref, tril3_ref, upper3_ref,
                    e3_ref, o_ref, s_ref, ext_ref, act_ref):
    n = pl.program_id(1)
    conv = functools.partial(_conv_block, shift_ref=shift_ref, w_ref=cw_ref, b_ref=cb_ref, ext_ref=ext_ref,
                             l2_cols=2 * GDN_QK_DIM, q_cols=GDN_QK_DIM, qscale=QSCALE)

    @pl.when(n == 0)
    def _():
        s_ref[...] = jnp.zeros(s_ref.shape, F32)
        _Filler(conv([raw0_ref], act_ref=act_ref.at[0], seq_start=True), 1).flush()

    @pl.when(n > 0)
    def _():
        act_ref[0] = act_ref[1]

    gt = _gdn_gates(graw_ref[...], alog_ref[...], dtb_ref[...], CHUNK)
    fill = _Filler(conv([rawn_ref], act_ref=act_ref.at[1], seq_start=False), GDN_FILL_SHARES)
    _gdn_block([(act_ref.at[0], gt, s_ref, s_ref, o_ref)], tril3_ref, upper3_ref, e3_ref, CHUNK, fill)


def _const_specs(consts, ngrid):
    zero = (lambda *_: (0, 0))
    return [pl.BlockSpec(c.shape, zero) for c in consts]


def _gdn_seq(pm, graw, conv_w, conv_b, alog_row, dtb_row, consts, nseq, seq_len):
    nc = seq_len // CHUNK
    last = nseq * nc - 1
    rowspec = pl.BlockSpec((1, LANES), lambda b, n: (0, 0))
    return pl.pallas_call(
        _gdn_seq_kernel,
        grid=(nseq, nc),
        in_specs=[
            pl.BlockSpec((CHUNK, GDN_CONV_DIM), lambda b, n: (b * nc, 0)),
            pl.BlockSpec((CHUNK, GDN_CONV_DIM), lambda b, n: (jnp.minimum(b * nc + n + 1, last), 0)),
            pl.BlockSpec((CHUNK, LANES), lambda b, n: (b * nc + n, 0)),
            pl.BlockSpec(((CONV_WIDTH - 1) * CHUNK, BHALO + CHUNK), lambda b, n: (0, 0)),
            pl.BlockSpec((CONV_WIDTH, GDN_CONV_DIM), lambda b, n: (0, 0)),
            pl.BlockSpec((1, GDN_CONV_DIM), lambda b, n: (0, 0)),
            rowspec, rowspec,
        ] + _const_specs(consts, 2),
        out_specs=[
            pl.BlockSpec((CHUNK, GDN_V_DIM), lambda b, n: (b * nc + n, 0)),
            pl.BlockSpec((None, GDN_V_HEADS, GDN_HEAD_DIM, GDN_HEAD_DIM), lambda b, n: (b, 0, 0, 0)),
        ],
        out_shape=[
            jax.ShapeDtypeStruct((nseq * seq_len, GDN_V_DIM), F32),
            jax.ShapeDtypeStruct((nseq, GDN_V_HEADS, GDN_HEAD_DIM, GDN_HEAD_DIM), F32),
        ],
        scratch_shapes=[
            pltpu.VMEM((CHUNK + BHALO, GDN_CONV_DIM), BF16),
            pltpu.VMEM((2, CHUNK, GDN_CONV_DIM), F32),
        ],
        compiler_params=_cparams("parallel", "arbitrary"),
        name="gdn_seq",
    )(pm, pm, graw, _shift_matrix(), conv_w, conv_b.reshape(1, GDN_CONV_DIM), alog_row, dtb_row, *consts)


SEQ_PER_STEP = 8
SEQ_INTERLEAVE = 4


def _load_padded(src_ref, s, pad_ref, lr):
    pad_ref[...] = jnp.zeros(pad_ref.shape, F32)
    for t in range(lr):
        pad_ref[t:t + 1, :] = src_ref[t, pl.ds(s, 1), :]


def _store_tokens(pad_ref, dst_ref, s, lr):
    for t in range(lr):
        dst_ref[t, pl.ds(s, 1), :] = pad_ref[t:t + 1, :]


def _gdn_step_kernel(act_ref, graw_ref, alog_ref, dtb_ref, tril3_ref, upper3_ref, e3_ref, s0_ref,
                     o_ref, s_ref, apad_ref, gpad_ref, opad_ref, *, lr):
    def some_sequences(it, carry):
        insts = []
        for j in range(SEQ_INTERLEAVE):
            s = it * SEQ_INTERLEAVE + j
            _load_padded(act_ref, s, apad_ref.at[j], lr)
            _load_padded(graw_ref, s, gpad_ref.at[j], lr)
            gt = _gdn_gates(gpad_ref[j], alog_ref[...], dtb_ref[...], lr)
            insts.append((apad_ref.at[j], gt, s0_ref.at[s], s_ref.at[s], opad_ref.at[j]))
        _gdn_block(insts, tril3_ref, upper3_ref, e3_ref, lr)
        for j in range(SEQ_INTERLEAVE):
            _store_tokens(opad_ref.at[j], o_ref, it * SEQ_INTERLEAVE + j, lr)
        return carry

    lax.fori_loop(0, SEQ_PER_STEP // SEQ_INTERLEAVE, some_sequences, 0)


def _step_spec(steps, width):
    return pl.BlockSpec((steps, SEQ_PER_STEP, width), lambda g: (0, g, 0))


def _gdn_step(act3, graw3, alog_row, dtb_row, consts, s0):
    steps, nb, _ = act3.shape
    assert nb % SEQ_PER_STEP == 0 and steps <= STEP_ROWS
    rowspec = pl.BlockSpec((1, LANES), lambda b: (0, 0))
    sspec = pl.BlockSpec((SEQ_PER_STEP, GDN_V_HEADS, GDN_HEAD_DIM, GDN_HEAD_DIM), lambda g: (g, 0, 0, 0))
    o, s = pl.pallas_call(
        functools.partial(_gdn_step_kernel, lr=steps),
        grid=(nb // SEQ_PER_STEP,),
        in_specs=[_step_spec(steps, GDN_CONV_DIM), _step_spec(steps, LANES), rowspec, rowspec]
        + _const_specs(consts, 1) + [sspec],
        out_specs=[_step_spec(steps, GDN_V_DIM), sspec],
        out_shape=[
            jax.ShapeDtypeStruct((steps, nb, GDN_V_DIM), F32),
            jax.ShapeDtypeStruct(s0.shape, F32),
        ],
        scratch_shapes=[
            pltpu.VMEM((SEQ_INTERLEAVE, STEP_ROWS, GDN_CONV_DIM), F32),
            pltpu.VMEM((SEQ_INTERLEAVE, CHUNK, LANES), F32),
            pltpu.VMEM((SEQ_INTERLEAVE, STEP_ROWS, GDN_V_DIM), F32),
        ],
        compiler_params=_cparams("parallel"),
        name="gdn_step",
    )(act3, graw3, alog_row, dtb_row, *consts, s0)
    return o.reshape(steps * nb, GDN_V_DIM), s


SSM_PAIRS = SSM_HEADS // 2
PAIRS_PER_GROUP = SSM_PAIRS // SSM_GROUPS


def _ssm_gates(raw, alog_row, dtb_row, lr):
    row, lane = _iotas((CHUNK, LANES))
    dt = _softplus(raw + dtb_row)
    tile = jnp.where(lane < SSM_HEADS, dt, jnp.where(lane < 2 * SSM_HEADS, -jnp.exp(alog_row) * dt, 0.0))
    if lr < CHUNK:
        tile = jnp.where(row < lr, tile, 0.0)
    return tile


def _ssd_block(insts, dskip_ref, tril3_ref, upper3_ref, e3_ref, fill=_NO_FILL):
    np_ = SSM_PAIRS
    rows = insts[0][0].shape[0]
    row, lane = _iotas((rows, LANES))
    causal = row >= jnp.bitwise_and(lane, HALF - 1)
    r2, l2 = _iotas((LANES, LANES))
    bdmask = jnp.right_shift(r2, 6) == jnp.right_shift(l2, 6)
    top = r2 < HALF
    _, lane_p = _iotas((np_, LANES))
    sl = lambda a, i: a[:, i * LANES:(i + 1) * LANES]
    grp = lambda it: (it[0], it[1] // PAIRS_PER_GROUP)

    pre = []
    for _, tile, _, _, _ in insts:
        t1 = _pad_t(tile)
        cum_ext = _dot(_cat3(t1[2 * np_:4 * np_], 1), upper3_ref[...])
        cum_t, last_b = cum_ext[:, :LANES], cum_ext[:, LANES:]
        cum_rp = _pair_rows(cum_t, np_)
        dt_rp = _pair_rows(t1[0:2 * np_], np_)
        last_rp = jnp.where(lane_p < HALF, last_b[0:np_], last_b[np_:2 * np_])
        cum = _dot(tril3_ref[...], _cat3(tile, 0))
        pre.append(dict(cum_rp=cum_rp, dt_rp=dt_rp, coef_rp=jnp.exp(last_rp - cum_rp) * dt_rp,
                        elast=jnp.exp(last_b), col_all=_dot(_cat3(cum[0:rows], 1), e3_ref[...])))

    seqs = range(len(insts))
    items = [(i, p) for i in seqs for p in range(np_)]
    groups = [(i, g) for i in seqs for g in range(SSM_GROUPS)]
    act = lambda i: insts[i][0]
    prow = lambda name, it: pre[it[0]][name][it[1]:it[1] + 1]
    bg = {ig: _pad_rows(sl(act(ig[0]), SSM_D_INNER // LANES + ig[1])).astype(BF16) for ig in groups}
    cg = {ig: sl(act(ig[0]), (SSM_D_INNER + SSM_BC) // LANES + ig[1]).astype(BF16) for ig in groups}
    bb = {ig: jnp.concatenate([bg[ig], bg[ig]], axis=0) for ig in groups}
    cb2 = {ig: _dot_nt(cg[ig], bb[ig]) for ig in groups}
    fill.emit()
    xp = {it: sl(act(it[0]), it[1]) for it in items}
    colc = {it: sl(pre[it[0]]["col_all"], it[1]) for it in items}
    x2 = {it: jnp.concatenate([_pad_rows(xp[it]), _pad_rows(xp[it])], axis=0) for it in items}
    lm = {it: cb2[grp(it)] * jnp.exp(jnp.where(causal, colc[it] - prow("cum_rp", it), NEG_BIG)) * prow("dt_rp", it)
          for it in items}
    y_diag = {}
    for n, it in enumerate(items):
        y_diag[it] = _dot(lm[it].astype(BF16), jnp.where(bdmask, x2[it], 0.0).astype(BF16))
        if n % (4 * len(insts)) == 4 * len(insts) - 1:
            fill.emit()
    hp = {it: insts[it[0]][2][it[1]] for it in items}
    y_off = {}
    for n, it in enumerate(items):
        y_off[it] = _dot_nt(cg[grp(it)], hp[it].astype(BF16))
        if n % (4 * len(insts)) == 4 * len(insts) - 1:
            fill.emit()
    for it in items:
        i, p = it
        insts[i][4][:, p * LANES:(p + 1) * LANES] = (y_diag[it] + jnp.exp(colc[it]) * y_off[it]
                                                     + dskip_ref[:, p * LANES:(p + 1) * LANES] * xp[it])
    dh = {}
    for n, it in enumerate(items):
        lhs = jnp.where(bdmask, x2[it].T * prow("coef_rp", it), 0.0)
        dh[it] = _dot(lhs.astype(BF16), bb[grp(it)])
        if n % (4 * len(insts)) == 4 * len(insts) - 1:
            fill.emit()
    for it in items:
        i, p = it
        elast = pre[i]["elast"]
        e_rows = jnp.where(top, elast[p:p + 1], elast[np_ + p:np_ + p + 1])
        insts[i][3][p] = e_rows * hp[it] + dh[it]
    fill.flush()


def _ssd_seq_kernel(x0_ref, bc0_ref, xn_ref, bcn_ref, graw_ref, shift_ref, cw_ref, cb_ref, alog_ref, dtb_ref,
                    dskip_ref, tril3_ref, upper3_ref, e3_ref, y_ref, h_ref, ext_ref, act_ref):
    n = pl.program_id(1)
    conv = functools.partial(_conv_block, shift_ref=shift_ref, w_ref=cw_ref, b_ref=cb_ref, ext_ref=ext_ref,
                             l2_cols=0, q_cols=0, qscale=1.0)

    @pl.when(n == 0)
    def _():
        h_ref[...] = jnp.zeros(h_ref.shape, F32)
        _Filler(conv([x0_ref, bc0_ref], act_ref=act_ref.at[0], seq_start=True), 1).flush()

    @pl.when(n > 0)
    def _():
        act_ref[0] = act_ref[1]

    tile = _ssm_gates(graw_ref[...], alog_ref[...], dtb_ref[...], CHUNK)
    fill = _Filler(conv([xn_ref, bcn_ref], act_ref=act_ref.at[1], seq_start=False), SSM_PAIRS)
    _ssd_block([(act_ref.at[0], tile, h_ref, h_ref, y_ref)], dskip_ref, tril3_ref, upper3_ref, e3_ref, fill)


def _ssd_seq(pm, graw, conv_w, conv_b, alog_row, dtb_row, dskip_row, consts, nseq, seq_len):
    nc = seq_len // CHUNK
    last = nseq * nc - 1
    bc_blk = 2 * SSM_D_INNER // (2 * SSM_BC)
    nxt = lambda b, n: jnp.minimum(b * nc + n + 1, last)
    rowspec = pl.BlockSpec((1, LANES), lambda b, n: (0, 0))
    return pl.pallas_call(
        _ssd_seq_kernel,
        grid=(nseq, nc),
        in_specs=[
            pl.BlockSpec((CHUNK, SSM_D_INNER), lambda b, n: (b * nc, 0)),
            pl.BlockSpec((CHUNK, 2 * SSM_BC), lambda b, n: (b * nc, bc_blk)),
            pl.BlockSpec((CHUNK, SSM_D_INNER), lambda b, n: (nxt(b, n), 0)),
            pl.BlockSpec((CHUNK, 2 * SSM_BC), lambda b, n: (nxt(b, n), bc_blk)),
            pl.BlockSpec((CHUNK, LANES), lambda b, n: (b * nc + n, 0)),
            pl.BlockSpec(((CONV_WIDTH - 1) * CHUNK, BHALO + CHUNK), lambda b, n: (0, 0)),
            pl.BlockSpec((CONV_WIDTH, SSM_CONV_DIM), lambda b, n: (0, 0)),
            pl.BlockSpec((1, SSM_CONV_DIM), lambda b, n: (0, 0)),
            rowspec, rowspec,
            pl.BlockSpec((1, SSM_D_INNER), lambda b, n: (0, 0)),
        ] + _const_specs(consts, 2),
        out_specs=[
            pl.BlockSpec((CHUNK, SSM_D_INNER), lambda b, n: (b * nc + n, 0)),
            pl.BlockSpec((None, SSM_PAIRS, LANES, SSM_STATE), lambda b, n: (b, 0, 0, 0)),
        ],
        out_shape=[
            jax.ShapeDtypeStruct((nseq * seq_len, SSM_D_INNER), F32),
            jax.ShapeDtypeStruct((nseq, SSM_PAIRS, LANES, SSM_STATE), F32),
        ],
        scratch_shapes=[
            pltpu.VMEM((CHUNK + BHALO, SSM_CONV_DIM), BF16),
            pltpu.VMEM((2, CHUNK, SSM_CONV_DIM), F32),
        ],
        compiler_params=_cparams("parallel", "arbitrary"),
        name="ssd_seq",
    )(pm, pm, pm, pm, graw, _shift_matrix(), conv_w, conv_b.reshape(1, SSM_CONV_DIM), alog_row, dtb_row, dskip_row, *consts)


def _ssd_step_kernel(act_ref, graw_ref, alog_ref, dtb_ref, dskip_ref, tril3_ref, upper3_ref, e3_ref, h0_ref,
                     y_ref, h_ref, apad_ref, gpad_ref, ypad_ref, *, lr):
    def some_sequences(it, carry):
        insts = []
        for j in range(SEQ_INTERLEAVE):
            s = it * SEQ_INTERLEAVE + j
            _load_padded(act_ref, s, apad_ref.at[j], lr)
            _load_padded(graw_ref, s, gpad_ref.at[j], lr)
            tile = _ssm_gates(gpad_ref[j], alog_ref[...], dtb_ref[...], lr)
            insts.append((apad_ref.at[j], tile, h0_ref.at[s], h_ref.at[s], ypad_ref.at[j]))
        _ssd_block(insts, dskip_ref, tril3_ref, upper3_ref, e3_ref)
        for j in range(SEQ_INTERLEAVE):
            _store_tokens(ypad_ref.at[j], y_ref, it * SEQ_INTERLEAVE + j, lr)
        return carry

    lax.fori_loop(0, SEQ_PER_STEP // SEQ_INTERLEAVE, some_sequences, 0)


def _ssd_step(act3, graw3, alog_row, dtb_row, dskip_row, consts, h0):
    steps, nb, _ = act3.shape
    assert nb % SEQ_PER_STEP == 0 and steps <= STEP_ROWS
    rowspec = pl.BlockSpec((1, LANES), lambda b: (0, 0))
    hspec = pl.BlockSpec((SEQ_PER_STEP, SSM_PAIRS, LANES, SSM_STATE), lambda g: (g, 0, 0, 0))
    y, h = pl.pallas_call(
        functools.partial(_ssd_step_kernel, lr=steps),
        grid=(nb // SEQ_PER_STEP,),
        in_specs=[_step_spec(steps, SSM_CONV_DIM), _step_spec(steps, LANES), rowspec, rowspec,
                  pl.BlockSpec((1, SSM_D_INNER), lambda b: (0, 0))] + _const_specs(consts, 1) + [hspec],
        out_specs=[_step_spec(steps, SSM_D_INNER), hspec],
        out_shape=[
            jax.ShapeDtypeStruct((steps, nb, SSM_D_INNER), F32),
            jax.ShapeDtypeStruct(h0.shape, F32),
        ],
        scratch_shapes=[
            pltpu.VMEM((SEQ_INTERLEAVE, STEP_ROWS, SSM_CONV_DIM), F32),
            pltpu.VMEM((SEQ_INTERLEAVE, CHUNK, LANES), F32),
            pltpu.VMEM((SEQ_INTERLEAVE, STEP_ROWS, SSM_D_INNER), F32),
        ],
        compiler_params=_cparams("parallel"),
        name="ssd_step",
    )(act3, graw3, alog_row, dtb_row, dskip_row, *consts, h0)
    return y.reshape(steps * nb, SSM_D_INNER), h


FFN_TILE = FFN_HIDDEN // 2
GDN_PROJ_TILE = GDN_MAIN // 3
SSM_PROJ_TILE = SSM_MAIN // 2


def _lane_row(pieces):
    row = jnp.zeros((1, LANES), F32)
    for off, vec in pieces:
        row = row.at[0, off:off + vec.shape[0]].set(vec.astype(F32))
    return row


def _stage_params(w_mod, b_mod, norm_mix, norm_ffn, norm_final, gdn_w_in, gdn_conv_w, gdn_a_log, gdn_dt_bias,
                  gdn_norm, gdn_w_out, ssm_w_in, ssm_conv_w, ssm_conv_b, ssm_a_log, ssm_dt_bias, ssm_d, ssm_norm,
                  ssm_w_out, ffn_w_gate_up, ffn_w_down):
    perm_g = np.concatenate([np.arange(0, GDN_V_HEADS, 2), np.arange(1, GDN_V_HEADS, 2)])
    perm_s = np.concatenate([np.arange(0, SSM_HEADS, 2), np.arange(1, SSM_HEADS, 2)])
    g_in, s_in = gdn_w_in[0], ssm_w_in[0]
    beta_cols = g_in[:, GDN_MAIN:GDN_MAIN + GDN_V_HEADS][:, perm_g]
    a_cols = g_in[:, GDN_MAIN + GDN_V_HEADS:GDN_MAIN + 2 * GDN_V_HEADS][:, perm_g]
    gdn_small = jnp.concatenate([a_cols, beta_cols, jnp.zeros((D_MODEL, LANES - 2 * GDN_V_HEADS), F32)], axis=1)
    dt_cols = s_in[:, SSM_MAIN:SSM_MAIN + SSM_HEADS][:, perm_s]
    ssm_small = jnp.concatenate([dt_cols, dt_cols, jnp.zeros((D_MODEL, LANES - 2 * SSM_HEADS), F32)], axis=1)
    return dict(
        w_mod=w_mod, b_mod=b_mod, norm_mix=norm_mix, norm_ffn=norm_ffn, norm_final=norm_final,
        gdn_main=g_in[:, :GDN_MAIN].astype(BF16), gdn_small=gdn_small.astype(BF16),
        gdn_conv_w=gdn_conv_w[0], gdn_conv_b=jnp.zeros((GDN_CONV_DIM,), F32),
        gdn_alog_row=_lane_row([(0, gdn_a_log[0][perm_g])]), gdn_dtb_row=_lane_row([(0, gdn_dt_bias[0][perm_g])]),
        gdn_norm=jnp.tile(gdn_norm[0], GDN_V_HEADS), gdn_w_out=gdn_w_out[0].astype(BF16),
        gdn_consts=_recurrence_consts(GDN_PAIRS, 0, GDN_PAIRS),
        ssm_main=jnp.concatenate(
            [s_in[:, SSM_D_INNER:2 * SSM_D_INNER], s_in[:, :SSM_D_INNER], s_in[:, 2 * SSM_D_INNER:SSM_MAIN]],
            axis=1).astype(BF16),
        ssm_small=ssm_small.astype(BF16),
        ssm_conv_w=ssm_conv_w[0], ssm_conv_b=ssm_conv_b[0],
        ssm_alog_row=_lane_row([(SSM_HEADS, ssm_a_log[0][perm_s])]),
        ssm_dtb_row=_lane_row([(0, ssm_dt_bias[0][perm_s]), (SSM_HEADS, ssm_dt_bias[0][perm_s])]),
        ssm_dskip_row=jnp.repeat(ssm_d[0], SSM_HEAD_DIM).reshape(1, SSM_D_INNER),
        ssm_norm=ssm_norm[0], ssm_w_out=ssm_w_out[0].astype(BF16),
        ssm_consts=_recurrence_consts(SSM_PAIRS, SSM_HEADS, SSM_HEADS + SSM_PAIRS),
        wg=[ffn_w_gate_up[i][:, :FFN_HIDDEN].astype(BF16) for i in range(2)],
        wu=[ffn_w_gate_up[i][:, FFN_HIDDEN:].astype(BF16) for i in range(2)],
        wd=[ffn_w_down[i].astype(BF16) for i in range(2)],
    )


def _ffn(x, layer, mod3, rows_up, rows_down, p, final_w):
    act = _ffn_up(x, p["norm_ffn"][layer], mod3, rows_up, p["wg"][layer], p["wu"][layer], FFN_TILE)
    return _ffn_down(act, x, mod3, rows_down, p["wd"][layer], final_w)


QSCALE = GDN_HEAD_DIM ** -0.5


def _trunk_seq(x3, mod, p):
    nseq, seq_len, _ = x3.shape
    m = nseq * seq_len
    x = x3.reshape(m, D_MODEL)
    mod3 = [mod[l].reshape(nseq, 1, 6 * D_MODEL) for l in range(2)]
    rows_a = _Rows(m, min(1024, seq_len), seq_len, 1)
    rows_b = _Rows(m, min(512, seq_len), seq_len, 1)

    pm, ps = _in_proj(x, p["norm_mix"][0], mod3[0], rows_a, 1, 0, p["gdn_main"], p["gdn_small"], GDN_PROJ_TILE)
    o, gdn_s = _gdn_seq(pm, ps, p["gdn_conv_w"], p["gdn_conv_b"], p["gdn_alog_row"], p["gdn_dtb_row"],
                        p["gdn_consts"], nseq, seq_len)
    tail = pm.reshape(nseq, seq_len, GDN_MAIN)[:, seq_len - (CONV_WIDTH - 1):].astype(F32)
    gdn_c = tail[..., :GDN_CONV_DIM]
    x = _mixer_out(o, pm, 2, p["gdn_norm"], x, mod3[0], rows_b, p["gdn_w_out"], GDN_HEAD_DIM, False)
    x = _ffn(x, 0, mod3[0], rows_a, rows_b, p, None)

    pm, ps = _in_proj(x, p["norm_mix"][1], mod3[1], rows_a, 1, 0, p["ssm_main"], p["ssm_small"], SSM_PROJ_TILE)
    y, ssm_h = _ssd_seq(pm, ps, p["ssm_conv_w"], p["ssm_conv_b"], p["ssm_alog_row"], p["ssm_dtb_row"],
                        p["ssm_dskip_row"], p["ssm_consts"], nseq, seq_len)
    tail = pm.reshape(nseq, seq_len, SSM_MAIN)[:, seq_len - (CONV_WIDTH - 1):].astype(F32)
    ssm_c = jnp.concatenate([tail[..., :SSM_D_INNER], tail[..., 2 * SSM_D_INNER:]], axis=-1)
    x = _mixer_out(y, pm, 1, p["ssm_norm"], x, mod3[1], rows_b, p["ssm_w_out"], SSM_D_INNER // SSM_GROUPS, True)
    _, y_out = _ffn(x, 1, mod3[1], rows_a, rows_b, p, p["norm_final"])

    return (y_out.reshape(nseq, seq_len, D_MODEL), gdn_s[None], gdn_c[None],
            ssm_h.reshape(nseq, SSM_HEADS, SSM_HEAD_DIM, SSM_STATE)[None], ssm_c[None])


def _trunk_step(x3, mod, st_gdn, cv_gdn, st_ssm, cv_ssm, p):
    nb, steps, _ = x3.shape
    assert steps >= CONV_WIDTH - 1
    m = nb * steps
    x = jnp.transpose(x3, (1, 0, 2)).reshape(m, D_MODEL)
    mod3 = [mod[l].reshape(1, nb, 6 * D_MODEL) for l in range(2)]
    rows = _Rows(m, m, None, nb)
    tok = lambda a: jnp.transpose(a, (1, 0, 2))

    pm, ps = _in_proj(x, p["norm_mix"][0], mod3[0], rows, 1, 0, p["gdn_main"], p["gdn_small"], GDN_PROJ_TILE)
    u3 = pm.reshape(steps, nb, GDN_MAIN)
    act3 = _conv_steps(u3, 0, 0, GDN_CONV_DIM, tok(cv_gdn[0]), p["gdn_conv_w"], p["gdn_conv_b"], 2, QSCALE)
    o, gdn_s = _gdn_step(act3, ps.reshape(steps, nb, LANES), p["gdn_alog_row"], p["gdn_dtb_row"], p["gdn_consts"],
                         st_gdn[0])
    gdn_c = tok(u3[steps - (CONV_WIDTH - 1):, :, :GDN_CONV_DIM].astype(F32))
    x = _mixer_out(o, pm, 2, p["gdn_norm"], x, mod3[0], rows, p["gdn_w_out"], GDN_HEAD_DIM, False)
    x = _ffn(x, 0, mod3[0], rows, rows, p, None)

    pm, ps = _in_proj(x, p["norm_mix"][1], mod3[1], rows, 1, 0, p["ssm_main"], p["ssm_small"], SSM_PROJ_TILE)
    u3 = pm.reshape(steps, nb, SSM_MAIN)
    act3 = _conv_steps(u3, SSM_D_INNER // CONV_COLS, SSM_D_INNER // CONV_COLS, SSM_CONV_DIM, tok(cv_ssm[0]),
                       p["ssm_conv_w"], p["ssm_conv_b"], 0, 1.0)
    h0 = st_ssm[0].reshape(nb, SSM_PAIRS, LANES, SSM_STATE)
    y, ssm_h = _ssd_step(act3, ps.reshape(steps, nb, LANES), p["ssm_alog_row"], p["ssm_dtb_row"],
                         p["ssm_dskip_row"], p["ssm_consts"], h0)
    tail = u3[steps - (CONV_WIDTH - 1):].astype(F32)
    ssm_c = tok(jnp.concatenate([tail[..., :SSM_D_INNER], tail[..., 2 * SSM_D_INNER:]], axis=-1))
    x = _mixer_out(y, pm, 1, p["ssm_norm"], x, mod3[1], rows, p["ssm_w_out"], SSM_D_INNER // SSM_GROUPS, True)
    _, y_out = _ffn(x, 1, mod3[1], rows, rows, p, p["norm_final"])

    return (tok(y_out.reshape(steps, nb, D_MODEL)), gdn_s[None], gdn_c[None],
            ssm_h.reshape(nb, SSM_HEADS, SSM_HEAD_DIM, SSM_STATE)[None], ssm_c[None])


def kernel(x_prompt, x_sample, c_prompt, c_sample, state_gdn, state_gdn_conv, state_ssm, state_ssm_conv, w_mod, b_mod,
           norm_mix, norm_ffn, norm_final, gdn_w_in, gdn_conv_w, gdn_a_log, gdn_dt_bias, gdn_norm, gdn_w_out, ssm_w_in,
           ssm_conv_w, ssm_conv_b, ssm_a_log, ssm_dt_bias, ssm_d, ssm_norm, ssm_w_out, ffn_w_gate_up, ffn_w_down):
    p = _stage_params(w_mod, b_mod, norm_mix, norm_ffn, norm_final, gdn_w_in, gdn_conv_w, gdn_a_log, gdn_dt_bias,
                      gdn_norm, gdn_w_out, ssm_w_in, ssm_conv_w, ssm_conv_b, ssm_a_log, ssm_dt_bias, ssm_d, ssm_norm,
                      ssm_w_out, ffn_w_gate_up, ffn_w_down)
    n_prompt = x_prompt.shape[0]
    mod = _modulation(jnp.concatenate([c_prompt, c_sample], axis=0), p["w_mod"], p["b_mod"])
    y_p, gs_p, gc_p, ss_p, sc_p = _trunk_seq(x_prompt, mod[:, :n_prompt], p)
    y_s, gs_s, gc_s, ss_s, sc_s = _trunk_step(x_sample, mod[:, n_prompt:], state_gdn, state_gdn_conv, state_ssm,
                                              state_ssm_conv, p)
    return (y_p, y_s, gs_p, gc_p, ss_p, sc_p, gs_s, gc_s, ss_s, sc_s)
```

```python
import functools

import numpy as np
import jax
import jax.numpy as jnp
from jax import lax
from jax.experimental import pallas as pl
from jax.experimental.pallas import tpu as pltpu

F32 = jnp.float32
BF16 = jnp.bfloat16

D_MODEL = 1024
EPS = 1e-6
CONV_WIDTH = 4
CHUNK = 64
LANES = 128
HALF = LANES // 2

GDN_QK_HEADS = 8
GDN_V_HEADS = 16
GDN_HEAD_DIM = 128
GDN_QK_DIM = GDN_QK_HEADS * GDN_HEAD_DIM
GDN_V_DIM = GDN_V_HEADS * GDN_HEAD_DIM
GDN_CONV_DIM = 2 * GDN_QK_DIM + GDN_V_DIM
GDN_MAIN = GDN_CONV_DIM + GDN_V_DIM

SSM_D_INNER = 2 * D_MODEL
SSM_HEAD_DIM = 64
SSM_HEADS = SSM_D_INNER // SSM_HEAD_DIM
SSM_GROUPS = 4
SSM_STATE = 128
SSM_BC = SSM_GROUPS * SSM_STATE
SSM_CONV_DIM = SSM_D_INNER + 2 * SSM_BC
SSM_MAIN = SSM_D_INNER + SSM_CONV_DIM

FFN_HIDDEN = 2816

VMEM_LIMIT = 56 * 1024 * 1024
NEG_BIG = -1e30


def _cparams(*sem):
    return pltpu.CompilerParams(dimension_semantics=sem, vmem_limit_bytes=VMEM_LIMIT)


def _silu(x):
    hx = 0.5 * x
    return hx + hx * jnp.tanh(hx)


def _softplus(x):
    return jnp.maximum(x, 0.0) + jnp.log1p(jnp.exp(-jnp.abs(x)))


def _dot(a, b):
    return jnp.dot(a, b, preferred_element_type=F32)


def _dot_nt(a, b):
    return lax.dot_general(a, b, (((1,), (1,)), ((), ())), preferred_element_type=F32)


def _tile_rows(v, rep):
    return v if rep == 1 else jnp.concatenate([v] * rep, axis=0)


def _mod_kernel(c_ref, w_ref, b_ref, o_ref):
    cs = _silu(c_ref[...]).astype(BF16)
    o_ref[...] = _dot(cs, w_ref[...].astype(BF16)) + b_ref[...]


def _modulation(c, w_mod, b_mod):
    depth, _, n = w_mod.shape
    bc = c.shape[0]
    tn = 1536
    return pl.pallas_call(
        _mod_kernel,
        grid=(depth, n // tn),
        in_specs=[
            pl.BlockSpec((bc, D_MODEL), lambda l, j: (0, 0)),
            pl.BlockSpec((None, D_MODEL, tn), lambda l, j: (l, 0, j)),
            pl.BlockSpec((None, 1, tn), lambda l, j: (l, 0, j)),
        ],
        out_specs=pl.BlockSpec((None, bc, tn), lambda l, j: (l, 0, j)),
        out_shape=jax.ShapeDtypeStruct((depth, bc, n), F32),
        compiler_params=_cparams("parallel", "parallel"),
        name="adaln_mod",
    )(c, w_mod, b_mod.reshape(depth, 1, n))


def _norm_mod(x, nw, sc, sh, rep):
    y = x * lax.rsqrt(jnp.mean(x * x, axis=-1, keepdims=True) + EPS) * nw
    return y * (1.0 + _tile_rows(sc, rep)) + _tile_rows(sh, rep)


def _in_proj_kernel(x_ref, nw_ref, sc_ref, sh_ref, w_ref, w2_ref, o_ref, o2_ref, h_ref, *, rep):
    @pl.when(pl.program_id(1) == 0)
    def _():
        h = _norm_mod(x_ref[...], nw_ref[...], sc_ref[...], sh_ref[...], rep).astype(BF16)
        h_ref[...] = h
        o2_ref[...] = _dot(h, w2_ref[...])

    o_ref[...] = _dot(h_ref[...], w_ref[...]).astype(o_ref.dtype)


def _ffn_up_kernel(x_ref, nw_ref, sc_ref, sh_ref, wg_ref, wu_ref, o_ref, h_ref, *, rep):
    @pl.when(pl.program_id(1) == 0)
    def _():
        h_ref[...] = _norm_mod(x_ref[...], nw_ref[...], sc_ref[...], sh_ref[...], rep).astype(BF16)

    h = h_ref[...]
    o_ref[...] = (_silu(_dot(h, wg_ref[...])) * _dot(h, wu_ref[...])).astype(BF16)


class _Rows:
    def __init__(self, m, tm, group_rows, mod_rows):
        assert m % tm == 0
        self.m, self.tm = m, tm
        if mod_rows == 1:
            assert group_rows % tm == 0
            self.rep = 1
            self.gmap = lambda i: (i * tm) // group_rows
        else:
            assert tm % mod_rows == 0
            self.rep = tm // mod_rows
            self.gmap = lambda i: 0
        self.mod_rows = mod_rows

    def mod_spec(self, col_block, with_j):
        if with_j:
            return pl.BlockSpec((None, self.mod_rows, D_MODEL), lambda i, j: (self.gmap(i), 0, col_block))
        return pl.BlockSpec((None, self.mod_rows, D_MODEL), lambda i: (self.gmap(i), 0, col_block))


def _in_proj(x, nw, mod3, rows, sc_blk, sh_blk, w, w2, tn):
    m, tm = rows.m, rows.tm
    n = w.shape[1]
    assert n % tn == 0
    return pl.pallas_call(
        functools.partial(_in_proj_kernel, rep=rows.rep),
        grid=(m // tm, n // tn),
        in_specs=[
            pl.BlockSpec((tm, D_MODEL), lambda i, j: (i, 0)),
            pl.BlockSpec((1, D_MODEL), lambda i, j: (0, 0)),
            rows.mod_spec(sc_blk, True),
            rows.mod_spec(sh_blk, True),
            pl.BlockSpec((D_MODEL, tn), lambda i, j: (0, j)),
            pl.BlockSpec((D_MODEL, LANES), lambda i, j: (0, 0)),
        ],
        out_specs=[
            pl.BlockSpec((tm, tn), lambda i, j: (i, j)),
            pl.BlockSpec((tm, LANES), lambda i, j: (i, 0)),
        ],
        out_shape=[jax.ShapeDtypeStruct((m, n), BF16), jax.ShapeDtypeStruct((m, LANES), F32)],
        scratch_shapes=[pltpu.VMEM((tm, D_MODEL), BF16)],
        compiler_params=_cparams("parallel", "arbitrary"),
        name="in_proj",
    )(x, nw.reshape(1, D_MODEL), mod3, mod3, w, w2)


def _ffn_up(x, nw, mod3, rows, wg, wu, th):
    m, tm = rows.m, rows.tm
    assert FFN_HIDDEN % th == 0
    return pl.pallas_call(
        functools.partial(_ffn_up_kernel, rep=rows.rep),
        grid=(m // tm, FFN_HIDDEN // th),
        in_specs=[
            pl.BlockSpec((tm, D_MODEL), lambda i, j: (i, 0)),
            pl.BlockSpec((1, D_MODEL), lambda i, j: (0, 0)),
            rows.mod_spec(4, True),
            rows.mod_spec(3, True),
            pl.BlockSpec((D_MODEL, th), lambda i, j: (0, j)),
            pl.BlockSpec((D_MODEL, th), lambda i, j: (0, j)),
        ],
        out_specs=pl.BlockSpec((tm, th), lambda i, j: (i, j)),
        out_shape=jax.ShapeDtypeStruct((m, FFN_HIDDEN), BF16),
        scratch_shapes=[pltpu.VMEM((tm, D_MODEL), BF16)],
        compiler_params=_cparams("parallel", "arbitrary"),
        name="ffn_up",
    )(x, nw.reshape(1, D_MODEL), mod3, mod3, wg, wu)


def _resid_store(acc, x_ref, gt_ref, o_ref, fnw_ref, y_ref, rep):
    xn = x_ref[...] + _tile_rows(gt_ref[...], rep) * acc
    o_ref[...] = xn
    if y_ref is not None:
        y_ref[...] = xn * lax.rsqrt(jnp.mean(xn * xn, axis=-1, keepdims=True) + EPS) * fnw_ref[...]


def _ffn_down_kernel(a_ref, x_ref, gt_ref, w_ref, *rest, rep, final):
    if final:
        fnw_ref, o_ref, y_ref = rest
    else:
        (o_ref,), fnw_ref, y_ref = rest, None, None
    _resid_store(_dot(a_ref[...], w_ref[...]), x_ref, gt_ref, o_ref, fnw_ref, y_ref, rep)


def _ffn_down(act, x, mod3, rows, w, fnw):
    m, tm = rows.m, rows.tm
    final = fnw is not None
    in_specs = [
        pl.BlockSpec((tm, FFN_HIDDEN), lambda i: (i, 0)),
        pl.BlockSpec((tm, D_MODEL), lambda i: (i, 0)),
        rows.mod_spec(5, False),
        pl.BlockSpec((FFN_HIDDEN, D_MODEL), lambda i: (0, 0)),
    ]
    args = [act, x, mod3, w]
    row_spec = pl.BlockSpec((tm, D_MODEL), lambda i: (i, 0))
    out_shape = jax.ShapeDtypeStruct((m, D_MODEL), F32)
    if final:
        in_specs.append(pl.BlockSpec((1, D_MODEL), lambda i: (0, 0)))
        args.append(fnw.reshape(1, D_MODEL))
        out_specs, out_shapes = [row_spec, row_spec], [out_shape, out_shape]
    else:
        out_specs, out_shapes = row_spec, out_shape
    return pl.pallas_call(
        functools.partial(_ffn_down_kernel, rep=rows.rep, final=final),
        grid=(m // tm,),
        in_specs=in_specs,
        out_specs=out_specs,
        out_shape=out_shapes,
        compiler_params=_cparams("parallel"),
        name="ffn_down",
    )(*args)


def _mixer_out_kernel(y_ref, z_ref, nw_ref, x_ref, gt_ref, w_ref, o_ref, a_ref, *, rep, group, gate_first):
    width = y_ref.shape[1]
    for s in range(0, width, group):
        y = y_ref[:, s:s + group]
        gate = _silu(z_ref[:, s:s + group].astype(F32))
        if gate_first:
            y = y * gate
        y = y * lax.rsqrt(jnp.mean(y * y, axis=-1, keepdims=True) + EPS) * nw_ref[:, s:s + group]
        if not gate_first:
            y = y * gate
        a_ref[:, s:s + group] = y.astype(BF16)
    _resid_store(_dot(a_ref[...], w_ref[...]), x_ref, gt_ref, o_ref, None, None, rep)


def _mixer_out(y, zsrc, z_blk, nw_full, x, mod3, rows, w, group, gate_first):
    m, tm = rows.m, rows.tm
    width = y.shape[1]
    return pl.pallas_call(
        functools.partial(_mixer_out_kernel, rep=rows.rep, group=group, gate_first=gate_first),
        grid=(m // tm,),
        in_specs=[
            pl.BlockSpec((tm, width), lambda i: (i, 0)),
            pl.BlockSpec((tm, width), lambda i: (i, z_blk)),
            pl.BlockSpec((1, width), lambda i: (0, 0)),
            pl.BlockSpec((tm, D_MODEL), lambda i: (i, 0)),
            rows.mod_spec(2, False),
            pl.BlockSpec((width, D_MODEL), lambda i: (0, 0)),
        ],
        out_specs=pl.BlockSpec((tm, D_MODEL), lambda i: (i, 0)),
        out_shape=jax.ShapeDtypeStruct((m, D_MODEL), F32),
        scratch_shapes=[pltpu.VMEM((tm, width), BF16)],
        compiler_params=_cparams("parallel"),
        name="mixer_out",
    )(y, zsrc, nw_full.reshape(1, width), x, mod3, w)


CONV_COLS = 1024


def _post_conv(acc, o_ref, cb, n_l2, qscale):
    y = _silu(acc)
    if n_l2 == 0:
        o_ref[...] = y
        return

    @pl.when(cb < n_l2)
    def _():
        scale = jnp.where(cb == 0, qscale, 1.0).astype(F32)
        for s in range(0, CONV_COLS, GDN_HEAD_DIM):
            yh = y[:, s:s + GDN_HEAD_DIM]
            o_ref[:, s:s + GDN_HEAD_DIM] = yh * lax.rsqrt(jnp.sum(yh * yh, axis=-1, keepdims=True) + EPS) * scale

    @pl.when(cb >= n_l2)
    def _():
        o_ref[...] = y


BHALO = 16
CONV_GROUP = 2 * LANES


def _shift_matrix():
    s = np.zeros(((CONV_WIDTH - 1) * CHUNK, BHALO + CHUNK), np.float32)
    for tap in range(CONV_WIDTH - 1):
        for r in range(CHUNK):
            s[tap * CHUNK + r, BHALO - (CONV_WIDTH - 1) + tap + r] = 1.0
    return jnp.asarray(s, BF16)


def _conv_block(raw_refs, shift_ref, w_ref, b_ref, ext_ref, act_ref, seq_start, l2_cols, q_cols, qscale):
    width = ext_ref.shape[1]
    if seq_start:
        ext_ref[0:BHALO, :] = jnp.zeros((BHALO, width), BF16)
    else:
        ext_ref[0:BHALO, :] = ext_ref[CHUNK:CHUNK + BHALO, :]

    off = 0
    for ref in raw_refs:
        ext_ref[BHALO:BHALO + CHUNK, off:off + ref.shape[1]] = ref[...]
        off += ref.shape[1]

    def lane_group(s):
        cols = slice(s, s + CONV_GROUP)
        sh = _dot(shift_ref[...], ext_ref[:, cols])
        acc = b_ref[:, cols] + w_ref[CONV_WIDTH - 1:CONV_WIDTH, cols] * ext_ref[BHALO:BHALO + CHUNK, cols].astype(F32)
        for tap in range(CONV_WIDTH - 1):
            acc = acc + w_ref[tap:tap + 1, cols] * sh[tap * CHUNK:(tap + 1) * CHUNK]
        y = _silu(acc)
        for h in range(s, s + CONV_GROUP, LANES):
            yh = y[:, h - s:h - s + LANES]
            if h < l2_cols:
                yh = yh * lax.rsqrt(jnp.sum(yh * yh, axis=-1, keepdims=True) + EPS)
                if h < q_cols:
                    yh = yh * qscale
            act_ref[:, h:h + LANES] = yh

    return [functools.partial(lane_group, s) for s in range(0, width, CONV_GROUP)]


class _Filler:
    def __init__(self, tasks, shares):
        self.tasks, self.per = list(tasks), -(-len(tasks) // shares)

    def emit(self):
        for task in self.tasks[:self.per]:
            task()
        self.tasks = self.tasks[self.per:]

    def flush(self):
        for task in self.tasks:
            task()
        self.tasks = []


_NO_FILL = _Filler([], 1)


def _conv_steps_kernel(u_ref, hist_ref, w_ref, b_ref, o_ref, *, steps, n_l2, qscale):
    cb = pl.program_id(0)
    ext = [hist_ref[i] for i in range(CONV_WIDTH - 1)] + [u_ref[i].astype(F32) for i in range(steps)]
    for t in range(steps):
        acc = b_ref[...] + w_ref[0:1, :] * ext[t]
        for tap in range(1, CONV_WIDTH):
            acc = acc + w_ref[tap:tap + 1, :] * ext[t + tap]
        _post_conv(acc, o_ref.at[t], cb, n_l2, qscale)


def _conv_steps(u3, skip_at, skip, n_cols, hist3, conv_w, conv_b, n_l2, qscale):
    steps, nb, _ = u3.shape
    return pl.pallas_call(
        functools.partial(_conv_steps_kernel, steps=steps, n_l2=n_l2, qscale=qscale),
        grid=(n_cols // CONV_COLS,),
        in_specs=[
            pl.BlockSpec((steps, nb, CONV_COLS), lambda c: (0, 0, c + skip * (c >= skip_at))),
            pl.BlockSpec((CONV_WIDTH - 1, nb, CONV_COLS), lambda c: (0, 0, c)),
            pl.BlockSpec((CONV_WIDTH, CONV_COLS), lambda c: (0, c)),
            pl.BlockSpec((1, CONV_COLS), lambda c: (0, c)),
        ],
        out_specs=pl.BlockSpec((steps, nb, CONV_COLS), lambda c: (0, 0, c)),
        out_shape=jax.ShapeDtypeStruct((steps, nb, n_cols), F32),
        compiler_params=_cparams("parallel"),
        name="conv_steps",
    )(u3, hist3, conv_w, conv_b.reshape(1, n_cols))


def _split3(x):
    hi = x.astype(BF16)
    r = x - hi.astype(F32)
    mid = r.astype(BF16)
    lo = (r - mid.astype(F32)).astype(BF16)
    return hi, mid, lo


def _cat3(x, axis):
    return jnp.concatenate(_split3(x), axis=axis)


def _pad_t(tile):
    return jnp.concatenate([tile, jnp.zeros_like(tile)], axis=0).T


def _pair_rows(t, n):
    return t[0:n] + pltpu.roll(t[n:2 * n], HALF, 1)


def _iotas(shape):
    return lax.broadcasted_iota(jnp.int32, shape, 0), lax.broadcasted_iota(jnp.int32, shape, 1)


def _pad_rows(a):
    if a.shape[0] == CHUNK:
        return a
    return jnp.concatenate([a, jnp.zeros((CHUNK - a.shape[0], a.shape[1]), a.dtype)], axis=0)


def _split2(x):
    hi = x.astype(BF16)
    return hi, (x - hi.astype(F32)).astype(BF16)


def _block_diag(pair_bf, bd_ones):
    pair_bf = _pad_rows(pair_bf)
    return jnp.concatenate([pair_bf, pair_bf], axis=0) * bd_ones


def _mm3(lhs_parts, rhs_hi, rhs_lo):
    lh, ll = lhs_parts
    return _dot(jnp.concatenate([lh, lh, ll], axis=1), jnp.concatenate([rhs_hi, rhs_lo, rhs_hi], axis=0))


def _mm_pairs(lhs_parts, rhs_parts, bd_fn, terms=3):
    if terms == 1:
        return [_dot(lh, bd_fn(rh)) for (lh, _), (rh, _) in zip(lhs_parts, rhs_parts)]
    return [_mm3(lp, bd_fn(rh), bd_fn(rl)) for lp, (rh, rl) in zip(lhs_parts, rhs_parts)]


def _mm_hl(lhs, rhs_bf):
    return _dot(lhs.astype(BF16), rhs_bf)


def _unit_lower_inverse(xs, levels, bd_fn, eye, fill, terms=3):
    ps = [eye + x for x in xs]
    if levels <= 1:
        return ps
    rows = xs[0].shape[0]
    stack = lambda a, b: tuple(jnp.concatenate([s, t], axis=0) for s, t in zip(a, b))
    ysp = [_split2(x) for x in xs]
    ys = _mm_pairs(ysp, ysp, bd_fn, terms)
    fill.emit()
    for _ in range(levels - 2):
        ysp = [_split2(y) for y in ys]
        rs = _mm_pairs([stack(yp, _split2(p)) for yp, p in zip(ysp, ps)], ysp, bd_fn, terms)
        fill.emit()
        ys = [r[0:rows] for r in rs]
        ps = [p + r[rows:2 * rows] for p, r in zip(ps, rs)]
    last = _mm_pairs([_split2(p) for p in ps], [_split2(y) for y in ys], bd_fn, terms)
    return [p + t for p, t in zip(ps, last)]


QUARTER = HALF // 2


def _unit_lower_inverse_blocked(xs, fill):
    row, lane = _iotas((QUARTER, LANES))
    r2, l2 = _iotas((LANES, LANES))
    even = jnp.bitwise_and(jnp.right_shift(lane, 5), 1) == 0
    eye_q = jnp.where(row == jnp.bitwise_and(lane, QUARTER - 1), 1.0, 0.0)
    bd4_ones = jnp.where(jnp.right_shift(r2, 5) == jnp.right_shift(l2, 5), 1.0, 0.0).astype(BF16)
    row_q = jnp.right_shift(r2, 5)
    place = jnp.where(((row_q == 1) & (jnp.right_shift(l2, 5) == 0)) | ((row_q == 3) & (jnp.right_shift(l2, 5) == 2)),
                      1.0, 0.0).astype(BF16)
    bd4 = lambda a: jnp.concatenate([a] * 4, axis=0) * bd4_ones

    tops = [x[0:QUARTER] for x in xs]
    bots = [x[QUARTER:CHUNK] for x in xs]
    diag = [jnp.where(even, t, b) for t, b in zip(tops, bots)]
    dinv = _unit_lower_inverse(diag, _levels(QUARTER), bd4, eye_q, fill)
    fill.emit()
    ai = [jnp.where(even, d, 0.0) for d in dinv]
    ci = [jnp.where(even, 0.0, d) for d in dinv]
    b_ai = _mm_pairs([_split2(jnp.where(even, b, 0.0)) for b in bots], [_split2(a) for a in ai], bd4)
    fill.emit()
    low = [_mm3(_split2(c), *[jnp.concatenate([part] * 4, axis=0) * place for part in _split2(e)])
           for c, e in zip(ci, b_ai)]
    return [jnp.concatenate([a, l + c], axis=0) for a, l, c in zip(ai, low, ci)]


def _two_blocks(a, b):
    a_bf, b_bf = _pad_rows(a).astype(BF16), _pad_rows(b).astype(BF16)
    z = jnp.zeros_like(a_bf)
    return jnp.concatenate([jnp.concatenate([a_bf, z], axis=1), jnp.concatenate([z, b_bf], axis=1)], axis=0)


def _recurrence_consts(n_pairs, chan_even0, chan_odd0):
    tril = np.tril(np.ones((CHUNK, CHUNK), np.float32))
    tril3 = np.concatenate([tril] * 3, axis=1)
    upper = np.zeros((LANES, 2 * LANES), np.float32)
    upper[:CHUNK, :CHUNK] = tril.T
    upper[:CHUNK, LANES:] = 1.0
    upper3 = np.concatenate([upper] * 3, axis=0)
    e = np.zeros((LANES, n_pairs * LANES), np.float32)
    for p in range(n_pairs):
        e[chan_even0 + p, p * LANES:p * LANES + HALF] = 1.0
        e[chan_odd0 + p, p * LANES + HALF:(p + 1) * LANES] = 1.0
    e3 = np.concatenate([e] * 3, axis=0)
    return jnp.asarray(tril3, BF16), jnp.asarray(upper3, BF16), jnp.asarray(e3, BF16)


def _levels(lr):
    return max(1, int(np.ceil(np.log2(lr))))


STEP_ROWS = 16
DECODE_TOKENS_SINGLE_TERM = 4


GDN_PAIRS = GDN_V_HEADS // 2
GDN_PAIR_BATCH = 8
GDN_FILL_SHARES = 12 * (GDN_PAIRS // GDN_PAIR_BATCH)


def _gdn_gates(raw, alog_row, dtb_row, lr):
    row, lane = _iotas((CHUNK, LANES))
    g = -jnp.exp(alog_row) * _softplus(raw + dtb_row)
    beta = jax.nn.sigmoid(raw)
    gt = jnp.where(lane < GDN_V_HEADS, g, jnp.where(lane < 2 * GDN_V_HEADS, beta, 0.0))
    if lr < CHUNK:
        gt = jnp.where(row < lr, gt, 0.0)
    return gt


def _gdn_block(insts, tril3_ref, upper3_ref, e3_ref, lr, fill=_NO_FILL):
    np_ = GDN_PAIRS
    rows = insts[0][0].shape[0]
    row, lane = _iotas((rows, LANES))
    jl = jnp.bitwise_and(lane, HALF - 1)
    left = lane < HALF
    causal = row >= jl
    strict = row > jl
    eye2 = jnp.where(row == jl, 1.0, 0.0)
    r2, l2 = _iotas((LANES, LANES))
    bd_ones = jnp.where(jnp.right_shift(r2, 6) == jnp.right_shift(l2, 6), 1.0, 0.0).astype(BF16)
    _, lane_p = _iotas((np_, LANES))
    sl = lambda a, i: a[:, i * LANES:(i + 1) * LANES]

    pre = []
    for _, gt, _, _, _ in insts:
        t1 = _pad_t(gt)
        cum_ext = _dot(_cat3(t1[0:2 * np_], 1), upper3_ref[...])
        cum_t, last_b = cum_ext[:, :LANES], cum_ext[:, LANES:]
        cum_rp = _pair_rows(cum_t, np_)
        beta_rp = _pair_rows(t1[2 * np_:4 * np_], np_)
        last_rp = jnp.where(lane_p < HALF, last_b[0:np_], last_b[np_:2 * np_])
        cum = _dot(tril3_ref[...], _cat3(gt, 0))
        pre.append(dict(cum_rp=cum_rp, beta_rp=beta_rp, ecum_rp=jnp.exp(cum_rp),
                        kdec_rp=jnp.exp(last_rp - cum_rp) * beta_rp, elast=jnp.exp(last_b),
                        col_all=_dot(_cat3(cum[0:rows], 1), e3_ref[...])))

    def run(items):
        heads = [(i, 2 * p + hh) for i, p in items for hh in range(2)]
        act = lambda i: insts[i][0]
        prow = lambda name, it: pre[it[0]][name][it[1]:it[1] + 1]
        q = {it: sl(act(it[0]), it[1]) for it in items}
        k = {it: _pad_rows(sl(act(it[0]), GDN_QK_HEADS + it[1])) for it in items}
        v = {ih: sl(act(ih[0]), 2 * GDN_QK_HEADS + ih[1]) for ih in heads}
        kb = {it: k[it].astype(BF16) for it in items}
        qb = {it: q[it].astype(BF16) for it in items}
        gq = {it: _dot_nt(jnp.concatenate([kb[it][0:rows], qb[it]], axis=0), jnp.concatenate([kb[it], kb[it]], axis=0))
              for it in items}
        fill.emit()
        colc = {it: sl(pre[it[0]]["col_all"], it[1]) for it in items}
        base = {it: jnp.exp(jnp.where(causal, colc[it] - prow("cum_rp", it), NEG_BIG)) * prow("beta_rp", it)
                for it in items}
        x = [jnp.where(strict, -(gq[it][0:rows] * base[it]), 0.0) for it in items]
        qkd = {it: (gq[it][rows:2 * rows] * base[it]).astype(BF16) for it in items}
        if lr == CHUNK:
            minv = dict(zip(items, _unit_lower_inverse_blocked(x, fill)))
        else:
            minv = dict(zip(items, _unit_lower_inverse(x, _levels(lr), lambda a: _block_diag(a, bd_ones), eye2, fill,
                                                       terms=1 if lr <= DECODE_TOKENS_SINGLE_TERM else 3)))
        fill.emit()
        pair_of = lambda ih: (ih[0], ih[1] // 2)
        u = {it: jnp.concatenate([v[(it[0], 2 * it[1])], v[(it[0], 2 * it[1] + 1)]], axis=1)
             + _mm_hl(minv[it] - eye2, _two_blocks(v[(it[0], 2 * it[1])], v[(it[0], 2 * it[1] + 1)])) for it in items}
        fill.emit()
        w = {it: _mm_hl(minv[it] * prow("ecum_rp", it), _two_blocks(kb[it], kb[it])) for it in items}
        fill.emit()
        s_old = {ih: insts[ih[0]][2][ih[1]] for ih in heads}
        r = {ih: _dot(jnp.concatenate([sl(w[pair_of(ih)], ih[1] % 2).astype(BF16), qb[pair_of(ih)]], axis=0),
                      s_old[ih].astype(BF16)) for ih in heads}
        fill.emit()
        delta = {ih: sl(u[pair_of(ih)], ih[1] % 2) - r[ih][0:rows] for ih in heads}
        bd_delta = {it: _two_blocks(delta[(it[0], 2 * it[1])], delta[(it[0], 2 * it[1] + 1)]) for it in items}
        fill.emit()
        od = {it: _dot(jnp.concatenate(
            [qkd[it], (jnp.concatenate([k[it], k[it]], axis=0).T * prow("kdec_rp", it)).astype(BF16)], axis=0),
            bd_delta[it]) for it in items}
        for it in items:
            i, p = it
            ecol = jnp.exp(colc[it])
            ecol_r = pltpu.roll(ecol, HALF, 1)
            efull = (jnp.where(left, ecol, ecol_r), jnp.where(left, ecol_r, ecol))
            for hh in range(2):
                h = 2 * p + hh
                insts[i][4][:, h * LANES:(h + 1) * LANES] = (efull[hh] * r[(i, h)][rows:2 * rows]
                                                             + sl(od[it][0:rows], hh))
                e_h = pre[i]["elast"][hh * np_ + p:hh * np_ + p + 1]
                insts[i][3][h] = e_h * s_old[(i, h)] + sl(od[it][rows:rows + LANES], hh)

    for b0 in range(0, np_, GDN_PAIR_BATCH):
        run([(i, p) for i in range(len(insts)) for p in range(b0, b0 + GDN_PAIR_BATCH)])
    fill.flush()


def _gdn_seq_kernel(raw0_ref, rawn_ref, graw_ref, shift_ref, cw_ref, cb_ref, alog_ref, dtb_ref, tril3_ref, upper3_ref,
                    e3_ref, o_ref, s_ref, ext_ref, act_ref):
    n = pl.program_id(1)
    seqs = range(SEQ_TOGETHER)
    conv = functools.partial(_conv_block, shift_ref=shift_ref, w_ref=cw_ref, b_ref=cb_ref,
                             l2_cols=2 * GDN_QK_DIM, q_cols=GDN_QK_DIM, qscale=QSCALE)

    @pl.when(n == 0)
    def _():
        s_ref[...] = jnp.zeros(s_ref.shape, F32)
        for j in seqs:
            _Filler(conv([raw0_ref.at[j]], ext_ref=ext_ref.at[j], act_ref=act_ref.at[j, 0], seq_start=True), 1).flush()

    @pl.when(n > 0)
    def _():
        for j in seqs:
            act_ref[j, 0] = act_ref[j, 1]

    tasks, insts = [], []
    for j in seqs:
        gt = _gdn_gates(graw_ref[j], alog_ref[...], dtb_ref[...], CHUNK)
        tasks += conv([rawn_ref.at[j]], ext_ref=ext_ref.at[j], act_ref=act_ref.at[j, 1], seq_start=False)
        insts.append((act_ref.at[j, 0], gt, s_ref.at[j], s_ref.at[j], o_ref.at[j]))
    _gdn_block(insts, tril3_ref, upper3_ref, e3_ref, CHUNK, _Filler(tasks, GDN_FILL_SHARES))


def _const_specs(consts, ngrid):
    zero = (lambda *_: (0, 0))
    return [pl.BlockSpec(c.shape, zero) for c in consts]


SEQ_TOGETHER = 2


def _seq_views(nseq, seq_len):
    assert nseq % SEQ_TOGETHER == 0
    nc = seq_len // CHUNK
    per = nseq // SEQ_TOGETHER
    last = per * nc - 1
    spec = lambda width, idx, col=0: pl.BlockSpec((SEQ_TOGETHER, CHUNK, width), lambda b, n: (0, idx(b, n), col))
    cur = lambda b, n: b * nc + n
    first = lambda b, n: b * nc
    nxt = lambda b, n: jnp.minimum(b * nc + n + 1, last)
    view = lambda a: a.reshape(SEQ_TOGETHER, a.shape[0] // SEQ_TOGETHER, a.shape[1])
    return nc, per, spec, cur, first, nxt, view


def _gdn_seq(pm, graw, conv_w, conv_b, alog_row, dtb_row, consts, nseq, seq_len):
    nc, per, spec, cur, first, nxt, view = _seq_views(nseq, seq_len)
    rowspec = pl.BlockSpec((1, LANES), lambda b, n: (0, 0))
    pm3 = view(pm)
    o, s = pl.pallas_call(
        _gdn_seq_kernel,
        grid=(per, nc),
        in_specs=[
            spec(GDN_CONV_DIM, first),
            spec(GDN_CONV_DIM, nxt),
            spec(LANES, cur),
            pl.BlockSpec(((CONV_WIDTH - 1) * CHUNK, BHALO + CHUNK), lambda b, n: (0, 0)),
            pl.BlockSpec((CONV_WIDTH, GDN_CONV_DIM), lambda b, n: (0, 0)),
            pl.BlockSpec((1, GDN_CONV_DIM), lambda b, n: (0, 0)),
            rowspec, rowspec,
        ] + _const_specs(consts, 2),
        out_specs=[
            spec(GDN_V_DIM, cur),
            pl.BlockSpec((SEQ_TOGETHER, None, GDN_V_HEADS, GDN_HEAD_DIM, GDN_HEAD_DIM), lambda b, n: (0, b, 0, 0, 0)),
        ],
        out_shape=[
            jax.ShapeDtypeStruct((SEQ_TOGETHER, per * seq_len, GDN_V_DIM), F32),
            jax.ShapeDtypeStruct((SEQ_TOGETHER, per, GDN_V_HEADS, GDN_HEAD_DIM, GDN_HEAD_DIM), F32),
        ],
        scratch_shapes=[
            pltpu.VMEM((SEQ_TOGETHER, CHUNK + BHALO, GDN_CONV_DIM), BF16),
            pltpu.VMEM((SEQ_TOGETHER, 2, CHUNK, GDN_CONV_DIM), F32),
        ],
        compiler_params=_cparams("parallel", "arbitrary"),
        name="gdn_seq",
    )(pm3, pm3, view(graw), _shift_matrix(), conv_w, conv_b.reshape(1, GDN_CONV_DIM), alog_row, dtb_row, *consts)
    return (o.reshape(nseq * seq_len, GDN_V_DIM),
            s.reshape(nseq, GDN_V_HEADS, GDN_HEAD_DIM, GDN_HEAD_DIM))


SEQ_PER_STEP = 8
SEQ_INTERLEAVE = 4


def _load_padded(src_ref, s, pad_ref, lr):
    pad_ref[...] = jnp.zeros(pad_ref.shape, F32)
    for t in range(lr):
        pad_ref[t:t + 1, :] = src_ref[t, pl.ds(s, 1), :]


def _store_tokens(pad_ref, dst_ref, s, lr):
    for t in range(lr):
        dst_ref[t, pl.ds(s, 1), :] = pad_ref[t:t + 1, :]


def _gdn_step_kernel(act_ref, graw_ref, alog_ref, dtb_ref, tril3_ref, upper3_ref, e3_ref, s0_ref,
                     o_ref, s_ref, apad_ref, gpad_ref, opad_ref, *, lr):
    def some_sequences(it, carry):
        insts = []
        for j in range(SEQ_INTERLEAVE):
            s = it * SEQ_INTERLEAVE + j
            _load_padded(act_ref, s, apad_ref.at[j], lr)
            _load_padded(graw_ref, s, gpad_ref.at[j], lr)
            gt = _gdn_gates(gpad_ref[j], alog_ref[...], dtb_ref[...], lr)
            insts.append((apad_ref.at[j], gt, s0_ref.at[s], s_ref.at[s], opad_ref.at[j]))
        _gdn_block(insts, tril3_ref, upper3_ref, e3_ref, lr)
        for j in range(SEQ_INTERLEAVE):
            _store_tokens(opad_ref.at[j], o_ref, it * SEQ_INTERLEAVE + j, lr)
        return carry

    lax.fori_loop(0, SEQ_PER_STEP // SEQ_INTERLEAVE, some_sequences, 0)


def _step_spec(steps, width):
    return pl.BlockSpec((steps, SEQ_PER_STEP, width), lambda g: (0, g, 0))


def _gdn_step(act3, graw3, alog_row, dtb_row, consts, s0):
    steps, nb, _ = act3.shape
    assert nb % SEQ_PER_STEP == 0 and steps <= STEP_ROWS
    rowspec = pl.BlockSpec((1, LANES), lambda b: (0, 0))
    sspec = pl.BlockSpec((SEQ_PER_STEP, GDN_V_HEADS, GDN_HEAD_DIM, GDN_HEAD_DIM), lambda g: (g, 0, 0, 0))
    o, s = pl.pallas_call(
        functools.partial(_gdn_step_kernel, lr=steps),
        grid=(nb // SEQ_PER_STEP,),
        in_specs=[_step_spec(steps, GDN_CONV_DIM), _step_spec(steps, LANES), rowspec, rowspec]
        + _const_specs(consts, 1) + [sspec],
        out_specs=[_step_spec(steps, GDN_V_DIM), sspec],
        out_shape=[
            jax.ShapeDtypeStruct((steps, nb, GDN_V_DIM), F32),
            jax.ShapeDtypeStruct(s0.shape, F32),
        ],
        scratch_shapes=[
            pltpu.VMEM((SEQ_INTERLEAVE, STEP_ROWS, GDN_CONV_DIM), F32),
            pltpu.VMEM((SEQ_INTERLEAVE, CHUNK, LANES), F32),
            pltpu.VMEM((SEQ_INTERLEAVE, STEP_ROWS, GDN_V_DIM), F32),
        ],
        compiler_params=_cparams("parallel"),
        name="gdn_step",
    )(act3, graw3, alog_row, dtb_row, *consts, s0)
    return o.reshape(steps * nb, GDN_V_DIM), s


SSM_PAIRS = SSM_HEADS // 2
PAIRS_PER_GROUP = SSM_PAIRS // SSM_GROUPS


def _ssm_gates(raw, alog_row, dtb_row, lr):
    row, lane = _iotas((CHUNK, LANES))
    dt = _softplus(raw + dtb_row)
    tile = jnp.where(lane < SSM_HEADS, dt, jnp.where(lane < 2 * SSM_HEADS, -jnp.exp(alog_row) * dt, 0.0))
    if lr < CHUNK:
        tile = jnp.where(row < lr, tile, 0.0)
    return tile


def _ssd_block(insts, dskip_ref, tril3_ref, upper3_ref, e3_ref, fill=_NO_FILL):
    np_ = SSM_PAIRS
    rows = insts[0][0].shape[0]
    row, lane = _iotas((rows, LANES))
    causal = row >= jnp.bitwise_and(lane, HALF - 1)
    r2, l2 = _iotas((LANES, LANES))
    bdmask = jnp.right_shift(r2, 6) == jnp.right_shift(l2, 6)
    top = r2 < HALF
    _, lane_p = _iotas((np_, LANES))
    sl = lambda a, i: a[:, i * LANES:(i + 1) * LANES]
    grp = lambda it: (it[0], it[1] // PAIRS_PER_GROUP)

    pre = []
    for _, tile, _, _, _ in insts:
        t1 = _pad_t(tile)
        cum_ext = _dot(_cat3(t1[2 * np_:4 * np_], 1), upper3_ref[...])
        cum_t, last_b = cum_ext[:, :LANES], cum_ext[:, LANES:]
        cum_rp = _pair_rows(cum_t, np_)
        dt_rp = _pair_rows(t1[0:2 * np_], np_)
        last_rp = jnp.where(lane_p < HALF, last_b[0:np_], last_b[np_:2 * np_])
        cum = _dot(tril3_ref[...], _cat3(tile, 0))
        pre.append(dict(cum_rp=cum_rp, dt_rp=dt_rp, coef_rp=jnp.exp(last_rp - cum_rp) * dt_rp,
                        elast=jnp.exp(last_b), col_all=_dot(_cat3(cum[0:rows], 1), e3_ref[...])))

    seqs = range(len(insts))
    items = [(i, p) for i in seqs for p in range(np_)]
    groups = [(i, g) for i in seqs for g in range(SSM_GROUPS)]
    act = lambda i: insts[i][0]
    prow = lambda name, it: pre[it[0]][name][it[1]:it[1] + 1]
    bg = {ig: _pad_rows(sl(act(ig[0]), SSM_D_INNER // LANES + ig[1])).astype(BF16) for ig in groups}
    cg = {ig: sl(act(ig[0]), (SSM_D_INNER + SSM_BC) // LANES + ig[1]).astype(BF16) for ig in groups}
    bb = {ig: jnp.concatenate([bg[ig], bg[ig]], axis=0) for ig in groups}
    cb2 = {ig: _dot_nt(cg[ig], bb[ig]) for ig in groups}
    fill.emit()
    xp = {it: sl(act(it[0]), it[1]) for it in items}
    colc = {it: sl(pre[it[0]]["col_all"], it[1]) for it in items}
    x2 = {it: jnp.concatenate([_pad_rows(xp[it]), _pad_rows(xp[it])], axis=0) for it in items}
    lm = {it: cb2[grp(it)] * jnp.exp(jnp.where(causal, colc[it] - prow("cum_rp", it), NEG_BIG)) * prow("dt_rp", it)
          for it in items}
    y_diag = {}
    for n, it in enumerate(items):
        y_diag[it] = _dot(lm[it].astype(BF16), jnp.where(bdmask, x2[it], 0.0).astype(BF16))
        if n % (4 * len(insts)) == 4 * len(insts) - 1:
            fill.emit()
    hp = {it: insts[it[0]][2][it[1]] for it in items}
    y_off = {}
    for n, it in enumerate(items):
        y_off[it] = _dot_nt(cg[grp(it)], hp[it].astype(BF16))
        if n % (4 * len(insts)) == 4 * len(insts) - 1:
            fill.emit()
    for it in items:
        i, p = it
        insts[i][4][:, p * LANES:(p + 1) * LANES] = (y_diag[it] + jnp.exp(colc[it]) * y_off[it]
                                                     + dskip_ref[:, p * LANES:(p + 1) * LANES] * xp[it])
    dh = {}
    for n, it in enumerate(items):
        lhs = jnp.where(bdmask, x2[it].T * prow("coef_rp", it), 0.0)
        dh[it] = _dot(lhs.astype(BF16), bb[grp(it)])
        if n % (4 * len(insts)) == 4 * len(insts) - 1:
            fill.emit()
    for it in items:
        i, p = it
        elast = pre[i]["elast"]
        e_rows = jnp.where(top, elast[p:p + 1], elast[np_ + p:np_ + p + 1])
        insts[i][3][p] = e_rows * hp[it] + dh[it]
    fill.flush()


def _ssd_seq_kernel(x0_ref, bc0_ref, xn_ref, bcn_ref, graw_ref, shift_ref, cw_ref, cb_ref, alog_ref, dtb_ref,
                    dskip_ref, tril3_ref, upper3_ref, e3_ref, y_ref, h_ref, ext_ref, act_ref):
    n = pl.program_id(1)
    seqs = range(SEQ_TOGETHER)
    conv = functools.partial(_conv_block, shift_ref=shift_ref, w_ref=cw_ref, b_ref=cb_ref,
                             l2_cols=0, q_cols=0, qscale=1.0)

    @pl.when(n == 0)
    def _():
        h_ref[...] = jnp.zeros(h_ref.shape, F32)
        for j in seqs:
            _Filler(conv([x0_ref.at[j], bc0_ref.at[j]], ext_ref=ext_ref.at[j], act_ref=act_ref.at[j, 0],
                         seq_start=True), 1).flush()

    @pl.when(n > 0)
    def _():
        for j in seqs:
            act_ref[j, 0] = act_ref[j, 1]

    tasks, insts = [], []
    for j in seqs:
        tile = _ssm_gates(graw_ref[j], alog_ref[...], dtb_ref[...], CHUNK)
        tasks += conv([xn_ref.at[j], bcn_ref.at[j]], ext_ref=ext_ref.at[j], act_ref=act_ref.at[j, 1], seq_start=False)
        insts.append((act_ref.at[j, 0], tile, h_ref.at[j], h_ref.at[j], y_ref.at[j]))
    _ssd_block(insts, dskip_ref, tril3_ref, upper3_ref, e3_ref, _Filler(tasks, SSD_FILL_SHARES))


SSD_FILL_SHARES = 13


def _ssd_seq(pm, graw, conv_w, conv_b, alog_row, dtb_row, dskip_row, consts, nseq, seq_len):
    nc, per, spec, cur, first, nxt, view = _seq_views(nseq, seq_len)
    bc_blk = 2 * SSM_D_INNER // (2 * SSM_BC)
    rowspec = pl.BlockSpec((1, LANES), lambda b, n: (0, 0))
    pm3 = view(pm)
    y, h = pl.pallas_call(
        _ssd_seq_kernel,
        grid=(per, nc),
        in_specs=[
            spec(SSM_D_INNER, first),
            spec(2 * SSM_BC, first, bc_blk),
            spec(SSM_D_INNER, nxt),
            spec(2 * SSM_BC, nxt, bc_blk),
            spec(LANES, cur),
            pl.BlockSpec(((CONV_WIDTH - 1) * CHUNK, BHALO + CHUNK), lambda b, n: (0, 0)),
            pl.BlockSpec((CONV_WIDTH, SSM_CONV_DIM), lambda b, n: (0, 0)),
            pl.BlockSpec((1, SSM_CONV_DIM), lambda b, n: (0, 0)),
            rowspec, rowspec,
            pl.BlockSpec((1, SSM_D_INNER), lambda b, n: (0, 0)),
        ] + _const_specs(consts, 2),
        out_specs=[
            spec(SSM_D_INNER, cur),
            pl.BlockSpec((SEQ_TOGETHER, None, SSM_PAIRS, LANES, SSM_STATE), lambda b, n: (0, b, 0, 0, 0)),
        ],
        out_shape=[
            jax.ShapeDtypeStruct((SEQ_TOGETHER, per * seq_len, SSM_D_INNER), F32),
            jax.ShapeDtypeStruct((SEQ_TOGETHER, per, SSM_PAIRS, LANES, SSM_STATE), F32),
        ],
        scratch_shapes=[
            pltpu.VMEM((SEQ_TOGETHER, CHUNK + BHALO, SSM_CONV_DIM), BF16),
            pltpu.VMEM((SEQ_TOGETHER, 2, CHUNK, SSM_CONV_DIM), F32),
        ],
        compiler_params=_cparams("parallel", "arbitrary"),
        name="ssd_seq",
    )(pm3, pm3, pm3, pm3, view(graw), _shift_matrix(), conv_w, conv_b.reshape(1, SSM_CONV_DIM), alog_row, dtb_row,
      dskip_row, *consts)
    return y.reshape(nseq * seq_len, SSM_D_INNER), h.reshape(nseq, SSM_PAIRS, LANES, SSM_STATE)


def _ssd_step_kernel(act_ref, graw_ref, alog_ref, dtb_ref, dskip_ref, tril3_ref, upper3_ref, e3_ref, h0_ref,
                     y_ref, h_ref, apad_ref, gpad_ref, ypad_ref, *, lr):
    def some_sequences(it, carry):
        insts = []
        for j in range(SEQ_INTERLEAVE):
            s = it * SEQ_INTERLEAVE + j
            _load_padded(act_ref, s, apad_ref.at[j], lr)
            _load_padded(graw_ref, s, gpad_ref.at[j], lr)
            tile = _ssm_gates(gpad_ref[j], alog_ref[...], dtb_ref[...], lr)
            insts.append((apad_ref.at[j], tile, h0_ref.at[s], h_ref.at[s], ypad_ref.at[j]))
        _ssd_block(insts, dskip_ref, tril3_ref, upper3_ref, e3_ref)
        for j in range(SEQ_INTERLEAVE):
            _store_tokens(ypad_ref.at[j], y_ref, it * SEQ_INTERLEAVE + j, lr)
        return carry

    lax.fori_loop(0, SEQ_PER_STEP // SEQ_INTERLEAVE, some_sequences, 0)


def _ssd_step(act3, graw3, alog_row, dtb_row, dskip_row, consts, h0):
    steps, nb, _ = act3.shape
    assert nb % SEQ_PER_STEP == 0 and steps <= STEP_ROWS
    rowspec = pl.BlockSpec((1, LANES), lambda b: (0, 0))
    hspec = pl.BlockSpec((SEQ_PER_STEP, SSM_PAIRS, LANES, SSM_STATE), lambda g: (g, 0, 0, 0))
    y, h = pl.pallas_call(
        functools.partial(_ssd_step_kernel, lr=steps),
        grid=(nb // SEQ_PER_STEP,),
        in_specs=[_step_spec(steps, SSM_CONV_DIM), _step_spec(steps, LANES), rowspec, rowspec,
                  pl.BlockSpec((1, SSM_D_INNER), lambda b: (0, 0))] + _const_specs(consts, 1) + [hspec],
        out_specs=[_step_spec(steps, SSM_D_INNER), hspec],
        out_shape=[
            jax.ShapeDtypeStruct((steps, nb, SSM_D_INNER), F32),
            jax.ShapeDtypeStruct(h0.shape, F32),
        ],
        scratch_shapes=[
            pltpu.VMEM((SEQ_INTERLEAVE, STEP_ROWS, SSM_CONV_DIM), F32),
            pltpu.VMEM((SEQ_INTERLEAVE, CHUNK, LANES), F32),
            pltpu.VMEM((SEQ_INTERLEAVE, STEP_ROWS, SSM_D_INNER), F32),
        ],
        compiler_params=_cparams("parallel"),
        name="ssd_step",
    )(act3, graw3, alog_row, dtb_row, dskip_row, *consts, h0)
    return y.reshape(steps * nb, SSM_D_INNER), h


FFN_TILE = FFN_HIDDEN // 2
GDN_PROJ_TILE = GDN_MAIN // 3
SSM_PROJ_TILE = SSM_MAIN // 2


def _lane_row(pieces):
    row = jnp.zeros((1, LANES), F32)
    for off, vec in pieces:
        row = row.at[0, off:off + vec.shape[0]].set(vec.astype(F32))
    return row


def _stage_params(w_mod, b_mod, norm_mix, norm_ffn, norm_final, gdn_w_in, gdn_conv_w, gdn_a_log, gdn_dt_bias,
                  gdn_norm, gdn_w_out, ssm_w_in, ssm_conv_w, ssm_conv_b, ssm_a_log, ssm_dt_bias, ssm_d, ssm_norm,
                  ssm_w_out, ffn_w_gate_up, ffn_w_down):
    perm_g = np.concatenate([np.arange(0, GDN_V_HEADS, 2), np.arange(1, GDN_V_HEADS, 2)])
    perm_s = np.concatenate([np.arange(0, SSM_HEADS, 2), np.arange(1, SSM_HEADS, 2)])
    g_in, s_in = gdn_w_in[0], ssm_w_in[0]
    beta_cols = g_in[:, GDN_MAIN:GDN_MAIN + GDN_V_HEADS][:, perm_g]
    a_cols = g_in[:, GDN_MAIN + GDN_V_HEADS:GDN_MAIN + 2 * GDN_V_HEADS][:, perm_g]
    gdn_small = jnp.concatenate([a_cols, beta_cols, jnp.zeros((D_MODEL, LANES - 2 * GDN_V_HEADS), F32)], axis=1)
    dt_cols = s_in[:, SSM_MAIN:SSM_MAIN + SSM_HEADS][:, perm_s]
    ssm_small = jnp.concatenate([dt_cols, dt_cols, jnp.zeros((D_MODEL, LANES - 2 * SSM_HEADS), F32)], axis=1)
    return dict(
        w_mod=w_mod, b_mod=b_mod, norm_mix=norm_mix, norm_ffn=norm_ffn, norm_final=norm_final,
        gdn_main=g_in[:, :GDN_MAIN].astype(BF16), gdn_small=gdn_small.astype(BF16),
        gdn_conv_w=gdn_conv_w[0], gdn_conv_b=jnp.zeros((GDN_CONV_DIM,), F32),
        gdn_alog_row=_lane_row([(0, gdn_a_log[0][perm_g])]), gdn_dtb_row=_lane_row([(0, gdn_dt_bias[0][perm_g])]),
        gdn_norm=jnp.tile(gdn_norm[0], GDN_V_HEADS), gdn_w_out=gdn_w_out[0].astype(BF16),
        gdn_consts=_recurrence_consts(GDN_PAIRS, 0, GDN_PAIRS),
        ssm_main=jnp.concatenate(
            [s_in[:, SSM_D_INNER:2 * SSM_D_INNER], s_in[:, :SSM_D_INNER], s_in[:, 2 * SSM_D_INNER:SSM_MAIN]],
            axis=1).astype(BF16),
        ssm_small=ssm_small.astype(BF16),
        ssm_conv_w=ssm_conv_w[0], ssm_conv_b=ssm_conv_b[0],
        ssm_alog_row=_lane_row([(SSM_HEADS, ssm_a_log[0][perm_s])]),
        ssm_dtb_row=_lane_row([(0, ssm_dt_bias[0][perm_s]), (SSM_HEADS, ssm_dt_bias[0][perm_s])]),
        ssm_dskip_row=jnp.repeat(ssm_d[0], SSM_HEAD_DIM).reshape(1, SSM_D_INNER),
        ssm_norm=ssm_norm[0], ssm_w_out=ssm_w_out[0].astype(BF16),
        ssm_consts=_recurrence_consts(SSM_PAIRS, SSM_HEADS, SSM_HEADS + SSM_PAIRS),
        wg=[ffn_w_gate_up[i][:, :FFN_HIDDEN].astype(BF16) for i in range(2)],
        wu=[ffn_w_gate_up[i][:, FFN_HIDDEN:].astype(BF16) for i in range(2)],
        wd=[ffn_w_down[i].astype(BF16) for i in range(2)],
    )


def _ffn(x, layer, mod3, rows_up, rows_down, p, final_w):
    act = _ffn_up(x, p["norm_ffn"][layer], mod3, rows_up, p["wg"][layer], p["wu"][layer], FFN_TILE)
    return _ffn_down(act, x, mod3, rows_down, p["wd"][layer], final_w)


QSCALE = GDN_HEAD_DIM ** -0.5


def _trunk_seq(x3, mod, p):
    nseq, seq_len, _ = x3.shape
    m = nseq * seq_len
    x = x3.reshape(m, D_MODEL)
    mod3 = [mod[l].reshape(nseq, 1, 6 * D_MODEL) for l in range(2)]
    rows_a = _Rows(m, min(1024, seq_len), seq_len, 1)
    rows_b = _Rows(m, min(512, seq_len), seq_len, 1)

    pm, ps = _in_proj(x, p["norm_mix"][0], mod3[0], rows_a, 1, 0, p["gdn_main"], p["gdn_small"], GDN_PROJ_TILE)
    o, gdn_s = _gdn_seq(pm, ps, p["gdn_conv_w"], p["gdn_conv_b"], p["gdn_alog_row"], p["gdn_dtb_row"],
                        p["gdn_consts"], nseq, seq_len)
    tail = pm.reshape(nseq, seq_len, GDN_MAIN)[:, seq_len - (CONV_WIDTH - 1):].astype(F32)
    gdn_c = tail[..., :GDN_CONV_DIM]
    x = _mixer_out(o, pm, 2, p["gdn_norm"], x, mod3[0], rows_b, p["gdn_w_out"], GDN_HEAD_DIM, False)
    x = _ffn(x, 0, mod3[0], rows_a, rows_b, p, None)

    pm, ps = _in_proj(x, p["norm_mix"][1], mod3[1], rows_a, 1, 0, p["ssm_main"], p["ssm_small"], SSM_PROJ_TILE)
    y, ssm_h = _ssd_seq(pm, ps, p["ssm_conv_w"], p["ssm_conv_b"], p["ssm_alog_row"], p["ssm_dtb_row"],
                        p["ssm_dskip_row"], p["ssm_consts"], nseq, seq_len)
    tail = pm.reshape(nseq, seq_len, SSM_MAIN)[:, seq_len - (CONV_WIDTH - 1):].astype(F32)
    ssm_c = jnp.concatenate([tail[..., :SSM_D_INNER], tail[..., 2 * SSM_D_INNER:]], axis=-1)
    x = _mixer_out(y, pm, 1, p["ssm_norm"], x, mod3[1], rows_b, p["ssm_w_out"], SSM_D_INNER // SSM_GROUPS, True)
    _, y_out = _ffn(x, 1, mod3[1], rows_a, rows_b, p, p["norm_final"])

    return (y_out.reshape(nseq, seq_len, D_MODEL), gdn_s[None], gdn_c[None],
            ssm_h.reshape(nseq, SSM_HEADS, SSM_HEAD_DIM, SSM_STATE)[None], ssm_c[None])


def _trunk_step(x3, mod, st_gdn, cv_gdn, st_ssm, cv_ssm, p):
    nb, steps, _ = x3.shape
    assert steps >= CONV_WIDTH - 1
    m = nb * steps
    x = jnp.transpose(x3, (1, 0, 2)).reshape(m, D_MODEL)
    mod3 = [mod[l].reshape(1, nb, 6 * D_MODEL) for l in range(2)]
    rows = _Rows(m, m, None, nb)
    tok = lambda a: jnp.transpose(a, (1, 0, 2))

    pm, ps = _in_proj(x, p["norm_mix"][0], mod3[0], rows, 1, 0, p["gdn_main"], p["gdn_small"], GDN_PROJ_TILE)
    u3 = pm.reshape(steps, nb, GDN_MAIN)
    act3 = _conv_steps(u3, 0, 0, GDN_CONV_DIM, tok(cv_gdn[0]), p["gdn_conv_w"], p["gdn_conv_b"], 2, QSCALE)
    o, gdn_s = _gdn_step(act3, ps.reshape(steps, nb, LANES), p["gdn_alog_row"], p["gdn_dtb_row"], p["gdn_consts"],
                         st_gdn[0])
    gdn_c = tok(u3[steps - (CONV_WIDTH - 1):, :, :GDN_CONV_DIM].astype(F32))
    x = _mixer_out(o, pm, 2, p["gdn_norm"], x, mod3[0], rows, p["gdn_w_out"], GDN_HEAD_DIM, False)
    x = _ffn(x, 0, mod3[0], rows, rows, p, None)

    pm, ps = _in_proj(x, p["norm_mix"][1], mod3[1], rows, 1, 0, p["ssm_main"], p["ssm_small"], SSM_PROJ_TILE)
    u3 = pm.reshape(steps, nb, SSM_MAIN)
    act3 = _conv_steps(u3, SSM_D_INNER // CONV_COLS, SSM_D_INNER // CONV_COLS, SSM_CONV_DIM, tok(cv_ssm[0]),
                       p["ssm_conv_w"], p["ssm_conv_b"], 0, 1.0)
    h0 = st_ssm[0].reshape(nb, SSM_PAIRS, LANES, SSM_STATE)
    y, ssm_h = _ssd_step(act3, ps.reshape(steps, nb, LANES), p["ssm_alog_row"], p["ssm_dtb_row"],
                         p["ssm_dskip_row"], p["ssm_consts"], h0)
    tail = u3[steps - (CONV_WIDTH - 1):].astype(F32)
    ssm_c = tok(jnp.concatenate([tail[..., :SSM_D_INNER], tail[..., 2 * SSM_D_INNER:]], axis=-1))
    x = _mixer_out(y, pm, 1, p["ssm_norm"], x, mod3[1], rows, p["ssm_w_out"], SSM_D_INNER // SSM_GROUPS, True)
    _, y_out = _ffn(x, 1, mod3[1], rows, rows, p, p["norm_final"])

    return (tok(y_out.reshape(steps, nb, D_MODEL)), gdn_s[None], gdn_c[None],
            ssm_h.reshape(nb, SSM_HEADS, SSM_HEAD_DIM, SSM_STATE)[None], ssm_c[None])


def kernel(x_prompt, x_sample, c_prompt, c_sample, state_gdn, state_gdn_conv, state_ssm, state_ssm_conv, w_mod, b_mod,
           norm_mix, norm_ffn, norm_final, gdn_w_in, gdn_conv_w, gdn_a_log, gdn_dt_bias, gdn_norm, gdn_w_out, ssm_w_in,
           ssm_conv_w, ssm_conv_b, ssm_a_log, ssm_dt_bias, ssm_d, ssm_norm, ssm_w_out, ffn_w_gate_up, ffn_w_down):
    p = _stage_params(w_mod, b_mod, norm_mix, norm_ffn, norm_final, gdn_w_in, gdn_conv_w, gdn_a_log, gdn_dt_bias,
                      gdn_norm, gdn_w_out, ssm_w_in, ssm_conv_w, ssm_conv_b, ssm_a_log, ssm_dt_bias, ssm_d, ssm_norm,
                      ssm_w_out, ffn_w_gate_up, ffn_w_down)
    n_prompt = x_prompt.shape[0]
    mod = _modulation(jnp.concatenate([c_prompt, c_sample], axis=0), p["w_mod"], p["b_mod"])
    y_p, gs_p, gc_p, ss_p, sc_p = _trunk_seq(x_prompt, mod[:, :n_prompt], p)
    y_s, gs_s, gc_s, ss_s, sc_s = _trunk_step(x_sample, mod[:, n_prompt:], state_gdn, state_gdn_conv, state_ssm,
                                              state_ssm_conv, p)
    return (y_p, y_s, gs_p, gc_p, ss_p, sc_p, gs_s, gc_s, ss_s, sc_s)
```

```python
import functools

import numpy as np
import jax
import jax.numpy as jnp
from jax import lax
from jax.experimental import pallas as pl
from jax.experimental.pallas import tpu as pltpu

F32 = jnp.float32
BF16 = jnp.bfloat16

D_MODEL = 1024
EPS = 1e-6
CONV_WIDTH = 4
CHUNK = 64
LANES = 128
HALF = LANES // 2

GDN_QK_HEADS = 8
GDN_V_HEADS = 16
GDN_HEAD_DIM = 128
GDN_QK_DIM = GDN_QK_HEADS * GDN_HEAD_DIM
GDN_V_DIM = GDN_V_HEADS * GDN_HEAD_DIM
GDN_CONV_DIM = 2 * GDN_QK_DIM + GDN_V_DIM
GDN_MAIN = GDN_CONV_DIM + GDN_V_DIM

SSM_D_INNER = 2 * D_MODEL
SSM_HEAD_DIM = 64
SSM_HEADS = SSM_D_INNER // SSM_HEAD_DIM
SSM_GROUPS = 4
SSM_STATE = 128
SSM_BC = SSM_GROUPS * SSM_STATE
SSM_CONV_DIM = SSM_D_INNER + 2 * SSM_BC
SSM_MAIN = SSM_D_INNER + SSM_CONV_DIM

FFN_HIDDEN = 2816

VMEM_LIMIT = 56 * 1024 * 1024
NEG_BIG = -1e30


def _cparams(*sem):
    return pltpu.CompilerParams(dimension_semantics=sem, vmem_limit_bytes=VMEM_LIMIT)


def _silu(x):
    hx = 0.5 * x
    return hx + hx * jnp.tanh(hx)


def _softplus(x):
    return jnp.maximum(x, 0.0) + jnp.log1p(jnp.exp(-jnp.abs(x)))


def _dot(a, b):
    return jnp.dot(a, b, preferred_element_type=F32)


def _dot_nt(a, b):
    return lax.dot_general(a, b, (((1,), (1,)), ((), ())), preferred_element_type=F32)


def _tile_rows(v, rep):
    return v if rep == 1 else jnp.concatenate([v] * rep, axis=0)


def _mod_kernel(c_ref, w_ref, b_ref, o_ref):
    cs = _silu(c_ref[...]).astype(BF16)
    o_ref[...] = _dot(cs, w_ref[...].astype(BF16)) + b_ref[...]


def _modulation(c, w_mod, b_mod):
    depth, _, n = w_mod.shape
    bc = c.shape[0]
    tn = 1536
    return pl.pallas_call(
        _mod_kernel,
        grid=(depth, n // tn),
        in_specs=[
            pl.BlockSpec((bc, D_MODEL), lambda l, j: (0, 0)),
            pl.BlockSpec((None, D_MODEL, tn), lambda l, j: (l, 0, j)),
            pl.BlockSpec((None, 1, tn), lambda l, j: (l, 0, j)),
        ],
        out_specs=pl.BlockSpec((None, bc, tn), lambda l, j: (l, 0, j)),
        out_shape=jax.ShapeDtypeStruct((depth, bc, n), F32),
        compiler_params=_cparams("parallel", "parallel"),
        name="adaln_mod",
    )(c, w_mod, b_mod.reshape(depth, 1, n))


def _norm_mod(x, nw, sc, sh, rep):
    y = x * lax.rsqrt(jnp.mean(x * x, axis=-1, keepdims=True) + EPS) * nw
    return y * (1.0 + _tile_rows(sc, rep)) + _tile_rows(sh, rep)


def _in_proj_kernel(x_ref, nw_ref, sc_ref, sh_ref, w_ref, w2_ref, o_ref, o2_ref, h_ref, *, rep, gate_cols):
    j = pl.program_id(1)
    tn = o_ref.shape[1]
    lo, hi = gate_cols

    @pl.when(j == 0)
    def _():
        h = _norm_mod(x_ref[...], nw_ref[...], sc_ref[...], sh_ref[...], rep).astype(BF16)
        h_ref[...] = h
        o2_ref[...] = _dot(h, w2_ref[...])

    has_gate = jnp.logical_and(j * tn < hi, (j + 1) * tn > lo)

    @pl.when(has_gate)
    def _():
        r = _dot(h_ref[...], w_ref[...])
        col = j * tn + lax.broadcasted_iota(jnp.int32, r.shape, 1)
        o_ref[...] = jnp.where(jnp.logical_and(col >= lo, col < hi), _silu(r), r).astype(o_ref.dtype)

    @pl.when(jnp.logical_not(has_gate))
    def _():
        o_ref[...] = _dot(h_ref[...], w_ref[...]).astype(o_ref.dtype)


def _ffn_up_kernel(x_ref, nw_ref, sc_ref, sh_ref, wg_ref, wu_ref, o_ref, h_ref, *, rep):
    @pl.when(pl.program_id(1) == 0)
    def _():
        h_ref[...] = _norm_mod(x_ref[...], nw_ref[...], sc_ref[...], sh_ref[...], rep).astype(BF16)

    h = h_ref[...]
    o_ref[...] = (_silu(_dot(h, wg_ref[...])) * _dot(h, wu_ref[...])).astype(BF16)


class _Rows:
    def __init__(self, m, tm, group_rows, mod_rows):
        assert m % tm == 0
        self.m, self.tm = m, tm
        if mod_rows == 1:
            assert group_rows % tm == 0
            self.rep = 1
            self.gmap = lambda i: (i * tm) // group_rows
        else:
            assert tm % mod_rows == 0
            self.rep = tm // mod_rows
            self.gmap = lambda i: 0
        self.mod_rows = mod_rows

    def mod_spec(self, col_block, with_j):
        if with_j:
            return pl.BlockSpec((None, self.mod_rows, D_MODEL), lambda i, j: (self.gmap(i), 0, col_block))
        return pl.BlockSpec((None, self.mod_rows, D_MODEL), lambda i: (self.gmap(i), 0, col_block))


def _in_proj(x, nw, mod3, rows, sc_blk, sh_blk, w, w2, tn, gate_cols):
    m, tm = rows.m, rows.tm
    n = w.shape[1]
    assert n % tn == 0
    return pl.pallas_call(
        functools.partial(_in_proj_kernel, rep=rows.rep, gate_cols=gate_cols),
        grid=(m // tm, n // tn),
        in_specs=[
            pl.BlockSpec((tm, D_MODEL), lambda i, j: (i, 0)),
            pl.BlockSpec((1, D_MODEL), lambda i, j: (0, 0)),
            rows.mod_spec(sc_blk, True),
            rows.mod_spec(sh_blk, True),
            pl.BlockSpec((D_MODEL, tn), lambda i, j: (0, j)),
            pl.BlockSpec((D_MODEL, LANES), lambda i, j: (0, 0)),
        ],
        out_specs=[
            pl.BlockSpec((tm, tn), lambda i, j: (i, j)),
            pl.BlockSpec((tm, LANES), lambda i, j: (i, 0)),
        ],
        out_shape=[jax.ShapeDtypeStruct((m, n), BF16), jax.ShapeDtypeStruct((m, LANES), F32)],
        scratch_shapes=[pltpu.VMEM((tm, D_MODEL), BF16)],
        compiler_params=_cparams("parallel", "arbitrary"),
        name="in_proj",
    )(x, nw.reshape(1, D_MODEL), mod3, mod3, w, w2)


def _ffn_up(x, nw, mod3, rows, wg, wu, th):
    m, tm = rows.m, rows.tm
    assert FFN_HIDDEN % th == 0
    return pl.pallas_call(
        functools.partial(_ffn_up_kernel, rep=rows.rep),
        grid=(m // tm, FFN_HIDDEN // th),
        in_specs=[
            pl.BlockSpec((tm, D_MODEL), lambda i, j: (i, 0)),
            pl.BlockSpec((1, D_MODEL), lambda i, j: (0, 0)),
            rows.mod_spec(4, True),
            rows.mod_spec(3, True),
            pl.BlockSpec((D_MODEL, th), lambda i, j: (0, j)),
            pl.BlockSpec((D_MODEL, th), lambda i, j: (0, j)),
        ],
        out_specs=pl.BlockSpec((tm, th), lambda i, j: (i, j)),
        out_shape=jax.ShapeDtypeStruct((m, FFN_HIDDEN), BF16),
        scratch_shapes=[pltpu.VMEM((tm, D_MODEL), BF16)],
        compiler_params=_cparams("parallel", "arbitrary"),
        name="ffn_up",
    )(x, nw.reshape(1, D_MODEL), mod3, mod3, wg, wu)


def _resid_store(acc, x_ref, gt_ref, o_ref, fnw_ref, y_ref, rep):
    xn = x_ref[...] + _tile_rows(gt_ref[...], rep) * acc
    o_ref[...] = xn
    if y_ref is not None:
        y_ref[...] = xn * lax.rsqrt(jnp.mean(xn * xn, axis=-1, keepdims=True) + EPS) * fnw_ref[...]


def _ffn_down_kernel(a_ref, x_ref, gt_ref, w_ref, *rest, rep, final):
    if final:
        fnw_ref, o_ref, y_ref = rest
    else:
        (o_ref,), fnw_ref, y_ref = rest, None, None
    _resid_store(_dot(a_ref[...], w_ref[...]), x_ref, gt_ref, o_ref, fnw_ref, y_ref, rep)


def _ffn_down(act, x, mod3, rows, w, fnw):
    m, tm = rows.m, rows.tm
    final = fnw is not None
    in_specs = [
        pl.BlockSpec((tm, FFN_HIDDEN), lambda i: (i, 0)),
        pl.BlockSpec((tm, D_MODEL), lambda i: (i, 0)),
        rows.mod_spec(5, False),
        pl.BlockSpec((FFN_HIDDEN, D_MODEL), lambda i: (0, 0)),
    ]
    args = [act, x, mod3, w]
    row_spec = pl.BlockSpec((tm, D_MODEL), lambda i: (i, 0))
    out_shape = jax.ShapeDtypeStruct((m, D_MODEL), F32)
    if final:
        in_specs.append(pl.BlockSpec((1, D_MODEL), lambda i: (0, 0)))
        args.append(fnw.reshape(1, D_MODEL))
        out_specs, out_shapes = [row_spec, row_spec], [out_shape, out_shape]
    else:
        out_specs, out_shapes = row_spec, out_shape
    return pl.pallas_call(
        functools.partial(_ffn_down_kernel, rep=rows.rep, final=final),
        grid=(m // tm,),
        in_specs=in_specs,
        out_specs=out_specs,
        out_shape=out_shapes,
        compiler_params=_cparams("parallel"),
        name="ffn_down",
    )(*args)


def _mixer_out_kernel(y_ref, z_ref, nw_ref, x_ref, gt_ref, w_ref, o_ref, a_ref, *, rep, group, gate_first):
    width = y_ref.shape[1]
    for s in range(0, width, group):
        y = y_ref[:, s:s + group]
        gate = z_ref[:, s:s + group].astype(F32)
        if gate_first:
            y = y * gate
        y = y * lax.rsqrt(jnp.mean(y * y, axis=-1, keepdims=True) + EPS) * nw_ref[:, s:s + group]
        if not gate_first:
            y = y * gate
        a_ref[:, s:s + group] = y.astype(BF16)
    _resid_store(_dot(a_ref[...], w_ref[...]), x_ref, gt_ref, o_ref, None, None, rep)


def _mixer_out(y, zsrc, z_blk, nw_full, x, mod3, rows, w, group, gate_first):
    m, tm = rows.m, rows.tm
    width = y.shape[1]
    return pl.pallas_call(
        functools.partial(_mixer_out_kernel, rep=rows.rep, group=group, gate_first=gate_first),
        grid=(m // tm,),
        in_specs=[
            pl.BlockSpec((tm, width), lambda i: (i, 0)),
            pl.BlockSpec((tm, width), lambda i: (i, z_blk)),
            pl.BlockSpec((1, width), lambda i: (0, 0)),
            pl.BlockSpec((tm, D_MODEL), lambda i: (i, 0)),
            rows.mod_spec(2, False),
            pl.BlockSpec((width, D_MODEL), lambda i: (0, 0)),
        ],
        out_specs=pl.BlockSpec((tm, D_MODEL), lambda i: (i, 0)),
        out_shape=jax.ShapeDtypeStruct((m, D_MODEL), F32),
        scratch_shapes=[pltpu.VMEM((tm, width), BF16)],
        compiler_params=_cparams("parallel"),
        name="mixer_out",
    )(y, zsrc, nw_full.reshape(1, width), x, mod3, w)


CONV_COLS = 1024


def _post_conv(acc, o_ref, cb, n_l2, qscale):
    y = _silu(acc)
    if n_l2 == 0:
        o_ref[...] = y
        return

    @pl.when(cb < n_l2)
    def _():
        scale = jnp.where(cb == 0, qscale, 1.0).astype(F32)
        for s in range(0, CONV_COLS, GDN_HEAD_DIM):
            yh = y[:, s:s + GDN_HEAD_DIM]
            o_ref[:, s:s + GDN_HEAD_DIM] = yh * lax.rsqrt(jnp.sum(yh * yh, axis=-1, keepdims=True) + EPS) * scale

    @pl.when(cb >= n_l2)
    def _():
        o_ref[...] = y


BHALO = 16
CONV_GROUP = 2 * LANES


def _shift_matrix():
    s = np.zeros(((CONV_WIDTH - 1) * CHUNK, BHALO + CHUNK), np.float32)
    for tap in range(CONV_WIDTH - 1):
        for r in range(CHUNK):
            s[tap * CHUNK + r, BHALO - (CONV_WIDTH - 1) + tap + r] = 1.0
    return jnp.asarray(s, BF16)


def _conv_block(raw_refs, shift_ref, w_ref, b_ref, ext_ref, act_ref, seq_start, l2_cols, q_cols, qscale):
    width = ext_ref.shape[1]
    if seq_start:
        ext_ref[0:BHALO, :] = jnp.zeros((BHALO, width), BF16)
    else:
        ext_ref[0:BHALO, :] = ext_ref[CHUNK:CHUNK + BHALO, :]

    off = 0
    for ref in raw_refs:
        ext_ref[BHALO:BHALO + CHUNK, off:off + ref.shape[1]] = ref[...]
        off += ref.shape[1]

    def lane_group(s):
        cols = slice(s, s + CONV_GROUP)
        sh = _dot(shift_ref[...], ext_ref[:, cols])
        acc = b_ref[:, cols] + w_ref[CONV_WIDTH - 1:CONV_WIDTH, cols] * ext_ref[BHALO:BHALO + CHUNK, cols].astype(F32)
        for tap in range(CONV_WIDTH - 1):
            acc = acc + w_ref[tap:tap + 1, cols] * sh[tap * CHUNK:(tap + 1) * CHUNK]
        y = _silu(acc)
        for h in range(s, s + CONV_GROUP, LANES):
            yh = y[:, h - s:h - s + LANES]
            if h < l2_cols:
                yh = yh * lax.rsqrt(jnp.sum(yh * yh, axis=-1, keepdims=True) + EPS)
                if h < q_cols:
                    yh = yh * qscale
            act_ref[:, h:h + LANES] = yh

    return [functools.partial(lane_group, s) for s in range(0, width, CONV_GROUP)]


class _Filler:
    def __init__(self, tasks, shares):
        self.tasks, self.per = list(tasks), -(-len(tasks) // shares)

    def emit(self):
        for task in self.tasks[:self.per]:
            task()
        self.tasks = self.tasks[self.per:]

    def flush(self):
        for task in self.tasks:
            task()
        self.tasks = []


_NO_FILL = _Filler([], 1)


def _conv_steps_kernel(u_ref, hist_ref, w_ref, b_ref, o_ref, *, steps, n_l2, qscale):
    cb = pl.program_id(0)
    ext = [hist_ref[i] for i in range(CONV_WIDTH - 1)] + [u_ref[i].astype(F32) for i in range(steps)]
    for t in range(steps):
        acc = b_ref[...] + w_ref[0:1, :] * ext[t]
        for tap in range(1, CONV_WIDTH):
            acc = acc + w_ref[tap:tap + 1, :] * ext[t + tap]
        _post_conv(acc, o_ref.at[t], cb, n_l2, qscale)


def _conv_steps(u3, skip_at, skip, n_cols, hist3, conv_w, conv_b, n_l2, qscale):
    steps, nb, _ = u3.shape
    return pl.pallas_call(
        functools.partial(_conv_steps_kernel, steps=steps, n_l2=n_l2, qscale=qscale),
        grid=(n_cols // CONV_COLS,),
        in_specs=[
            pl.BlockSpec((steps, nb, CONV_COLS), lambda c: (0, 0, c + skip * (c >= skip_at))),
            pl.BlockSpec((CONV_WIDTH - 1, nb, CONV_COLS), lambda c: (0, 0, c)),
            pl.BlockSpec((CONV_WIDTH, CONV_COLS), lambda c: (0, c)),
            pl.BlockSpec((1, CONV_COLS), lambda c: (0, c)),
        ],
        out_specs=pl.BlockSpec((steps, nb, CONV_COLS), lambda c: (0, 0, c)),
        out_shape=jax.ShapeDtypeStruct((steps, nb, n_cols), F32),
        compiler_params=_cparams("parallel"),
        name="conv_steps",
    )(u3, hist3, conv_w, conv_b.reshape(1, n_cols))


def _split3(x):
    hi = x.astype(BF16)
    r = x - hi.astype(F32)
    mid = r.astype(BF16)
    lo = (r - mid.astype(F32)).astype(BF16)
    return hi, mid, lo


def _cat3(x, axis):
    return jnp.concatenate(_split3(x), axis=axis)


def _pad_t(tile):
    return jnp.concatenate([tile, jnp.zeros_like(tile)], axis=0).T


def _pair_rows(t, n):
    return t[0:n] + pltpu.roll(t[n:2 * n], HALF, 1)


def _iotas(shape):
    return lax.broadcasted_iota(jnp.int32, shape, 0), lax.broadcasted_iota(jnp.int32, shape, 1)


def _pad_rows(a):
    if a.shape[0] == CHUNK:
        return a
    return jnp.concatenate([a, jnp.zeros((CHUNK - a.shape[0], a.shape[1]), a.dtype)], axis=0)


def _split2(x):
    hi = x.astype(BF16)
    return hi, (x - hi.astype(F32)).astype(BF16)


def _block_diag(pair_bf, bd_ones):
    pair_bf = _pad_rows(pair_bf)
    return jnp.concatenate([pair_bf, pair_bf], axis=0) * bd_ones


def _mm3(lhs_parts, rhs_hi, rhs_lo):
    lh, ll = lhs_parts
    return _dot(jnp.concatenate([lh, lh, ll], axis=1), jnp.concatenate([rhs_hi, rhs_lo, rhs_hi], axis=0))


def _mm_pairs(lhs_parts, rhs_parts, bd_fn, terms=3):
    if terms == 1:
        return [_dot(lh, bd_fn(rh)) for (lh, _), (rh, _) in zip(lhs_parts, rhs_parts)]
    return [_mm3(lp, bd_fn(rh), bd_fn(rl)) for lp, (rh, rl) in zip(lhs_parts, rhs_parts)]


def _mm_hl(lhs, rhs_bf):
    return _dot(lhs.astype(BF16), rhs_bf)


def _unit_lower_inverse(xs, levels, bd_fn, eye, fill, terms=3):
    ps = [eye + x for x in xs]
    if levels <= 1:
        return ps
    rows = xs[0].shape[0]
    stack = lambda a, b: tuple(jnp.concatenate([s, t], axis=0) for s, t in zip(a, b))
    ysp = [_split2(x) for x in xs]
    ys = _mm_pairs(ysp, ysp, bd_fn, terms)
    fill.emit()
    for _ in range(levels - 2):
        ysp = [_split2(y) for y in ys]
        rs = _mm_pairs([stack(yp, _split2(p)) for yp, p in zip(ysp, ps)], ysp, bd_fn, terms)
        fill.emit()
        ys = [r[0:rows] for r in rs]
        ps = [p + r[rows:2 * rows] for p, r in zip(ps, rs)]
    last = _mm_pairs([_split2(p) for p in ps], [_split2(y) for y in ys], bd_fn, terms)
    return [p + t for p, t in zip(ps, last)]


QUARTER = HALF // 2


def _unit_lower_inverse_blocked(xs, fill):
    row, lane = _iotas((QUARTER, LANES))
    r2, l2 = _iotas((LANES, LANES))
    even = jnp.bitwise_and(jnp.right_shift(lane, 5), 1) == 0
    eye_q = jnp.where(row == jnp.bitwise_and(lane, QUARTER - 1), 1.0, 0.0)
    bd4_ones = jnp.where(jnp.right_shift(r2, 5) == jnp.right_shift(l2, 5), 1.0, 0.0).astype(BF16)
    row_q = jnp.right_shift(r2, 5)
    place = jnp.where(((row_q == 1) & (jnp.right_shift(l2, 5) == 0)) | ((row_q == 3) & (jnp.right_shift(l2, 5) == 2)),
                      1.0, 0.0).astype(BF16)
    bd4 = lambda a: jnp.concatenate([a] * 4, axis=0) * bd4_ones

    tops = [x[0:QUARTER] for x in xs]
    bots = [x[QUARTER:CHUNK] for x in xs]
    diag = [jnp.where(even, t, b) for t, b in zip(tops, bots)]
    dinv = _unit_lower_inverse(diag, _levels(QUARTER), bd4, eye_q, fill)
    fill.emit()
    ai = [jnp.where(even, d, 0.0) for d in dinv]
    ci = [jnp.where(even, 0.0, d) for d in dinv]
    b_ai = _mm_pairs([_split2(jnp.where(even, b, 0.0)) for b in bots], [_split2(a) for a in ai], bd4)
    fill.emit()
    low = [_mm3(_split2(c), *[jnp.concatenate([part] * 4, axis=0) * place for part in _split2(e)])
           for c, e in zip(ci, b_ai)]
    return [jnp.concatenate([a, l + c], axis=0) for a, l, c in zip(ai, low, ci)]


def _two_blocks(a, b):
    a_bf, b_bf = _pad_rows(a).astype(BF16), _pad_rows(b).astype(BF16)
    z = jnp.zeros_like(a_bf)
    return jnp.concatenate([jnp.concatenate([a_bf, z], axis=1), jnp.concatenate([z, b_bf], axis=1)], axis=0)


def _recurrence_consts(n_pairs, chan_even0, chan_odd0):
    tril = np.tril(np.ones((CHUNK, CHUNK), np.float32))
    tril3 = np.concatenate([tril] * 3, axis=1)
    upper = np.zeros((LANES, 2 * LANES), np.float32)
    upper[:CHUNK, :CHUNK] = tril.T
    upper[:CHUNK, LANES:] = 1.0
    upper3 = np.concatenate([upper] * 3, axis=0)
    e = np.zeros((LANES, n_pairs * LANES), np.float32)
    for p in range(n_pairs):
        e[chan_even0 + p, p * LANES:p * LANES + HALF] = 1.0
        e[chan_odd0 + p, p * LANES + HALF:(p + 1) * LANES] = 1.0
    e3 = np.concatenate([e] * 3, axis=0)
    return jnp.asarray(tril3, BF16), jnp.asarray(upper3, BF16), jnp.asarray(e3, BF16)


def _levels(lr):
    return max(1, int(np.ceil(np.log2(lr))))


STEP_ROWS = 16
DECODE_TOKENS_SINGLE_TERM = 4


GDN_PAIRS = GDN_V_HEADS // 2
GDN_PAIR_BATCH = 8
GDN_FILL_SHARES = 12 * (GDN_PAIRS // GDN_PAIR_BATCH)


def _gdn_gates(raw, alog_row, dtb_row, lr):
    row, lane = _iotas((CHUNK, LANES))
    g = -jnp.exp(alog_row) * _softplus(raw + dtb_row)
    beta = jax.nn.sigmoid(raw)
    gt = jnp.where(lane < GDN_V_HEADS, g, jnp.where(lane < 2 * GDN_V_HEADS, beta, 0.0))
    if lr < CHUNK:
        gt = jnp.where(row < lr, gt, 0.0)
    return gt


def _gdn_block(insts, tril3_ref, upper3_ref, e3_ref, lr, fill=_NO_FILL):
    np_ = GDN_PAIRS
    rows = insts[0][0].shape[0]
    row, lane = _iotas((rows, LANES))
    jl = jnp.bitwise_and(lane, HALF - 1)
    left = lane < HALF
    causal = row >= jl
    strict = row > jl
    eye2 = jnp.where(row == jl, 1.0, 0.0)
    r2, l2 = _iotas((LANES, LANES))
    bd_ones = jnp.where(jnp.right_shift(r2, 6) == jnp.right_shift(l2, 6), 1.0, 0.0).astype(BF16)
    _, lane_p = _iotas((np_, LANES))
    sl = lambda a, i: a[:, i * LANES:(i + 1) * LANES]

    pre = []
    for _, gt, _, _, _ in insts:
        t1 = _pad_t(gt)
        cum_ext = _dot(_cat3(t1[0:2 * np_], 1), upper3_ref[...])
        cum_t, last_b = cum_ext[:, :LANES], cum_ext[:, LANES:]
        cum_rp = _pair_rows(cum_t, np_)
        beta_rp = _pair_rows(t1[2 * np_:4 * np_], np_)
        last_rp = jnp.where(lane_p < HALF, last_b[0:np_], last_b[np_:2 * np_])
        cum = _dot(tril3_ref[...], _cat3(gt, 0))
        pre.append(dict(cum_rp=cum_rp, beta_rp=beta_rp, ecum_rp=jnp.exp(cum_rp),
                        kdec_rp=jnp.exp(last_rp - cum_rp) * beta_rp, elast=jnp.exp(last_b),
                        col_all=_dot(_cat3(cum[0:rows], 1), e3_ref[...])))

    def run(items):
        heads = [(i, 2 * p + hh) for i, p in items for hh in range(2)]
        act = lambda i: insts[i][0]
        prow = lambda name, it: pre[it[0]][name][it[1]:it[1] + 1]
        q = {it: sl(act(it[0]), it[1]) for it in items}
        k = {it: _pad_rows(sl(act(it[0]), GDN_QK_HEADS + it[1])) for it in items}
        v = {ih: sl(act(ih[0]), 2 * GDN_QK_HEADS + ih[1]) for ih in heads}
        kb = {it: k[it].astype(BF16) for it in items}
        qb = {it: q[it].astype(BF16) for it in items}
        gq = {it: _dot_nt(jnp.concatenate([kb[it][0:rows], qb[it]], axis=0), jnp.concatenate([kb[it], kb[it]], axis=0))
              for it in items}
        fill.emit()
        colc = {it: sl(pre[it[0]]["col_all"], it[1]) for it in items}
        base = {it: jnp.exp(jnp.where(causal, colc[it] - prow("cum_rp", it), NEG_BIG)) * prow("beta_rp", it)
                for it in items}
        x = [jnp.where(strict, -(gq[it][0:rows] * base[it]), 0.0) for it in items]
        qkd = {it: (gq[it][rows:2 * rows] * base[it]).astype(BF16) for it in items}
        if lr == CHUNK:
            minv = dict(zip(items, _unit_lower_inverse_blocked(x, fill)))
        else:
            minv = dict(zip(items, _unit_lower_inverse(x, _levels(lr), lambda a: _block_diag(a, bd_ones), eye2, fill,
                                                       terms=1 if lr <= DECODE_TOKENS_SINGLE_TERM else 3)))
        fill.emit()
        pair_of = lambda ih: (ih[0], ih[1] // 2)
        u = {it: jnp.concatenate([v[(it[0], 2 * it[1])], v[(it[0], 2 * it[1] + 1)]], axis=1)
             + _mm_hl(minv[it] - eye2, _two_blocks(v[(it[0], 2 * it[1])], v[(it[0], 2 * it[1] + 1)])) for it in items}
        fill.emit()
        w = {it: _mm_hl(minv[it] * prow("ecum_rp", it), _two_blocks(kb[it], kb[it])) for it in items}
        fill.emit()
        s_old = {ih: insts[ih[0]][2][ih[1]] for ih in heads}
        r = {ih: _dot(jnp.concatenate([sl(w[pair_of(ih)], ih[1] % 2).astype(BF16), qb[pair_of(ih)]], axis=0),
                      s_old[ih].astype(BF16)) for ih in heads}
        fill.emit()
        delta = {ih: sl(u[pair_of(ih)], ih[1] % 2) - r[ih][0:rows] for ih in heads}
        bd_delta = {it: _two_blocks(delta[(it[0], 2 * it[1])], delta[(it[0], 2 * it[1] + 1)]) for it in items}
        fill.emit()
        od = {it: _dot(jnp.concatenate(
            [qkd[it], (jnp.concatenate([k[it], k[it]], axis=0).T * prow("kdec_rp", it)).astype(BF16)], axis=0),
            bd_delta[it]) for it in items}
        for it in items:
            i, p = it
            ecol = jnp.exp(colc[it])
            ecol_r = pltpu.roll(ecol, HALF, 1)
            efull = (jnp.where(left, ecol, ecol_r), jnp.where(left, ecol_r, ecol))
            for hh in range(2):
                h = 2 * p + hh
                insts[i][4][:, h * LANES:(h + 1) * LANES] = (efull[hh] * r[(i, h)][rows:2 * rows]
                                                             + sl(od[it][0:rows], hh))
                e_h = pre[i]["elast"][hh * np_ + p:hh * np_ + p + 1]
                insts[i][3][h] = e_h * s_old[(i, h)] + sl(od[it][rows:rows + LANES], hh)

    for b0 in range(0, np_, GDN_PAIR_BATCH):
        run([(i, p) for i in range(len(insts)) for p in range(b0, b0 + GDN_PAIR_BATCH)])
    fill.flush()


def _gdn_seq_kernel(raw0_ref, rawn_ref, graw_ref, shift_ref, cw_ref, cb_ref, alog_ref, dtb_ref, tril3_ref, upper3_ref,
                    e3_ref, o_ref, s_ref, ext_ref, act_ref):
    n = pl.program_id(1)
    seqs = range(graw_ref.shape[0])
    conv = functools.partial(_conv_block, shift_ref=shift_ref, w_ref=cw_ref, b_ref=cb_ref,
                             l2_cols=2 * GDN_QK_DIM, q_cols=GDN_QK_DIM, qscale=QSCALE)

    @pl.when(n == 0)
    def _():
        s_ref[...] = jnp.zeros(s_ref.shape, F32)
        for j in seqs:
            _Filler(conv([raw0_ref.at[j]], ext_ref=ext_ref.at[j], act_ref=act_ref.at[j, 0], seq_start=True), 1).flush()

    @pl.when(n > 0)
    def _():
        for j in seqs:
            act_ref[j, 0] = act_ref[j, 1]

    tasks, insts = [], []
    for j in seqs:
        gt = _gdn_gates(graw_ref[j], alog_ref[...], dtb_ref[...], CHUNK)
        tasks += conv([rawn_ref.at[j]], ext_ref=ext_ref.at[j], act_ref=act_ref.at[j, 1], seq_start=False)
        insts.append((act_ref.at[j, 0], gt, s_ref.at[j], s_ref.at[j], o_ref.at[j]))
    _gdn_block(insts, tril3_ref, upper3_ref, e3_ref, CHUNK, _Filler(tasks, GDN_FILL_SHARES))


def _const_specs(consts, ngrid):
    zero = (lambda *_: (0, 0))
    return [pl.BlockSpec(c.shape, zero) for c in consts]


GDN_SEQ_TOGETHER = 2
SSD_SEQ_TOGETHER = 1


def _seq_views(nseq, seq_len, together):
    together = min(together, nseq)
    assert nseq % together == 0
    nc = seq_len // CHUNK
    per = nseq // together
    last = per * nc - 1
    spec = lambda width, idx, col=0: pl.BlockSpec((together, CHUNK, width), lambda b, n: (0, idx(b, n), col))
    cur = lambda b, n: b * nc + n
    first = lambda b, n: b * nc
    nxt = lambda b, n: jnp.minimum(b * nc + n + 1, last)
    view = lambda a: a.reshape(together, a.shape[0] // together, a.shape[1])
    return together, nc, per, spec, cur, first, nxt, view


def _gdn_seq(pm, graw, conv_w, conv_b, alog_row, dtb_row, consts, nseq, seq_len):
    together, nc, per, spec, cur, first, nxt, view = _seq_views(nseq, seq_len, GDN_SEQ_TOGETHER)
    rowspec = pl.BlockSpec((1, LANES), lambda b, n: (0, 0))
    pm3 = view(pm)
    o, s = pl.pallas_call(
        _gdn_seq_kernel,
        grid=(per, nc),
        in_specs=[
            spec(GDN_CONV_DIM, first),
            spec(GDN_CONV_DIM, nxt),
            spec(LANES, cur),
            pl.BlockSpec(((CONV_WIDTH - 1) * CHUNK, BHALO + CHUNK), lambda b, n: (0, 0)),
            pl.BlockSpec((CONV_WIDTH, GDN_CONV_DIM), lambda b, n: (0, 0)),
            pl.BlockSpec((1, GDN_CONV_DIM), lambda b, n: (0, 0)),
            rowspec, rowspec,
        ] + _const_specs(consts, 2),
        out_specs=[
            spec(GDN_V_DIM, cur),
            pl.BlockSpec((together, None, GDN_V_HEADS, GDN_HEAD_DIM, GDN_HEAD_DIM), lambda b, n: (0, b, 0, 0, 0)),
        ],
        out_shape=[
            jax.ShapeDtypeStruct((together, per * seq_len, GDN_V_DIM), F32),
            jax.ShapeDtypeStruct((together, per, GDN_V_HEADS, GDN_HEAD_DIM, GDN_HEAD_DIM), F32),
        ],
        scratch_shapes=[
            pltpu.VMEM((together, CHUNK + BHALO, GDN_CONV_DIM), BF16),
            pltpu.VMEM((together, 2, CHUNK, GDN_CONV_DIM), F32),
        ],
        compiler_params=_cparams("parallel", "arbitrary"),
        name="gdn_seq",
    )(pm3, pm3, view(graw), _shift_matrix(), conv_w, conv_b.reshape(1, GDN_CONV_DIM), alog_row, dtb_row, *consts)
    return (o.reshape(nseq * seq_len, GDN_V_DIM),
            s.reshape(nseq, GDN_V_HEADS, GDN_HEAD_DIM, GDN_HEAD_DIM))


SEQ_PER_STEP = 8
SEQ_INTERLEAVE = 4


def _load_padded(src_ref, s, pad_ref, lr):
    pad_ref[...] = jnp.zeros(pad_ref.shape, F32)
    for t in range(lr):
        pad_ref[t:t + 1, :] = src_ref[t, pl.ds(s, 1), :]


def _store_tokens(pad_ref, dst_ref, s, lr):
    for t in range(lr):
        dst_ref[t, pl.ds(s, 1), :] = pad_ref[t:t + 1, :]


def _gdn_step_kernel(act_ref, graw_ref, alog_ref, dtb_ref, tril3_ref, upper3_ref, e3_ref, s0_ref,
                     o_ref, s_ref, apad_ref, gpad_ref, opad_ref, *, lr):
    def some_sequences(it, carry):
        insts = []
        for j in range(SEQ_INTERLEAVE):
            s = it * SEQ_INTERLEAVE + j
            _load_padded(act_ref, s, apad_ref.at[j], lr)
            _load_padded(graw_ref, s, gpad_ref.at[j], lr)
            gt = _gdn_gates(gpad_ref[j], alog_ref[...], dtb_ref[...], lr)
            insts.append((apad_ref.at[j], gt, s0_ref.at[s], s_ref.at[s], opad_ref.at[j]))
        _gdn_block(insts, tril3_ref, upper3_ref, e3_ref, lr)
        for j in range(SEQ_INTERLEAVE):
            _store_tokens(opad_ref.at[j], o_ref, it * SEQ_INTERLEAVE + j, lr)
        return carry

    lax.fori_loop(0, SEQ_PER_STEP // SEQ_INTERLEAVE, some_sequences, 0)


def _step_spec(steps, width):
    return pl.BlockSpec((steps, SEQ_PER_STEP, width), lambda g: (0, g, 0))


def _gdn_step(act3, graw3, alog_row, dtb_row, consts, s0):
    steps, nb, _ = act3.shape
    assert nb % SEQ_PER_STEP == 0 and steps <= STEP_ROWS
    rowspec = pl.BlockSpec((1, LANES), lambda b: (0, 0))
    sspec = pl.BlockSpec((SEQ_PER_STEP, GDN_V_HEADS, GDN_HEAD_DIM, GDN_HEAD_DIM), lambda g: (g, 0, 0, 0))
    o, s = pl.pallas_call(
        functools.partial(_gdn_step_kernel, lr=steps),
        grid=(nb // SEQ_PER_STEP,),
        in_specs=[_step_spec(steps, GDN_CONV_DIM), _step_spec(steps, LANES), rowspec, rowspec]
        + _const_specs(consts, 1) + [sspec],
        out_specs=[_step_spec(steps, GDN_V_DIM), sspec],
        out_shape=[
            jax.ShapeDtypeStruct((steps, nb, GDN_V_DIM), F32),
            jax.ShapeDtypeStruct(s0.shape, F32),
        ],
        scratch_shapes=[
            pltpu.VMEM((SEQ_INTERLEAVE, STEP_ROWS, GDN_CONV_DIM), F32),
            pltpu.VMEM((SEQ_INTERLEAVE, CHUNK, LANES), F32),
            pltpu.VMEM((SEQ_INTERLEAVE, STEP_ROWS, GDN_V_DIM), F32),
        ],
        compiler_params=_cparams("parallel"),
        name="gdn_step",
    )(act3, graw3, alog_row, dtb_row, *consts, s0)
    return o.reshape(steps * nb, GDN_V_DIM), s


SSM_PAIRS = SSM_HEADS // 2
PAIRS_PER_GROUP = SSM_PAIRS // SSM_GROUPS


def _ssm_gates(raw, alog_row, dtb_row, lr):
    row, lane = _iotas((CHUNK, LANES))
    dt = _softplus(raw + dtb_row)
    tile = jnp.where(lane < SSM_HEADS, dt, jnp.where(lane < 2 * SSM_HEADS, -jnp.exp(alog_row) * dt, 0.0))
    if lr < CHUNK:
        tile = jnp.where(row < lr, tile, 0.0)
    return tile


def _ssd_block(insts, dskip_ref, tril3_ref, upper3_ref, e3_ref, fill=_NO_FILL):
    np_ = SSM_PAIRS
    rows = insts[0][0].shape[0]
    row, lane = _iotas((rows, LANES))
    causal = row >= jnp.bitwise_and(lane, HALF - 1)
    r2, l2 = _iotas((LANES, LANES))
    bdmask = jnp.right_shift(r2, 6) == jnp.right_shift(l2, 6)
    top = r2 < HALF
    _, lane_p = _iotas((np_, LANES))
    sl = lambda a, i: a[:, i * LANES:(i + 1) * LANES]
    grp = lambda it: (it[0], it[1] // PAIRS_PER_GROUP)

    pre = []
    for _, tile, _, _, _ in insts:
        t1 = _pad_t(tile)
        cum_ext = _dot(_cat3(t1[2 * np_:4 * np_], 1), upper3_ref[...])
        cum_t, last_b = cum_ext[:, :LANES], cum_ext[:, LANES:]
        cum_rp = _pair_rows(cum_t, np_)
        dt_rp = _pair_rows(t1[0:2 * np_], np_)
        last_rp = jnp.where(lane_p < HALF, last_b[0:np_], last_b[np_:2 * np_])
        cum = _dot(tril3_ref[...], _cat3(tile, 0))
        pre.append(dict(cum_rp=cum_rp, dt_rp=dt_rp, coef_rp=jnp.exp(last_rp - cum_rp) * dt_rp,
                        elast=jnp.exp(last_b), col_all=_dot(_cat3(cum[0:rows], 1), e3_ref[...])))

    seqs = range(len(insts))
    items = [(i, p) for i in seqs for p in range(np_)]
    groups = [(i, g) for i in seqs for g in range(SSM_GROUPS)]
    act = lambda i: insts[i][0]
    prow = lambda name, it: pre[it[0]][name][it[1]:it[1] + 1]
    bg = {ig: _pad_rows(sl(act(ig[0]), SSM_D_INNER // LANES + ig[1])).astype(BF16) for ig in groups}
    cg = {ig: sl(act(ig[0]), (SSM_D_INNER + SSM_BC) // LANES + ig[1]).astype(BF16) for ig in groups}
    bb = {ig: jnp.concatenate([bg[ig], bg[ig]], axis=0) for ig in groups}
    cb2 = {ig: _dot_nt(cg[ig], bb[ig]) for ig in groups}
    fill.emit()
    xp = {it: sl(act(it[0]), it[1]) for it in items}
    colc = {it: sl(pre[it[0]]["col_all"], it[1]) for it in items}
    x2 = {it: jnp.concatenate([_pad_rows(xp[it]), _pad_rows(xp[it])], axis=0) for it in items}
    lm = {it: cb2[grp(it)] * jnp.exp(jnp.where(causal, colc[it] - prow("cum_rp", it), NEG_BIG)) * prow("dt_rp", it)
          for it in items}
    y_diag = {}
    for n, it in enumerate(items):
        y_diag[it] = _dot(lm[it].astype(BF16), jnp.where(bdmask, x2[it], 0.0).astype(BF16))
        if n % (4 * len(insts)) == 4 * len(insts) - 1:
            fill.emit()
    hp = {it: insts[it[0]][2][it[1]] for it in items}
    y_off = {}
    for n, it in enumerate(items):
        y_off[it] = _dot_nt(cg[grp(it)], hp[it].astype(BF16))
        if n % (4 * len(insts)) == 4 * len(insts) - 1:
            fill.emit()
    for it in items:
        i, p = it
        insts[i][4][:, p * LANES:(p + 1) * LANES] = (y_diag[it] + jnp.exp(colc[it]) * y_off[it]
                                                     + dskip_ref[:, p * LANES:(p + 1) * LANES] * xp[it])
    dh = {}
    for n, it in enumerate(items):
        lhs = jnp.where(bdmask, x2[it].T * prow("coef_rp", it), 0.0)
        dh[it] = _dot(lhs.astype(BF16), bb[grp(it)])
        if n % (4 * len(insts)) == 4 * len(insts) - 1:
            fill.emit()
    for it in items:
        i, p = it
        elast = pre[i]["elast"]
        e_rows = jnp.where(top, elast[p:p + 1], elast[np_ + p:np_ + p + 1])
        insts[i][3][p] = e_rows * hp[it] + dh[it]
    fill.flush()


def _ssd_seq_kernel(x0_ref, bc0_ref, xn_ref, bcn_ref, graw_ref, shift_ref, cw_ref, cb_ref, alog_ref, dtb_ref,
                    dskip_ref, tril3_ref, upper3_ref, e3_ref, y_ref, h_ref, ext_ref, act_ref):
    n = pl.program_id(1)
    seqs = range(graw_ref.shape[0])
    conv = functools.partial(_conv_block, shift_ref=shift_ref, w_ref=cw_ref, b_ref=cb_ref,
                             l2_cols=0, q_cols=0, qscale=1.0)

    @pl.when(n == 0)
    def _():
        h_ref[...] = jnp.zeros(h_ref.shape, F32)
        for j in seqs:
            _Filler(conv([x0_ref.at[j], bc0_ref.at[j]], ext_ref=ext_ref.at[j], act_ref=act_ref.at[j, 0],
                         seq_start=True), 1).flush()

    @pl.when(n > 0)
    def _():
        for j in seqs:
            act_ref[j, 0] = act_ref[j, 1]

    tasks, insts = [], []
    for j in seqs:
        tile = _ssm_gates(graw_ref[j], alog_ref[...], dtb_ref[...], CHUNK)
        tasks += conv([xn_ref.at[j], bcn_ref.at[j]], ext_ref=ext_ref.at[j], act_ref=act_ref.at[j, 1], seq_start=False)
        insts.append((act_ref.at[j, 0], tile, h_ref.at[j], h_ref.at[j], y_ref.at[j]))
    _ssd_block(insts, dskip_ref, tril3_ref, upper3_ref, e3_ref, _Filler(tasks, SSD_FILL_SHARES))


SSD_FILL_SHARES = 13


def _ssd_seq(pm, graw, conv_w, conv_b, alog_row, dtb_row, dskip_row, consts, nseq, seq_len):
    together, nc, per, spec, cur, first, nxt, view = _seq_views(nseq, seq_len, SSD_SEQ_TOGETHER)
    bc_blk = 2 * SSM_D_INNER // (2 * SSM_BC)
    rowspec = pl.BlockSpec((1, LANES), lambda b, n: (0, 0))
    pm3 = view(pm)
    y, h = pl.pallas_call(
        _ssd_seq_kernel,
        grid=(per, nc),
        in_specs=[
            spec(SSM_D_INNER, first),
            spec(2 * SSM_BC, first, bc_blk),
            spec(SSM_D_INNER, nxt),
            spec(2 * SSM_BC, nxt, bc_blk),
            spec(LANES, cur),
            pl.BlockSpec(((CONV_WIDTH - 1) * CHUNK, BHALO + CHUNK), lambda b, n: (0, 0)),
            pl.BlockSpec((CONV_WIDTH, SSM_CONV_DIM), lambda b, n: (0, 0)),
            pl.BlockSpec((1, SSM_CONV_DIM), lambda b, n: (0, 0)),
            rowspec, rowspec,
            pl.BlockSpec((1, SSM_D_INNER), lambda b, n: (0, 0)),
        ] + _const_specs(consts, 2),
        out_specs=[
            spec(SSM_D_INNER, cur),
            pl.BlockSpec((together, None, SSM_PAIRS, LANES, SSM_STATE), lambda b, n: (0, b, 0, 0, 0)),
        ],
        out_shape=[
            jax.ShapeDtypeStruct((together, per * seq_len, SSM_D_INNER), F32),
            jax.ShapeDtypeStruct((together, per, SSM_PAIRS, LANES, SSM_STATE), F32),
        ],
        scratch_shapes=[
            pltpu.VMEM((together, CHUNK + BHALO, SSM_CONV_DIM), BF16),
            pltpu.VMEM((together, 2, CHUNK, SSM_CONV_DIM), F32),
        ],
        compiler_params=_cparams("parallel", "arbitrary"),
        name="ssd_seq",
    )(pm3, pm3, pm3, pm3, view(graw), _shift_matrix(), conv_w, conv_b.reshape(1, SSM_CONV_DIM), alog_row, dtb_row,
      dskip_row, *consts)
    return y.reshape(nseq * seq_len, SSM_D_INNER), h.reshape(nseq, SSM_PAIRS, LANES, SSM_STATE)


def _ssd_step_kernel(act_ref, graw_ref, alog_ref, dtb_ref, dskip_ref, tril3_ref, upper3_ref, e3_ref, h0_ref,
                     y_ref, h_ref, apad_ref, gpad_ref, ypad_ref, *, lr):
    def some_sequences(it, carry):
        insts = []
        for j in range(SEQ_INTERLEAVE):
            s = it * SEQ_INTERLEAVE + j
            _load_padded(act_ref, s, apad_ref.at[j], lr)
            _load_padded(graw_ref, s, gpad_ref.at[j], lr)
            tile = _ssm_gates(gpad_ref[j], alog_ref[...], dtb_ref[...], lr)
            insts.append((apad_ref.at[j], tile, h0_ref.at[s], h_ref.at[s], ypad_ref.at[j]))
        _ssd_block(insts, dskip_ref, tril3_ref, upper3_ref, e3_ref)
        for j in range(SEQ_INTERLEAVE):
            _store_tokens(ypad_ref.at[j], y_ref, it * SEQ_INTERLEAVE + j, lr)
        return carry

    lax.fori_loop(0, SEQ_PER_STEP // SEQ_INTERLEAVE, some_sequences, 0)


def _ssd_step(act3, graw3, alog_row, dtb_row, dskip_row, consts, h0):
    steps, nb, _ = act3.shape
    assert nb % SEQ_PER_STEP == 0 and steps <= STEP_ROWS
    rowspec = pl.BlockSpec((1, LANES), lambda b: (0, 0))
    hspec = pl.BlockSpec((SEQ_PER_STEP, SSM_PAIRS, LANES, SSM_STATE), lambda g: (g, 0, 0, 0))
    y, h = pl.pallas_call(
        functools.partial(_ssd_step_kernel, lr=steps),
        grid=(nb // SEQ_PER_STEP,),
        in_specs=[_step_spec(steps, SSM_CONV_DIM), _step_spec(steps, LANES), rowspec, rowspec,
                  pl.BlockSpec((1, SSM_D_INNER), lambda b: (0, 0))] + _const_specs(consts, 1) + [hspec],
        out_specs=[_step_spec(steps, SSM_D_INNER), hspec],
        out_shape=[
            jax.ShapeDtypeStruct((steps, nb, SSM_D_INNER), F32),
            jax.ShapeDtypeStruct(h0.shape, F32),
        ],
        scratch_shapes=[
            pltpu.VMEM((SEQ_INTERLEAVE, STEP_ROWS, SSM_CONV_DIM), F32),
            pltpu.VMEM((SEQ_INTERLEAVE, CHUNK, LANES), F32),
            pltpu.VMEM((SEQ_INTERLEAVE, STEP_ROWS, SSM_D_INNER), F32),
        ],
        compiler_params=_cparams("parallel"),
        name="ssd_step",
    )(act3, graw3, alog_row, dtb_row, dskip_row, *consts, h0)
    return y.reshape(steps * nb, SSM_D_INNER), h


FFN_TILE = FFN_HIDDEN // 2
GDN_PROJ_TILE = GDN_MAIN // 3
SSM_PROJ_TILE = SSM_MAIN // 2


def _lane_row(pieces):
    row = jnp.zeros((1, LANES), F32)
    for off, vec in pieces:
        row = row.at[0, off:off + vec.shape[0]].set(vec.astype(F32))
    return row


def _stage_params(w_mod, b_mod, norm_mix, norm_ffn, norm_final, gdn_w_in, gdn_conv_w, gdn_a_log, gdn_dt_bias,
                  gdn_norm, gdn_w_out, ssm_w_in, ssm_conv_w, ssm_conv_b, ssm_a_log, ssm_dt_bias, ssm_d, ssm_norm,
                  ssm_w_out, ffn_w_gate_up, ffn_w_down):
    perm_g = np.concatenate([np.arange(0, GDN_V_HEADS, 2), np.arange(1, GDN_V_HEADS, 2)])
    perm_s = np.concatenate([np.arange(0, SSM_HEADS, 2), np.arange(1, SSM_HEADS, 2)])
    g_in, s_in = gdn_w_in[0], ssm_w_in[0]
    beta_cols = g_in[:, GDN_MAIN:GDN_MAIN + GDN_V_HEADS][:, perm_g]
    a_cols = g_in[:, GDN_MAIN + GDN_V_HEADS:GDN_MAIN + 2 * GDN_V_HEADS][:, perm_g]
    gdn_small = jnp.concatenate([a_cols, beta_cols, jnp.zeros((D_MODEL, LANES - 2 * GDN_V_HEADS), F32)], axis=1)
    dt_cols = s_in[:, SSM_MAIN:SSM_MAIN + SSM_HEADS][:, perm_s]
    ssm_small = jnp.concatenate([dt_cols, dt_cols, jnp.zeros((D_MODEL, LANES - 2 * SSM_HEADS), F32)], axis=1)
    return dict(
        w_mod=w_mod, b_mod=b_mod, norm_mix=norm_mix, norm_ffn=norm_ffn, norm_final=norm_final,
        gdn_main=g_in[:, :GDN_MAIN].astype(BF16), gdn_small=gdn_small.astype(BF16),
        gdn_conv_w=gdn_conv_w[0], gdn_conv_b=jnp.zeros((GDN_CONV_DIM,), F32),
        gdn_alog_row=_lane_row([(0, gdn_a_log[0][perm_g])]), gdn_dtb_row=_lane_row([(0, gdn_dt_bias[0][perm_g])]),
        gdn_norm=jnp.tile(gdn_norm[0], GDN_V_HEADS), gdn_w_out=gdn_w_out[0].astype(BF16),
        gdn_consts=_recurrence_consts(GDN_PAIRS, 0, GDN_PAIRS),
        ssm_main=jnp.concatenate(
            [s_in[:, SSM_D_INNER:2 * SSM_D_INNER], s_in[:, :SSM_D_INNER], s_in[:, 2 * SSM_D_INNER:SSM_MAIN]],
            axis=1).astype(BF16),
        ssm_small=ssm_small.astype(BF16),
        ssm_conv_w=ssm_conv_w[0], ssm_conv_b=ssm_conv_b[0],
        ssm_alog_row=_lane_row([(SSM_HEADS, ssm_a_log[0][perm_s])]),
        ssm_dtb_row=_lane_row([(0, ssm_dt_bias[0][perm_s]), (SSM_HEADS, ssm_dt_bias[0][perm_s])]),
        ssm_dskip_row=jnp.repeat(ssm_d[0], SSM_HEAD_DIM).reshape(1, SSM_D_INNER),
        ssm_norm=ssm_norm[0], ssm_w_out=ssm_w_out[0].astype(BF16),
        ssm_consts=_recurrence_consts(SSM_PAIRS, SSM_HEADS, SSM_HEADS + SSM_PAIRS),
        wg=[ffn_w_gate_up[i][:, :FFN_HIDDEN].astype(BF16) for i in range(2)],
        wu=[ffn_w_gate_up[i][:, FFN_HIDDEN:].astype(BF16) for i in range(2)],
        wd=[ffn_w_down[i].astype(BF16) for i in range(2)],
    )


def _ffn(x, layer, mod3, rows_up, rows_down, p, final_w):
    act = _ffn_up(x, p["norm_ffn"][layer], mod3, rows_up, p["wg"][layer], p["wu"][layer], FFN_TILE)
    return _ffn_down(act, x, mod3, rows_down, p["wd"][layer], final_w)


QSCALE = GDN_HEAD_DIM ** -0.5


def _trunk_seq(x3, mod, p):
    nseq, seq_len, _ = x3.shape
    m = nseq * seq_len
    x = x3.reshape(m, D_MODEL)
    mod3 = [mod[l].reshape(nseq, 1, 6 * D_MODEL) for l in range(2)]
    rows_a = _Rows(m, min(1024, seq_len), seq_len, 1)
    rows_b = _Rows(m, min(512, seq_len), seq_len, 1)

    pm, ps = _in_proj(x, p["norm_mix"][0], mod3[0], rows_a, 1, 0, p["gdn_main"], p["gdn_small"], GDN_PROJ_TILE,
                      (GDN_CONV_DIM, GDN_MAIN))
    o, gdn_s = _gdn_seq(pm, ps, p["gdn_conv_w"], p["gdn_conv_b"], p["gdn_alog_row"], p["gdn_dtb_row"],
                        p["gdn_consts"], nseq, seq_len)
    tail = pm.reshape(nseq, seq_len, GDN_MAIN)[:, seq_len - (CONV_WIDTH - 1):].astype(F32)
    gdn_c = tail[..., :GDN_CONV_DIM]
    x = _mixer_out(o, pm, 2, p["gdn_norm"], x, mod3[0], rows_b, p["gdn_w_out"], GDN_HEAD_DIM, False)
    x = _ffn(x, 0, mod3[0], rows_a, rows_b, p, None)

    pm, ps = _in_proj(x, p["norm_mix"][1], mod3[1], rows_a, 1, 0, p["ssm_main"], p["ssm_small"], SSM_PROJ_TILE,
                      (SSM_D_INNER, 2 * SSM_D_INNER))
    y, ssm_h = _ssd_seq(pm, ps, p["ssm_conv_w"], p["ssm_conv_b"], p["ssm_alog_row"], p["ssm_dtb_row"],
                        p["ssm_dskip_row"], p["ssm_consts"], nseq, seq_len)
    tail = pm.reshape(nseq, seq_len, SSM_MAIN)[:, seq_len - (CONV_WIDTH - 1):].astype(F32)
    ssm_c = jnp.concatenate([tail[..., :SSM_D_INNER], tail[..., 2 * SSM_D_INNER:]], axis=-1)
    x = _mixer_out(y, pm, 1, p["ssm_norm"], x, mod3[1], rows_b, p["ssm_w_out"], SSM_D_INNER // SSM_GROUPS, True)
    _, y_out = _ffn(x, 1, mod3[1], rows_a, rows_b, p, p["norm_final"])

    return (y_out.reshape(nseq, seq_len, D_MODEL), gdn_s[None], gdn_c[None],
            ssm_h.reshape(nseq, SSM_HEADS, SSM_HEAD_DIM, SSM_STATE)[None], ssm_c[None])


def _trunk_step(x3, mod, st_gdn, cv_gdn, st_ssm, cv_ssm, p):
    nb, steps, _ = x3.shape
    assert steps >= CONV_WIDTH - 1
    m = nb * steps
    x = jnp.transpose(x3, (1, 0, 2)).reshape(m, D_MODEL)
    mod3 = [mod[l].reshape(1, nb, 6 * D_MODEL) for l in range(2)]
    rows = _Rows(m, m, None, nb)
    tok = lambda a: jnp.transpose(a, (1, 0, 2))

    pm, ps = _in_proj(x, p["norm_mix"][0], mod3[0], rows, 1, 0, p["gdn_main"], p["gdn_small"], GDN_PROJ_TILE,
                      (GDN_CONV_DIM, GDN_MAIN))
    u3 = pm.reshape(steps, nb, GDN_MAIN)
    act3 = _conv_steps(u3, 0, 0, GDN_CONV_DIM, tok(cv_gdn[0]), p["gdn_conv_w"], p["gdn_conv_b"], 2, QSCALE)
    o, gdn_s = _gdn_step(act3, ps.reshape(steps, nb, LANES), p["gdn_alog_row"], p["gdn_dtb_row"], p["gdn_consts"],
                         st_gdn[0])
    gdn_c = tok(u3[steps - (CONV_WIDTH - 1):, :, :GDN_CONV_DIM].astype(F32))
    x = _mixer_out(o, pm, 2, p["gdn_norm"], x, mod3[0], rows, p["gdn_w_out"], GDN_HEAD_DIM, False)
    x = _ffn(x, 0, mod3[0], rows, rows, p, None)

    pm, ps = _in_proj(x, p["norm_mix"][1], mod3[1], rows, 1, 0, p["ssm_main"], p["ssm_small"], SSM_PROJ_TILE,
                      (SSM_D_INNER, 2 * SSM_D_INNER))
    u3 = pm.reshape(steps, nb, SSM_MAIN)
    act3 = _conv_steps(u3, SSM_D_INNER // CONV_COLS, SSM_D_INNER // CONV_COLS, SSM_CONV_DIM, tok(cv_ssm[0]),
                       p["ssm_conv_w"], p["ssm_conv_b"], 0, 1.0)
    h0 = st_ssm[0].reshape(nb, SSM_PAIRS, LANES, SSM_STATE)
    y, ssm_h = _ssd_step(act3, ps.reshape(steps, nb, LANES), p["ssm_alog_row"], p["ssm_dtb_row"],
                         p["ssm_dskip_row"], p["ssm_consts"], h0)
    tail = u3[steps - (CONV_WIDTH - 1):].astype(F32)
    ssm_c = tok(jnp.concatenate([tail[..., :SSM_D_INNER], tail[..., 2 * SSM_D_INNER:]], axis=-1))
    x = _mixer_out(y, pm, 1, p["ssm_norm"], x, mod3[1], rows, p["ssm_w_out"], SSM_D_INNER // SSM_GROUPS, True)
    _, y_out = _ffn(x, 1, mod3[1], rows, rows, p, p["norm_final"])

    return (tok(y_out.reshape(steps, nb, D_MODEL)), gdn_s[None], gdn_c[None],
            ssm_h.reshape(nb, SSM_HEADS, SSM_HEAD_DIM, SSM_STATE)[None], ssm_c[None])


def kernel(x_prompt, x_sample, c_prompt, c_sample, state_gdn, state_gdn_conv, state_ssm, state_ssm_conv, w_mod, b_mod,
           norm_mix, norm_ffn, norm_final, gdn_w_in, gdn_conv_w, gdn_a_log, gdn_dt_bias, gdn_norm, gdn_w_out, ssm_w_in,
           ssm_conv_w, ssm_conv_b, ssm_a_log, ssm_dt_bias, ssm_d, ssm_norm, ssm_w_out, ffn_w_gate_up, ffn_w_down):
    p = _stage_params(w_mod, b_mod, norm_mix, norm_ffn, norm_final, gdn_w_in, gdn_conv_w, gdn_a_log, gdn_dt_bias,
                      gdn_norm, gdn_w_out, ssm_w_in, ssm_conv_w, ssm_conv_b, ssm_a_log, ssm_dt_bias, ssm_d, ssm_norm,
                      ssm_w_out, ffn_w_gate_up, ffn_w_down)
    n_prompt = x_prompt.shape[0]
    mod = _modulation(jnp.concatenate([c_prompt, c_sample], axis=0), p["w_mod"], p["b_mod"])
    y_p, gs_p, gc_p, ss_p, sc_p = _trunk_seq(x_prompt, mod[:, :n_prompt], p)
    y_s, gs_s, gc_s, ss_s, sc_s = _trunk_step(x_sample, mod[:, n_prompt:], state_gdn, state_gdn_conv, state_ssm,
                                              state_ssm_conv, p)
    return (y_p, y_s, gs_p, gc_p, ss_p, sc_p, gs_s, gc_s, ss_s, sc_s)
```

```python
import functools

import numpy as np
import jax
import jax.numpy as jnp
from jax import lax
from jax.experimental import pallas as pl
from jax.experimental.pallas import tpu as pltpu

F32 = jnp.float32
BF16 = jnp.bfloat16

D_MODEL = 1024
EPS = 1e-6
CONV_WIDTH = 4
CHUNK = 64
LANES = 128
HALF = LANES // 2

GDN_QK_HEADS = 8
GDN_V_HEADS = 16
GDN_HEAD_DIM = 128
GDN_QK_DIM = GDN_QK_HEADS * GDN_HEAD_DIM
GDN_V_DIM = GDN_V_HEADS * GDN_HEAD_DIM
GDN_CONV_DIM = 2 * GDN_QK_DIM + GDN_V_DIM
GDN_MAIN = GDN_CONV_DIM + GDN_V_DIM

SSM_D_INNER = 2 * D_MODEL
SSM_HEAD_DIM = 64
SSM_HEADS = SSM_D_INNER // SSM_HEAD_DIM
SSM_GROUPS = 4
SSM_STATE = 128
SSM_BC = SSM_GROUPS * SSM_STATE
SSM_CONV_DIM = SSM_D_INNER + 2 * SSM_BC
SSM_MAIN = SSM_D_INNER + SSM_CONV_DIM

FFN_HIDDEN = 2816

VMEM_LIMIT = 56 * 1024 * 1024
NEG_BIG = -1e30


def _cparams(*sem):
    return pltpu.CompilerParams(dimension_semantics=sem, vmem_limit_bytes=VMEM_LIMIT)


def _silu(x):
    hx = 0.5 * x
    return hx + hx * jnp.tanh(hx)


def _softplus(x):
    return jnp.maximum(x, 0.0) + jnp.log1p(jnp.exp(-jnp.abs(x)))


def _dot(a, b):
    return jnp.dot(a, b, preferred_element_type=F32)


def _dot_nt(a, b):
    return lax.dot_general(a, b, (((1,), (1,)), ((), ())), preferred_element_type=F32)


def _tile_rows(v, rep):
    return v if rep == 1 else jnp.concatenate([v] * rep, axis=0)


def _mod_kernel(c_ref, w_ref, b_ref, o_ref):
    cs = _silu(c_ref[...]).astype(BF16)
    o_ref[...] = _dot(cs, w_ref[...].astype(BF16)) + b_ref[...]


def _modulation(c, w_mod, b_mod):
    depth, _, n = w_mod.shape
    bc = c.shape[0]
    tn = 1536
    return pl.pallas_call(
        _mod_kernel,
        grid=(depth, n // tn),
        in_specs=[
            pl.BlockSpec((bc, D_MODEL), lambda l, j: (0, 0)),
            pl.BlockSpec((None, D_MODEL, tn), lambda l, j: (l, 0, j)),
            pl.BlockSpec((None, 1, tn), lambda l, j: (l, 0, j)),
        ],
        out_specs=pl.BlockSpec((None, bc, tn), lambda l, j: (l, 0, j)),
        out_shape=jax.ShapeDtypeStruct((depth, bc, n), F32),
        compiler_params=_cparams("parallel", "parallel"),
        name="adaln_mod",
    )(c, w_mod, b_mod.reshape(depth, 1, n))


def _norm_mod(x, nw, sc, sh, rep):
    y = x * lax.rsqrt(jnp.mean(x * x, axis=-1, keepdims=True) + EPS) * nw
    return y * (1.0 + _tile_rows(sc, rep)) + _tile_rows(sh, rep)


def _in_proj_kernel(x_ref, nw_ref, sc_ref, sh_ref, w_ref, w2_ref, o_ref, o2_ref, h_ref, *, rep, gate_cols):
    j = pl.program_id(1)
    tn = o_ref.shape[1]
    lo, hi = gate_cols

    @pl.when(j == 0)
    def _():
        h = _norm_mod(x_ref[...], nw_ref[...], sc_ref[...], sh_ref[...], rep).astype(BF16)
        h_ref[...] = h
        o2_ref[...] = _dot(h, w2_ref[...])

    has_gate = jnp.logical_and(j * tn < hi, (j + 1) * tn > lo)

    @pl.when(has_gate)
    def _():
        r = _dot(h_ref[...], w_ref[...])
        col = j * tn + lax.broadcasted_iota(jnp.int32, r.shape, 1)
        o_ref[...] = jnp.where(jnp.logical_and(col >= lo, col < hi), _silu(r), r).astype(o_ref.dtype)

    @pl.when(jnp.logical_not(has_gate))
    def _():
        o_ref[...] = _dot(h_ref[...], w_ref[...]).astype(o_ref.dtype)


def _ffn_up_kernel(x_ref, nw_ref, sc_ref, sh_ref, wg_ref, wu_ref, o_ref, h_ref, *, rep):
    @pl.when(pl.program_id(1) == 0)
    def _():
        h_ref[...] = _norm_mod(x_ref[...], nw_ref[...], sc_ref[...], sh_ref[...], rep).astype(BF16)

    h = h_ref[...]
    o_ref[...] = (_silu(_dot(h, wg_ref[...])) * _dot(h, wu_ref[...])).astype(BF16)


class _Rows:
    def __init__(self, m, tm, group_rows, mod_rows):
        assert m % tm == 0
        self.m, self.tm = m, tm
        if mod_rows == 1:
            assert group_rows % tm == 0
            self.rep = 1
            self.gmap = lambda i: (i * tm) // group_rows
        else:
            assert tm % mod_rows == 0
            self.rep = tm // mod_rows
            self.gmap = lambda i: 0
        self.mod_rows = mod_rows

    def mod_spec(self, col_block, with_j):
        if with_j:
            return pl.BlockSpec((None, self.mod_rows, D_MODEL), lambda i, j: (self.gmap(i), 0, col_block))
        return pl.BlockSpec((None, self.mod_rows, D_MODEL), lambda i: (self.gmap(i), 0, col_block))


def _in_proj(x, nw, mod3, rows, sc_blk, sh_blk, w, w2, tn, gate_cols):
    m, tm = rows.m, rows.tm
    n = w.shape[1]
    assert n % tn == 0
    return pl.pallas_call(
        functools.partial(_in_proj_kernel, rep=rows.rep, gate_cols=gate_cols),
        grid=(m // tm, n // tn),
        in_specs=[
            pl.BlockSpec((tm, D_MODEL), lambda i, j: (i, 0)),
            pl.BlockSpec((1, D_MODEL), lambda i, j: (0, 0)),
            rows.mod_spec(sc_blk, True),
            rows.mod_spec(sh_blk, True),
            pl.BlockSpec((D_MODEL, tn), lambda i, j: (0, j)),
            pl.BlockSpec((D_MODEL, LANES), lambda i, j: (0, 0)),
        ],
        out_specs=[
            pl.BlockSpec((tm, tn), lambda i, j: (i, j)),
            pl.BlockSpec((tm, LANES), lambda i, j: (i, 0)),
        ],
        out_shape=[jax.ShapeDtypeStruct((m, n), BF16), jax.ShapeDtypeStruct((m, LANES), F32)],
        scratch_shapes=[pltpu.VMEM((tm, D_MODEL), BF16)],
        compiler_params=_cparams("parallel", "arbitrary"),
        name="in_proj",
    )(x, nw.reshape(1, D_MODEL), mod3, mod3, w, w2)


def _ffn_up(x, nw, mod3, rows, wg, wu, th):
    m, tm = rows.m, rows.tm
    assert FFN_HIDDEN % th == 0
    return pl.pallas_call(
        functools.partial(_ffn_up_kernel, rep=rows.rep),
        grid=(m // tm, FFN_HIDDEN // th),
        in_specs=[
            pl.BlockSpec((tm, D_MODEL), lambda i, j: (i, 0)),
            pl.BlockSpec((1, D_MODEL), lambda i, j: (0, 0)),
            rows.mod_spec(4, True),
            rows.mod_spec(3, True),
            pl.BlockSpec((D_MODEL, th), lambda i, j: (0, j)),
            pl.BlockSpec((D_MODEL, th), lambda i, j: (0, j)),
        ],
        out_specs=pl.BlockSpec((tm, th), lambda i, j: (i, j)),
        out_shape=jax.ShapeDtypeStruct((m, FFN_HIDDEN), BF16),
        scratch_shapes=[pltpu.VMEM((tm, D_MODEL), BF16)],
        compiler_params=_cparams("parallel", "arbitrary"),
        name="ffn_up",
    )(x, nw.reshape(1, D_MODEL), mod3, mod3, wg, wu)


def _resid_store(acc, x_ref, gt_ref, o_ref, fnw_ref, y_ref, rep):
    xn = x_ref[...] + _tile_rows(gt_ref[...], rep) * acc
    o_ref[...] = xn
    if y_ref is not None:
        y_ref[...] = xn * lax.rsqrt(jnp.mean(xn * xn, axis=-1, keepdims=True) + EPS) * fnw_ref[...]


def _ffn_down_kernel(a_ref, x_ref, gt_ref, w_ref, *rest, rep, final):
    if final:
        fnw_ref, o_ref, y_ref = rest
    else:
        (o_ref,), fnw_ref, y_ref = rest, None, None
    _resid_store(_dot(a_ref[...], w_ref[...]), x_ref, gt_ref, o_ref, fnw_ref, y_ref, rep)


def _ffn_down(act, x, mod3, rows, w, fnw):
    m, tm = rows.m, rows.tm
    final = fnw is not None
    in_specs = [
        pl.BlockSpec((tm, FFN_HIDDEN), lambda i: (i, 0)),
        pl.BlockSpec((tm, D_MODEL), lambda i: (i, 0)),
        rows.mod_spec(5, False),
        pl.BlockSpec((FFN_HIDDEN, D_MODEL), lambda i: (0, 0)),
    ]
    args = [act, x, mod3, w]
    row_spec = pl.BlockSpec((tm, D_MODEL), lambda i: (i, 0))
    out_shape = jax.ShapeDtypeStruct((m, D_MODEL), F32)
    if final:
        in_specs.append(pl.BlockSpec((1, D_MODEL), lambda i: (0, 0)))
        args.append(fnw.reshape(1, D_MODEL))
        out_specs, out_shapes = [row_spec, row_spec], [out_shape, out_shape]
    else:
        out_specs, out_shapes = row_spec, out_shape
    return pl.pallas_call(
        functools.partial(_ffn_down_kernel, rep=rows.rep, final=final),
        grid=(m // tm,),
        in_specs=in_specs,
        out_specs=out_specs,
        out_shape=out_shapes,
        compiler_params=_cparams("parallel"),
        name="ffn_down",
    )(*args)


def _mixer_out_kernel(y_ref, z_ref, nw_ref, x_ref, gt_ref, w_ref, o_ref, a_ref, *, rep, group, gate_first):
    width = y_ref.shape[1]
    for s in range(0, width, group):
        y = y_ref[:, s:s + group].astype(F32)
        gate = z_ref[:, s:s + group].astype(F32)
        if gate_first:
            y = y * gate
        y = y * lax.rsqrt(jnp.mean(y * y, axis=-1, keepdims=True) + EPS) * nw_ref[:, s:s + group]
        if not gate_first:
            y = y * gate
        a_ref[:, s:s + group] = y.astype(BF16)
    _resid_store(_dot(a_ref[...], w_ref[...]), x_ref, gt_ref, o_ref, None, None, rep)


def _mixer_out(y, zsrc, z_blk, nw_full, x, mod3, rows, w, group, gate_first):
    m, tm = rows.m, rows.tm
    width = y.shape[1]
    return pl.pallas_call(
        functools.partial(_mixer_out_kernel, rep=rows.rep, group=group, gate_first=gate_first),
        grid=(m // tm,),
        in_specs=[
            pl.BlockSpec((tm, width), lambda i: (i, 0)),
            pl.BlockSpec((tm, width), lambda i: (i, z_blk)),
            pl.BlockSpec((1, width), lambda i: (0, 0)),
            pl.BlockSpec((tm, D_MODEL), lambda i: (i, 0)),
            rows.mod_spec(2, False),
            pl.BlockSpec((width, D_MODEL), lambda i: (0, 0)),
        ],
        out_specs=pl.BlockSpec((tm, D_MODEL), lambda i: (i, 0)),
        out_shape=jax.ShapeDtypeStruct((m, D_MODEL), F32),
        scratch_shapes=[pltpu.VMEM((tm, width), BF16)],
        compiler_params=_cparams("parallel"),
        name="mixer_out",
    )(y, zsrc, nw_full.reshape(1, width), x, mod3, w)


CONV_COLS = 1024


def _post_conv(acc, o_ref, cb, n_l2, qscale):
    y = _silu(acc)
    if n_l2 == 0:
        o_ref[...] = y
        return

    @pl.when(cb < n_l2)
    def _():
        scale = jnp.where(cb == 0, qscale, 1.0).astype(F32)
        for s in range(0, CONV_COLS, GDN_HEAD_DIM):
            yh = y[:, s:s + GDN_HEAD_DIM]
            o_ref[:, s:s + GDN_HEAD_DIM] = yh * lax.rsqrt(jnp.sum(yh * yh, axis=-1, keepdims=True) + EPS) * scale

    @pl.when(cb >= n_l2)
    def _():
        o_ref[...] = y


BHALO = 16
CONV_GROUP = 2 * LANES


def _shift_matrix():
    s = np.zeros(((CONV_WIDTH - 1) * CHUNK, BHALO + CHUNK), np.float32)
    for tap in range(CONV_WIDTH - 1):
        for r in range(CHUNK):
            s[tap * CHUNK + r, BHALO - (CONV_WIDTH - 1) + tap + r] = 1.0
    return jnp.asarray(s, BF16)


def _conv_block(raw_refs, shift_ref, w_ref, b_ref, ext_ref, act_ref, seq_start, l2_cols, q_cols, qscale):
    width = ext_ref.shape[1]
    if seq_start:
        ext_ref[0:BHALO, :] = jnp.zeros((BHALO, width), BF16)
    else:
        ext_ref[0:BHALO, :] = ext_ref[CHUNK:CHUNK + BHALO, :]

    off = 0
    for ref in raw_refs:
        ext_ref[BHALO:BHALO + CHUNK, off:off + ref.shape[1]] = ref[...]
        off += ref.shape[1]

    def lane_group(s):
        cols = slice(s, s + CONV_GROUP)
        sh = _dot(shift_ref[...], ext_ref[:, cols])
        acc = b_ref[:, cols] + w_ref[CONV_WIDTH - 1:CONV_WIDTH, cols] * ext_ref[BHALO:BHALO + CHUNK, cols].astype(F32)
        for tap in range(CONV_WIDTH - 1):
            acc = acc + w_ref[tap:tap + 1, cols] * sh[tap * CHUNK:(tap + 1) * CHUNK]
        y = _silu(acc)
        for h in range(s, s + CONV_GROUP, LANES):
            yh = y[:, h - s:h - s + LANES]
            if h < l2_cols:
                yh = yh * lax.rsqrt(jnp.sum(yh * yh, axis=-1, keepdims=True) + EPS)
                if h < q_cols:
                    yh = yh * qscale
            act_ref[:, h:h + LANES] = yh

    return [functools.partial(lane_group, s) for s in range(0, width, CONV_GROUP)]


class _Filler:
    def __init__(self, tasks, shares):
        self.tasks, self.per = list(tasks), -(-len(tasks) // shares)

    def emit(self):
        for task in self.tasks[:self.per]:
            task()
        self.tasks = self.tasks[self.per:]

    def flush(self):
        for task in self.tasks:
            task()
        self.tasks = []


_NO_FILL = _Filler([], 1)


def _conv_steps_kernel(u_ref, hist_ref, w_ref, b_ref, o_ref, *, steps, n_l2, qscale):
    cb = pl.program_id(0)
    ext = [hist_ref[i] for i in range(CONV_WIDTH - 1)] + [u_ref[i].astype(F32) for i in range(steps)]
    for t in range(steps):
        acc = b_ref[...] + w_ref[0:1, :] * ext[t]
        for tap in range(1, CONV_WIDTH):
            acc = acc + w_ref[tap:tap + 1, :] * ext[t + tap]
        _post_conv(acc, o_ref.at[t], cb, n_l2, qscale)


def _conv_steps(u3, skip_at, skip, n_cols, hist3, conv_w, conv_b, n_l2, qscale):
    steps, nb, _ = u3.shape
    return pl.pallas_call(
        functools.partial(_conv_steps_kernel, steps=steps, n_l2=n_l2, qscale=qscale),
        grid=(n_cols // CONV_COLS,),
        in_specs=[
            pl.BlockSpec((steps, nb, CONV_COLS), lambda c: (0, 0, c + skip * (c >= skip_at))),
            pl.BlockSpec((CONV_WIDTH - 1, nb, CONV_COLS), lambda c: (0, 0, c)),
            pl.BlockSpec((CONV_WIDTH, CONV_COLS), lambda c: (0, c)),
            pl.BlockSpec((1, CONV_COLS), lambda c: (0, c)),
        ],
        out_specs=pl.BlockSpec((steps, nb, CONV_COLS), lambda c: (0, 0, c)),
        out_shape=jax.ShapeDtypeStruct((steps, nb, n_cols), F32),
        compiler_params=_cparams("parallel"),
        name="conv_steps",
    )(u3, hist3, conv_w, conv_b.reshape(1, n_cols))


def _split3(x):
    hi = x.astype(BF16)
    r = x - hi.astype(F32)
    mid = r.astype(BF16)
    lo = (r - mid.astype(F32)).astype(BF16)
    return hi, mid, lo


def _cat3(x, axis):
    return jnp.concatenate(_split3(x), axis=axis)


def _pad_t(tile):
    return jnp.concatenate([tile, jnp.zeros_like(tile)], axis=0).T


def _pair_rows(t, n):
    return t[0:n] + pltpu.roll(t[n:2 * n], HALF, 1)


def _iotas(shape):
    return lax.broadcasted_iota(jnp.int32, shape, 0), lax.broadcasted_iota(jnp.int32, shape, 1)


def _pad_rows(a):
    if a.shape[0] == CHUNK:
        return a
    return jnp.concatenate([a, jnp.zeros((CHUNK - a.shape[0], a.shape[1]), a.dtype)], axis=0)


def _split2(x):
    hi = x.astype(BF16)
    return hi, (x - hi.astype(F32)).astype(BF16)


def _block_diag(pair_bf, bd_ones):
    pair_bf = _pad_rows(pair_bf)
    return jnp.concatenate([pair_bf, pair_bf], axis=0) * bd_ones


def _mm3(lhs_parts, rhs_hi, rhs_lo):
    lh, ll = lhs_parts
    return _dot(jnp.concatenate([lh, lh, ll], axis=1), jnp.concatenate([rhs_hi, rhs_lo, rhs_hi], axis=0))


def _mm_pairs(lhs_parts, rhs_parts, bd_fn, terms=3):
    if terms == 1:
        return [_dot(lh, bd_fn(rh)) for (lh, _), (rh, _) in zip(lhs_parts, rhs_parts)]
    return [_mm3(lp, bd_fn(rh), bd_fn(rl)) for lp, (rh, rl) in zip(lhs_parts, rhs_parts)]


def _mm_hl(lhs, rhs_bf):
    return _dot(lhs.astype(BF16), rhs_bf)


def _unit_lower_inverse(xs, levels, bd_fn, eye, fill, terms=3):
    ps = [eye + x for x in xs]
    if levels <= 1:
        return ps
    rows = xs[0].shape[0]
    stack = lambda a, b: tuple(jnp.concatenate([s, t], axis=0) for s, t in zip(a, b))
    ysp = [_split2(x) for x in xs]
    ys = _mm_pairs(ysp, ysp, bd_fn, terms)
    fill.emit()
    for _ in range(levels - 2):
        ysp = [_split2(y) for y in ys]
        rs = _mm_pairs([stack(yp, _split2(p)) for yp, p in zip(ysp, ps)], ysp, bd_fn, terms)
        fill.emit()
        ys = [r[0:rows] for r in rs]
        ps = [p + r[rows:2 * rows] for p, r in zip(ps, rs)]
    last = _mm_pairs([_split2(p) for p in ps], [_split2(y) for y in ys], bd_fn, terms)
    return [p + t for p, t in zip(ps, last)]


QUARTER = HALF // 2


def _unit_lower_inverse_blocked(xs, fill):
    row, lane = _iotas((QUARTER, LANES))
    r2, l2 = _iotas((LANES, LANES))
    even = jnp.bitwise_and(jnp.right_shift(lane, 5), 1) == 0
    eye_q = jnp.where(row == jnp.bitwise_and(lane, QUARTER - 1), 1.0, 0.0)
    bd4_ones = jnp.where(jnp.right_shift(r2, 5) == jnp.right_shift(l2, 5), 1.0, 0.0).astype(BF16)
    row_q = jnp.right_shift(r2, 5)
    place = jnp.where(((row_q == 1) & (jnp.right_shift(l2, 5) == 0)) | ((row_q == 3) & (jnp.right_shift(l2, 5) == 2)),
                      1.0, 0.0).astype(BF16)
    bd4 = lambda a: jnp.concatenate([a] * 4, axis=0) * bd4_ones

    tops = [x[0:QUARTER] for x in xs]
    bots = [x[QUARTER:CHUNK] for x in xs]
    diag = [jnp.where(even, t, b) for t, b in zip(tops, bots)]
    dinv = _unit_lower_inverse(diag, _levels(QUARTER), bd4, eye_q, fill)
    fill.emit()
    ai = [jnp.where(even, d, 0.0) for d in dinv]
    ci = [jnp.where(even, 0.0, d) for d in dinv]
    b_ai = _mm_pairs([_split2(jnp.where(even, b, 0.0)) for b in bots], [_split2(a) for a in ai], bd4)
    fill.emit()
    low = [_mm3(_split2(c), *[jnp.concatenate([part] * 4, axis=0) * place for part in _split2(e)])
           for c, e in zip(ci, b_ai)]
    return [jnp.concatenate([a, l + c], axis=0) for a, l, c in zip(ai, low, ci)]


def _two_blocks(a, b):
    a_bf, b_bf = _pad_rows(a).astype(BF16), _pad_rows(b).astype(BF16)
    z = jnp.zeros_like(a_bf)
    return jnp.concatenate([jnp.concatenate([a_bf, z], axis=1), jnp.concatenate([z, b_bf], axis=1)], axis=0)


def _recurrence_consts(n_pairs, chan_even0, chan_odd0):
    tril = np.tril(np.ones((CHUNK, CHUNK), np.float32))
    tril3 = np.concatenate([tril] * 3, axis=1)
    upper = np.zeros((LANES, 2 * LANES), np.float32)
    upper[:CHUNK, :CHUNK] = tril.T
    upper[:CHUNK, LANES:] = 1.0
    upper3 = np.concatenate([upper] * 3, axis=0)
    e = np.zeros((LANES, n_pairs * LANES), np.float32)
    for p in range(n_pairs):
        e[chan_even0 + p, p * LANES:p * LANES + HALF] = 1.0
        e[chan_odd0 + p, p * LANES + HALF:(p + 1) * LANES] = 1.0
    e3 = np.concatenate([e] * 3, axis=0)
    return jnp.asarray(tril3, BF16), jnp.asarray(upper3, BF16), jnp.asarray(e3, BF16)


def _levels(lr):
    return max(1, int(np.ceil(np.log2(lr))))


STEP_ROWS = 16
DECODE_TOKENS_SINGLE_TERM = 4


GDN_PAIRS = GDN_V_HEADS // 2
GDN_PAIR_BATCH = 8
GDN_FILL_SHARES = 12 * (GDN_PAIRS // GDN_PAIR_BATCH)


def _gdn_gates(raw, alog_row, dtb_row, lr):
    row, lane = _iotas((CHUNK, LANES))
    g = -jnp.exp(alog_row) * _softplus(raw + dtb_row)
    beta = jax.nn.sigmoid(raw)
    gt = jnp.where(lane < GDN_V_HEADS, g, jnp.where(lane < 2 * GDN_V_HEADS, beta, 0.0))
    if lr < CHUNK:
        gt = jnp.where(row < lr, gt, 0.0)
    return gt


def _gdn_block(insts, tril3_ref, upper3_ref, e3_ref, lr, fill=_NO_FILL):
    np_ = GDN_PAIRS
    rows = insts[0][0].shape[0]
    row, lane = _iotas((rows, LANES))
    jl = jnp.bitwise_and(lane, HALF - 1)
    left = lane < HALF
    causal = row >= jl
    strict = row > jl
    eye2 = jnp.where(row == jl, 1.0, 0.0)
    r2, l2 = _iotas((LANES, LANES))
    bd_ones = jnp.where(jnp.right_shift(r2, 6) == jnp.right_shift(l2, 6), 1.0, 0.0).astype(BF16)
    _, lane_p = _iotas((np_, LANES))
    sl = lambda a, i: a[:, i * LANES:(i + 1) * LANES]

    pre = []
    for _, gt, _, _, _ in insts:
        t1 = _pad_t(gt)
        cum_ext = _dot(_cat3(t1[0:2 * np_], 1), upper3_ref[...])
        cum_t, last_b = cum_ext[:, :LANES], cum_ext[:, LANES:]
        cum_rp = _pair_rows(cum_t, np_)
        beta_rp = _pair_rows(t1[2 * np_:4 * np_], np_)
        last_rp = jnp.where(lane_p < HALF, last_b[0:np_], last_b[np_:2 * np_])
        cum = _dot(tril3_ref[...], _cat3(gt, 0))
        pre.append(dict(cum_rp=cum_rp, beta_rp=beta_rp, ecum_rp=jnp.exp(cum_rp),
                        kdec_rp=jnp.exp(last_rp - cum_rp) * beta_rp, elast=jnp.exp(last_b),
                        col_all=_dot(_cat3(cum[0:rows], 1), e3_ref[...])))

    def run(items):
        heads = [(i, 2 * p + hh) for i, p in items for hh in range(2)]
        act = lambda i: insts[i][0]
        prow = lambda name, it: pre[it[0]][name][it[1]:it[1] + 1]
        q = {it: sl(act(it[0]), it[1]) for it in items}
        k = {it: _pad_rows(sl(act(it[0]), GDN_QK_HEADS + it[1])) for it in items}
        v = {ih: sl(act(ih[0]), 2 * GDN_QK_HEADS + ih[1]) for ih in heads}
        kb = {it: k[it].astype(BF16) for it in items}
        qb = {it: q[it].astype(BF16) for it in items}
        gq = {it: _dot_nt(jnp.concatenate([kb[it][0:rows], qb[it]], axis=0), jnp.concatenate([kb[it], kb[it]], axis=0))
              for it in items}
        fill.emit()
        colc = {it: sl(pre[it[0]]["col_all"], it[1]) for it in items}
        base = {it: jnp.exp(jnp.where(causal, colc[it] - prow("cum_rp", it), NEG_BIG)) * prow("beta_rp", it)
                for it in items}
        x = [jnp.where(strict, -(gq[it][0:rows] * base[it]), 0.0) for it in items]
        qkd = {it: (gq[it][rows:2 * rows] * base[it]).astype(BF16) for it in items}
        if lr == CHUNK:
            minv = dict(zip(items, _unit_lower_inverse_blocked(x, fill)))
        else:
            minv = dict(zip(items, _unit_lower_inverse(x, _levels(lr), lambda a: _block_diag(a, bd_ones), eye2, fill,
                                                       terms=1 if lr <= DECODE_TOKENS_SINGLE_TERM else 3)))
        fill.emit()
        pair_of = lambda ih: (ih[0], ih[1] // 2)
        u = {it: jnp.concatenate([v[(it[0], 2 * it[1])], v[(it[0], 2 * it[1] + 1)]], axis=1)
             + _mm_hl(minv[it] - eye2, _two_blocks(v[(it[0], 2 * it[1])], v[(it[0], 2 * it[1] + 1)])) for it in items}
        fill.emit()
        w = {it: _mm_hl(minv[it] * prow("ecum_rp", it), _two_blocks(kb[it], kb[it])) for it in items}
        fill.emit()
        s_old = {ih: insts[ih[0]][2][ih[1]] for ih in heads}
        r = {ih: _dot(jnp.concatenate([sl(w[pair_of(ih)], ih[1] % 2).astype(BF16), qb[pair_of(ih)]], axis=0),
                      s_old[ih].astype(BF16)) for ih in heads}
        fill.emit()
        delta = {ih: sl(u[pair_of(ih)], ih[1] % 2) - r[ih][0:rows] for ih in heads}
        bd_delta = {it: _two_blocks(delta[(it[0], 2 * it[1])], delta[(it[0], 2 * it[1] + 1)]) for it in items}
        fill.emit()
        od = {it: _dot(jnp.concatenate(
            [qkd[it], (jnp.concatenate([k[it], k[it]], axis=0).T * prow("kdec_rp", it)).astype(BF16)], axis=0),
            bd_delta[it]) for it in items}
        for it in items:
            i, p = it
            ecol = jnp.exp(colc[it])
            ecol_r = pltpu.roll(ecol, HALF, 1)
            efull = (jnp.where(left, ecol, ecol_r), jnp.where(left, ecol_r, ecol))
            for hh in range(2):
                h = 2 * p + hh
                insts[i][4][:, h * LANES:(h + 1) * LANES] = (
                    efull[hh] * r[(i, h)][rows:2 * rows] + sl(od[it][0:rows], hh)).astype(insts[i][4].dtype)
                e_h = pre[i]["elast"][hh * np_ + p:hh * np_ + p + 1]
                insts[i][3][h] = e_h * s_old[(i, h)] + sl(od[it][rows:rows + LANES], hh)

    for b0 in range(0, np_, GDN_PAIR_BATCH):
        run([(i, p) for i in range(len(insts)) for p in range(b0, b0 + GDN_PAIR_BATCH)])
    fill.flush()


def _gdn_seq_kernel(raw0_ref, rawn_ref, graw_ref, shift_ref, cw_ref, cb_ref, alog_ref, dtb_ref, tril3_ref, upper3_ref,
                    e3_ref, o_ref, s_ref, ext_ref, act_ref):
    n = pl.program_id(1)
    seqs = range(graw_ref.shape[0])
    conv = functools.partial(_conv_block, shift_ref=shift_ref, w_ref=cw_ref, b_ref=cb_ref,
                             l2_cols=2 * GDN_QK_DIM, q_cols=GDN_QK_DIM, qscale=QSCALE)

    @pl.when(n == 0)
    def _():
        s_ref[...] = jnp.zeros(s_ref.shape, F32)
        for j in seqs:
            _Filler(conv([raw0_ref.at[j]], ext_ref=ext_ref.at[j], act_ref=act_ref.at[j, 0], seq_start=True), 1).flush()

    @pl.when(n > 0)
    def _():
        for j in seqs:
            act_ref[j, 0] = act_ref[j, 1]

    tasks, insts = [], []
    for j in seqs:
        gt = _gdn_gates(graw_ref[j], alog_ref[...], dtb_ref[...], CHUNK)
        tasks += conv([rawn_ref.at[j]], ext_ref=ext_ref.at[j], act_ref=act_ref.at[j, 1], seq_start=False)
        insts.append((act_ref.at[j, 0], gt, s_ref.at[j], s_ref.at[j], o_ref.at[j]))
    _gdn_block(insts, tril3_ref, upper3_ref, e3_ref, CHUNK, _Filler(tasks, GDN_FILL_SHARES))


def _const_specs(consts, ngrid):
    zero = (lambda *_: (0, 0))
    return [pl.BlockSpec(c.shape, zero) for c in consts]


GDN_SEQ_TOGETHER = 2
SSD_SEQ_TOGETHER = 1


def _seq_views(nseq, seq_len, together):
    together = min(together, nseq)
    assert nseq % together == 0
    nc = seq_len // CHUNK
    per = nseq // together
    last = per * nc - 1
    spec = lambda width, idx, col=0: pl.BlockSpec((together, CHUNK, width), lambda b, n: (0, idx(b, n), col))
    cur = lambda b, n: b * nc + n
    first = lambda b, n: b * nc
    nxt = lambda b, n: jnp.minimum(b * nc + n + 1, last)
    view = lambda a: a.reshape(together, a.shape[0] // together, a.shape[1])
    return together, nc, per, spec, cur, first, nxt, view


def _gdn_seq(pm, graw, conv_w, conv_b, alog_row, dtb_row, consts, nseq, seq_len):
    together, nc, per, spec, cur, first, nxt, view = _seq_views(nseq, seq_len, GDN_SEQ_TOGETHER)
    rowspec = pl.BlockSpec((1, LANES), lambda b, n: (0, 0))
    pm3 = view(pm)
    o, s = pl.pallas_call(
        _gdn_seq_kernel,
        grid=(per, nc),
        in_specs=[
            spec(GDN_CONV_DIM, first),
            spec(GDN_CONV_DIM, nxt),
            spec(LANES, cur),
            pl.BlockSpec(((CONV_WIDTH - 1) * CHUNK, BHALO + CHUNK), lambda b, n: (0, 0)),
            pl.BlockSpec((CONV_WIDTH, GDN_CONV_DIM), lambda b, n: (0, 0)),
            pl.BlockSpec((1, GDN_CONV_DIM), lambda b, n: (0, 0)),
            rowspec, rowspec,
        ] + _const_specs(consts, 2),
        out_specs=[
            spec(GDN_V_DIM, cur),
            pl.BlockSpec((together, None, GDN_V_HEADS, GDN_HEAD_DIM, GDN_HEAD_DIM), lambda b, n: (0, b, 0, 0, 0)),
        ],
        out_shape=[
            jax.ShapeDtypeStruct((together, per * seq_len, GDN_V_DIM), BF16),
            jax.ShapeDtypeStruct((together, per, GDN_V_HEADS, GDN_HEAD_DIM, GDN_HEAD_DIM), F32),
        ],
        scratch_shapes=[
            pltpu.VMEM((together, CHUNK + BHALO, GDN_CONV_DIM), BF16),
            pltpu.VMEM((together, 2, CHUNK, GDN_CONV_DIM), F32),
        ],
        compiler_params=_cparams("parallel", "arbitrary"),
        name="gdn_seq",
    )(pm3, pm3, view(graw), _shift_matrix(), conv_w, conv_b.reshape(1, GDN_CONV_DIM), alog_row, dtb_row, *consts)
    return (o.reshape(nseq * seq_len, GDN_V_DIM),
            s.reshape(nseq, GDN_V_HEADS, GDN_HEAD_DIM, GDN_HEAD_DIM))


SEQ_PER_STEP = 8
SEQ_INTERLEAVE = 4


def _load_padded(src_ref, s, pad_ref, lr):
    pad_ref[...] = jnp.zeros(pad_ref.shape, F32)
    for t in range(lr):
        pad_ref[t:t + 1, :] = src_ref[t, pl.ds(s, 1), :]


def _store_tokens(pad_ref, dst_ref, s, lr):
    for t in range(lr):
        dst_ref[t, pl.ds(s, 1), :] = pad_ref[t:t + 1, :]


def _gdn_step_kernel(act_ref, graw_ref, alog_ref, dtb_ref, tril3_ref, upper3_ref, e3_ref, s0_ref,
                     o_ref, s_ref, apad_ref, gpad_ref, opad_ref, *, lr):
    def some_sequences(it, carry):
        insts = []
        for j in range(SEQ_INTERLEAVE):
            s = it * SEQ_INTERLEAVE + j
            _load_padded(act_ref, s, apad_ref.at[j], lr)
            _load_padded(graw_ref, s, gpad_ref.at[j], lr)
            gt = _gdn_gates(gpad_ref[j], alog_ref[...], dtb_ref[...], lr)
            insts.append((apad_ref.at[j], gt, s0_ref.at[s], s_ref.at[s], opad_ref.at[j]))
        _gdn_block(insts, tril3_ref, upper3_ref, e3_ref, lr)
        for j in range(SEQ_INTERLEAVE):
            _store_tokens(opad_ref.at[j], o_ref, it * SEQ_INTERLEAVE + j, lr)
        return carry

    lax.fori_loop(0, SEQ_PER_STEP // SEQ_INTERLEAVE, some_sequences, 0)


def _step_spec(steps, width):
    return pl.BlockSpec((steps, SEQ_PER_STEP, width), lambda g: (0, g, 0))


def _gdn_step(act3, graw3, alog_row, dtb_row, consts, s0):
    steps, nb, _ = act3.shape
    assert nb % SEQ_PER_STEP == 0 and steps <= STEP_ROWS
    rowspec = pl.BlockSpec((1, LANES), lambda b: (0, 0))
    sspec = pl.BlockSpec((SEQ_PER_STEP, GDN_V_HEADS, GDN_HEAD_DIM, GDN_HEAD_DIM), lambda g: (g, 0, 0, 0))
    o, s = pl.pallas_call(
        functools.partial(_gdn_step_kernel, lr=steps),
        grid=(nb // SEQ_PER_STEP,),
        in_specs=[_step_spec(steps, GDN_CONV_DIM), _step_spec(steps, LANES), rowspec, rowspec]
        + _const_specs(consts, 1) + [sspec],
        out_specs=[_step_spec(steps, GDN_V_DIM), sspec],
        out_shape=[
            jax.ShapeDtypeStruct((steps, nb, GDN_V_DIM), F32),
            jax.ShapeDtypeStruct(s0.shape, F32),
        ],
        scratch_shapes=[
            pltpu.VMEM((SEQ_INTERLEAVE, STEP_ROWS, GDN_CONV_DIM), F32),
            pltpu.VMEM((SEQ_INTERLEAVE, CHUNK, LANES), F32),
            pltpu.VMEM((SEQ_INTERLEAVE, STEP_ROWS, GDN_V_DIM), F32),
        ],
        compiler_params=_cparams("parallel"),
        name="gdn_step",
    )(act3, graw3, alog_row, dtb_row, *consts, s0)
    return o.reshape(steps * nb, GDN_V_DIM), s


SSM_PAIRS = SSM_HEADS // 2
PAIRS_PER_GROUP = SSM_PAIRS // SSM_GROUPS


def _ssm_gates(raw, alog_row, dtb_row, lr):
    row, lane = _iotas((CHUNK, LANES))
    dt = _softplus(raw + dtb_row)
    tile = jnp.where(lane < SSM_HEADS, dt, jnp.where(lane < 2 * SSM_HEADS, -jnp.exp(alog_row) * dt, 0.0))
    if lr < CHUNK:
        tile = jnp.where(row < lr, tile, 0.0)
    return tile


def _ssd_block(insts, dskip_ref, tril3_ref, upper3_ref, e3_ref, fill=_NO_FILL):
    np_ = SSM_PAIRS
    rows = insts[0][0].shape[0]
    row, lane = _iotas((rows, LANES))
    causal = row >= jnp.bitwise_and(lane, HALF - 1)
    r2, l2 = _iotas((LANES, LANES))
    bdmask = jnp.right_shift(r2, 6) == jnp.right_shift(l2, 6)
    top = r2 < HALF
    _, lane_p = _iotas((np_, LANES))
    sl = lambda a, i: a[:, i * LANES:(i + 1) * LANES]
    grp = lambda it: (it[0], it[1] // PAIRS_PER_GROUP)

    pre = []
    for _, tile, _, _, _ in insts:
        t1 = _pad_t(tile)
        cum_ext = _dot(_cat3(t1[2 * np_:4 * np_], 1), upper3_ref[...])
        cum_t, last_b = cum_ext[:, :LANES], cum_ext[:, LANES:]
        cum_rp = _pair_rows(cum_t, np_)
        dt_rp = _pair_rows(t1[0:2 * np_], np_)
        last_rp = jnp.where(lane_p < HALF, last_b[0:np_], last_b[np_:2 * np_])
        cum = _dot(tril3_ref[...], _cat3(tile, 0))
        pre.append(dict(cum_rp=cum_rp, dt_rp=dt_rp, coef_rp=jnp.exp(last_rp - cum_rp) * dt_rp,
                        elast=jnp.exp(last_b), col_all=_dot(_cat3(cum[0:rows], 1), e3_ref[...])))

    seqs = range(len(insts))
    items = [(i, p) for i in seqs for p in range(np_)]
    groups = [(i, g) for i in seqs for g in range(SSM_GROUPS)]
    act = lambda i: insts[i][0]
    prow = lambda name, it: pre[it[0]][name][it[1]:it[1] + 1]
    bg = {ig: _pad_rows(sl(act(ig[0]), SSM_D_INNER // LANES + ig[1])).astype(BF16) for ig in groups}
    cg = {ig: sl(act(ig[0]), (SSM_D_INNER + SSM_BC) // LANES + ig[1]).astype(BF16) for ig in groups}
    bb = {ig: jnp.concatenate([bg[ig], bg[ig]], axis=0) for ig in groups}
    cb2 = {ig: _dot_nt(cg[ig], bb[ig]) for ig in groups}
    fill.emit()
    xp = {it: sl(act(it[0]), it[1]) for it in items}
    colc = {it: sl(pre[it[0]]["col_all"], it[1]) for it in items}
    x2 = {it: jnp.concatenate([_pad_rows(xp[it]), _pad_rows(xp[it])], axis=0) for it in items}
    lm = {it: cb2[grp(it)] * jnp.exp(jnp.where(causal, colc[it] - prow("cum_rp", it), NEG_BIG)) * prow("dt_rp", it)
          for it in items}
    y_diag = {}
    for n, it in enumerate(items):
        y_diag[it] = _dot(lm[it].astype(BF16), jnp.where(bdmask, x2[it], 0.0).astype(BF16))
        if n % (4 * len(insts)) == 4 * len(insts) - 1:
            fill.emit()
    hp = {it: insts[it[0]][2][it[1]] for it in items}
    y_off = {}
    for n, it in enumerate(items):
        y_off[it] = _dot_nt(cg[grp(it)], hp[it].astype(BF16))
        if n % (4 * len(insts)) == 4 * len(insts) - 1:
            fill.emit()
    for it in items:
        i, p = it
        insts[i][4][:, p * LANES:(p + 1) * LANES] = (
            y_diag[it] + jnp.exp(colc[it]) * y_off[it]
            + dskip_ref[:, p * LANES:(p + 1) * LANES] * xp[it]).astype(insts[i][4].dtype)
    dh = {}
    for n, it in enumerate(items):
        lhs = jnp.where(bdmask, x2[it].T * prow("coef_rp", it), 0.0)
        dh[it] = _dot(lhs.astype(BF16), bb[grp(it)])
        if n % (4 * len(insts)) == 4 * len(insts) - 1:
            fill.emit()
    for it in items:
        i, p = it
        elast = pre[i]["elast"]
        e_rows = jnp.where(top, elast[p:p + 1], elast[np_ + p:np_ + p + 1])
        insts[i][3][p] = e_rows * hp[it] + dh[it]
    fill.flush()


def _ssd_seq_kernel(x0_ref, bc0_ref, xn_ref, bcn_ref, graw_ref, shift_ref, cw_ref, cb_ref, alog_ref, dtb_ref,
                    dskip_ref, tril3_ref, upper3_ref, e3_ref, y_ref, h_ref, ext_ref, act_ref):
    n = pl.program_id(1)
    seqs = range(graw_ref.shape[0])
    conv = functools.partial(_conv_block, shift_ref=shift_ref, w_ref=cw_ref, b_ref=cb_ref,
                             l2_cols=0, q_cols=0, qscale=1.0)

    @pl.when(n == 0)
    def _():
        h_ref[...] = jnp.zeros(h_ref.shape, F32)
        for j in seqs:
            _Filler(conv([x0_ref.at[j], bc0_ref.at[j]], ext_ref=ext_ref.at[j], act_ref=act_ref.at[j, 0],
                         seq_start=True), 1).flush()

    @pl.when(n > 0)
    def _():
        for j in seqs:
            act_ref[j, 0] = act_ref[j, 1]

    tasks, insts = [], []
    for j in seqs:
        tile = _ssm_gates(graw_ref[j], alog_ref[...], dtb_ref[...], CHUNK)
        tasks += conv([xn_ref.at[j], bcn_ref.at[j]], ext_ref=ext_ref.at[j], act_ref=act_ref.at[j, 1], seq_start=False)
        insts.append((act_ref.at[j, 0], tile, h_ref.at[j], h_ref.at[j], y_ref.at[j]))
    _ssd_block(insts, dskip_ref, tril3_ref, upper3_ref, e3_ref, _Filler(tasks, SSD_FILL_SHARES))


SSD_FILL_SHARES = 13


def _ssd_seq(pm, graw, conv_w, conv_b, alog_row, dtb_row, dskip_row, consts, nseq, seq_len):
    together, nc, per, spec, cur, first, nxt, view = _seq_views(nseq, seq_len, SSD_SEQ_TOGETHER)
    bc_blk = 2 * SSM_D_INNER // (2 * SSM_BC)
    rowspec = pl.BlockSpec((1, LANES), lambda b, n: (0, 0))
    pm3 = view(pm)
    y, h = pl.pallas_call(
        _ssd_seq_kernel,
        grid=(per, nc),
        in_specs=[
            spec(SSM_D_INNER, first),
            spec(2 * SSM_BC, first, bc_blk),
            spec(SSM_D_INNER, nxt),
            spec(2 * SSM_BC, nxt, bc_blk),
            spec(LANES, cur),
            pl.BlockSpec(((CONV_WIDTH - 1) * CHUNK, BHALO + CHUNK), lambda b, n: (0, 0)),
            pl.BlockSpec((CONV_WIDTH, SSM_CONV_DIM), lambda b, n: (0, 0)),
            pl.BlockSpec((1, SSM_CONV_DIM), lambda b, n: (0, 0)),
            rowspec, rowspec,
            pl.BlockSpec((1, SSM_D_INNER), lambda b, n: (0, 0)),
        ] + _const_specs(consts, 2),
        out_specs=[
            spec(SSM_D_INNER, cur),
            pl.BlockSpec((together, None, SSM_PAIRS, LANES, SSM_STATE), lambda b, n: (0, b, 0, 0, 0)),
        ],
        out_shape=[
            jax.ShapeDtypeStruct((together, per * seq_len, SSM_D_INNER), BF16),
            jax.ShapeDtypeStruct((together, per, SSM_PAIRS, LANES, SSM_STATE), F32),
        ],
        scratch_shapes=[
            pltpu.VMEM((together, CHUNK + BHALO, SSM_CONV_DIM), BF16),
            pltpu.VMEM((together, 2, CHUNK, SSM_CONV_DIM), F32),
        ],
        compiler_params=_cparams("parallel", "arbitrary"),
        name="ssd_seq",
    )(pm3, pm3, pm3, pm3, view(graw), _shift_matrix(), conv_w, conv_b.reshape(1, SSM_CONV_DIM), alog_row, dtb_row,
      dskip_row, *consts)
    return y.reshape(nseq * seq_len, SSM_D_INNER), h.reshape(nseq, SSM_PAIRS, LANES, SSM_STATE)


def _ssd_step_kernel(act_ref, graw_ref, alog_ref, dtb_ref, dskip_ref, tril3_ref, upper3_ref, e3_ref, h0_ref,
                     y_ref, h_ref, apad_ref, gpad_ref, ypad_ref, *, lr):
    def some_sequences(it, carry):
        insts = []
        for j in range(SEQ_INTERLEAVE):
            s = it * SEQ_INTERLEAVE + j
            _load_padded(act_ref, s, apad_ref.at[j], lr)
            _load_padded(graw_ref, s, gpad_ref.at[j], lr)
            tile = _ssm_gates(gpad_ref[j], alog_ref[...], dtb_ref[...], lr)
            insts.append((apad_ref.at[j], tile, h0_ref.at[s], h_ref.at[s], ypad_ref.at[j]))
        _ssd_block(insts, dskip_ref, tril3_ref, upper3_ref, e3_ref)
        for j in range(SEQ_INTERLEAVE):
            _store_tokens(ypad_ref.at[j], y_ref, it * SEQ_INTERLEAVE + j, lr)
        return carry

    lax.fori_loop(0, SEQ_PER_STEP // SEQ_INTERLEAVE, some_sequences, 0)


def _ssd_step(act3, graw3, alog_row, dtb_row, dskip_row, consts, h0):
    steps, nb, _ = act3.shape
    assert nb % SEQ_PER_STEP == 0 and steps <= STEP_ROWS
    rowspec = pl.BlockSpec((1, LANES), lambda b: (0, 0))
    hspec = pl.BlockSpec((SEQ_PER_STEP, SSM_PAIRS, LANES, SSM_STATE), lambda g: (g, 0, 0, 0))
    y, h = pl.pallas_call(
        functools.partial(_ssd_step_kernel, lr=steps),
        grid=(nb // SEQ_PER_STEP,),
        in_specs=[_step_spec(steps, SSM_CONV_DIM), _step_spec(steps, LANES), rowspec, rowspec,
                  pl.BlockSpec((1, SSM_D_INNER), lambda b: (0, 0))] + _const_specs(consts, 1) + [hspec],
        out_specs=[_step_spec(steps, SSM_D_INNER), hspec],
        out_shape=[
            jax.ShapeDtypeStruct((steps, nb, SSM_D_INNER), F32),
            jax.ShapeDtypeStruct(h0.shape, F32),
        ],
        scratch_shapes=[
            pltpu.VMEM((SEQ_INTERLEAVE, STEP_ROWS, SSM_CONV_DIM), F32),
            pltpu.VMEM((SEQ_INTERLEAVE, CHUNK, LANES), F32),
            pltpu.VMEM((SEQ_INTERLEAVE, STEP_ROWS, SSM_D_INNER), F32),
        ],
        compiler_params=_cparams("parallel"),
        name="ssd_step",
    )(act3, graw3, alog_row, dtb_row, dskip_row, *consts, h0)
    return y.reshape(steps * nb, SSM_D_INNER), h


FFN_TILE = FFN_HIDDEN // 2
GDN_PROJ_TILE = GDN_MAIN // 3
SSM_PROJ_TILE = SSM_MAIN // 2


def _lane_row(pieces):
    row = jnp.zeros((1, LANES), F32)
    for off, vec in pieces:
        row = row.at[0, off:off + vec.shape[0]].set(vec.astype(F32))
    return row


def _stage_params(w_mod, b_mod, norm_mix, norm_ffn, norm_final, gdn_w_in, gdn_conv_w, gdn_a_log, gdn_dt_bias,
                  gdn_norm, gdn_w_out, ssm_w_in, ssm_conv_w, ssm_conv_b, ssm_a_log, ssm_dt_bias, ssm_d, ssm_norm,
                  ssm_w_out, ffn_w_gate_up, ffn_w_down):
    perm_g = np.concatenate([np.arange(0, GDN_V_HEADS, 2), np.arange(1, GDN_V_HEADS, 2)])
    perm_s = np.concatenate([np.arange(0, SSM_HEADS, 2), np.arange(1, SSM_HEADS, 2)])
    g_in, s_in = gdn_w_in[0], ssm_w_in[0]
    beta_cols = g_in[:, GDN_MAIN:GDN_MAIN + GDN_V_HEADS][:, perm_g]
    a_cols = g_in[:, GDN_MAIN + GDN_V_HEADS:GDN_MAIN + 2 * GDN_V_HEADS][:, perm_g]
    gdn_small = jnp.concatenate([a_cols, beta_cols, jnp.zeros((D_MODEL, LANES - 2 * GDN_V_HEADS), F32)], axis=1)
    dt_cols = s_in[:, SSM_MAIN:SSM_MAIN + SSM_HEADS][:, perm_s]
    ssm_small = jnp.concatenate([dt_cols, dt_cols, jnp.zeros((D_MODEL, LANES - 2 * SSM_HEADS), F32)], axis=1)
    return dict(
        w_mod=w_mod, b_mod=b_mod, norm_mix=norm_mix, norm_ffn=norm_ffn, norm_final=norm_final,
        gdn_main=g_in[:, :GDN_MAIN].astype(BF16), gdn_small=gdn_small.astype(BF16),
        gdn_conv_w=gdn_conv_w[0], gdn_conv_b=jnp.zeros((GDN_CONV_DIM,), F32),
        gdn_alog_row=_lane_row([(0, gdn_a_log[0][perm_g])]), gdn_dtb_row=_lane_row([(0, gdn_dt_bias[0][perm_g])]),
        gdn_norm=jnp.tile(gdn_norm[0], GDN_V_HEADS), gdn_w_out=gdn_w_out[0].astype(BF16),
        gdn_consts=_recurrence_consts(GDN_PAIRS, 0, GDN_PAIRS),
        ssm_main=jnp.concatenate(
            [s_in[:, SSM_D_INNER:2 * SSM_D_INNER], s_in[:, :SSM_D_INNER], s_in[:, 2 * SSM_D_INNER:SSM_MAIN]],
            axis=1).astype(BF16),
        ssm_small=ssm_small.astype(BF16),
        ssm_conv_w=ssm_conv_w[0], ssm_conv_b=ssm_conv_b[0],
        ssm_alog_row=_lane_row([(SSM_HEADS, ssm_a_log[0][perm_s])]),
        ssm_dtb_row=_lane_row([(0, ssm_dt_bias[0][perm_s]), (SSM_HEADS, ssm_dt_bias[0][perm_s])]),
        ssm_dskip_row=jnp.repeat(ssm_d[0], SSM_HEAD_DIM).reshape(1, SSM_D_INNER),
        ssm_norm=ssm_norm[0], ssm_w_out=ssm_w_out[0].astype(BF16),
        ssm_consts=_recurrence_consts(SSM_PAIRS, SSM_HEADS, SSM_HEADS + SSM_PAIRS),
        wg=[ffn_w_gate_up[i][:, :FFN_HIDDEN].astype(BF16) for i in range(2)],
        wu=[ffn_w_gate_up[i][:, FFN_HIDDEN:].astype(BF16) for i in range(2)],
        wd=[ffn_w_down[i].astype(BF16) for i in range(2)],
    )


def _ffn(x, layer, mod3, rows_up, rows_down, p, final_w):
    act = _ffn_up(x, p["norm_ffn"][layer], mod3, rows_up, p["wg"][layer], p["wu"][layer], FFN_TILE)
    return _ffn_down(act, x, mod3, rows_down, p["wd"][layer], final_w)


QSCALE = GDN_HEAD_DIM ** -0.5


def _trunk_seq(x3, mod, p):
    nseq, seq_len, _ = x3.shape
    m = nseq * seq_len
    x = x3.reshape(m, D_MODEL)
    mod3 = [mod[l].reshape(nseq, 1, 6 * D_MODEL) for l in range(2)]
    rows_a = _Rows(m, min(1024, seq_len), seq_len, 1)
    rows_b = _Rows(m, min(512, seq_len), seq_len, 1)

    pm, ps = _in_proj(x, p["norm_mix"][0], mod3[0], rows_a, 1, 0, p["gdn_main"], p["gdn_small"], GDN_PROJ_TILE,
                      (GDN_CONV_DIM, GDN_MAIN))
    o, gdn_s = _gdn_seq(pm, ps, p["gdn_conv_w"], p["gdn_conv_b"], p["gdn_alog_row"], p["gdn_dtb_row"],
                        p["gdn_consts"], nseq, seq_len)
    tail = pm.reshape(nseq, seq_len, GDN_MAIN)[:, seq_len - (CONV_WIDTH - 1):].astype(F32)
    gdn_c = tail[..., :GDN_CONV_DIM]
    x = _mixer_out(o, pm, 2, p["gdn_norm"], x, mod3[0], rows_b, p["gdn_w_out"], GDN_HEAD_DIM, False)
    x = _ffn(x, 0, mod3[0], rows_a, rows_b, p, None)

    pm, ps = _in_proj(x, p["norm_mix"][1], mod3[1], rows_a, 1, 0, p["ssm_main"], p["ssm_small"], SSM_PROJ_TILE,
                      (SSM_D_INNER, 2 * SSM_D_INNER))
    y, ssm_h = _ssd_seq(pm, ps, p["ssm_conv_w"], p["ssm_conv_b"], p["ssm_alog_row"], p["ssm_dtb_row"],
                        p["ssm_dskip_row"], p["ssm_consts"], nseq, seq_len)
    tail = pm.reshape(nseq, seq_len, SSM_MAIN)[:, seq_len - (CONV_WIDTH - 1):].astype(F32)
    ssm_c = jnp.concatenate([tail[..., :SSM_D_INNER], tail[..., 2 * SSM_D_INNER:]], axis=-1)
    x = _mixer_out(y, pm, 1, p["ssm_norm"], x, mod3[1], rows_b, p["ssm_w_out"], SSM_D_INNER // SSM_GROUPS, True)
    _, y_out = _ffn(x, 1, mod3[1], rows_a, rows_b, p, p["norm_final"])

    return (y_out.reshape(nseq, seq_len, D_MODEL), gdn_s[None], gdn_c[None],
            ssm_h.reshape(nseq, SSM_HEADS, SSM_HEAD_DIM, SSM_STATE)[None], ssm_c[None])


def _trunk_step(x3, mod, st_gdn, cv_gdn, st_ssm, cv_ssm, p):
    nb, steps, _ = x3.shape
    assert steps >= CONV_WIDTH - 1
    m = nb * steps
    x = jnp.transpose(x3, (1, 0, 2)).reshape(m, D_MODEL)
    mod3 = [mod[l].reshape(1, nb, 6 * D_MODEL) for l in range(2)]
    rows = _Rows(m, m, None, nb)
    tok = lambda a: jnp.transpose(a, (1, 0, 2))

    pm, ps = _in_proj(x, p["norm_mix"][0], mod3[0], rows, 1, 0, p["gdn_main"], p["gdn_small"], GDN_PROJ_TILE,
                      (GDN_CONV_DIM, GDN_MAIN))
    u3 = pm.reshape(steps, nb, GDN_MAIN)
    act3 = _conv_steps(u3, 0, 0, GDN_CONV_DIM, tok(cv_gdn[0]), p["gdn_conv_w"], p["gdn_conv_b"], 2, QSCALE)
    o, gdn_s = _gdn_step(act3, ps.reshape(steps, nb, LANES), p["gdn_alog_row"], p["gdn_dtb_row"], p["gdn_consts"],
                         st_gdn[0])
    gdn_c = tok(u3[steps - (CONV_WIDTH - 1):, :, :GDN_CONV_DIM].astype(F32))
    x = _mixer_out(o, pm, 2, p["gdn_norm"], x, mod3[0], rows, p["gdn_w_out"], GDN_HEAD_DIM, False)
    x = _ffn(x, 0, mod3[0], rows, rows, p, None)

    pm, ps = _in_proj(x, p["norm_mix"][1], mod3[1], rows, 1, 0, p["ssm_main"], p["ssm_small"], SSM_PROJ_TILE,
                      (SSM_D_INNER, 2 * SSM_D_INNER))
    u3 = pm.reshape(steps, nb, SSM_MAIN)
    act3 = _conv_steps(u3, SSM_D_INNER // CONV_COLS, SSM_D_INNER // CONV_COLS, SSM_CONV_DIM, tok(cv_ssm[0]),
                       p["ssm_conv_w"], p["ssm_conv_b"], 0, 1.0)
    h0 = st_ssm[0].reshape(nb, SSM_PAIRS, LANES, SSM_STATE)
    y, ssm_h = _ssd_step(act3, ps.reshape(steps, nb, LANES), p["ssm_alog_row"], p["ssm_dtb_row"],
                         p["ssm_dskip_row"], p["ssm_consts"], h0)
    tail = u3[steps - (CONV_WIDTH - 1):].astype(F32)
    ssm_c = tok(jnp.concatenate([tail[..., :SSM_D_INNER], tail[..., 2 * SSM_D_INNER:]], axis=-1))
    x = _mixer_out(y, pm, 1, p["ssm_norm"], x, mod3[1], rows, p["ssm_w_out"], SSM_D_INNER // SSM_GROUPS, True)
    _, y_out = _ffn(x, 1, mod3[1], rows, rows, p, p["norm_final"])

    return (tok(y_out.reshape(steps, nb, D_MODEL)), gdn_s[None], gdn_c[None],
            ssm_h.reshape(nb, SSM_HEADS, SSM_HEAD_DIM, SSM_STATE)[None], ssm_c[None])


def kernel(x_prompt, x_sample, c_prompt, c_sample, state_gdn, state_gdn_conv, state_ssm, state_ssm_conv, w_mod, b_mod,
           norm_mix, norm_ffn, norm_final, gdn_w_in, gdn_conv_w, gdn_a_log, gdn_dt_bias, gdn_norm, gdn_w_out, ssm_w_in,
           ssm_conv_w, ssm_conv_b, ssm_a_log, ssm_dt_bias, ssm_d, ssm_norm, ssm_w_out, ffn_w_gate_up, ffn_w_down):
    p = _stage_params(w_mod, b_mod, norm_mix, norm_ffn, norm_final, gdn_w_in, gdn_conv_w, gdn_a_log, gdn_dt_bias,
                      gdn_norm, gdn_w_out, ssm_w_in, ssm_conv_w, ssm_conv_b, ssm_a_log, ssm_dt_bias, ssm_d, ssm_norm,
                      ssm_w_out, ffn_w_gate_up, ffn_w_down)
    n_prompt = x_prompt.shape[0]
    mod = _modulation(jnp.concatenate([c_prompt, c_sample], axis=0), p["w_mod"], p["b_mod"])
    y_p, gs_p, gc_p, ss_p, sc_p = _trunk_seq(x_prompt, mod[:, :n_prompt], p)
    y_s, gs_s, gc_s, ss_s, sc_s = _trunk_step(x_sample, mod[:, n_prompt:], state_gdn, state_gdn_conv, state_ssm,
                                              state_ssm_conv, p)
    return (y_p, y_s, gs_p, gc_p, ss_p, sc_p, gs_s, gc_s, ss_s, sc_s)
```

```python
import functools

import numpy as np
import jax
import jax.numpy as jnp
from jax import lax
from jax.experimental import pallas as pl
from jax.experimental.pallas import tpu as pltpu

F32 = jnp.float32
BF16 = jnp.bfloat16

D_MODEL = 1024
EPS = 1e-6
CONV_WIDTH = 4
CHUNK = 64
LANES = 128
HALF = LANES // 2

GDN_QK_HEADS = 8
GDN_V_HEADS = 16
GDN_HEAD_DIM = 128
GDN_QK_DIM = GDN_QK_HEADS * GDN_HEAD_DIM
GDN_V_DIM = GDN_V_HEADS * GDN_HEAD_DIM
GDN_CONV_DIM = 2 * GDN_QK_DIM + GDN_V_DIM
GDN_MAIN = GDN_CONV_DIM + GDN_V_DIM

SSM_D_INNER = 2 * D_MODEL
SSM_HEAD_DIM = 64
SSM_HEADS = SSM_D_INNER // SSM_HEAD_DIM
SSM_GROUPS = 4
SSM_STATE = 128
SSM_BC = SSM_GROUPS * SSM_STATE
SSM_CONV_DIM = SSM_D_INNER + 2 * SSM_BC
SSM_MAIN = SSM_D_INNER + SSM_CONV_DIM

FFN_HIDDEN = 2816

VMEM_LIMIT = 56 * 1024 * 1024
NEG_BIG = -1e30


def _cparams(*sem):
    return pltpu.CompilerParams(dimension_semantics=sem, vmem_limit_bytes=VMEM_LIMIT)


def _silu(x):
    hx = 0.5 * x
    return hx + hx * jnp.tanh(hx)


def _softplus(x):
    return jnp.maximum(x, 0.0) + jnp.log1p(jnp.exp(-jnp.abs(x)))


def _dot(a, b):
    return jnp.dot(a, b, preferred_element_type=F32)


def _dot_nt(a, b):
    return lax.dot_general(a, b, (((1,), (1,)), ((), ())), preferred_element_type=F32)


def _tile_rows(v, rep):
    return v if rep == 1 else jnp.concatenate([v] * rep, axis=0)


def _mod_kernel(c_ref, w_ref, b_ref, o_ref):
    cs = _silu(c_ref[...]).astype(BF16)
    o_ref[...] = _dot(cs, w_ref[...].astype(BF16)) + b_ref[...]


def _modulation(c, w_mod, b_mod):
    depth, _, n = w_mod.shape
    bc = c.shape[0]
    tn = 1536
    return pl.pallas_call(
        _mod_kernel,
        grid=(depth, n // tn),
        in_specs=[
            pl.BlockSpec((bc, D_MODEL), lambda l, j: (0, 0)),
            pl.BlockSpec((None, D_MODEL, tn), lambda l, j: (l, 0, j)),
            pl.BlockSpec((None, 1, tn), lambda l, j: (l, 0, j)),
        ],
        out_specs=pl.BlockSpec((None, bc, tn), lambda l, j: (l, 0, j)),
        out_shape=jax.ShapeDtypeStruct((depth, bc, n), F32),
        compiler_params=_cparams("parallel", "parallel"),
        name="adaln_mod",
    )(c, w_mod, b_mod.reshape(depth, 1, n))


def _norm_mod(x, nw, sc, sh, rep):
    y = x * lax.rsqrt(jnp.mean(x * x, axis=-1, keepdims=True) + EPS) * nw
    return y * (1.0 + _tile_rows(sc, rep)) + _tile_rows(sh, rep)


def _in_proj_kernel(x_ref, nw_ref, sc_ref, sh_ref, w_ref, w2_ref, o_ref, o2_ref, h_ref, *, rep, gate_cols):
    j = pl.program_id(1)
    tn = o_ref.shape[1]
    lo, hi = gate_cols

    @pl.when(j == 0)
    def _():
        h = _norm_mod(x_ref[...], nw_ref[...], sc_ref[...], sh_ref[...], rep).astype(BF16)
        h_ref[...] = h
        o2_ref[...] = _dot(h, w2_ref[...])

    has_gate = jnp.logical_and(j * tn < hi, (j + 1) * tn > lo)

    @pl.when(has_gate)
    def _():
        r = _dot(h_ref[...], w_ref[...])
        col = j * tn + lax.broadcasted_iota(jnp.int32, r.shape, 1)
        o_ref[...] = jnp.where(jnp.logical_and(col >= lo, col < hi), _silu(r), r).astype(o_ref.dtype)

    @pl.when(jnp.logical_not(has_gate))
    def _():
        o_ref[...] = _dot(h_ref[...], w_ref[...]).astype(o_ref.dtype)


def _ffn_up_kernel(x_ref, nw_ref, sc_ref, sh_ref, wg_ref, wu_ref, o_ref, h_ref, *, rep):
    @pl.when(pl.program_id(1) == 0)
    def _():
        h_ref[...] = _norm_mod(x_ref[...], nw_ref[...], sc_ref[...], sh_ref[...], rep).astype(BF16)

    h = h_ref[...]
    o_ref[...] = (_silu(_dot(h, wg_ref[...])) * _dot(h, wu_ref[...])).astype(BF16)


class _Rows:
    def __init__(self, m, tm, group_rows, mod_rows):
        assert m % tm == 0
        self.m, self.tm = m, tm
        if mod_rows == 1:
            assert group_rows % tm == 0
            self.rep = 1
            self.gmap = lambda i: (i * tm) // group_rows
        else:
            assert tm % mod_rows == 0
            self.rep = tm // mod_rows
            self.gmap = lambda i: 0
        self.mod_rows = mod_rows

    def mod_spec(self, col_block, with_j):
        if with_j:
            return pl.BlockSpec((None, self.mod_rows, D_MODEL), lambda i, j: (self.gmap(i), 0, col_block))
        return pl.BlockSpec((None, self.mod_rows, D_MODEL), lambda i: (self.gmap(i), 0, col_block))


def _in_proj(x, nw, mod3, rows, sc_blk, sh_blk, w, w2, tn, gate_cols):
    m, tm = rows.m, rows.tm
    n = w.shape[1]
    assert n % tn == 0
    return pl.pallas_call(
        functools.partial(_in_proj_kernel, rep=rows.rep, gate_cols=gate_cols),
        grid=(m // tm, n // tn),
        in_specs=[
            pl.BlockSpec((tm, D_MODEL), lambda i, j: (i, 0)),
            pl.BlockSpec((1, D_MODEL), lambda i, j: (0, 0)),
            rows.mod_spec(sc_blk, True),
            rows.mod_spec(sh_blk, True),
            pl.BlockSpec((D_MODEL, tn), lambda i, j: (0, j)),
            pl.BlockSpec((D_MODEL, LANES), lambda i, j: (0, 0)),
        ],
        out_specs=[
            pl.BlockSpec((tm, tn), lambda i, j: (i, j)),
            pl.BlockSpec((tm, LANES), lambda i, j: (i, 0)),
        ],
        out_shape=[jax.ShapeDtypeStruct((m, n), BF16), jax.ShapeDtypeStruct((m, LANES), F32)],
        scratch_shapes=[pltpu.VMEM((tm, D_MODEL), BF16)],
        compiler_params=_cparams("parallel", "arbitrary"),
        name="in_proj",
    )(x, nw.reshape(1, D_MODEL), mod3, mod3, w, w2)


def _ffn_up(x, nw, mod3, rows, wg, wu, th):
    m, tm = rows.m, rows.tm
    assert FFN_HIDDEN % th == 0
    return pl.pallas_call(
        functools.partial(_ffn_up_kernel, rep=rows.rep),
        grid=(m // tm, FFN_HIDDEN // th),
        in_specs=[
            pl.BlockSpec((tm, D_MODEL), lambda i, j: (i, 0)),
            pl.BlockSpec((1, D_MODEL), lambda i, j: (0, 0)),
            rows.mod_spec(4, True),
            rows.mod_spec(3, True),
            pl.BlockSpec((D_MODEL, th), lambda i, j: (0, j)),
            pl.BlockSpec((D_MODEL, th), lambda i, j: (0, j)),
        ],
        out_specs=pl.BlockSpec((tm, th), lambda i, j: (i, j)),
        out_shape=jax.ShapeDtypeStruct((m, FFN_HIDDEN), BF16),
        scratch_shapes=[pltpu.VMEM((tm, D_MODEL), BF16)],
        compiler_params=_cparams("parallel", "arbitrary"),
        name="ffn_up",
    )(x, nw.reshape(1, D_MODEL), mod3, mod3, wg, wu)


def _resid_store(acc, x_ref, gt_ref, o_ref, fnw_ref, rep):
    xn = x_ref[...] + _tile_rows(gt_ref[...], rep) * acc
    if fnw_ref is not None:
        xn = xn * lax.rsqrt(jnp.mean(xn * xn, axis=-1, keepdims=True) + EPS) * fnw_ref[...]
    o_ref[...] = xn


def _ffn_down_kernel(a_ref, x_ref, gt_ref, w_ref, *rest, rep, final):
    fnw_ref, o_ref = rest if final else (None, rest[0])
    _resid_store(_dot(a_ref[...], w_ref[...]), x_ref, gt_ref, o_ref, fnw_ref, rep)


def _ffn_down(act, x, mod3, rows, w, fnw):
    m, tm = rows.m, rows.tm
    final = fnw is not None
    in_specs = [
        pl.BlockSpec((tm, FFN_HIDDEN), lambda i: (i, 0)),
        pl.BlockSpec((tm, D_MODEL), lambda i: (i, 0)),
        rows.mod_spec(5, False),
        pl.BlockSpec((FFN_HIDDEN, D_MODEL), lambda i: (0, 0)),
    ]
    args = [act, x, mod3, w]
    if final:
        in_specs.append(pl.BlockSpec((1, D_MODEL), lambda i: (0, 0)))
        args.append(fnw.reshape(1, D_MODEL))
    return pl.pallas_call(
        functools.partial(_ffn_down_kernel, rep=rows.rep, final=final),
        grid=(m // tm,),
        in_specs=in_specs,
        out_specs=pl.BlockSpec((tm, D_MODEL), lambda i: (i, 0)),
        out_shape=jax.ShapeDtypeStruct((m, D_MODEL), F32),
        compiler_params=_cparams("parallel"),
        name="ffn_down",
    )(*args)


def _mixer_out_kernel(y_ref, z_ref, nw_ref, x_ref, gt_ref, w_ref, o_ref, a_ref, *, rep, group, gate_first):
    width = y_ref.shape[1]
    for s in range(0, width, group):
        y = y_ref[:, s:s + group].astype(F32)
        gate = z_ref[:, s:s + group].astype(F32)
        if gate_first:
            y = y * gate
        y = y * lax.rsqrt(jnp.mean(y * y, axis=-1, keepdims=True) + EPS) * nw_ref[:, s:s + group]
        if not gate_first:
            y = y * gate
        a_ref[:, s:s + group] = y.astype(BF16)
    _resid_store(_dot(a_ref[...], w_ref[...]), x_ref, gt_ref, o_ref, None, rep)


def _mixer_out(y, zsrc, z_blk, nw_full, x, mod3, rows, w, group, gate_first):
    m, tm = rows.m, rows.tm
    width = y.shape[1]
    return pl.pallas_call(
        functools.partial(_mixer_out_kernel, rep=rows.rep, group=group, gate_first=gate_first),
        grid=(m // tm,),
        in_specs=[
            pl.BlockSpec((tm, width), lambda i: (i, 0)),
            pl.BlockSpec((tm, width), lambda i: (i, z_blk)),
            pl.BlockSpec((1, width), lambda i: (0, 0)),
            pl.BlockSpec((tm, D_MODEL), lambda i: (i, 0)),
            rows.mod_spec(2, False),
            pl.BlockSpec((width, D_MODEL), lambda i: (0, 0)),
        ],
        out_specs=pl.BlockSpec((tm, D_MODEL), lambda i: (i, 0)),
        out_shape=jax.ShapeDtypeStruct((m, D_MODEL), F32),
        scratch_shapes=[pltpu.VMEM((tm, width), BF16)],
        compiler_params=_cparams("parallel"),
        name="mixer_out",
    )(y, zsrc, nw_full.reshape(1, width), x, mod3, w)


CONV_COLS = 1024


def _post_conv(acc, o_ref, cb, n_l2, qscale):
    y = _silu(acc)
    if n_l2 == 0:
        o_ref[...] = y
        return

    @pl.when(cb < n_l2)
    def _():
        scale = jnp.where(cb == 0, qscale, 1.0).astype(F32)
        for s in range(0, CONV_COLS, GDN_HEAD_DIM):
            yh = y[:, s:s + GDN_HEAD_DIM]
            o_ref[:, s:s + GDN_HEAD_DIM] = yh * lax.rsqrt(jnp.sum(yh * yh, axis=-1, keepdims=True) + EPS) * scale

    @pl.when(cb >= n_l2)
    def _():
        o_ref[...] = y


BHALO = 16
CONV_GROUP = 2 * LANES


def _shift_matrix():
    s = np.zeros(((CONV_WIDTH - 1) * CHUNK, BHALO + CHUNK), np.float32)
    for tap in range(CONV_WIDTH - 1):
        for r in range(CHUNK):
            s[tap * CHUNK + r, BHALO - (CONV_WIDTH - 1) + tap + r] = 1.0
    return jnp.asarray(s, BF16)


def _conv_block(raw_refs, shift_ref, w_ref, b_ref, ext_ref, act_ref, seq_start, l2_cols, q_cols, qscale):
    width = ext_ref.shape[1]
    if seq_start:
        ext_ref[0:BHALO, :] = jnp.zeros((BHALO, width), BF16)
    else:
        ext_ref[0:BHALO, :] = ext_ref[CHUNK:CHUNK + BHALO, :]

    off = 0
    for ref in raw_refs:
        ext_ref[BHALO:BHALO + CHUNK, off:off + ref.shape[1]] = ref[...]
        off += ref.shape[1]

    def lane_group(s):
        cols = slice(s, s + CONV_GROUP)
        sh = _dot(shift_ref[...], ext_ref[:, cols])
        acc = b_ref[:, cols] + w_ref[CONV_WIDTH - 1:CONV_WIDTH, cols] * ext_ref[BHALO:BHALO + CHUNK, cols].astype(F32)
        for tap in range(CONV_WIDTH - 1):
            acc = acc + w_ref[tap:tap + 1, cols] * sh[tap * CHUNK:(tap + 1) * CHUNK]
        y = _silu(acc)
        for h in range(s, s + CONV_GROUP, LANES):
            yh = y[:, h - s:h - s + LANES]
            if h < l2_cols:
                yh = yh * lax.rsqrt(jnp.sum(yh * yh, axis=-1, keepdims=True) + EPS)
                if h < q_cols:
                    yh = yh * qscale
            act_ref[:, h:h + LANES] = yh

    return [functools.partial(lane_group, s) for s in range(0, width, CONV_GROUP)]


class _Filler:
    def __init__(self, tasks, shares):
        self.tasks, self.per = list(tasks), -(-len(tasks) // shares)

    def emit(self):
        for task in self.tasks[:self.per]:
            task()
        self.tasks = self.tasks[self.per:]

    def flush(self):
        for task in self.tasks:
            task()
        self.tasks = []


_NO_FILL = _Filler([], 1)


def _conv_steps_kernel(u_ref, hist_ref, w_ref, b_ref, o_ref, *, steps, n_l2, qscale):
    cb = pl.program_id(0)
    ext = [hist_ref[i] for i in range(CONV_WIDTH - 1)] + [u_ref[i].astype(F32) for i in range(steps)]
    for t in range(steps):
        acc = b_ref[...] + w_ref[0:1, :] * ext[t]
        for tap in range(1, CONV_WIDTH):
            acc = acc + w_ref[tap:tap + 1, :] * ext[t + tap]
        _post_conv(acc, o_ref.at[t], cb, n_l2, qscale)


def _conv_steps(u3, skip_at, skip, n_cols, hist3, conv_w, conv_b, n_l2, qscale):
    steps, nb, _ = u3.shape
    return pl.pallas_call(
        functools.partial(_conv_steps_kernel, steps=steps, n_l2=n_l2, qscale=qscale),
        grid=(n_cols // CONV_COLS,),
        in_specs=[
            pl.BlockSpec((steps, nb, CONV_COLS), lambda c: (0, 0, c + skip * (c >= skip_at))),
            pl.BlockSpec((CONV_WIDTH - 1, nb, CONV_COLS), lambda c: (0, 0, c)),
            pl.BlockSpec((CONV_WIDTH, CONV_COLS), lambda c: (0, c)),
            pl.BlockSpec((1, CONV_COLS), lambda c: (0, c)),
        ],
        out_specs=pl.BlockSpec((steps, nb, CONV_COLS), lambda c: (0, 0, c)),
        out_shape=jax.ShapeDtypeStruct((steps, nb, n_cols), F32),
        compiler_params=_cparams("parallel"),
        name="conv_steps",
    )(u3, hist3, conv_w, conv_b.reshape(1, n_cols))


def _split3(x):
    hi = x.astype(BF16)
    r = x - hi.astype(F32)
    mid = r.astype(BF16)
    lo = (r - mid.astype(F32)).astype(BF16)
    return hi, mid, lo


def _cat3(x, axis):
    return jnp.concatenate(_split3(x), axis=axis)


def _pad_t(tile):
    return jnp.concatenate([tile, jnp.zeros_like(tile)], axis=0).T


def _pair_rows(t, n):
    return t[0:n] + pltpu.roll(t[n:2 * n], HALF, 1)


def _iotas(shape):
    return lax.broadcasted_iota(jnp.int32, shape, 0), lax.broadcasted_iota(jnp.int32, shape, 1)


def _pad_rows(a):
    if a.shape[0] == CHUNK:
        return a
    return jnp.concatenate([a, jnp.zeros((CHUNK - a.shape[0], a.shape[1]), a.dtype)], axis=0)


def _split2(x):
    hi = x.astype(BF16)
    return hi, (x - hi.astype(F32)).astype(BF16)


def _block_diag(pair_bf, bd_ones):
    pair_bf = _pad_rows(pair_bf)
    return jnp.concatenate([pair_bf, pair_bf], axis=0) * bd_ones


def _mm3(lhs_parts, rhs_hi, rhs_lo):
    lh, ll = lhs_parts
    return _dot(jnp.concatenate([lh, lh, ll], axis=1), jnp.concatenate([rhs_hi, rhs_lo, rhs_hi], axis=0))


def _mm_pairs(lhs_parts, rhs_parts, bd_fn, terms=3):
    if terms == 1:
        return [_dot(lh, bd_fn(rh)) for (lh, _), (rh, _) in zip(lhs_parts, rhs_parts)]
    return [_mm3(lp, bd_fn(rh), bd_fn(rl)) for lp, (rh, rl) in zip(lhs_parts, rhs_parts)]


def _mm_hl(lhs, rhs_bf):
    return _dot(lhs.astype(BF16), rhs_bf)


def _unit_lower_inverse(xs, levels, bd_fn, eye, fill, terms=3):
    ps = [eye + x for x in xs]
    if levels <= 1:
        return ps
    rows = xs[0].shape[0]
    stack = lambda a, b: tuple(jnp.concatenate([s, t], axis=0) for s, t in zip(a, b))
    ysp = [_split2(x) for x in xs]
    ys = _mm_pairs(ysp, ysp, bd_fn, terms)
    fill.emit()
    for _ in range(levels - 2):
        ysp = [_split2(y) for y in ys]
        rs = _mm_pairs([stack(yp, _split2(p)) for yp, p in zip(ysp, ps)], ysp, bd_fn, terms)
        fill.emit()
        ys = [r[0:rows] for r in rs]
        ps = [p + r[rows:2 * rows] for p, r in zip(ps, rs)]
    last = _mm_pairs([_split2(p) for p in ps], [_split2(y) for y in ys], bd_fn, terms)
    return [p + t for p, t in zip(ps, last)]


QUARTER = HALF // 2


def _unit_lower_inverse_blocked(xs, fill):
    row, lane = _iotas((QUARTER, LANES))
    r2, l2 = _iotas((LANES, LANES))
    even = jnp.bitwise_and(jnp.right_shift(lane, 5), 1) == 0
    eye_q = jnp.where(row == jnp.bitwise_and(lane, QUARTER - 1), 1.0, 0.0)
    bd4_ones = jnp.where(jnp.right_shift(r2, 5) == jnp.right_shift(l2, 5), 1.0, 0.0).astype(BF16)
    row_q = jnp.right_shift(r2, 5)
    place = jnp.where(((row_q == 1) & (jnp.right_shift(l2, 5) == 0)) | ((row_q == 3) & (jnp.right_shift(l2, 5) == 2)),
                      1.0, 0.0).astype(BF16)
    bd4 = lambda a: jnp.concatenate([a] * 4, axis=0) * bd4_ones

    tops = [x[0:QUARTER] for x in xs]
    bots = [x[QUARTER:CHUNK] for x in xs]
    diag = [jnp.where(even, t, b) for t, b in zip(tops, bots)]
    dinv = _unit_lower_inverse(diag, _levels(QUARTER), bd4, eye_q, fill)
    fill.emit()
    ai = [jnp.where(even, d, 0.0) for d in dinv]
    ci = [jnp.where(even, 0.0, d) for d in dinv]
    b_ai = _mm_pairs([_split2(jnp.where(even, b, 0.0)) for b in bots], [_split2(a) for a in ai], bd4)
    fill.emit()
    low = [_mm3(_split2(c), *[jnp.concatenate([part] * 4, axis=0) * place for part in _split2(e)])
           for c, e in zip(ci, b_ai)]
    return [jnp.concatenate([a, l + c], axis=0) for a, l, c in zip(ai, low, ci)]


def _two_blocks(a, b):
    a_bf, b_bf = _pad_rows(a).astype(BF16), _pad_rows(b).astype(BF16)
    z = jnp.zeros_like(a_bf)
    return jnp.concatenate([jnp.concatenate([a_bf, z], axis=1), jnp.concatenate([z, b_bf], axis=1)], axis=0)


def _recurrence_consts(n_pairs, chan_even0, chan_odd0):
    tril = np.tril(np.ones((CHUNK, CHUNK), np.float32))
    tril3 = np.concatenate([tril] * 3, axis=1)
    upper = np.zeros((LANES, 2 * LANES), np.float32)
    upper[:CHUNK, :CHUNK] = tril.T
    upper[:CHUNK, LANES:] = 1.0
    upper3 = np.concatenate([upper] * 3, axis=0)
    e = np.zeros((LANES, n_pairs * LANES), np.float32)
    for p in range(n_pairs):
        e[chan_even0 + p, p * LANES:p * LANES + HALF] = 1.0
        e[chan_odd0 + p, p * LANES + HALF:(p + 1) * LANES] = 1.0
    e3 = np.concatenate([e] * 3, axis=0)
    return jnp.asarray(tril3, BF16), jnp.asarray(upper3, BF16), jnp.asarray(e3, BF16)


def _levels(lr):
    return max(1, int(np.ceil(np.log2(lr))))


STEP_ROWS = 16
DECODE_TOKENS_SINGLE_TERM = 4


GDN_PAIRS = GDN_V_HEADS // 2
GDN_PAIR_BATCH = 8
GDN_FILL_SHARES = 12 * (GDN_PAIRS // GDN_PAIR_BATCH)


def _gdn_gates(raw, alog_row, dtb_row, lr):
    row, lane = _iotas((CHUNK, LANES))
    g = -jnp.exp(alog_row) * _softplus(raw + dtb_row)
    beta = jax.nn.sigmoid(raw)
    gt = jnp.where(lane < GDN_V_HEADS, g, jnp.where(lane < 2 * GDN_V_HEADS, beta, 0.0))
    if lr < CHUNK:
        gt = jnp.where(row < lr, gt, 0.0)
    return gt


def _gdn_block(insts, tril3_ref, upper3_ref, e3_ref, lr, fill=_NO_FILL):
    np_ = GDN_PAIRS
    rows = insts[0][0].shape[0]
    row, lane = _iotas((rows, LANES))
    jl = jnp.bitwise_and(lane, HALF - 1)
    left = lane < HALF
    causal = row >= jl
    strict = row > jl
    eye2 = jnp.where(row == jl, 1.0, 0.0)
    r2, l2 = _iotas((LANES, LANES))
    bd_ones = jnp.where(jnp.right_shift(r2, 6) == jnp.right_shift(l2, 6), 1.0, 0.0).astype(BF16)
    _, lane_p = _iotas((np_, LANES))
    sl = lambda a, i: a[:, i * LANES:(i + 1) * LANES]

    pre = []
    for _, gt, _, _, _ in insts:
        t1 = _pad_t(gt)
        cum_ext = _dot(_cat3(t1[0:2 * np_], 1), upper3_ref[...])
        cum_t, last_b = cum_ext[:, :LANES], cum_ext[:, LANES:]
        cum_rp = _pair_rows(cum_t, np_)
        beta_rp = _pair_rows(t1[2 * np_:4 * np_], np_)
        last_rp = jnp.where(lane_p < HALF, last_b[0:np_], last_b[np_:2 * np_])
        cum = _dot(tril3_ref[...], _cat3(gt, 0))
        pre.append(dict(cum_rp=cum_rp, beta_rp=beta_rp, ecum_rp=jnp.exp(cum_rp),
                        kdec_rp=jnp.exp(last_rp - cum_rp) * beta_rp, elast=jnp.exp(last_b),
                        col_all=_dot(_cat3(cum[0:rows], 1), e3_ref[...])))

    def run(items):
        heads = [(i, 2 * p + hh) for i, p in items for hh in range(2)]
        act = lambda i: insts[i][0]
        prow = lambda name, it: pre[it[0]][name][it[1]:it[1] + 1]
        q = {it: sl(act(it[0]), it[1]) for it in items}
        k = {it: _pad_rows(sl(act(it[0]), GDN_QK_HEADS + it[1])) for it in items}
        v = {ih: sl(act(ih[0]), 2 * GDN_QK_HEADS + ih[1]) for ih in heads}
        kb = {it: k[it].astype(BF16) for it in items}
        qb = {it: q[it].astype(BF16) for it in items}
        gq = {it: _dot_nt(jnp.concatenate([kb[it][0:rows], qb[it]], axis=0), jnp.concatenate([kb[it], kb[it]], axis=0))
              for it in items}
        fill.emit()
        colc = {it: sl(pre[it[0]]["col_all"], it[1]) for it in items}
        base = {it: jnp.exp(jnp.where(causal, colc[it] - prow("cum_rp", it), NEG_BIG)) * prow("beta_rp", it)
                for it in items}
        x = [jnp.where(strict, -(gq[it][0:rows] * base[it]), 0.0) for it in items]
        qkd = {it: (gq[it][rows:2 * rows] * base[it]).astype(BF16) for it in items}
        if lr == CHUNK:
            minv = dict(zip(items, _unit_lower_inverse_blocked(x, fill)))
        else:
            minv = dict(zip(items, _unit_lower_inverse(x, _levels(lr), lambda a: _block_diag(a, bd_ones), eye2, fill,
                                                       terms=1 if lr <= DECODE_TOKENS_SINGLE_TERM else 3)))
        fill.emit()
        pair_of = lambda ih: (ih[0], ih[1] // 2)
        u = {it: jnp.concatenate([v[(it[0], 2 * it[1])], v[(it[0], 2 * it[1] + 1)]], axis=1)
             + _mm_hl(minv[it] - eye2, _two_blocks(v[(it[0], 2 * it[1])], v[(it[0], 2 * it[1] + 1)])) for it in items}
        fill.emit()
        w = {it: _mm_hl(minv[it] * prow("ecum_rp", it), _two_blocks(kb[it], kb[it])) for it in items}
        fill.emit()
        s_old = {ih: insts[ih[0]][2][ih[1]] for ih in heads}
        r = {ih: _dot(jnp.concatenate([sl(w[pair_of(ih)], ih[1] % 2).astype(BF16), qb[pair_of(ih)]], axis=0),
                      s_old[ih].astype(BF16)) for ih in heads}
        fill.emit()
        delta = {ih: sl(u[pair_of(ih)], ih[1] % 2) - r[ih][0:rows] for ih in heads}
        bd_delta = {it: _two_blocks(delta[(it[0], 2 * it[1])], delta[(it[0], 2 * it[1] + 1)]) for it in items}
        fill.emit()
        od = {it: _dot(jnp.concatenate(
            [qkd[it], (jnp.concatenate([k[it], k[it]], axis=0).T * prow("kdec_rp", it)).astype(BF16)], axis=0),
            bd_delta[it]) for it in items}
        for it in items:
            i, p = it
            ecol = jnp.exp(colc[it])
            ecol_r = pltpu.roll(ecol, HALF, 1)
            efull = (jnp.where(left, ecol, ecol_r), jnp.where(left, ecol_r, ecol))
            for hh in range(2):
                h = 2 * p + hh
                insts[i][4][:, h * LANES:(h + 1) * LANES] = (
                    efull[hh] * r[(i, h)][rows:2 * rows] + sl(od[it][0:rows], hh)).astype(insts[i][4].dtype)
                e_h = pre[i]["elast"][hh * np_ + p:hh * np_ + p + 1]
                insts[i][3][h] = e_h * s_old[(i, h)] + sl(od[it][rows:rows + LANES], hh)

    for b0 in range(0, np_, GDN_PAIR_BATCH):
        run([(i, p) for i in range(len(insts)) for p in range(b0, b0 + GDN_PAIR_BATCH)])
    fill.flush()


def _gdn_seq_kernel(raw0_ref, rawn_ref, graw_ref, shift_ref, cw_ref, cb_ref, alog_ref, dtb_ref, tril3_ref, upper3_ref,
                    e3_ref, o_ref, s_ref, ext_ref, act_ref):
    n = pl.program_id(1)
    seqs = range(graw_ref.shape[0])
    conv = functools.partial(_conv_block, shift_ref=shift_ref, w_ref=cw_ref, b_ref=cb_ref,
                             l2_cols=2 * GDN_QK_DIM, q_cols=GDN_QK_DIM, qscale=QSCALE)

    @pl.when(n == 0)
    def _():
        s_ref[...] = jnp.zeros(s_ref.shape, F32)
        for j in seqs:
            _Filler(conv([raw0_ref.at[j]], ext_ref=ext_ref.at[j], act_ref=act_ref.at[j, 0], seq_start=True), 1).flush()

    @pl.when(n > 0)
    def _():
        for j in seqs:
            act_ref[j, 0] = act_ref[j, 1]

    tasks, insts = [], []
    for j in seqs:
        gt = _gdn_gates(graw_ref[j], alog_ref[...], dtb_ref[...], CHUNK)
        tasks += conv([rawn_ref.at[j]], ext_ref=ext_ref.at[j], act_ref=act_ref.at[j, 1], seq_start=False)
        insts.append((act_ref.at[j, 0], gt, s_ref.at[j], s_ref.at[j], o_ref.at[j]))
    _gdn_block(insts, tril3_ref, upper3_ref, e3_ref, CHUNK, _Filler(tasks, GDN_FILL_SHARES))


def _const_specs(consts, ngrid):
    zero = (lambda *_: (0, 0))
    return [pl.BlockSpec(c.shape, zero) for c in consts]


GDN_SEQ_TOGETHER = 2
SSD_SEQ_TOGETHER = 1


def _seq_views(nseq, seq_len, together):
    together = min(together, nseq)
    assert nseq % together == 0
    nc = seq_len // CHUNK
    per = nseq // together
    last = per * nc - 1
    spec = lambda width, idx, col=0: pl.BlockSpec((together, CHUNK, width), lambda b, n: (0, idx(b, n), col))
    cur = lambda b, n: b * nc + n
    first = lambda b, n: b * nc
    nxt = lambda b, n: jnp.minimum(b * nc + n + 1, last)
    view = lambda a: a.reshape(together, a.shape[0] // together, a.shape[1])
    return together, nc, per, spec, cur, first, nxt, view


def _gdn_seq(pm, graw, conv_w, conv_b, alog_row, dtb_row, consts, nseq, seq_len):
    together, nc, per, spec, cur, first, nxt, view = _seq_views(nseq, seq_len, GDN_SEQ_TOGETHER)
    rowspec = pl.BlockSpec((1, LANES), lambda b, n: (0, 0))
    pm3 = view(pm)
    o, s = pl.pallas_call(
        _gdn_seq_kernel,
        grid=(per, nc),
        in_specs=[
            spec(GDN_CONV_DIM, first),
            spec(GDN_CONV_DIM, nxt),
            spec(LANES, cur),
            pl.BlockSpec(((CONV_WIDTH - 1) * CHUNK, BHALO + CHUNK), lambda b, n: (0, 0)),
            pl.BlockSpec((CONV_WIDTH, GDN_CONV_DIM), lambda b, n: (0, 0)),
            pl.BlockSpec((1, GDN_CONV_DIM), lambda b, n: (0, 0)),
            rowspec, rowspec,
        ] + _const_specs(consts, 2),
        out_specs=[
            spec(GDN_V_DIM, cur),
            pl.BlockSpec((together, None, GDN_V_HEADS, GDN_HEAD_DIM, GDN_HEAD_DIM), lambda b, n: (0, b, 0, 0, 0)),
        ],
        out_shape=[
            jax.ShapeDtypeStruct((together, per * seq_len, GDN_V_DIM), BF16),
            jax.ShapeDtypeStruct((together, per, GDN_V_HEADS, GDN_HEAD_DIM, GDN_HEAD_DIM), F32),
        ],
        scratch_shapes=[
            pltpu.VMEM((together, CHUNK + BHALO, GDN_CONV_DIM), BF16),
            pltpu.VMEM((together, 2, CHUNK, GDN_CONV_DIM), F32),
        ],
        compiler_params=_cparams("parallel", "arbitrary"),
        name="gdn_seq",
    )(pm3, pm3, view(graw), _shift_matrix(), conv_w, conv_b.reshape(1, GDN_CONV_DIM), alog_row, dtb_row, *consts)
    return (o.reshape(nseq * seq_len, GDN_V_DIM),
            s.reshape(nseq, GDN_V_HEADS, GDN_HEAD_DIM, GDN_HEAD_DIM))


SEQ_PER_STEP = 8
SEQ_INTERLEAVE = 4


def _load_padded(src_ref, s, pad_ref, lr):
    pad_ref[...] = jnp.zeros(pad_ref.shape, F32)
    for t in range(lr):
        pad_ref[t:t + 1, :] = src_ref[t, pl.ds(s, 1), :]


def _store_tokens(pad_ref, dst_ref, s, lr):
    for t in range(lr):
        dst_ref[t, pl.ds(s, 1), :] = pad_ref[t:t + 1, :]


def _gdn_step_kernel(act_ref, graw_ref, alog_ref, dtb_ref, tril3_ref, upper3_ref, e3_ref, s0_ref,
                     o_ref, s_ref, apad_ref, gpad_ref, opad_ref, *, lr):
    def some_sequences(it, carry):
        insts = []
        for j in range(SEQ_INTERLEAVE):
            s = it * SEQ_INTERLEAVE + j
            _load_padded(act_ref, s, apad_ref.at[j], lr)
            _load_padded(graw_ref, s, gpad_ref.at[j], lr)
            gt = _gdn_gates(gpad_ref[j], alog_ref[...], dtb_ref[...], lr)
            insts.append((apad_ref.at[j], gt, s0_ref.at[s], s_ref.at[s], opad_ref.at[j]))
        _gdn_block(insts, tril3_ref, upper3_ref, e3_ref, lr)
        for j in range(SEQ_INTERLEAVE):
            _store_tokens(opad_ref.at[j], o_ref, it * SEQ_INTERLEAVE + j, lr)
        return carry

    lax.fori_loop(0, SEQ_PER_STEP // SEQ_INTERLEAVE, some_sequences, 0)


def _step_spec(steps, width):
    return pl.BlockSpec((steps, SEQ_PER_STEP, width), lambda g: (0, g, 0))


def _gdn_step(act3, graw3, alog_row, dtb_row, consts, s0):
    steps, nb, _ = act3.shape
    assert nb % SEQ_PER_STEP == 0 and steps <= STEP_ROWS
    rowspec = pl.BlockSpec((1, LANES), lambda b: (0, 0))
    sspec = pl.BlockSpec((SEQ_PER_STEP, GDN_V_HEADS, GDN_HEAD_DIM, GDN_HEAD_DIM), lambda g: (g, 0, 0, 0))
    o, s = pl.pallas_call(
        functools.partial(_gdn_step_kernel, lr=steps),
        grid=(nb // SEQ_PER_STEP,),
        in_specs=[_step_spec(steps, GDN_CONV_DIM), _step_spec(steps, LANES), rowspec, rowspec]
        + _const_specs(consts, 1) + [sspec],
        out_specs=[_step_spec(steps, GDN_V_DIM), sspec],
        out_shape=[
            jax.ShapeDtypeStruct((steps, nb, GDN_V_DIM), F32),
            jax.ShapeDtypeStruct(s0.shape, F32),
        ],
        scratch_shapes=[
            pltpu.VMEM((SEQ_INTERLEAVE, STEP_ROWS, GDN_CONV_DIM), F32),
            pltpu.VMEM((SEQ_INTERLEAVE, CHUNK, LANES), F32),
            pltpu.VMEM((SEQ_INTERLEAVE, STEP_ROWS, GDN_V_DIM), F32),
        ],
        compiler_params=_cparams("parallel"),
        name="gdn_step",
    )(act3, graw3, alog_row, dtb_row, *consts, s0)
    return o.reshape(steps * nb, GDN_V_DIM), s


SSM_PAIRS = SSM_HEADS // 2
PAIRS_PER_GROUP = SSM_PAIRS // SSM_GROUPS


def _ssm_gates(raw, alog_row, dtb_row, lr):
    row, lane = _iotas((CHUNK, LANES))
    dt = _softplus(raw + dtb_row)
    tile = jnp.where(lane < SSM_HEADS, dt, jnp.where(lane < 2 * SSM_HEADS, -jnp.exp(alog_row) * dt, 0.0))
    if lr < CHUNK:
        tile = jnp.where(row < lr, tile, 0.0)
    return tile


def _ssd_block(insts, dskip_ref, tril3_ref, upper3_ref, e3_ref, fill=_NO_FILL):
    np_ = SSM_PAIRS
    rows = insts[0][0].shape[0]
    row, lane = _iotas((rows, LANES))
    causal = row >= jnp.bitwise_and(lane, HALF - 1)
    r2, l2 = _iotas((LANES, LANES))
    bdmask = jnp.right_shift(r2, 6) == jnp.right_shift(l2, 6)
    top = r2 < HALF
    _, lane_p = _iotas((np_, LANES))
    sl = lambda a, i: a[:, i * LANES:(i + 1) * LANES]
    grp = lambda it: (it[0], it[1] // PAIRS_PER_GROUP)

    pre = []
    for _, tile, _, _, _ in insts:
        t1 = _pad_t(tile)
        cum_ext = _dot(_cat3(t1[2 * np_:4 * np_], 1), upper3_ref[...])
        cum_t, last_b = cum_ext[:, :LANES], cum_ext[:, LANES:]
        cum_rp = _pair_rows(cum_t, np_)
        dt_rp = _pair_rows(t1[0:2 * np_], np_)
        last_rp = jnp.where(lane_p < HALF, last_b[0:np_], last_b[np_:2 * np_])
        cum = _dot(tril3_ref[...], _cat3(tile, 0))
        pre.append(dict(cum_rp=cum_rp, dt_rp=dt_rp, coef_rp=jnp.exp(last_rp - cum_rp) * dt_rp,
                        elast=jnp.exp(last_b), col_all=_dot(_cat3(cum[0:rows], 1), e3_ref[...])))

    seqs = range(len(insts))
    items = [(i, p) for i in seqs for p in range(np_)]
    groups = [(i, g) for i in seqs for g in range(SSM_GROUPS)]
    act = lambda i: insts[i][0]
    prow = lambda name, it: pre[it[0]][name][it[1]:it[1] + 1]
    bg = {ig: _pad_rows(sl(act(ig[0]), SSM_D_INNER // LANES + ig[1])).astype(BF16) for ig in groups}
    cg = {ig: sl(act(ig[0]), (SSM_D_INNER + SSM_BC) // LANES + ig[1]).astype(BF16) for ig in groups}
    bb = {ig: jnp.concatenate([bg[ig], bg[ig]], axis=0) for ig in groups}
    cb2 = {ig: _dot_nt(cg[ig], bb[ig]) for ig in groups}
    fill.emit()
    xp = {it: sl(act(it[0]), it[1]) for it in items}
    colc = {it: sl(pre[it[0]]["col_all"], it[1]) for it in items}
    x2 = {it: jnp.concatenate([_pad_rows(xp[it]), _pad_rows(xp[it])], axis=0) for it in items}
    lm = {it: cb2[grp(it)] * jnp.exp(jnp.where(causal, colc[it] - prow("cum_rp", it), NEG_BIG)) * prow("dt_rp", it)
          for it in items}
    y_diag = {}
    for n, it in enumerate(items):
        y_diag[it] = _dot(lm[it].astype(BF16), jnp.where(bdmask, x2[it], 0.0).astype(BF16))
        if n % (4 * len(insts)) == 4 * len(insts) - 1:
            fill.emit()
    hp = {it: insts[it[0]][2][it[1]] for it in items}
    y_off = {}
    for n, it in enumerate(items):
        y_off[it] = _dot_nt(cg[grp(it)], hp[it].astype(BF16))
        if n % (4 * len(insts)) == 4 * len(insts) - 1:
            fill.emit()
    for it in items:
        i, p = it
        insts[i][4][:, p * LANES:(p + 1) * LANES] = (
            y_diag[it] + jnp.exp(colc[it]) * y_off[it]
            + dskip_ref[:, p * LANES:(p + 1) * LANES] * xp[it]).astype(insts[i][4].dtype)
    dh = {}
    for n, it in enumerate(items):
        lhs = jnp.where(bdmask, x2[it].T * prow("coef_rp", it), 0.0)
        dh[it] = _dot(lhs.astype(BF16), bb[grp(it)])
        if n % (4 * len(insts)) == 4 * len(insts) - 1:
            fill.emit()
    for it in items:
        i, p = it
        elast = pre[i]["elast"]
        e_rows = jnp.where(top, elast[p:p + 1], elast[np_ + p:np_ + p + 1])
        insts[i][3][p] = e_rows * hp[it] + dh[it]
    fill.flush()


def _ssd_seq_kernel(x0_ref, bc0_ref, xn_ref, bcn_ref, graw_ref, shift_ref, cw_ref, cb_ref, alog_ref, dtb_ref,
                    dskip_ref, tril3_ref, upper3_ref, e3_ref, y_ref, h_ref, ext_ref, act_ref):
    n = pl.program_id(1)
    seqs = range(graw_ref.shape[0])
    conv = functools.partial(_conv_block, shift_ref=shift_ref, w_ref=cw_ref, b_ref=cb_ref,
                             l2_cols=0, q_cols=0, qscale=1.0)

    @pl.when(n == 0)
    def _():
        h_ref[...] = jnp.zeros(h_ref.shape, F32)
        for j in seqs:
            _Filler(conv([x0_ref.at[j], bc0_ref.at[j]], ext_ref=ext_ref.at[j], act_ref=act_ref.at[j, 0],
                         seq_start=True), 1).flush()

    @pl.when(n > 0)
    def _():
        for j in seqs:
            act_ref[j, 0] = act_ref[j, 1]

    tasks, insts = [], []
    for j in seqs:
        tile = _ssm_gates(graw_ref[j], alog_ref[...], dtb_ref[...], CHUNK)
        tasks += conv([xn_ref.at[j], bcn_ref.at[j]], ext_ref=ext_ref.at[j], act_ref=act_ref.at[j, 1], seq_start=False)
        insts.append((act_ref.at[j, 0], tile, h_ref.at[j], h_ref.at[j], y_ref.at[j]))
    _ssd_block(insts, dskip_ref, tril3_ref, upper3_ref, e3_ref, _Filler(tasks, SSD_FILL_SHARES))


SSD_FILL_SHARES = 13


def _ssd_seq(pm, graw, conv_w, conv_b, alog_row, dtb_row, dskip_row, consts, nseq, seq_len):
    together, nc, per, spec, cur, first, nxt, view = _seq_views(nseq, seq_len, SSD_SEQ_TOGETHER)
    bc_blk = 2 * SSM_D_INNER // (2 * SSM_BC)
    rowspec = pl.BlockSpec((1, LANES), lambda b, n: (0, 0))
    pm3 = view(pm)
    y, h = pl.pallas_call(
        _ssd_seq_kernel,
        grid=(per, nc),
        in_specs=[
            spec(SSM_D_INNER, first),
            spec(2 * SSM_BC, first, bc_blk),
            spec(SSM_D_INNER, nxt),
            spec(2 * SSM_BC, nxt, bc_blk),
            spec(LANES, cur),
            pl.BlockSpec(((CONV_WIDTH - 1) * CHUNK, BHALO + CHUNK), lambda b, n: (0, 0)),
            pl.BlockSpec((CONV_WIDTH, SSM_CONV_DIM), lambda b, n: (0, 0)),
            pl.BlockSpec((1, SSM_CONV_DIM), lambda b, n: (0, 0)),
            rowspec, rowspec,
            pl.BlockSpec((1, SSM_D_INNER), lambda b, n: (0, 0)),
        ] + _const_specs(consts, 2),
        out_specs=[
            spec(SSM_D_INNER, cur),
            pl.BlockSpec((together, None, SSM_PAIRS, LANES, SSM_STATE), lambda b, n: (0, b, 0, 0, 0)),
        ],
        out_shape=[
            jax.ShapeDtypeStruct((together, per * seq_len, SSM_D_INNER), BF16),
            jax.ShapeDtypeStruct((together, per, SSM_PAIRS, LANES, SSM_STATE), F32),
        ],
        scratch_shapes=[
            pltpu.VMEM((together, CHUNK + BHALO, SSM_CONV_DIM), BF16),
            pltpu.VMEM((together, 2, CHUNK, SSM_CONV_DIM), F32),
        ],
        compiler_params=_cparams("parallel", "arbitrary"),
        name="ssd_seq",
    )(pm3, pm3, pm3, pm3, view(graw), _shift_matrix(), conv_w, conv_b.reshape(1, SSM_CONV_DIM), alog_row, dtb_row,
      dskip_row, *consts)
    return y.reshape(nseq * seq_len, SSM_D_INNER), h.reshape(nseq, SSM_PAIRS, LANES, SSM_STATE)


def _ssd_step_kernel(act_ref, graw_ref, alog_ref, dtb_ref, dskip_ref, tril3_ref, upper3_ref, e3_ref, h0_ref,
                     y_ref, h_ref, apad_ref, gpad_ref, ypad_ref, *, lr):
    def some_sequences(it, carry):
        insts = []
        for j in range(SEQ_INTERLEAVE):
            s = it * SEQ_INTERLEAVE + j
            _load_padded(act_ref, s, apad_ref.at[j], lr)
            _load_padded(graw_ref, s, gpad_ref.at[j], lr)
            tile = _ssm_gates(gpad_ref[j], alog_ref[...], dtb_ref[...], lr)
            insts.append((apad_ref.at[j], tile, h0_ref.at[s], h_ref.at[s], ypad_ref.at[j]))
        _ssd_block(insts, dskip_ref, tril3_ref, upper3_ref, e3_ref)
        for j in range(SEQ_INTERLEAVE):
            _store_tokens(ypad_ref.at[j], y_ref, it * SEQ_INTERLEAVE + j, lr)
        return carry

    lax.fori_loop(0, SEQ_PER_STEP // SEQ_INTERLEAVE, some_sequences, 0)


def _ssd_step(act3, graw3, alog_row, dtb_row, dskip_row, consts, h0):
    steps, nb, _ = act3.shape
    assert nb % SEQ_PER_STEP == 0 and steps <= STEP_ROWS
    rowspec = pl.BlockSpec((1, LANES), lambda b: (0, 0))
    hspec = pl.BlockSpec((SEQ_PER_STEP, SSM_PAIRS, LANES, SSM_STATE), lambda g: (g, 0, 0, 0))
    y, h = pl.pallas_call(
        functools.partial(_ssd_step_kernel, lr=steps),
        grid=(nb // SEQ_PER_STEP,),
        in_specs=[_step_spec(steps, SSM_CONV_DIM), _step_spec(steps, LANES), rowspec, rowspec,
                  pl.BlockSpec((1, SSM_D_INNER), lambda b: (0, 0))] + _const_specs(consts, 1) + [hspec],
        out_specs=[_step_spec(steps, SSM_D_INNER), hspec],
        out_shape=[
            jax.ShapeDtypeStruct((steps, nb, SSM_D_INNER), F32),
            jax.ShapeDtypeStruct(h0.shape, F32),
        ],
        scratch_shapes=[
            pltpu.VMEM((SEQ_INTERLEAVE, STEP_ROWS, SSM_CONV_DIM), F32),
            pltpu.VMEM((SEQ_INTERLEAVE, CHUNK, LANES), F32),
            pltpu.VMEM((SEQ_INTERLEAVE, STEP_ROWS, SSM_D_INNER), F32),
        ],
        compiler_params=_cparams("parallel"),
        name="ssd_step",
    )(act3, graw3, alog_row, dtb_row, dskip_row, *consts, h0)
    return y.reshape(steps * nb, SSM_D_INNER), h


FFN_TILE = FFN_HIDDEN // 2
GDN_PROJ_TILE = GDN_MAIN // 3
SSM_PROJ_TILE = SSM_MAIN // 2


def _lane_row(pieces):
    row = jnp.zeros((1, LANES), F32)
    for off, vec in pieces:
        row = row.at[0, off:off + vec.shape[0]].set(vec.astype(F32))
    return row


def _stage_params(w_mod, b_mod, norm_mix, norm_ffn, norm_final, gdn_w_in, gdn_conv_w, gdn_a_log, gdn_dt_bias,
                  gdn_norm, gdn_w_out, ssm_w_in, ssm_conv_w, ssm_conv_b, ssm_a_log, ssm_dt_bias, ssm_d, ssm_norm,
                  ssm_w_out, ffn_w_gate_up, ffn_w_down):
    perm_g = np.concatenate([np.arange(0, GDN_V_HEADS, 2), np.arange(1, GDN_V_HEADS, 2)])
    perm_s = np.concatenate([np.arange(0, SSM_HEADS, 2), np.arange(1, SSM_HEADS, 2)])
    g_in, s_in = gdn_w_in[0], ssm_w_in[0]
    beta_cols = g_in[:, GDN_MAIN:GDN_MAIN + GDN_V_HEADS][:, perm_g]
    a_cols = g_in[:, GDN_MAIN + GDN_V_HEADS:GDN_MAIN + 2 * GDN_V_HEADS][:, perm_g]
    gdn_small = jnp.concatenate([a_cols, beta_cols, jnp.zeros((D_MODEL, LANES - 2 * GDN_V_HEADS), F32)], axis=1)
    dt_cols = s_in[:, SSM_MAIN:SSM_MAIN + SSM_HEADS][:, perm_s]
    ssm_small = jnp.concatenate([dt_cols, dt_cols, jnp.zeros((D_MODEL, LANES - 2 * SSM_HEADS), F32)], axis=1)
    return dict(
        w_mod=w_mod, b_mod=b_mod, norm_mix=norm_mix, norm_ffn=norm_ffn, norm_final=norm_final,
        gdn_main=g_in[:, :GDN_MAIN].astype(BF16), gdn_small=gdn_small.astype(BF16),
        gdn_conv_w=gdn_conv_w[0], gdn_conv_b=jnp.zeros((GDN_CONV_DIM,), F32),
        gdn_alog_row=_lane_row([(0, gdn_a_log[0][perm_g])]), gdn_dtb_row=_lane_row([(0, gdn_dt_bias[0][perm_g])]),
        gdn_norm=jnp.tile(gdn_norm[0], GDN_V_HEADS), gdn_w_out=gdn_w_out[0].astype(BF16),
        gdn_consts=_recurrence_consts(GDN_PAIRS, 0, GDN_PAIRS),
        ssm_main=jnp.concatenate(
            [s_in[:, SSM_D_INNER:2 * SSM_D_INNER], s_in[:, :SSM_D_INNER], s_in[:, 2 * SSM_D_INNER:SSM_MAIN]],
            axis=1).astype(BF16),
        ssm_small=ssm_small.astype(BF16),
        ssm_conv_w=ssm_conv_w[0], ssm_conv_b=ssm_conv_b[0],
        ssm_alog_row=_lane_row([(SSM_HEADS, ssm_a_log[0][perm_s])]),
        ssm_dtb_row=_lane_row([(0, ssm_dt_bias[0][perm_s]), (SSM_HEADS, ssm_dt_bias[0][perm_s])]),
        ssm_dskip_row=jnp.repeat(ssm_d[0], SSM_HEAD_DIM).reshape(1, SSM_D_INNER),
        ssm_norm=ssm_norm[0], ssm_w_out=ssm_w_out[0].astype(BF16),
        ssm_consts=_recurrence_consts(SSM_PAIRS, SSM_HEADS, SSM_HEADS + SSM_PAIRS),
        wg=[ffn_w_gate_up[i][:, :FFN_HIDDEN].astype(BF16) for i in range(2)],
        wu=[ffn_w_gate_up[i][:, FFN_HIDDEN:].astype(BF16) for i in range(2)],
        wd=[ffn_w_down[i].astype(BF16) for i in range(2)],
    )


def _ffn(x, layer, mod3, rows_up, rows_down, p, final_w):
    act = _ffn_up(x, p["norm_ffn"][layer], mod3, rows_up, p["wg"][layer], p["wu"][layer], FFN_TILE)
    return _ffn_down(act, x, mod3, rows_down, p["wd"][layer], final_w)


QSCALE = GDN_HEAD_DIM ** -0.5


def _trunk_seq(x3, mod, p):
    nseq, seq_len, _ = x3.shape
    m = nseq * seq_len
    x = x3.reshape(m, D_MODEL)
    mod3 = [mod[l].reshape(nseq, 1, 6 * D_MODEL) for l in range(2)]
    rows_a = _Rows(m, min(1024, seq_len), seq_len, 1)
    rows_b = _Rows(m, min(512, seq_len), seq_len, 1)

    pm, ps = _in_proj(x, p["norm_mix"][0], mod3[0], rows_a, 1, 0, p["gdn_main"], p["gdn_small"], GDN_PROJ_TILE,
                      (GDN_CONV_DIM, GDN_MAIN))
    o, gdn_s = _gdn_seq(pm, ps, p["gdn_conv_w"], p["gdn_conv_b"], p["gdn_alog_row"], p["gdn_dtb_row"],
                        p["gdn_consts"], nseq, seq_len)
    tail = pm.reshape(nseq, seq_len, GDN_MAIN)[:, seq_len - (CONV_WIDTH - 1):].astype(F32)
    gdn_c = tail[..., :GDN_CONV_DIM]
    x = _mixer_out(o, pm, 2, p["gdn_norm"], x, mod3[0], rows_b, p["gdn_w_out"], GDN_HEAD_DIM, False)
    x = _ffn(x, 0, mod3[0], rows_a, rows_b, p, None)

    pm, ps = _in_proj(x, p["norm_mix"][1], mod3[1], rows_a, 1, 0, p["ssm_main"], p["ssm_small"], SSM_PROJ_TILE,
                      (SSM_D_INNER, 2 * SSM_D_INNER))
    y, ssm_h = _ssd_seq(pm, ps, p["ssm_conv_w"], p["ssm_conv_b"], p["ssm_alog_row"], p["ssm_dtb_row"],
                        p["ssm_dskip_row"], p["ssm_consts"], nseq, seq_len)
    tail = pm.reshape(nseq, seq_len, SSM_MAIN)[:, seq_len - (CONV_WIDTH - 1):].astype(F32)
    ssm_c = jnp.concatenate([tail[..., :SSM_D_INNER], tail[..., 2 * SSM_D_INNER:]], axis=-1)
    x = _mixer_out(y, pm, 1, p["ssm_norm"], x, mod3[1], rows_b, p["ssm_w_out"], SSM_D_INNER // SSM_GROUPS, True)
    y_out = _ffn(x, 1, mod3[1], rows_a, rows_b, p, p["norm_final"])

    return (y_out.reshape(nseq, seq_len, D_MODEL), gdn_s[None], gdn_c[None],
            ssm_h.reshape(nseq, SSM_HEADS, SSM_HEAD_DIM, SSM_STATE)[None], ssm_c[None])


def _trunk_step(x3, mod, st_gdn, cv_gdn, st_ssm, cv_ssm, p):
    nb, steps, _ = x3.shape
    assert steps >= CONV_WIDTH - 1
    m = nb * steps
    x = jnp.transpose(x3, (1, 0, 2)).reshape(m, D_MODEL)
    mod3 = [mod[l].reshape(1, nb, 6 * D_MODEL) for l in range(2)]
    rows = _Rows(m, m, None, nb)
    tok = lambda a: jnp.transpose(a, (1, 0, 2))

    pm, ps = _in_proj(x, p["norm_mix"][0], mod3[0], rows, 1, 0, p["gdn_main"], p["gdn_small"], GDN_PROJ_TILE,
                      (GDN_CONV_DIM, GDN_MAIN))
    u3 = pm.reshape(steps, nb, GDN_MAIN)
    act3 = _conv_steps(u3, 0, 0, GDN_CONV_DIM, tok(cv_gdn[0]), p["gdn_conv_w"], p["gdn_conv_b"], 2, QSCALE)
    o, gdn_s = _gdn_step(act3, ps.reshape(steps, nb, LANES), p["gdn_alog_row"], p["gdn_dtb_row"], p["gdn_consts"],
                         st_gdn[0])
    gdn_c = tok(u3[steps - (CONV_WIDTH - 1):, :, :GDN_CONV_DIM].astype(F32))
    x = _mixer_out(o, pm, 2, p["gdn_norm"], x, mod3[0], rows, p["gdn_w_out"], GDN_HEAD_DIM, False)
    x = _ffn(x, 0, mod3[0], rows, rows, p, None)

    pm, ps = _in_proj(x, p["norm_mix"][1], mod3[1], rows, 1, 0, p["ssm_main"], p["ssm_small"], SSM_PROJ_TILE,
                      (SSM_D_INNER, 2 * SSM_D_INNER))
    u3 = pm.reshape(steps, nb, SSM_MAIN)
    act3 = _conv_steps(u3, SSM_D_INNER // CONV_COLS, SSM_D_INNER // CONV_COLS, SSM_CONV_DIM, tok(cv_ssm[0]),
                       p["ssm_conv_w"], p["ssm_conv_b"], 0, 1.0)
    h0 = st_ssm[0].reshape(nb, SSM_PAIRS, LANES, SSM_STATE)
    y, ssm_h = _ssd_step(act3, ps.reshape(steps, nb, LANES), p["ssm_alog_row"], p["ssm_dtb_row"],
                         p["ssm_dskip_row"], p["ssm_consts"], h0)
    tail = u3[steps - (CONV_WIDTH - 1):].astype(F32)
    ssm_c = tok(jnp.concatenate([tail[..., :SSM_D_INNER], tail[..., 2 * SSM_D_INNER:]], axis=-1))
    x = _mixer_out(y, pm, 1, p["ssm_norm"], x, mod3[1], rows, p["ssm_w_out"], SSM_D_INNER // SSM_GROUPS, True)
    y_out = _ffn(x, 1, mod3[1], rows, rows, p, p["norm_final"])

    return (tok(y_out.reshape(steps, nb, D_MODEL)), gdn_s[None], gdn_c[None],
            ssm_h.reshape(nb, SSM_HEADS, SSM_HEAD_DIM, SSM_STATE)[None], ssm_c[None])


def kernel(x_prompt, x_sample, c_prompt, c_sample, state_gdn, state_gdn_conv, state_ssm, state_ssm_conv, w_mod, b_mod,
           norm_mix, norm_ffn, norm_final, gdn_w_in, gdn_conv_w, gdn_a_log, gdn_dt_bias, gdn_norm, gdn_w_out, ssm_w_in,
           ssm_conv_w, ssm_conv_b, ssm_a_log, ssm_dt_bias, ssm_d, ssm_norm, ssm_w_out, ffn_w_gate_up, ffn_w_down):
    p = _stage_params(w_mod, b_mod, norm_mix, norm_ffn, norm_final, gdn_w_in, gdn_conv_w, gdn_a_log, gdn_dt_bias,
                      gdn_norm, gdn_w_out, ssm_w_in, ssm_conv_w, ssm_conv_b, ssm_a_log, ssm_dt_bias, ssm_d, ssm_norm,
                      ssm_w_out, ffn_w_gate_up, ffn_w_down)
    n_prompt = x_prompt.shape[0]
    mod = _modulation(jnp.concatenate([c_prompt, c_sample], axis=0), p["w_mod"], p["b_mod"])
    y_p, gs_p, gc_p, ss_p, sc_p = _trunk_seq(x_prompt, mod[:, :n_prompt], p)
    y_s, gs_s, gc_s, ss_s, sc_s = _trunk_step(x_sample, mod[:, n_prompt:], state_gdn, state_gdn_conv, state_ssm,
                                              state_ssm_conv, p)
    return (y_p, y_s, gs_p, gc_p, ss_p, sc_p, gs_s, gc_s, ss_s, sc_s)
```

```python
import functools

import numpy as np
import jax
import jax.numpy as jnp
from jax import lax
from jax.experimental import pallas as pl
from jax.experimental.pallas import tpu as pltpu

F32 = jnp.float32
BF16 = jnp.bfloat16

D_MODEL = 1024
EPS = 1e-6
CONV_WIDTH = 4
CHUNK = 64
LANES = 128
HALF = LANES // 2

GDN_QK_HEADS = 8
GDN_V_HEADS = 16
GDN_HEAD_DIM = 128
GDN_QK_DIM = GDN_QK_HEADS * GDN_HEAD_DIM
GDN_V_DIM = GDN_V_HEADS * GDN_HEAD_DIM
GDN_CONV_DIM = 2 * GDN_QK_DIM + GDN_V_DIM
GDN_MAIN = GDN_CONV_DIM + GDN_V_DIM

SSM_D_INNER = 2 * D_MODEL
SSM_HEAD_DIM = 64
SSM_HEADS = SSM_D_INNER // SSM_HEAD_DIM
SSM_GROUPS = 4
SSM_STATE = 128
SSM_BC = SSM_GROUPS * SSM_STATE
SSM_CONV_DIM = SSM_D_INNER + 2 * SSM_BC
SSM_MAIN = SSM_D_INNER + SSM_CONV_DIM

FFN_HIDDEN = 2816

VMEM_LIMIT = 56 * 1024 * 1024
NEG_BIG = -1e30


def _cparams(*sem):
    return pltpu.CompilerParams(dimension_semantics=sem, vmem_limit_bytes=VMEM_LIMIT)


def _silu(x):
    hx = 0.5 * x
    return hx + hx * jnp.tanh(hx)


def _softplus(x):
    return jnp.maximum(x, 0.0) + jnp.log1p(jnp.exp(-jnp.abs(x)))


def _dot(a, b):
    return jnp.dot(a, b, preferred_element_type=F32)


def _dot_nt(a, b):
    return lax.dot_general(a, b, (((1,), (1,)), ((), ())), preferred_element_type=F32)


def _tile_rows(v, rep):
    return v if rep == 1 else jnp.concatenate([v] * rep, axis=0)


def _mod_kernel(c_ref, w_ref, b_ref, o_ref):
    cs = _silu(c_ref[...]).astype(BF16)
    o_ref[...] = _dot(cs, w_ref[...].astype(BF16)) + b_ref[...]


def _modulation(c, w_mod, b_mod):
    depth, _, n = w_mod.shape
    bc = c.shape[0]
    tn = 1536
    return pl.pallas_call(
        _mod_kernel,
        grid=(depth, n // tn),
        in_specs=[
            pl.BlockSpec((bc, D_MODEL), lambda l, j: (0, 0)),
            pl.BlockSpec((None, D_MODEL, tn), lambda l, j: (l, 0, j)),
            pl.BlockSpec((None, 1, tn), lambda l, j: (l, 0, j)),
        ],
        out_specs=pl.BlockSpec((None, bc, tn), lambda l, j: (l, 0, j)),
        out_shape=jax.ShapeDtypeStruct((depth, bc, n), F32),
        compiler_params=_cparams("parallel", "parallel"),
        name="adaln_mod",
    )(c, w_mod, b_mod.reshape(depth, 1, n))


def _norm_mod(x, nw, sc, sh, rep):
    y = x * lax.rsqrt(jnp.mean(x * x, axis=-1, keepdims=True) + EPS) * nw
    return y * (1.0 + _tile_rows(sc, rep)) + _tile_rows(sh, rep)


def _in_proj_kernel(x_ref, nw_ref, sc_ref, sh_ref, w_ref, w2_ref, o_ref, o2_ref, h_ref, *, rep, gate_cols):
    j = pl.program_id(1)
    tn = o_ref.shape[1]
    lo, hi = gate_cols

    @pl.when(j == 0)
    def _():
        h = _norm_mod(x_ref[...], nw_ref[...], sc_ref[...], sh_ref[...], rep).astype(BF16)
        h_ref[...] = h
        o2_ref[...] = _dot(h, w2_ref[...])

    has_gate = jnp.logical_and(j * tn < hi, (j + 1) * tn > lo)

    @pl.when(has_gate)
    def _():
        r = _dot(h_ref[...], w_ref[...])
        col = j * tn + lax.broadcasted_iota(jnp.int32, r.shape, 1)
        o_ref[...] = jnp.where(jnp.logical_and(col >= lo, col < hi), _silu(r), r).astype(o_ref.dtype)

    @pl.when(jnp.logical_not(has_gate))
    def _():
        o_ref[...] = _dot(h_ref[...], w_ref[...]).astype(o_ref.dtype)


def _ffn_up_kernel(x_ref, nw_ref, sc_ref, sh_ref, wg_ref, wu_ref, o_ref, h_ref, *, rep):
    @pl.when(pl.program_id(1) == 0)
    def _():
        h_ref[...] = _norm_mod(x_ref[...], nw_ref[...], sc_ref[...], sh_ref[...], rep).astype(BF16)

    h = h_ref[...]
    o_ref[...] = (_silu(_dot(h, wg_ref[...])) * _dot(h, wu_ref[...])).astype(BF16)


class _Rows:
    def __init__(self, m, tm, group_rows, mod_rows):
        assert m % tm == 0
        self.m, self.tm = m, tm
        if mod_rows == 1:
            assert group_rows % tm == 0
            self.rep = 1
            self.gmap = lambda i: (i * tm) // group_rows
        else:
            assert tm % mod_rows == 0
            self.rep = tm // mod_rows
            self.gmap = lambda i: 0
        self.mod_rows = mod_rows

    def mod_spec(self, col_block, with_j):
        if with_j:
            return pl.BlockSpec((None, self.mod_rows, D_MODEL), lambda i, j: (self.gmap(i), 0, col_block))
        return pl.BlockSpec((None, self.mod_rows, D_MODEL), lambda i: (self.gmap(i), 0, col_block))


def _in_proj(x, nw, mod3, rows, sc_blk, sh_blk, w, w2, tn, gate_cols):
    m, tm = rows.m, rows.tm
    n = w.shape[1]
    assert n % tn == 0
    return pl.pallas_call(
        functools.partial(_in_proj_kernel, rep=rows.rep, gate_cols=gate_cols),
        grid=(m // tm, n // tn),
        in_specs=[
            pl.BlockSpec((tm, D_MODEL), lambda i, j: (i, 0)),
            pl.BlockSpec((1, D_MODEL), lambda i, j: (0, 0)),
            rows.mod_spec(sc_blk, True),
            rows.mod_spec(sh_blk, True),
            pl.BlockSpec((D_MODEL, tn), lambda i, j: (0, j)),
            pl.BlockSpec((D_MODEL, LANES), lambda i, j: (0, 0)),
        ],
        out_specs=[
            pl.BlockSpec((tm, tn), lambda i, j: (i, j)),
            pl.BlockSpec((tm, LANES), lambda i, j: (i, 0)),
        ],
        out_shape=[jax.ShapeDtypeStruct((m, n), BF16), jax.ShapeDtypeStruct((m, LANES), F32)],
        scratch_shapes=[pltpu.VMEM((tm, D_MODEL), BF16)],
        compiler_params=_cparams("parallel", "arbitrary"),
        name="in_proj",
    )(x, nw.reshape(1, D_MODEL), mod3, mod3, w, w2)


def _ffn_up(x, nw, mod3, rows, wg, wu, th):
    m, tm = rows.m, rows.tm
    assert FFN_HIDDEN % th == 0
    return pl.pallas_call(
        functools.partial(_ffn_up_kernel, rep=rows.rep),
        grid=(m // tm, FFN_HIDDEN // th),
        in_specs=[
            pl.BlockSpec((tm, D_MODEL), lambda i, j: (i, 0)),
            pl.BlockSpec((1, D_MODEL), lambda i, j: (0, 0)),
            rows.mod_spec(4, True),
            rows.mod_spec(3, True),
            pl.BlockSpec((D_MODEL, th), lambda i, j: (0, j)),
            pl.BlockSpec((D_MODEL, th), lambda i, j: (0, j)),
        ],
        out_specs=pl.BlockSpec((tm, th), lambda i, j: (i, j)),
        out_shape=jax.ShapeDtypeStruct((m, FFN_HIDDEN), BF16),
        scratch_shapes=[pltpu.VMEM((tm, D_MODEL), BF16)],
        compiler_params=_cparams("parallel", "arbitrary"),
        name="ffn_up",
    )(x, nw.reshape(1, D_MODEL), mod3, mod3, wg, wu)


def _resid_store(acc, x_ref, gt_ref, o_ref, fnw_ref, rep):
    xn = x_ref[...] + _tile_rows(gt_ref[...], rep) * acc
    if fnw_ref is not None:
        xn = xn * lax.rsqrt(jnp.mean(xn * xn, axis=-1, keepdims=True) + EPS) * fnw_ref[...]
    o_ref[...] = xn


def _ffn_down_kernel(a_ref, x_ref, gt_ref, w_ref, *rest, rep, final):
    fnw_ref, o_ref = rest if final else (None, rest[0])
    _resid_store(_dot(a_ref[...], w_ref[...]), x_ref, gt_ref, o_ref, fnw_ref, rep)


def _ffn_down(act, x, mod3, rows, w, fnw):
    m, tm = rows.m, rows.tm
    final = fnw is not None
    in_specs = [
        pl.BlockSpec((tm, FFN_HIDDEN), lambda i: (i, 0)),
        pl.BlockSpec((tm, D_MODEL), lambda i: (i, 0)),
        rows.mod_spec(5, False),
        pl.BlockSpec((FFN_HIDDEN, D_MODEL), lambda i: (0, 0), pipeline_mode=pl.Buffered(1)),
    ]
    args = [act, x, mod3, w]
    if final:
        in_specs.append(pl.BlockSpec((1, D_MODEL), lambda i: (0, 0)))
        args.append(fnw.reshape(1, D_MODEL))
    return pl.pallas_call(
        functools.partial(_ffn_down_kernel, rep=rows.rep, final=final),
        grid=(m // tm,),
        in_specs=in_specs,
        out_specs=pl.BlockSpec((tm, D_MODEL), lambda i: (i, 0)),
        out_shape=jax.ShapeDtypeStruct((m, D_MODEL), F32),
        compiler_params=_cparams("parallel"),
        name="ffn_down",
    )(*args)


def _mixer_out_kernel(y_ref, z_ref, nw_ref, x_ref, gt_ref, w_ref, o_ref, a_ref, *, rep, group, gate_first):
    width = y_ref.shape[1]
    for s in range(0, width, group):
        y = y_ref[:, s:s + group].astype(F32)
        gate = z_ref[:, s:s + group].astype(F32)
        if gate_first:
            y = y * gate
        y = y * lax.rsqrt(jnp.mean(y * y, axis=-1, keepdims=True) + EPS) * nw_ref[:, s:s + group]
        if not gate_first:
            y = y * gate
        a_ref[:, s:s + group] = y.astype(BF16)
    _resid_store(_dot(a_ref[...], w_ref[...]), x_ref, gt_ref, o_ref, None, rep)


def _mixer_out(y, zsrc, z_blk, nw_full, x, mod3, rows, w, group, gate_first):
    m, tm = rows.m, rows.tm
    width = y.shape[1]
    return pl.pallas_call(
        functools.partial(_mixer_out_kernel, rep=rows.rep, group=group, gate_first=gate_first),
        grid=(m // tm,),
        in_specs=[
            pl.BlockSpec((tm, width), lambda i: (i, 0)),
            pl.BlockSpec((tm, width), lambda i: (i, z_blk)),
            pl.BlockSpec((1, width), lambda i: (0, 0)),
            pl.BlockSpec((tm, D_MODEL), lambda i: (i, 0)),
            rows.mod_spec(2, False),
            pl.BlockSpec((width, D_MODEL), lambda i: (0, 0), pipeline_mode=pl.Buffered(1)),
        ],
        out_specs=pl.BlockSpec((tm, D_MODEL), lambda i: (i, 0)),
        out_shape=jax.ShapeDtypeStruct((m, D_MODEL), F32),
        scratch_shapes=[pltpu.VMEM((tm, width), BF16)],
        compiler_params=_cparams("parallel"),
        name="mixer_out",
    )(y, zsrc, nw_full.reshape(1, width), x, mod3, w)


CONV_COLS = 1024


def _post_conv(acc, o_ref, cb, n_l2, qscale):
    y = _silu(acc)
    if n_l2 == 0:
        o_ref[...] = y
        return

    @pl.when(cb < n_l2)
    def _():
        scale = jnp.where(cb == 0, qscale, 1.0).astype(F32)
        for s in range(0, CONV_COLS, GDN_HEAD_DIM):
            yh = y[:, s:s + GDN_HEAD_DIM]
            o_ref[:, s:s + GDN_HEAD_DIM] = yh * lax.rsqrt(jnp.sum(yh * yh, axis=-1, keepdims=True) + EPS) * scale

    @pl.when(cb >= n_l2)
    def _():
        o_ref[...] = y


BHALO = 16
CONV_GROUP = 2 * LANES


def _shift_matrix():
    s = np.zeros(((CONV_WIDTH - 1) * CHUNK, BHALO + CHUNK), np.float32)
    for tap in range(CONV_WIDTH - 1):
        for r in range(CHUNK):
            s[tap * CHUNK + r, BHALO - (CONV_WIDTH - 1) + tap + r] = 1.0
    return jnp.asarray(s, BF16)


def _conv_block(raw_refs, shift_ref, w_ref, b_ref, ext_ref, act_ref, seq_start, l2_cols, q_cols, qscale):
    width = ext_ref.shape[1]
    if seq_start:
        ext_ref[0:BHALO, :] = jnp.zeros((BHALO, width), BF16)
    else:
        ext_ref[0:BHALO, :] = ext_ref[CHUNK:CHUNK + BHALO, :]

    off = 0
    for ref in raw_refs:
        ext_ref[BHALO:BHALO + CHUNK, off:off + ref.shape[1]] = ref[...]
        off += ref.shape[1]

    def lane_group(s):
        cols = slice(s, s + CONV_GROUP)
        sh = _dot(shift_ref[...], ext_ref[:, cols])
        acc = b_ref[:, cols] + w_ref[CONV_WIDTH - 1:CONV_WIDTH, cols] * ext_ref[BHALO:BHALO + CHUNK, cols].astype(F32)
        for tap in range(CONV_WIDTH - 1):
            acc = acc + w_ref[tap:tap + 1, cols] * sh[tap * CHUNK:(tap + 1) * CHUNK]
        y = _silu(acc)
        for h in range(s, s + CONV_GROUP, LANES):
            yh = y[:, h - s:h - s + LANES]
            if h < l2_cols:
                yh = yh * lax.rsqrt(jnp.sum(yh * yh, axis=-1, keepdims=True) + EPS)
                if h < q_cols:
                    yh = yh * qscale
            act_ref[:, h:h + LANES] = yh

    return [functools.partial(lane_group, s) for s in range(0, width, CONV_GROUP)]


class _Filler:
    def __init__(self, tasks, shares):
        self.tasks, self.per = list(tasks), -(-len(tasks) // shares)

    def emit(self):
        for task in self.tasks[:self.per]:
            task()
        self.tasks = self.tasks[self.per:]

    def flush(self):
        for task in self.tasks:
            task()
        self.tasks = []


_NO_FILL = _Filler([], 1)


def _conv_steps_kernel(u_ref, hist_ref, w_ref, b_ref, o_ref, *, steps, n_l2, qscale):
    cb = pl.program_id(0)
    ext = [hist_ref[i] for i in range(CONV_WIDTH - 1)] + [u_ref[i].astype(F32) for i in range(steps)]
    for t in range(steps):
        acc = b_ref[...] + w_ref[0:1, :] * ext[t]
        for tap in range(1, CONV_WIDTH):
            acc = acc + w_ref[tap:tap + 1, :] * ext[t + tap]
        _post_conv(acc, o_ref.at[t], cb, n_l2, qscale)


def _conv_steps(u3, skip_at, skip, n_cols, hist3, conv_w, conv_b, n_l2, qscale):
    steps, nb, _ = u3.shape
    return pl.pallas_call(
        functools.partial(_conv_steps_kernel, steps=steps, n_l2=n_l2, qscale=qscale),
        grid=(n_cols // CONV_COLS,),
        in_specs=[
            pl.BlockSpec((steps, nb, CONV_COLS), lambda c: (0, 0, c + skip * (c >= skip_at))),
            pl.BlockSpec((CONV_WIDTH - 1, nb, CONV_COLS), lambda c: (0, 0, c)),
            pl.BlockSpec((CONV_WIDTH, CONV_COLS), lambda c: (0, c)),
            pl.BlockSpec((1, CONV_COLS), lambda c: (0, c)),
        ],
        out_specs=pl.BlockSpec((steps, nb, CONV_COLS), lambda c: (0, 0, c)),
        out_shape=jax.ShapeDtypeStruct((steps, nb, n_cols), F32),
        compiler_params=_cparams("parallel"),
        name="conv_steps",
    )(u3, hist3, conv_w, conv_b.reshape(1, n_cols))


def _split3(x):
    hi = x.astype(BF16)
    r = x - hi.astype(F32)
    mid = r.astype(BF16)
    lo = (r - mid.astype(F32)).astype(BF16)
    return hi, mid, lo


def _cat3(x, axis):
    return jnp.concatenate(_split3(x), axis=axis)


def _pad_t(tile):
    return jnp.concatenate([tile, jnp.zeros_like(tile)], axis=0).T


def _pair_rows(t, n):
    return t[0:n] + pltpu.roll(t[n:2 * n], HALF, 1)


def _iotas(shape):
    return lax.broadcasted_iota(jnp.int32, shape, 0), lax.broadcasted_iota(jnp.int32, shape, 1)


def _pad_rows(a):
    if a.shape[0] == CHUNK:
        return a
    return jnp.concatenate([a, jnp.zeros((CHUNK - a.shape[0], a.shape[1]), a.dtype)], axis=0)


def _split2(x):
    hi = x.astype(BF16)
    return hi, (x - hi.astype(F32)).astype(BF16)


def _block_diag(pair_bf, bd_ones):
    pair_bf = _pad_rows(pair_bf)
    return jnp.concatenate([pair_bf, pair_bf], axis=0) * bd_ones


def _mm3(lhs_parts, rhs_hi, rhs_lo):
    lh, ll = lhs_parts
    return _dot(jnp.concatenate([lh, lh, ll], axis=1), jnp.concatenate([rhs_hi, rhs_lo, rhs_hi], axis=0))


def _mm_pairs(lhs_parts, rhs_parts, bd_fn, terms=3):
    if terms == 1:
        return [_dot(lh, bd_fn(rh)) for (lh, _), (rh, _) in zip(lhs_parts, rhs_parts)]
    return [_mm3(lp, bd_fn(rh), bd_fn(rl)) for lp, (rh, rl) in zip(lhs_parts, rhs_parts)]


def _mm_hl(lhs, rhs_bf):
    return _dot(lhs.astype(BF16), rhs_bf)


def _unit_lower_inverse(xs, levels, bd_fn, eye, fill, terms=3):
    ps = [eye + x for x in xs]
    if levels <= 1:
        return ps
    rows = xs[0].shape[0]
    stack = lambda a, b: tuple(jnp.concatenate([s, t], axis=0) for s, t in zip(a, b))
    ysp = [_split2(x) for x in xs]
    ys = _mm_pairs(ysp, ysp, bd_fn, terms)
    fill.emit()
    for _ in range(levels - 2):
        ysp = [_split2(y) for y in ys]
        rs = _mm_pairs([stack(yp, _split2(p)) for yp, p in zip(ysp, ps)], ysp, bd_fn, terms)
        fill.emit()
        ys = [r[0:rows] for r in rs]
        ps = [p + r[rows:2 * rows] for p, r in zip(ps, rs)]
    last = _mm_pairs([_split2(p) for p in ps], [_split2(y) for y in ys], bd_fn, terms)
    return [p + t for p, t in zip(ps, last)]


QUARTER = HALF // 2


def _unit_lower_inverse_blocked(xs, fill):
    row, lane = _iotas((QUARTER, LANES))
    r2, l2 = _iotas((LANES, LANES))
    even = jnp.bitwise_and(jnp.right_shift(lane, 5), 1) == 0
    eye_q = jnp.where(row == jnp.bitwise_and(lane, QUARTER - 1), 1.0, 0.0)
    bd4_ones = jnp.where(jnp.right_shift(r2, 5) == jnp.right_shift(l2, 5), 1.0, 0.0).astype(BF16)
    row_q = jnp.right_shift(r2, 5)
    place = jnp.where(((row_q == 1) & (jnp.right_shift(l2, 5) == 0)) | ((row_q == 3) & (jnp.right_shift(l2, 5) == 2)),
                      1.0, 0.0).astype(BF16)
    bd4 = lambda a: jnp.concatenate([a] * 4, axis=0) * bd4_ones

    tops = [x[0:QUARTER] for x in xs]
    bots = [x[QUARTER:CHUNK] for x in xs]
    diag = [jnp.where(even, t, b) for t, b in zip(tops, bots)]
    dinv = _unit_lower_inverse(diag, _levels(QUARTER), bd4, eye_q, fill)
    fill.emit()
    ai = [jnp.where(even, d, 0.0) for d in dinv]
    ci = [jnp.where(even, 0.0, d) for d in dinv]
    b_ai = _mm_pairs([_split2(jnp.where(even, b, 0.0)) for b in bots], [_split2(a) for a in ai], bd4)
    fill.emit()
    low = [_mm3(_split2(c), *[jnp.concatenate([part] * 4, axis=0) * place for part in _split2(e)])
           for c, e in zip(ci, b_ai)]
    return [jnp.concatenate([a, l + c], axis=0) for a, l, c in zip(ai, low, ci)]


def _two_blocks(a, b):
    a_bf, b_bf = _pad_rows(a).astype(BF16), _pad_rows(b).astype(BF16)
    z = jnp.zeros_like(a_bf)
    return jnp.concatenate([jnp.concatenate([a_bf, z], axis=1), jnp.concatenate([z, b_bf], axis=1)], axis=0)


def _recurrence_consts(n_pairs, chan_even0, chan_odd0):
    tril = np.tril(np.ones((CHUNK, CHUNK), np.float32))
    tril3 = np.concatenate([tril] * 3, axis=1)
    upper = np.zeros((LANES, 2 * LANES), np.float32)
    upper[:CHUNK, :CHUNK] = tril.T
    upper[:CHUNK, LANES:] = 1.0
    upper3 = np.concatenate([upper] * 3, axis=0)
    e = np.zeros((LANES, n_pairs * LANES), np.float32)
    for p in range(n_pairs):
        e[chan_even0 + p, p * LANES:p * LANES + HALF] = 1.0
        e[chan_odd0 + p, p * LANES + HALF:(p + 1) * LANES] = 1.0
    e3 = np.concatenate([e] * 3, axis=0)
    return jnp.asarray(tril3, BF16), jnp.asarray(upper3, BF16), jnp.asarray(e3, BF16)


def _levels(lr):
    return max(1, int(np.ceil(np.log2(lr))))


STEP_ROWS = 16
DECODE_TOKENS_SINGLE_TERM = 4


GDN_PAIRS = GDN_V_HEADS // 2
GDN_PAIR_BATCH = 8
GDN_FILL_SHARES = 12 * (GDN_PAIRS // GDN_PAIR_BATCH)


def _gdn_gates(raw, alog_row, dtb_row, lr):
    row, lane = _iotas((CHUNK, LANES))
    g = -jnp.exp(alog_row) * _softplus(raw + dtb_row)
    beta = jax.nn.sigmoid(raw)
    gt = jnp.where(lane < GDN_V_HEADS, g, jnp.where(lane < 2 * GDN_V_HEADS, beta, 0.0))
    if lr < CHUNK:
        gt = jnp.where(row < lr, gt, 0.0)
    return gt


def _gdn_block(insts, tril3_ref, upper3_ref, e3_ref, lr, fill=_NO_FILL):
    np_ = GDN_PAIRS
    rows = insts[0][0].shape[0]
    row, lane = _iotas((rows, LANES))
    jl = jnp.bitwise_and(lane, HALF - 1)
    left = lane < HALF
    causal = row >= jl
    strict = row > jl
    eye2 = jnp.where(row == jl, 1.0, 0.0)
    r2, l2 = _iotas((LANES, LANES))
    bd_ones = jnp.where(jnp.right_shift(r2, 6) == jnp.right_shift(l2, 6), 1.0, 0.0).astype(BF16)
    _, lane_p = _iotas((np_, LANES))
    sl = lambda a, i: a[:, i * LANES:(i + 1) * LANES]

    pre = []
    for _, gt, _, _, _ in insts:
        t1 = _pad_t(gt)
        cum_ext = _dot(_cat3(t1[0:2 * np_], 1), upper3_ref[...])
        cum_t, last_b = cum_ext[:, :LANES], cum_ext[:, LANES:]
        cum_rp = _pair_rows(cum_t, np_)
        beta_rp = _pair_rows(t1[2 * np_:4 * np_], np_)
        last_rp = jnp.where(lane_p < HALF, last_b[0:np_], last_b[np_:2 * np_])
        cum = _dot(tril3_ref[...], _cat3(gt, 0))
        pre.append(dict(cum_rp=cum_rp, beta_rp=beta_rp, ecum_rp=jnp.exp(cum_rp),
                        kdec_rp=jnp.exp(last_rp - cum_rp) * beta_rp, elast=jnp.exp(last_b),
                        col_all=_dot(_cat3(cum[0:rows], 1), e3_ref[...])))

    def run(items):
        heads = [(i, 2 * p + hh) for i, p in items for hh in range(2)]
        act = lambda i: insts[i][0]
        prow = lambda name, it: pre[it[0]][name][it[1]:it[1] + 1]
        q = {it: sl(act(it[0]), it[1]) for it in items}
        k = {it: _pad_rows(sl(act(it[0]), GDN_QK_HEADS + it[1])) for it in items}
        v = {ih: sl(act(ih[0]), 2 * GDN_QK_HEADS + ih[1]) for ih in heads}
        kb = {it: k[it].astype(BF16) for it in items}
        qb = {it: q[it].astype(BF16) for it in items}
        gq = {it: _dot_nt(jnp.concatenate([kb[it][0:rows], qb[it]], axis=0), jnp.concatenate([kb[it], kb[it]], axis=0))
              for it in items}
        fill.emit()
        colc = {it: sl(pre[it[0]]["col_all"], it[1]) for it in items}
        base = {it: jnp.exp(jnp.where(causal, colc[it] - prow("cum_rp", it), NEG_BIG)) * prow("beta_rp", it)
                for it in items}
        x = [jnp.where(strict, -(gq[it][0:rows] * base[it]), 0.0) for it in items]
        qkd = {it: (gq[it][rows:2 * rows] * base[it]).astype(BF16) for it in items}
        if lr == CHUNK:
            minv = dict(zip(items, _unit_lower_inverse_blocked(x, fill)))
        else:
            minv = dict(zip(items, _unit_lower_inverse(x, _levels(lr), lambda a: _block_diag(a, bd_ones), eye2, fill,
                                                       terms=1 if lr <= DECODE_TOKENS_SINGLE_TERM else 3)))
        fill.emit()
        pair_of = lambda ih: (ih[0], ih[1] // 2)
        u = {it: jnp.concatenate([v[(it[0], 2 * it[1])], v[(it[0], 2 * it[1] + 1)]], axis=1)
             + _mm_hl(minv[it] - eye2, _two_blocks(v[(it[0], 2 * it[1])], v[(it[0], 2 * it[1] + 1)])) for it in items}
        fill.emit()
        w = {it: _mm_hl(minv[it] * prow("ecum_rp", it), _two_blocks(kb[it], kb[it])) for it in items}
        fill.emit()
        s_old = {ih: insts[ih[0]][2][ih[1]] for ih in heads}
        r = {ih: _dot(jnp.concatenate([sl(w[pair_of(ih)], ih[1] % 2).astype(BF16), qb[pair_of(ih)]], axis=0),
                      s_old[ih].astype(BF16)) for ih in heads}
        fill.emit()
        delta = {ih: sl(u[pair_of(ih)], ih[1] % 2) - r[ih][0:rows] for ih in heads}
        bd_delta = {it: _two_blocks(delta[(it[0], 2 * it[1])], delta[(it[0], 2 * it[1] + 1)]) for it in items}
        fill.emit()
        od = {it: _dot(jnp.concatenate(
            [qkd[it], (jnp.concatenate([k[it], k[it]], axis=0).T * prow("kdec_rp", it)).astype(BF16)], axis=0),
            bd_delta[it]) for it in items}
        for it in items:
            i, p = it
            ecol = jnp.exp(colc[it])
            ecol_r = pltpu.roll(ecol, HALF, 1)
            efull = (jnp.where(left, ecol, ecol_r), jnp.where(left, ecol_r, ecol))
            for hh in range(2):
                h = 2 * p + hh
                insts[i][4][:, h * LANES:(h + 1) * LANES] = (
                    efull[hh] * r[(i, h)][rows:2 * rows] + sl(od[it][0:rows], hh)).astype(insts[i][4].dtype)
                e_h = pre[i]["elast"][hh * np_ + p:hh * np_ + p + 1]
                insts[i][3][h] = e_h * s_old[(i, h)] + sl(od[it][rows:rows + LANES], hh)

    for b0 in range(0, np_, GDN_PAIR_BATCH):
        run([(i, p) for i in range(len(insts)) for p in range(b0, b0 + GDN_PAIR_BATCH)])
    fill.flush()


def _gdn_seq_kernel(raw0_ref, rawn_ref, graw_ref, shift_ref, cw_ref, cb_ref, alog_ref, dtb_ref, tril3_ref, upper3_ref,
                    e3_ref, o_ref, s_ref, ext_ref, act_ref):
    n = pl.program_id(1)
    seqs = range(graw_ref.shape[0])
    conv = functools.partial(_conv_block, shift_ref=shift_ref, w_ref=cw_ref, b_ref=cb_ref,
                             l2_cols=2 * GDN_QK_DIM, q_cols=GDN_QK_DIM, qscale=QSCALE)

    @pl.when(n == 0)
    def _():
        s_ref[...] = jnp.zeros(s_ref.shape, F32)
        for j in seqs:
            _Filler(conv([raw0_ref.at[j]], ext_ref=ext_ref.at[j], act_ref=act_ref.at[j, 0], seq_start=True), 1).flush()

    @pl.when(n > 0)
    def _():
        for j in seqs:
            act_ref[j, 0] = act_ref[j, 1]

    tasks, insts = [], []
    for j in seqs:
        gt = _gdn_gates(graw_ref[j], alog_ref[...], dtb_ref[...], CHUNK)
        tasks += conv([rawn_ref.at[j]], ext_ref=ext_ref.at[j], act_ref=act_ref.at[j, 1], seq_start=False)
        insts.append((act_ref.at[j, 0], gt, s_ref.at[j], s_ref.at[j], o_ref.at[j]))
    _gdn_block(insts, tril3_ref, upper3_ref, e3_ref, CHUNK, _Filler(tasks, GDN_FILL_SHARES))


def _const_specs(consts, ngrid):
    zero = (lambda *_: (0, 0))
    return [pl.BlockSpec(c.shape, zero) for c in consts]


GDN_SEQ_TOGETHER = 2
SSD_SEQ_TOGETHER = 1


def _seq_views(nseq, seq_len, together):
    together = min(together, nseq)
    assert nseq % together == 0
    nc = seq_len // CHUNK
    per = nseq // together
    last = per * nc - 1
    spec = lambda width, idx, col=0: pl.BlockSpec((together, CHUNK, width), lambda b, n: (0, idx(b, n), col))
    cur = lambda b, n: b * nc + n
    first = lambda b, n: b * nc
    nxt = lambda b, n: jnp.minimum(b * nc + n + 1, last)
    view = lambda a: a.reshape(together, a.shape[0] // together, a.shape[1])
    return together, nc, per, spec, cur, first, nxt, view


def _gdn_seq(pm, graw, conv_w, conv_b, alog_row, dtb_row, consts, nseq, seq_len):
    together, nc, per, spec, cur, first, nxt, view = _seq_views(nseq, seq_len, GDN_SEQ_TOGETHER)
    rowspec = pl.BlockSpec((1, LANES), lambda b, n: (0, 0))
    pm3 = view(pm)
    o, s = pl.pallas_call(
        _gdn_seq_kernel,
        grid=(per, nc),
        in_specs=[
            spec(GDN_CONV_DIM, first),
            spec(GDN_CONV_DIM, nxt),
            spec(LANES, cur),
            pl.BlockSpec(((CONV_WIDTH - 1) * CHUNK, BHALO + CHUNK), lambda b, n: (0, 0)),
            pl.BlockSpec((CONV_WIDTH, GDN_CONV_DIM), lambda b, n: (0, 0)),
            pl.BlockSpec((1, GDN_CONV_DIM), lambda b, n: (0, 0)),
            rowspec, rowspec,
        ] + _const_specs(consts, 2),
        out_specs=[
            spec(GDN_V_DIM, cur),
            pl.BlockSpec((together, None, GDN_V_HEADS, GDN_HEAD_DIM, GDN_HEAD_DIM), lambda b, n: (0, b, 0, 0, 0)),
        ],
        out_shape=[
            jax.ShapeDtypeStruct((together, per * seq_len, GDN_V_DIM), BF16),
            jax.ShapeDtypeStruct((together, per, GDN_V_HEADS, GDN_HEAD_DIM, GDN_HEAD_DIM), F32),
        ],
        scratch_shapes=[
            pltpu.VMEM((together, CHUNK + BHALO, GDN_CONV_DIM), BF16),
            pltpu.VMEM((together, 2, CHUNK, GDN_CONV_DIM), F32),
        ],
        compiler_params=_cparams("parallel", "arbitrary"),
        name="gdn_seq",
    )(pm3, pm3, view(graw), _shift_matrix(), conv_w, conv_b.reshape(1, GDN_CONV_DIM), alog_row, dtb_row, *consts)
    return (o.reshape(nseq * seq_len, GDN_V_DIM),
            s.reshape(nseq, GDN_V_HEADS, GDN_HEAD_DIM, GDN_HEAD_DIM))


SEQ_PER_STEP = 8
SEQ_INTERLEAVE = 4


def _load_padded(src_ref, s, pad_ref, lr):
    pad_ref[...] = jnp.zeros(pad_ref.shape, F32)
    for t in range(lr):
        pad_ref[t:t + 1, :] = src_ref[t, pl.ds(s, 1), :]


def _store_tokens(pad_ref, dst_ref, s, lr):
    for t in range(lr):
        dst_ref[t, pl.ds(s, 1), :] = pad_ref[t:t + 1, :]


def _gdn_step_kernel(act_ref, graw_ref, alog_ref, dtb_ref, tril3_ref, upper3_ref, e3_ref, s0_ref,
                     o_ref, s_ref, apad_ref, gpad_ref, opad_ref, *, lr):
    def some_sequences(it, carry):
        insts = []
        for j in range(SEQ_INTERLEAVE):
            s = it * SEQ_INTERLEAVE + j
            _load_padded(act_ref, s, apad_ref.at[j], lr)
            _load_padded(graw_ref, s, gpad_ref.at[j], lr)
            gt = _gdn_gates(gpad_ref[j], alog_ref[...], dtb_ref[...], lr)
            insts.append((apad_ref.at[j], gt, s0_ref.at[s], s_ref.at[s], opad_ref.at[j]))
        _gdn_block(insts, tril3_ref, upper3_ref, e3_ref, lr)
        for j in range(SEQ_INTERLEAVE):
            _store_tokens(opad_ref.at[j], o_ref, it * SEQ_INTERLEAVE + j, lr)
        return carry

    lax.fori_loop(0, SEQ_PER_STEP // SEQ_INTERLEAVE, some_sequences, 0)


def _step_spec(steps, width):
    return pl.BlockSpec((steps, SEQ_PER_STEP, width), lambda g: (0, g, 0))


def _gdn_step(act3, graw3, alog_row, dtb_row, consts, s0):
    steps, nb, _ = act3.shape
    assert nb % SEQ_PER_STEP == 0 and steps <= STEP_ROWS
    rowspec = pl.BlockSpec((1, LANES), lambda b: (0, 0))
    sspec = pl.BlockSpec((SEQ_PER_STEP, GDN_V_HEADS, GDN_HEAD_DIM, GDN_HEAD_DIM), lambda g: (g, 0, 0, 0))
    o, s = pl.pallas_call(
        functools.partial(_gdn_step_kernel, lr=steps),
        grid=(nb // SEQ_PER_STEP,),
        in_specs=[_step_spec(steps, GDN_CONV_DIM), _step_spec(steps, LANES), rowspec, rowspec]
        + _const_specs(consts, 1) + [sspec],
        out_specs=[_step_spec(steps, GDN_V_DIM), sspec],
        out_shape=[
            jax.ShapeDtypeStruct((steps, nb, GDN_V_DIM), F32),
            jax.ShapeDtypeStruct(s0.shape, F32),
        ],
        scratch_shapes=[
            pltpu.VMEM((SEQ_INTERLEAVE, STEP_ROWS, GDN_CONV_DIM), F32),
            pltpu.VMEM((SEQ_INTERLEAVE, CHUNK, LANES), F32),
            pltpu.VMEM((SEQ_INTERLEAVE, STEP_ROWS, GDN_V_DIM), F32),
        ],
        compiler_params=_cparams("parallel"),
        name="gdn_step",
    )(act3, graw3, alog_row, dtb_row, *consts, s0)
    return o.reshape(steps * nb, GDN_V_DIM), s


SSM_PAIRS = SSM_HEADS // 2
PAIRS_PER_GROUP = SSM_PAIRS // SSM_GROUPS


def _ssm_gates(raw, alog_row, dtb_row, lr):
    row, lane = _iotas((CHUNK, LANES))
    dt = _softplus(raw + dtb_row)
    tile = jnp.where(lane < SSM_HEADS, dt, jnp.where(lane < 2 * SSM_HEADS, -jnp.exp(alog_row) * dt, 0.0))
    if lr < CHUNK:
        tile = jnp.where(row < lr, tile, 0.0)
    return tile


def _ssd_block(insts, dskip_ref, tril3_ref, upper3_ref, e3_ref, fill=_NO_FILL):
    np_ = SSM_PAIRS
    rows = insts[0][0].shape[0]
    row, lane = _iotas((rows, LANES))
    causal = row >= jnp.bitwise_and(lane, HALF - 1)
    r2, l2 = _iotas((LANES, LANES))
    bdmask = jnp.right_shift(r2, 6) == jnp.right_shift(l2, 6)
    top = r2 < HALF
    _, lane_p = _iotas((np_, LANES))
    sl = lambda a, i: a[:, i * LANES:(i + 1) * LANES]
    grp = lambda it: (it[0], it[1] // PAIRS_PER_GROUP)

    pre = []
    for _, tile, _, _, _ in insts:
        t1 = _pad_t(tile)
        cum_ext = _dot(_cat3(t1[2 * np_:4 * np_], 1), upper3_ref[...])
        cum_t, last_b = cum_ext[:, :LANES], cum_ext[:, LANES:]
        cum_rp = _pair_rows(cum_t, np_)
        dt_rp = _pair_rows(t1[0:2 * np_], np_)
        last_rp = jnp.where(lane_p < HALF, last_b[0:np_], last_b[np_:2 * np_])
        cum = _dot(tril3_ref[...], _cat3(tile, 0))
        pre.append(dict(cum_rp=cum_rp, dt_rp=dt_rp, coef_rp=jnp.exp(last_rp - cum_rp) * dt_rp,
                        elast=jnp.exp(last_b), col_all=_dot(_cat3(cum[0:rows], 1), e3_ref[...])))

    seqs = range(len(insts))
    items = [(i, p) for i in seqs for p in range(np_)]
    groups = [(i, g) for i in seqs for g in range(SSM_GROUPS)]
    act = lambda i: insts[i][0]
    prow = lambda name, it: pre[it[0]][name][it[1]:it[1] + 1]
    bg = {ig: _pad_rows(sl(act(ig[0]), SSM_D_INNER // LANES + ig[1])).astype(BF16) for ig in groups}
    cg = {ig: sl(act(ig[0]), (SSM_D_INNER + SSM_BC) // LANES + ig[1]).astype(BF16) for ig in groups}
    bb = {ig: jnp.concatenate([bg[ig], bg[ig]], axis=0) for ig in groups}
    cb2 = {ig: _dot_nt(cg[ig], bb[ig]) for ig in groups}
    fill.emit()
    xp = {it: sl(act(it[0]), it[1]) for it in items}
    colc = {it: sl(pre[it[0]]["col_all"], it[1]) for it in items}
    x2 = {it: jnp.concatenate([_pad_rows(xp[it]), _pad_rows(xp[it])], axis=0) for it in items}
    lm = {it: cb2[grp(it)] * jnp.exp(jnp.where(causal, colc[it] - prow("cum_rp", it), NEG_BIG)) * prow("dt_rp", it)
          for it in items}
    y_diag = {}
    for n, it in enumerate(items):
        y_diag[it] = _dot(lm[it].astype(BF16), jnp.where(bdmask, x2[it], 0.0).astype(BF16))
        if n % (4 * len(insts)) == 4 * len(insts) - 1:
            fill.emit()
    hp = {it: insts[it[0]][2][it[1]] for it in items}
    y_off = {}
    for n, it in enumerate(items):
        y_off[it] = _dot_nt(cg[grp(it)], hp[it].astype(BF16))
        if n % (4 * len(insts)) == 4 * len(insts) - 1:
            fill.emit()
    for it in items:
        i, p = it
        insts[i][4][:, p * LANES:(p + 1) * LANES] = (
            y_diag[it] + jnp.exp(colc[it]) * y_off[it]
            + dskip_ref[:, p * LANES:(p + 1) * LANES] * xp[it]).astype(insts[i][4].dtype)
    dh = {}
    for n, it in enumerate(items):
        lhs = jnp.where(bdmask, x2[it].T * prow("coef_rp", it), 0.0)
        dh[it] = _dot(lhs.astype(BF16), bb[grp(it)])
        if n % (4 * len(insts)) == 4 * len(insts) - 1:
            fill.emit()
    for it in items:
        i, p = it
        elast = pre[i]["elast"]
        e_rows = jnp.where(top, elast[p:p + 1], elast[np_ + p:np_ + p + 1])
        insts[i][3][p] = e_rows * hp[it] + dh[it]
    fill.flush()


def _ssd_seq_kernel(x0_ref, bc0_ref, xn_ref, bcn_ref, graw_ref, shift_ref, cw_ref, cb_ref, alog_ref, dtb_ref,
                    dskip_ref, tril3_ref, upper3_ref, e3_ref, y_ref, h_ref, ext_ref, act_ref):
    n = pl.program_id(1)
    seqs = range(graw_ref.shape[0])
    conv = functools.partial(_conv_block, shift_ref=shift_ref, w_ref=cw_ref, b_ref=cb_ref,
                             l2_cols=0, q_cols=0, qscale=1.0)

    @pl.when(n == 0)
    def _():
        h_ref[...] = jnp.zeros(h_ref.shape, F32)
        for j in seqs:
            _Filler(conv([x0_ref.at[j], bc0_ref.at[j]], ext_ref=ext_ref.at[j], act_ref=act_ref.at[j, 0],
                         seq_start=True), 1).flush()

    @pl.when(n > 0)
    def _():
        for j in seqs:
            act_ref[j, 0] = act_ref[j, 1]

    tasks, insts = [], []
    for j in seqs:
        tile = _ssm_gates(graw_ref[j], alog_ref[...], dtb_ref[...], CHUNK)
        tasks += conv([xn_ref.at[j], bcn_ref.at[j]], ext_ref=ext_ref.at[j], act_ref=act_ref.at[j, 1], seq_start=False)
        insts.append((act_ref.at[j, 0], tile, h_ref.at[j], h_ref.at[j], y_ref.at[j]))
    _ssd_block(insts, dskip_ref, tril3_ref, upper3_ref, e3_ref, _Filler(tasks, SSD_FILL_SHARES))


SSD_FILL_SHARES = 13


def _ssd_seq(pm, graw, conv_w, conv_b, alog_row, dtb_row, dskip_row, consts, nseq, seq_len):
    together, nc, per, spec, cur, first, nxt, view = _seq_views(nseq, seq_len, SSD_SEQ_TOGETHER)
    bc_blk = 2 * SSM_D_INNER // (2 * SSM_BC)
    rowspec = pl.BlockSpec((1, LANES), lambda b, n: (0, 0))
    pm3 = view(pm)
    y, h = pl.pallas_call(
        _ssd_seq_kernel,
        grid=(per, nc),
        in_specs=[
            spec(SSM_D_INNER, first),
            spec(2 * SSM_BC, first, bc_blk),
            spec(SSM_D_INNER, nxt),
            spec(2 * SSM_BC, nxt, bc_blk),
            spec(LANES, cur),
            pl.BlockSpec(((CONV_WIDTH - 1) * CHUNK, BHALO + CHUNK), lambda b, n: (0, 0)),
            pl.BlockSpec((CONV_WIDTH, SSM_CONV_DIM), lambda b, n: (0, 0)),
            pl.BlockSpec((1, SSM_CONV_DIM), lambda b, n: (0, 0)),
            rowspec, rowspec,
            pl.BlockSpec((1, SSM_D_INNER), lambda b, n: (0, 0)),
        ] + _const_specs(consts, 2),
        out_specs=[
            spec(SSM_D_INNER, cur),
            pl.BlockSpec((together, None, SSM_PAIRS, LANES, SSM_STATE), lambda b, n: (0, b, 0, 0, 0)),
        ],
        out_shape=[
            jax.ShapeDtypeStruct((together, per * seq_len, SSM_D_INNER), BF16),
            jax.ShapeDtypeStruct((together, per, SSM_PAIRS, LANES, SSM_STATE), F32),
        ],
        scratch_shapes=[
            pltpu.VMEM((together, CHUNK + BHALO, SSM_CONV_DIM), BF16),
            pltpu.VMEM((together, 2, CHUNK, SSM_CONV_DIM), F32),
        ],
        compiler_params=_cparams("parallel", "arbitrary"),
        name="ssd_seq",
    )(pm3, pm3, pm3, pm3, view(graw), _shift_matrix(), conv_w, conv_b.reshape(1, SSM_CONV_DIM), alog_row, dtb_row,
      dskip_row, *consts)
    return y.reshape(nseq * seq_len, SSM_D_INNER), h.reshape(nseq, SSM_PAIRS, LANES, SSM_STATE)


def _ssd_step_kernel(act_ref, graw_ref, alog_ref, dtb_ref, dskip_ref, tril3_ref, upper3_ref, e3_ref, h0_ref,
                     y_ref, h_ref, apad_ref, gpad_ref, ypad_ref, *, lr):
    def some_sequences(it, carry):
        insts = []
        for j in range(SEQ_INTERLEAVE):
            s = it * SEQ_INTERLEAVE + j
            _load_padded(act_ref, s, apad_ref.at[j], lr)
            _load_padded(graw_ref, s, gpad_ref.at[j], lr)
            tile = _ssm_gates(gpad_ref[j], alog_ref[...], dtb_ref[...], lr)
            insts.append((apad_ref.at[j], tile, h0_ref.at[s], h_ref.at[s], ypad_ref.at[j]))
        _ssd_block(insts, dskip_ref, tril3_ref, upper3_ref, e3_ref)
        for j in range(SEQ_INTERLEAVE):
            _store_tokens(ypad_ref.at[j], y_ref, it * SEQ_INTERLEAVE + j, lr)
        return carry

    lax.fori_loop(0, SEQ_PER_STEP // SEQ_INTERLEAVE, some_sequences, 0)


def _ssd_step(act3, graw3, alog_row, dtb_row, dskip_row, consts, h0):
    steps, nb, _ = act3.shape
    assert nb % SEQ_PER_STEP == 0 and steps <= STEP_ROWS
    rowspec = pl.BlockSpec((1, LANES), lambda b: (0, 0))
    hspec = pl.BlockSpec((SEQ_PER_STEP, SSM_PAIRS, LANES, SSM_STATE), lambda g: (g, 0, 0, 0))
    y, h = pl.pallas_call(
        functools.partial(_ssd_step_kernel, lr=steps),
        grid=(nb // SEQ_PER_STEP,),
        in_specs=[_step_spec(steps, SSM_CONV_DIM), _step_spec(steps, LANES), rowspec, rowspec,
                  pl.BlockSpec((1, SSM_D_INNER), lambda b: (0, 0))] + _const_specs(consts, 1) + [hspec],
        out_specs=[_step_spec(steps, SSM_D_INNER), hspec],
        out_shape=[
            jax.ShapeDtypeStruct((steps, nb, SSM_D_INNER), F32),
            jax.ShapeDtypeStruct(h0.shape, F32),
        ],
        scratch_shapes=[
            pltpu.VMEM((SEQ_INTERLEAVE, STEP_ROWS, SSM_CONV_DIM), F32),
            pltpu.VMEM((SEQ_INTERLEAVE, CHUNK, LANES), F32),
            pltpu.VMEM((SEQ_INTERLEAVE, STEP_ROWS, SSM_D_INNER), F32),
        ],
        compiler_params=_cparams("parallel"),
        name="ssd_step",
    )(act3, graw3, alog_row, dtb_row, dskip_row, *consts, h0)
    return y.reshape(steps * nb, SSM_D_INNER), h


FFN_TILE = FFN_HIDDEN // 2
GDN_PROJ_TILE = GDN_MAIN // 3
SSM_PROJ_TILE = SSM_MAIN // 2


def _lane_row(pieces):
    row = jnp.zeros((1, LANES), F32)
    for off, vec in pieces:
        row = row.at[0, off:off + vec.shape[0]].set(vec.astype(F32))
    return row


def _stage_params(w_mod, b_mod, norm_mix, norm_ffn, norm_final, gdn_w_in, gdn_conv_w, gdn_a_log, gdn_dt_bias,
                  gdn_norm, gdn_w_out, ssm_w_in, ssm_conv_w, ssm_conv_b, ssm_a_log, ssm_dt_bias, ssm_d, ssm_norm,
                  ssm_w_out, ffn_w_gate_up, ffn_w_down):
    perm_g = np.concatenate([np.arange(0, GDN_V_HEADS, 2), np.arange(1, GDN_V_HEADS, 2)])
    perm_s = np.concatenate([np.arange(0, SSM_HEADS, 2), np.arange(1, SSM_HEADS, 2)])
    g_in, s_in = gdn_w_in[0], ssm_w_in[0]
    beta_cols = g_in[:, GDN_MAIN:GDN_MAIN + GDN_V_HEADS][:, perm_g]
    a_cols = g_in[:, GDN_MAIN + GDN_V_HEADS:GDN_MAIN + 2 * GDN_V_HEADS][:, perm_g]
    gdn_small = jnp.concatenate([a_cols, beta_cols, jnp.zeros((D_MODEL, LANES - 2 * GDN_V_HEADS), F32)], axis=1)
    dt_cols = s_in[:, SSM_MAIN:SSM_MAIN + SSM_HEADS][:, perm_s]
    ssm_small = jnp.concatenate([dt_cols, dt_cols, jnp.zeros((D_MODEL, LANES - 2 * SSM_HEADS), F32)], axis=1)
    return dict(
        w_mod=w_mod, b_mod=b_mod, norm_mix=norm_mix, norm_ffn=norm_ffn, norm_final=norm_final,
        gdn_main=g_in[:, :GDN_MAIN].astype(BF16), gdn_small=gdn_small.astype(BF16),
        gdn_conv_w=gdn_conv_w[0], gdn_conv_b=jnp.zeros((GDN_CONV_DIM,), F32),
        gdn_alog_row=_lane_row([(0, gdn_a_log[0][perm_g])]), gdn_dtb_row=_lane_row([(0, gdn_dt_bias[0][perm_g])]),
        gdn_norm=jnp.tile(gdn_norm[0], GDN_V_HEADS), gdn_w_out=gdn_w_out[0].astype(BF16),
        gdn_consts=_recurrence_consts(GDN_PAIRS, 0, GDN_PAIRS),
        ssm_main=jnp.concatenate(
            [s_in[:, SSM_D_INNER:2 * SSM_D_INNER], s_in[:, :SSM_D_INNER], s_in[:, 2 * SSM_D_INNER:SSM_MAIN]],
            axis=1).astype(BF16),
        ssm_small=ssm_small.astype(BF16),
        ssm_conv_w=ssm_conv_w[0], ssm_conv_b=ssm_conv_b[0],
        ssm_alog_row=_lane_row([(SSM_HEADS, ssm_a_log[0][perm_s])]),
        ssm_dtb_row=_lane_row([(0, ssm_dt_bias[0][perm_s]), (SSM_HEADS, ssm_dt_bias[0][perm_s])]),
        ssm_dskip_row=jnp.repeat(ssm_d[0], SSM_HEAD_DIM).reshape(1, SSM_D_INNER),
        ssm_norm=ssm_norm[0], ssm_w_out=ssm_w_out[0].astype(BF16),
        ssm_consts=_recurrence_consts(SSM_PAIRS, SSM_HEADS, SSM_HEADS + SSM_PAIRS),
        wg=[ffn_w_gate_up[i][:, :FFN_HIDDEN].astype(BF16) for i in range(2)],
        wu=[ffn_w_gate_up[i][:, FFN_HIDDEN:].astype(BF16) for i in range(2)],
        wd=[ffn_w_down[i].astype(BF16) for i in range(2)],
    )


def _ffn(x, layer, mod3, rows_up, rows_down, p, final_w):
    act = _ffn_up(x, p["norm_ffn"][layer], mod3, rows_up, p["wg"][layer], p["wu"][layer], FFN_TILE)
    return _ffn_down(act, x, mod3, rows_down, p["wd"][layer], final_w)


QSCALE = GDN_HEAD_DIM ** -0.5


def _trunk_seq(x3, mod, p):
    nseq, seq_len, _ = x3.shape
    m = nseq * seq_len
    x = x3.reshape(m, D_MODEL)
    mod3 = [mod[l].reshape(nseq, 1, 6 * D_MODEL) for l in range(2)]
    rows_a = _Rows(m, min(1024, seq_len), seq_len, 1)
    rows_b = _Rows(m, min(1024, seq_len), seq_len, 1)

    pm, ps = _in_proj(x, p["norm_mix"][0], mod3[0], rows_a, 1, 0, p["gdn_main"], p["gdn_small"], GDN_PROJ_TILE,
                      (GDN_CONV_DIM, GDN_MAIN))
    o, gdn_s = _gdn_seq(pm, ps, p["gdn_conv_w"], p["gdn_conv_b"], p["gdn_alog_row"], p["gdn_dtb_row"],
                        p["gdn_consts"], nseq, seq_len)
    tail = pm.reshape(nseq, seq_len, GDN_MAIN)[:, seq_len - (CONV_WIDTH - 1):].astype(F32)
    gdn_c = tail[..., :GDN_CONV_DIM]
    x = _mixer_out(o, pm, 2, p["gdn_norm"], x, mod3[0], rows_b, p["gdn_w_out"], GDN_HEAD_DIM, False)
    x = _ffn(x, 0, mod3[0], rows_a, rows_b, p, None)

    pm, ps = _in_proj(x, p["norm_mix"][1], mod3[1], rows_a, 1, 0, p["ssm_main"], p["ssm_small"], SSM_PROJ_TILE,
                      (SSM_D_INNER, 2 * SSM_D_INNER))
    y, ssm_h = _ssd_seq(pm, ps, p["ssm_conv_w"], p["ssm_conv_b"], p["ssm_alog_row"], p["ssm_dtb_row"],
                        p["ssm_dskip_row"], p["ssm_consts"], nseq, seq_len)
    tail = pm.reshape(nseq, seq_len, SSM_MAIN)[:, seq_len - (CONV_WIDTH - 1):].astype(F32)
    ssm_c = jnp.concatenate([tail[..., :SSM_D_INNER], tail[..., 2 * SSM_D_INNER:]], axis=-1)
    x = _mixer_out(y, pm, 1, p["ssm_norm"], x, mod3[1], rows_b, p["ssm_w_out"], SSM_D_INNER // SSM_GROUPS, True)
    y_out = _ffn(x, 1, mod3[1], rows_a, rows_b, p, p["norm_final"])

    return (y_out.reshape(nseq, seq_len, D_MODEL), gdn_s[None], gdn_c[None],
            ssm_h.reshape(nseq, SSM_HEADS, SSM_HEAD_DIM, SSM_STATE)[None], ssm_c[None])


def _trunk_step(x3, mod, st_gdn, cv_gdn, st_ssm, cv_ssm, p):
    nb, steps, _ = x3.shape
    assert steps >= CONV_WIDTH - 1
    m = nb * steps
    x = jnp.transpose(x3, (1, 0, 2)).reshape(m, D_MODEL)
    mod3 = [mod[l].reshape(1, nb, 6 * D_MODEL) for l in range(2)]
    rows = _Rows(m, m, None, nb)
    tok = lambda a: jnp.transpose(a, (1, 0, 2))

    pm, ps = _in_proj(x, p["norm_mix"][0], mod3[0], rows, 1, 0, p["gdn_main"], p["gdn_small"], GDN_PROJ_TILE,
                      (GDN_CONV_DIM, GDN_MAIN))
    u3 = pm.reshape(steps, nb, GDN_MAIN)
    act3 = _conv_steps(u3, 0, 0, GDN_CONV_DIM, tok(cv_gdn[0]), p["gdn_conv_w"], p["gdn_conv_b"], 2, QSCALE)
    o, gdn_s = _gdn_step(act3, ps.reshape(steps, nb, LANES), p["gdn_alog_row"], p["gdn_dtb_row"], p["gdn_consts"],
                         st_gdn[0])
    gdn_c = tok(u3[steps - (CONV_WIDTH - 1):, :, :GDN_CONV_DIM].astype(F32))
    x = _mixer_out(o, pm, 2, p["gdn_norm"], x, mod3[0], rows, p["gdn_w_out"], GDN_HEAD_DIM, False)
    x = _ffn(x, 0, mod3[0], rows, rows, p, None)

    pm, ps = _in_proj(x, p["norm_mix"][1], mod3[1], rows, 1, 0, p["ssm_main"], p["ssm_small"], SSM_PROJ_TILE,
                      (SSM_D_INNER, 2 * SSM_D_INNER))
    u3 = pm.reshape(steps, nb, SSM_MAIN)
    act3 = _conv_steps(u3, SSM_D_INNER // CONV_COLS, SSM_D_INNER // CONV_COLS, SSM_CONV_DIM, tok(cv_ssm[0]),
                       p["ssm_conv_w"], p["ssm_conv_b"], 0, 1.0)
    h0 = st_ssm[0].reshape(nb, SSM_PAIRS, LANES, SSM_STATE)
    y, ssm_h = _ssd_step(act3, ps.reshape(steps, nb, LANES), p["ssm_alog_row"], p["ssm_dtb_row"],
                         p["ssm_dskip_row"], p["ssm_consts"], h0)
    tail = u3[steps - (CONV_WIDTH - 1):].astype(F32)
    ssm_c = tok(jnp.concatenate([tail[..., :SSM_D_INNER], tail[..., 2 * SSM_D_INNER:]], axis=-1))
    x = _mixer_out(y, pm, 1, p["ssm_norm"], x, mod3[1], rows, p["ssm_w_out"], SSM_D_INNER // SSM_GROUPS, True)
    y_out = _ffn(x, 1, mod3[1], rows, rows, p, p["norm_final"])

    return (tok(y_out.reshape(steps, nb, D_MODEL)), gdn_s[None], gdn_c[None],
            ssm_h.reshape(nb, SSM_HEADS, SSM_HEAD_DIM, SSM_STATE)[None], ssm_c[None])


def kernel(x_prompt, x_sample, c_prompt, c_sample, state_gdn, state_gdn_conv, state_ssm, state_ssm_conv, w_mod, b_mod,
           norm_mix, norm_ffn, norm_final, gdn_w_in, gdn_conv_w, gdn_a_log, gdn_dt_bias, gdn_norm, gdn_w_out, ssm_w_in,
           ssm_conv_w, ssm_conv_b, ssm_a_log, ssm_dt_bias, ssm_d, ssm_norm, ssm_w_out, ffn_w_gate_up, ffn_w_down):
    p = _stage_params(w_mod, b_mod, norm_mix, norm_ffn, norm_final, gdn_w_in, gdn_conv_w, gdn_a_log, gdn_dt_bias,
                      gdn_norm, gdn_w_out, ssm_w_in, ssm_conv_w, ssm_conv_b, ssm_a_log, ssm_dt_bias, ssm_d, ssm_norm,
                      ssm_w_out, ffn_w_gate_up, ffn_w_down)
    n_prompt = x_prompt.shape[0]
    mod = _modulation(jnp.concatenate([c_prompt, c_sample], axis=0), p["w_mod"], p["b_mod"])
    y_p, gs_p, gc_p, ss_p, sc_p = _trunk_seq(x_prompt, mod[:, :n_prompt], p)
    y_s, gs_s, gc_s, ss_s, sc_s = _trunk_step(x_sample, mod[:, n_prompt:], state_gdn, state_gdn_conv, state_ssm,
                                              state_ssm_conv, p)
    return (y_p, y_s, gs_p, gc_p, ss_p, sc_p, gs_s, gc_s, ss_s, sc_s)
```
